```python
import math
import jax, jax.numpy as jnp
from jax import lax
import numpy as np

D_MODEL = 1024
BATCH = 16
SEQ = 2048
DEPTH = 2

N_EVEN = (DEPTH + 1) // 2
N_ODD = DEPTH // 2
EPS = 1e-6
NEG = -1e30
TINY = 1e-30
Q_BLOCK = 128

D_FF = 2816

NUM_BUCKETS = 32
MAX_DISTANCE = 2048
N_BIAS_COLS = 16

MLA_HEADS = 8
MLA_Q_RANK = 256
MLA_KV_RANK = 128
MLA_NOPE = 64
MLA_ROPE = 32
MLA_V = 64
MLA_QK_DIM = MLA_NOPE + MLA_ROPE
ROPE_THETA = 10000.0

DIL_PATTERNS = ((128, 1), (512, 4), (2048, 16))
DIL_HEADS_PER_GROUP = 4
DIL_HEADS = DIL_HEADS_PER_GROUP * len(DIL_PATTERNS)
DIL_HEAD_DIM = 64
DIL_BIAS_COL0 = 0

SWA_HEADS = 8
SWA_KV_HEADS = 2
SWA_HEAD_DIM = 64
SWA_WINDOW = 128
SWA_BIAS_COL0 = 0

NSA_HEADS = 8
NSA_KV_HEADS = 2
NSA_HEAD_DIM = 64
NSA_CMP_LEN = 32
NSA_CMP_STRIDE = 16
NSA_CMP_HIDDEN = 128
NSA_SLC_BLOCK = 64
NSA_TOP_N = 16
NSA_WINDOW = 512
NSA_FORCE = 1e6
NSA_BIAS_COL0 = 8

AB_MLA_COLS = MLA_Q_RANK + MLA_KV_RANK + MLA_ROPE
AB_DIL_COLS = 3 * DIL_HEADS * DIL_HEAD_DIM
AB_IN = AB_MLA_COLS + AB_DIL_COLS
AB_MIX = MLA_HEADS * MLA_V + DIL_HEADS_PER_GROUP * DIL_HEAD_DIM
CD_SWA_COLS = (SWA_HEADS + 2 * SWA_KV_HEADS) * SWA_HEAD_DIM
CD_NSA_COLS = NSA_HEADS * NSA_HEAD_DIM + 6 * NSA_KV_HEADS * NSA_HEAD_DIM + 3 * NSA_HEADS
CD_IN = CD_SWA_COLS + CD_NSA_COLS
CD_MIX = SWA_HEADS * SWA_HEAD_DIM + NSA_HEADS * NSA_HEAD_DIM

kernel_name = "hybrid_mla_dilated_swa_nsa_macaron"


def rms_norm(x, g):
    xf = x.astype(jnp.float32)
    y = xf * lax.rsqrt(jnp.mean(xf * xf, axis=-1, keepdims=True) + EPS)
    return (y * g.astype(jnp.float32)).astype(x.dtype)


def qk_norm(x, g):
    xf = x.astype(jnp.float32)
    return xf * lax.rsqrt(jnp.mean(xf * xf, axis=-1, keepdims=True) + EPS) * g.astype(jnp.float32)


def swiglu_half(x, g, w_gate, w_up, w_down):
    h = rms_norm(x, g)
    return (jax.nn.silu(h @ w_gate) * (h @ w_up)) @ w_down


def t5_bucket(dist):
    n = jnp.maximum(dist, 0)
    max_exact = NUM_BUCKETS // 2
    nf = jnp.maximum(n, 1).astype(jnp.float32)
    large = max_exact + (jnp.log(nf / max_exact) / math.log(MAX_DISTANCE / max_exact)
                         * (NUM_BUCKETS - max_exact)).astype(jnp.int32)
    large = jnp.minimum(large, NUM_BUCKETS - 1)
    return jnp.where(n < max_exact, n, large)


def rope(x, pos):
    half = x.shape[-1] // 2
    inv = jnp.power(ROPE_THETA, -jnp.arange(half, dtype=jnp.float32) / half)
    ang = pos[:, None] * inv[None, :]
    cos, sin = jnp.cos(ang)[:, None, :], jnp.sin(ang)[:, None, :]
    x1, x2 = x[..., :half], x[..., half:]
    return jnp.concatenate([x1 * cos - x2 * sin, x2 * cos + x1 * sin], axis=-1)


def causal_block_attention(q, k, v, scale):
    B_, S, H, dk = q.shape
    nb = S // Q_BLOCK
    qb = q.reshape(B_, nb, Q_BLOCK, H, dk).swapaxes(0, 1)
    kpos = jnp.arange(S)

    def one_block(args):
        qi, i = args
        qpos = i * Q_BLOCK + jnp.arange(Q_BLOCK)
        logits = jnp.einsum('bqhd,bkhd->bhqk', qi, k) * scale
        logits = jnp.where(kpos[None, :] <= qpos[:, None], logits, NEG)
        p = jax.nn.softmax(logits, axis=-1)
        return jnp.einsum('bhqk,bkhd->bqhd', p, v)

    out = lax.map(one_block, (qb, jnp.arange(nb)))
    return out.swapaxes(0, 1).reshape(B_, S, H, v.shape[-1])


def banded_attention(q, k, v, window, bias_cols, dist_scale=1, sinks=None):
    B_, S, G, R, dh = q.shape
    nb = -(-S // Q_BLOCK)
    Sp = nb * Q_BLOCK
    n_prev = -(-(window - 1) // Q_BLOCK)
    KW = (n_prev + 1) * Q_BLOCK
    q = jnp.pad(q, ((0, 0), (0, Sp - S), (0, 0), (0, 0), (0, 0)))
    kv_pad = ((0, 0), (n_prev * Q_BLOCK, Sp - S), (0, 0), (0, 0))
    k = jnp.pad(k, kv_pad)
    v = jnp.pad(v, kv_pad)
    qi_ = jnp.arange(Q_BLOCK)
    kc = jnp.arange(KW)
    delta = qi_[:, None] + n_prev * Q_BLOCK - kc[None, :]
    band = (delta >= 0) & (delta < window)
    bias = bias_cols.astype(jnp.float32)[t5_bucket(delta * dist_scale)]
    bias = jnp.transpose(bias.reshape(Q_BLOCK, KW, G, R), (2, 3, 0, 1))
    scale = dh ** -0.5
    qb = q.reshape(B_, nb, Q_BLOCK, G, R, dh).swapaxes(0, 1)

    def one_block(args):
        qblk, i = args
        start = i * Q_BLOCK
        kw = lax.dynamic_slice_in_dim(k, start, KW, axis=1)
        vw = lax.dynamic_slice_in_dim(v, start, KW, axis=1)
        logits = jnp.einsum('bqgrd,bkgd->bgrqk', qblk, kw) * scale + bias
        valid = band & (start - n_prev * Q_BLOCK + kc >= 0)[None, :]
        logits = jnp.where(valid, logits, NEG)
        if sinks is not None:
            sink_col = jnp.broadcast_to(sinks.astype(jnp.float32)[:, :, None, None],
                                        logits.shape[:-1] + (1,))
            logits = jnp.concatenate([logits, sink_col], axis=-1)
        lse = jax.nn.logsumexp(logits, axis=-1, keepdims=True)
        p = jnp.exp(logits - lse)[..., :KW]
        out = jnp.einsum('bgrqk,bkgd->bqgrd', p, vw)
        return out, lse[..., 0]

    out, lse = lax.map(one_block, (qb, jnp.arange(nb)))
    out = out.swapaxes(0, 1).reshape(B_, Sp, G, R, dh)[:, :S]
    lse = jnp.transpose(lse, (1, 0, 4, 2, 3)).reshape(B_, Sp, G, R)[:, :S]
    return out, lse


def mla_mixer(z, q_a_norm, w_q_b, kv_a_norm, w_kv_b, q_gain, k_gain, pos):
    B_, S, _ = z.shape
    c_q, c_kv, k_rope = jnp.split(z, [MLA_Q_RANK, MLA_Q_RANK + MLA_KV_RANK], axis=-1)
    q = (rms_norm(c_q, q_a_norm) @ w_q_b).reshape(B_, S, MLA_HEADS, MLA_QK_DIM)
    kv = (rms_norm(c_kv, kv_a_norm) @ w_kv_b).reshape(B_, S, MLA_HEADS, MLA_NOPE + MLA_V)
    k_nope, v = kv[..., :MLA_NOPE], kv[..., MLA_NOPE:]
    k = jnp.concatenate([k_nope, jnp.broadcast_to(k_rope[:, :, None, :], (B_, S, MLA_HEADS, MLA_ROPE))], axis=-1)
    q = qk_norm(q, q_gain)
    k = qk_norm(k, k_gain)
    q = jnp.concatenate([q[..., :MLA_NOPE], rope(q[..., MLA_NOPE:], pos)], axis=-1)
    k = jnp.concatenate([k[..., :MLA_NOPE], rope(k[..., MLA_NOPE:], pos)], axis=-1)
    return causal_block_attention(q, k, v.astype(jnp.float32), MLA_QK_DIM ** -0.5)


def dilated_mixer(z, q_gain, k_gain, rel_bias):
    B_, S, _ = z.shape
    Hg, dh = DIL_HEADS_PER_GROUP, DIL_HEAD_DIM
    zd = z.reshape(B_, S, 3, len(DIL_PATTERNS), Hg, dh)
    q = qk_norm(zd[:, :, 0], q_gain)
    k = qk_norm(zd[:, :, 1], k_gain)
    v = zd[:, :, 2].astype(jnp.float32)
    outs, lses = [], []
    for g, (w, d) in enumerate(DIL_PATTERNS):
        Sd = S // d

        def to_res(t):
            return t.reshape(B_, Sd, d, Hg, dh).transpose(0, 2, 1, 3, 4).reshape(B_ * d, Sd, Hg, dh)

        cols = rel_bias[:, DIL_BIAS_COL0 + g * Hg: DIL_BIAS_COL0 + (g + 1) * Hg]
        o, lse = banded_attention(to_res(q[:, :, g])[:, :, :, None], to_res(k[:, :, g]), to_res(v[:, :, g]),
                                  w // d + 1, cols, dist_scale=d)
        outs.append(o[:, :, :, 0].reshape(B_, d, Sd, Hg, dh).transpose(0, 2, 1, 3, 4).reshape(B_, S, Hg, dh))
        lses.append(lse[..., 0].reshape(B_, d, Sd, Hg).transpose(0, 2, 1, 3).reshape(B_, S, Hg))
    wts = jax.nn.softmax(jnp.stack(lses), axis=0)
    return jnp.sum(wts[..., None] * jnp.stack(outs), axis=0)


def swa_mixer(z, q_gain, k_gain, sinks, rel_bias):
    B_, S, _ = z.shape
    G, R, dh = SWA_KV_HEADS, SWA_HEADS // SWA_KV_HEADS, SWA_HEAD_DIM
    zq, zk, zv = jnp.split(z, [SWA_HEADS * dh, SWA_HEADS * dh + G * dh], axis=-1)
    q = qk_norm(zq.reshape(B_, S, G, R, dh), q_gain)
    k = qk_norm(zk.reshape(B_, S, G, dh), k_gain)
    v = zv.reshape(B_, S, G, dh).astype(jnp.float32)
    cols = rel_bias[:, SWA_BIAS_COL0:SWA_BIAS_COL0 + SWA_HEADS]
    o, _ = banded_attention(q, k, v, SWA_WINDOW, cols, sinks=sinks.reshape(G, R))
    return o.reshape(B_, S, SWA_HEADS * dh)


def nsa_mixer(z, q_gain, k_gains, cmp_pos, cmp_w1, cmp_w2, rel_bias):
    B_, S, _ = z.shape
    G, R, dh = NSA_KV_HEADS, NSA_HEADS // NSA_KV_HEADS, NSA_HEAD_DIM
    f32 = jnp.float32
    offs = np.cumsum([NSA_HEADS * dh] + [G * dh] * 6).tolist()
    zq, zkc, zvc, zks, zvs, zkw, zvw, zg = jnp.split(z, offs, axis=-1)
    q = qk_norm(zq.reshape(B_, S, G, R, dh), q_gain)

    def kvs(t):
        return t.reshape(B_, S, G, dh)

    scale = dh ** -0.5
    cols = rel_bias[:, NSA_BIAS_COL0:NSA_BIAS_COL0 + NSA_HEADS]
    tq = jnp.arange(S)

    n_cmp = (S - NSA_CMP_LEN) // NSA_CMP_STRIDE + 1
    blk_idx = np.arange(n_cmp)[:, None] * NSA_CMP_STRIDE + np.arange(NSA_CMP_LEN)[None, :]

    def compress(t, pos_emb, w1, w2):
        blocks = kvs(t)[:, blk_idx] + pos_emb[:, None, :]
        flat = blocks.transpose(0, 1, 3, 2, 4).reshape(B_, n_cmp, G, NSA_CMP_LEN * dh)
        return jax.nn.gelu(flat @ w1) @ w2

    k_c = qk_norm(compress(zkc, cmp_pos[0], cmp_w1[0], cmp_w2[0]), k_gains[0])
    v_c = compress(zvc, cmp_pos[1], cmp_w1[1], cmp_w2[1]).astype(f32)
    cmp_ok = jnp.asarray(blk_idx[:, -1])[None, :] <= tq[:, None]
    logits = jnp.einsum('bsgrd,bcgd->bgrsc', q, k_c) * scale
    logits = jnp.where(cmp_ok, logits, NEG)
    e = jnp.exp(logits - jnp.max(logits, axis=-1, keepdims=True)) * cmp_ok
    p_cmp = e / jnp.maximum(jnp.sum(e, axis=-1, keepdims=True), TINY)
    o_cmp = jnp.einsum('bgrsc,bcgd->bsgrd', p_cmp, v_c)

    n_slc = S // NSA_SLC_BLOCK
    ci = np.arange(n_cmp)[:, None] * NSA_CMP_STRIDE
    sj = np.arange(n_slc)[None, :] * NSA_SLC_BLOCK
    overlap = jnp.asarray(((ci < sj + NSA_SLC_BLOCK) & (ci + NSA_CMP_LEN > sj)).astype(np.float32))
    imp = jnp.einsum('bgrsc,cj->bgsj', p_cmp, overlap)
    tb = tq // NSA_SLC_BLOCK
    jj = jnp.arange(n_slc)
    causal_blk = jj[None, :] <= tb[:, None]
    forced = (jj[None, :] == 0) | (jj[None, :] == tb[:, None]) | (jj[None, :] == tb[:, None] - 1)
    score = jnp.where(causal_blk, imp + jnp.where(forced, NSA_FORCE, 0.0), -NSA_FORCE)
    top_n = min(NSA_TOP_N, n_slc)
    _, sel = lax.top_k(score, top_n)

    k_s = qk_norm(kvs(zks), k_gains[1]).transpose(0, 2, 1, 3).reshape(B_, G, n_slc, NSA_SLC_BLOCK, dh)
    v_s = kvs(zvs).astype(f32).transpose(0, 2, 1, 3).reshape(B_, G, n_slc, NSA_SLC_BLOCK, dh)
    nb = S // Q_BLOCK
    q_blk = q.transpose(0, 2, 3, 1, 4).reshape(B_, G, R, nb, Q_BLOCK, dh).transpose(3, 0, 1, 2, 4, 5)
    sel_blk = sel.reshape(B_, G, nb, Q_BLOCK, top_n).transpose(2, 0, 1, 3, 4)
    bias_gr = cols.astype(f32).T.reshape(G, R, NUM_BUCKETS)
    b_ix = jnp.arange(B_)[:, None, None, None]
    g_ix = jnp.arange(G)[None, :, None, None]
    g6 = jnp.arange(G)[None, :, None, None, None, None]
    r6 = jnp.arange(R)[None, None, :, None, None, None]
    tok = jnp.arange(NSA_SLC_BLOCK)

    def slc_block(args):
        qi, si, i = args
        kg = k_s[b_ix, g_ix, si]
        vg = v_s[b_ix, g_ix, si]
        dist = (i * Q_BLOCK + jnp.arange(Q_BLOCK))[None, None, :, None, None] - (si[..., None] * NSA_SLC_BLOCK + tok)
        bias = bias_gr[g6, r6, t5_bucket(dist)[:, :, None]]
        lg = jnp.einsum('bgrqd,bgqnkd->bgrqnk', qi, kg) * scale + bias
        lg = jnp.where((dist >= 0)[:, :, None], lg, NEG)
        p = jax.nn.softmax(lg, axis=(-2, -1))
        return jnp.einsum('bgrqnk,bgqnkd->bqgrd', p, vg)

    o_slc = lax.map(slc_block, (q_blk, sel_blk, jnp.arange(nb))).swapaxes(0, 1).reshape(B_, S, G, R, dh)

    o_win, _ = banded_attention(q, qk_norm(kvs(zkw), k_gains[2]), kvs(zvw).astype(f32), NSA_WINDOW, cols)

    gate = jax.nn.sigmoid(zg.astype(f32)).reshape(B_, S, G, R, 3)
    out = gate[..., 0:1] * o_cmp + gate[..., 1:2] * o_slc + gate[..., 2:3] * o_win
    return out.reshape(B_, S, NSA_HEADS * dh)


def setup_inputs(seed: int = 0) -> dict:
    key = jax.random.key(seed)
    keys = iter(jax.random.split(key, 48))
    f32 = jnp.float32

    def w(shape, fan_in):
        return jax.random.normal(next(keys), shape, f32) * fan_in ** -0.5

    def gain(shape):
        return 1.0 + 0.05 * jax.random.normal(next(keys), shape, f32)

    NE, NO = N_EVEN, N_ODD
    return {
        "x": jax.random.normal(next(keys), (BATCH, SEQ, D_MODEL), f32),
        "rel_bias": 0.5 * jax.random.normal(next(keys), (NUM_BUCKETS, N_BIAS_COLS), f32),
        "ffn1_norm": gain((DEPTH, D_MODEL)),
        "ffn1_w_gate": w((DEPTH, D_MODEL, D_FF), D_MODEL),
        "ffn1_w_up": w((DEPTH, D_MODEL, D_FF), D_MODEL),
        "ffn1_w_down": w((DEPTH, D_FF, D_MODEL), D_FF),
        "mix_norm": gain((DEPTH, D_MODEL)),
        "ffn2_norm": gain((DEPTH, D_MODEL)),
        "ffn2_w_gate": w((DEPTH, D_MODEL, D_FF), D_MODEL),
        "ffn2_w_up": w((DEPTH, D_MODEL, D_FF), D_MODEL),
        "ffn2_w_down": w((DEPTH, D_FF, D_MODEL), D_FF),
        "ab_w_in": w((NE, D_MODEL, AB_IN), D_MODEL),
        "mla_q_a_norm": gain((NE, MLA_Q_RANK)),
        "mla_w_q_b": w((NE, MLA_Q_RANK, MLA_HEADS * MLA_QK_DIM), MLA_Q_RANK),
        "mla_kv_a_norm": gain((NE, MLA_KV_RANK)),
        "mla_w_kv_b": w((NE, MLA_KV_RANK, MLA_HEADS * (MLA_NOPE + MLA_V)), MLA_KV_RANK),
        "mla_q_norm": gain((NE, MLA_QK_DIM)),
        "mla_k_norm": gain((NE, MLA_QK_DIM)),
        "dil_q_norm": gain((NE, DIL_HEAD_DIM)),
        "dil_k_norm": gain((NE, DIL_HEAD_DIM)),
        "ab_w_out": w((NE, AB_MIX, D_MODEL), AB_MIX),
        "cd_w_in": w((NO, D_MODEL, CD_IN), D_MODEL),
        "swa_q_norm": gain((NO, SWA_HEAD_DIM)),
        "swa_k_norm": gain((NO, SWA_HEAD_DIM)),
        "swa_sinks": jax.random.normal(next(keys), (NO, SWA_HEADS), f32),
        "nsa_q_norm": gain((NO, NSA_HEAD_DIM)),
        "nsa_k_norm": gain((NO, 3, NSA_HEAD_DIM)),
        "nsa_cmp_pos": 0.1 * jax.random.normal(next(keys), (NO, 2, NSA_CMP_LEN, NSA_HEAD_DIM), f32),
        "nsa_cmp_w1": w((NO, 2, NSA_CMP_LEN * NSA_HEAD_DIM, NSA_CMP_HIDDEN), NSA_CMP_LEN * NSA_HEAD_DIM),
        "nsa_cmp_w2": w((NO, 2, NSA_CMP_HIDDEN, NSA_HEAD_DIM), NSA_CMP_HIDDEN),
        "cd_w_out": w((NO, CD_MIX, D_MODEL), CD_MIX),
    }


def reference(x, rel_bias, ffn1_norm, ffn1_w_gate, ffn1_w_up, ffn1_w_down, mix_norm,
              ffn2_norm, ffn2_w_gate, ffn2_w_up, ffn2_w_down, ab_w_in, mla_q_a_norm, mla_w_q_b,
              mla_kv_a_norm, mla_w_kv_b, mla_q_norm, mla_k_norm, dil_q_norm, dil_k_norm, ab_w_out,
              cd_w_in, swa_q_norm, swa_k_norm, swa_sinks, nsa_q_norm, nsa_k_norm, nsa_cmp_pos,
              nsa_cmp_w1, nsa_cmp_w2, cd_w_out):
    B_, S, _ = x.shape
    pos = jnp.arange(S, dtype=jnp.float32)
    for layer in range(DEPTH):
        x = x + 0.5 * swiglu_half(x, ffn1_norm[layer], ffn1_w_gate[layer], ffn1_w_up[layer], ffn1_w_down[layer])
        h = rms_norm(x, mix_norm[layer])
        if layer % 2 == 0:
            e = layer // 2
            z = h @ ab_w_in[e]
            o_a = mla_mixer(z[..., :AB_MLA_COLS], mla_q_a_norm[e], mla_w_q_b[e], mla_kv_a_norm[e],
                            mla_w_kv_b[e], mla_q_norm[e], mla_k_norm[e], pos)
            o_b = dilated_mixer(z[..., AB_MLA_COLS:], dil_q_norm[e], dil_k_norm[e], rel_bias)
            mixed = jnp.concatenate([o_a.reshape(B_, S, -1), o_b.reshape(B_, S, -1)], axis=-1)
            mixed = mixed.astype(x.dtype) @ ab_w_out[e]
        else:
            o = layer // 2
            z = h @ cd_w_in[o]
            o_c = swa_mixer(z[..., :CD_SWA_COLS], swa_q_norm[o], swa_k_norm[o], swa_sinks[o], rel_bias)
            o_d = nsa_mixer(z[..., CD_SWA_COLS:], nsa_q_norm[o], nsa_k_norm[o], nsa_cmp_pos[o],
                            nsa_cmp_w1[o], nsa_cmp_w2[o], rel_bias)
            mixed = jnp.concatenate([o_c, o_d], axis=-1).astype(x.dtype) @ cd_w_out[o]
        x = x + mixed
        x = x + 0.5 * swiglu_half(x, ffn2_norm[layer], ffn2_w_gate[layer], ffn2_w_up[layer], ffn2_w_down[layer])
    return x
```

```python
import functools
import math

import numpy as np
import jax
import jax.numpy as jnp
from jax import lax
from jax.experimental import pallas as pl
from jax.experimental.pallas import tpu as pltpu

F32 = jnp.float32
BF16 = jnp.bfloat16

EPS = 1e-6
NEG = -1e30
TINY = 1e-30
D_MODEL = 1024
D_FF = 2816
NUM_BUCKETS = 32
MAX_DISTANCE = 2048
HEAD_DIM = 64
QB = 128

MLA_HEADS = 8
MLA_Q_RANK = 256
MLA_KV_RANK = 128
MLA_NOPE = 64
MLA_ROPE = 32
MLA_V = 64
MLA_QK_DIM = MLA_NOPE + MLA_ROPE
MLA_PAD = 128
ROPE_THETA = 10000.0

DIL_PATTERNS = ((128, 1), (512, 4), (2048, 16))
DIL_HPG = 4
SWA_HEADS = 8
SWA_KV_HEADS = 2
SWA_WINDOW = 128
NSA_HEADS = 8
NSA_KV_HEADS = 2
NSA_CMP_LEN = 32
NSA_CMP_STRIDE = 16
NSA_CMP_HIDDEN = 128
NSA_SLC_BLOCK = 64
NSA_TOP_N = 16
NSA_WINDOW = 512
NSA_FORCE = 1e6
NSA_BIAS_COL0 = 8
NSA_GROUP_COLS = 768

VMEM_LIMIT = 56 * 1024 * 1024
TM = 512

_NT = (((1,), (1,)), ((), ()))
_HI = lax.Precision.HIGHEST


def _cparams(n_axes):
    return pltpu.CompilerParams(dimension_semantics=("arbitrary",) * n_axes,
                                vmem_limit_bytes=VMEM_LIMIT)


def _resident(shape):
    nd = len(shape)
    return pl.BlockSpec(shape, lambda *_: (0,) * nd, pipeline_mode=pl.Buffered(1))


def _rms(x):
    return x * lax.rsqrt(jnp.mean(x * x, axis=-1, keepdims=True) + EPS)


def _dot(a, b):
    return jnp.dot(a, b, preferred_element_type=F32)


def _dot_nt(a, b, precision=None):
    return lax.dot_general(a, b, _NT, preferred_element_type=F32, precision=precision)


FF_CHUNK = 256


def _ffn_kernel(x_ref, g_ref, wg_ref, wu_ref, wd_ref, o_ref):
    x = x_ref[...]
    hb = (_rms(x) * g_ref[...]).astype(BF16)
    acc = jnp.zeros(x.shape, F32)
    for c in range(D_FF // FF_CHUNK):
        sl = slice(c * FF_CHUNK, (c + 1) * FF_CHUNK)
        gate = _dot(hb, wg_ref[:, sl])
        up = _dot(hb, wu_ref[:, sl])
        act = (gate * jax.nn.sigmoid(gate) * up).astype(BF16)
        acc = acc + _dot(act, wd_ref[sl, :])
    o_ref[...] = x + 0.5 * acc


def _ffn(x, g, wg, wu, wd):
    n = x.shape[0]
    return pl.pallas_call(
        _ffn_kernel,
        grid=(n // TM,),
        in_specs=[pl.BlockSpec((TM, D_MODEL), lambda i: (i, 0)),
                  _resident((1, D_MODEL)),
                  _resident((D_MODEL, D_FF)), _resident((D_MODEL, D_FF)), _resident((D_FF, D_MODEL))],
        out_specs=pl.BlockSpec((TM, D_MODEL), lambda i: (i, 0)),
        out_shape=jax.ShapeDtypeStruct((n, D_MODEL), F32),
        compiler_params=_cparams(1),
        name="ffn",
    )(x, g.reshape(1, D_MODEL), wg, wu, wd)


N_CHUNK = 512


def _norm_proj_kernel(n_out, x_ref, g_ref, *refs):
    w_refs, o_refs = refs[:n_out], refs[n_out:]
    hb = (_rms(x_ref[...]) * g_ref[...]).astype(BF16)
    for w_ref, o_ref in zip(w_refs, o_refs):
        width = w_ref.shape[1]
        for c0 in range(0, width, N_CHUNK):
            sl = slice(c0, min(c0 + N_CHUNK, width))
            o_ref[:, sl] = _dot(hb, w_ref[:, sl])


def _norm_proj(x, g, ws):
    n = x.shape[0]
    return pl.pallas_call(
        functools.partial(_norm_proj_kernel, len(ws)),
        grid=(n // TM,),
        in_specs=[pl.BlockSpec((TM, D_MODEL), lambda i: (i, 0)), _resident((1, D_MODEL))]
        + [_resident(w.shape) for w in ws],
        out_specs=[pl.BlockSpec((TM, w.shape[1]), lambda i: (i, 0)) for w in ws],
        out_shape=[jax.ShapeDtypeStruct((n, w.shape[1]), F32) for w in ws],
        compiler_params=_cparams(1),
        name="norm_proj",
    )(x, g.reshape(1, D_MODEL), *ws)


def _out_ab_kernel(x_ref, oa_ref, o0_ref, o1_ref, o2_ref, l0_ref, l1_ref, l2_ref, wa_ref, wb_ref, out_ref):
    l0, l1, l2 = l0_ref[...], l1_ref[...], l2_ref[...]
    m = jnp.maximum(jnp.maximum(l0, l1), l2)
    e0, e1, e2 = jnp.exp(l0 - m), jnp.exp(l1 - m), jnp.exp(l2 - m)
    ob = (e0 * o0_ref[...] + e1 * o1_ref[...] + e2 * o2_ref[...]) / (e0 + e1 + e2)
    out_ref[...] = x_ref[...] + _dot(oa_ref[...], wa_ref[...]) + _dot(ob.astype(BF16), wb_ref[...])


def _out_ab(x, oa, dil_outs, dil_lses, wa, wb):
    n = x.shape[0]
    dw = DIL_HPG * HEAD_DIM
    tile = lambda w: pl.BlockSpec((TM, w), lambda i: (i, 0))
    return pl.pallas_call(
        _out_ab_kernel,
        grid=(n // TM,),
        in_specs=[tile(D_MODEL), tile(oa.shape[1])] + [tile(dw)] * 6 + [_resident(wa.shape), _resident(wb.shape)],
        out_specs=tile(D_MODEL),
        out_shape=jax.ShapeDtypeStruct((n, D_MODEL), F32),
        compiler_params=_cparams(1),
        name="out_ab",
    )(x, oa, *dil_outs, *dil_lses, wa, wb)


def _out_cd_kernel(x_ref, oc_ref, od_ref, wc_ref, wd_ref, out_ref):
    out_ref[...] = x_ref[...] + _dot(oc_ref[...], wc_ref[...]) + _dot(od_ref[...], wd_ref[...])


def _out_cd(x, oc, od, wc, wd):
    n = x.shape[0]
    tile = lambda w: pl.BlockSpec((TM, w), lambda i: (i, 0))
    return pl.pallas_call(
        _out_cd_kernel,
        grid=(n // TM,),
        in_specs=[tile(D_MODEL), tile(oc.shape[1]), tile(od.shape[1]), _resident(wc.shape), _resident(wd.shape)],
        out_specs=tile(D_MODEL),
        out_shape=jax.ShapeDtypeStruct((n, D_MODEL), F32),
        compiler_params=_cparams(1),
        name="out_cd",
    )(x, oc, od, wc, wd)


def _mla_prep_kernel(z_ref, qan_ref, kvan_ref, wq_ref, wk_ref, wv_ref, gq_ref, gk_ref,
                     rc_ref, rm_ref, rp_ref, q_ref, k_ref, v_ref):
    cq = (_rms(z_ref[:, :MLA_Q_RANK]) * qan_ref[...]).astype(BF16)
    ckv = (_rms(z_ref[:, MLA_Q_RANK:MLA_Q_RANK + MLA_KV_RANK]) * kvan_ref[...]).astype(BF16)
    kr = z_ref[:, MLA_Q_RANK + MLA_KV_RANK:]
    rc, rm, rp = rc_ref[...], rm_ref[...], rp_ref[...]
    scale = MLA_QK_DIM ** -0.5

    def head_norm_rope(x, gain):
        x = x * lax.rsqrt(jnp.sum(x * x, axis=-1, keepdims=True) * (1.0 / MLA_QK_DIM) + EPS) * gain
        half = MLA_ROPE // 2
        return x * rc + pltpu.roll(x, MLA_PAD - half, 1) * rm + pltpu.roll(x, half, 1) * rp

    for h in range(MLA_HEADS):
        sl = slice(h * MLA_PAD, (h + 1) * MLA_PAD)
        qh = head_norm_rope(_dot(cq, wq_ref[:, sl]), gq_ref[...])
        q_ref[:, sl] = (qh * scale).astype(BF16)
        kh = head_norm_rope(_dot(ckv, wk_ref[:, sl]) + kr, gk_ref[...])
        k_ref[:, sl] = kh.astype(BF16)
    v_ref[...] = _dot(ckv, wv_ref[...]).astype(BF16)


def _mla_prep(z, qan, kvan, wq, wk, wv, gq, gk, rope_c, rope_m, rope_p, seq):
    n = z.shape[0]
    per_seq = seq // TM
    tile = lambda w: pl.BlockSpec((TM, w), lambda i: (i, 0))
    rope_spec = pl.BlockSpec((TM, MLA_PAD), lambda i: (i % per_seq, 0))
    hw = MLA_HEADS * MLA_PAD
    return pl.pallas_call(
        _mla_prep_kernel,
        grid=(n // TM,),
        in_specs=[tile(z.shape[1]), _resident(qan.shape), _resident(kvan.shape), _resident(wq.shape),
                  _resident(wk.shape), _resident(wv.shape), _resident(gq.shape), _resident(gk.shape),
                  rope_spec, rope_spec, rope_spec],
        out_specs=[tile(hw), tile(hw), tile(MLA_HEADS * MLA_V)],
        out_shape=[jax.ShapeDtypeStruct((n, hw), BF16), jax.ShapeDtypeStruct((n, hw), BF16),
                   jax.ShapeDtypeStruct((n, MLA_HEADS * MLA_V), BF16)],
        compiler_params=_cparams(1),
        name="mla_prep",
    )(z, qan, kvan, wq, wk, wv, gq, gk, rope_c, rope_m, rope_p)


MLA_TQ = 256


def _mla_attn_kernel(seq, q_ref, k_ref, v_ref, o_ref):
    t = MLA_TQ
    row = lax.broadcasted_iota(jnp.int32, (t, t), 0)
    col = lax.broadcasted_iota(jnp.int32, (t, t), 1)
    diag_ok = col <= row
    for hh in range(2):
        ql = slice(hh * MLA_PAD, (hh + 1) * MLA_PAD)
        vl = slice(hh * MLA_V, (hh + 1) * MLA_V)

        def q_block(i, carry):
            r0 = pl.multiple_of(i * t, t)
            q = q_ref[pl.ds(r0, t), ql]

            def step(s, m, l, acc, v):
                m_new = jnp.maximum(m, jnp.max(s, axis=-1, keepdims=True))
                p = jnp.exp(s - m_new)
                alpha = jnp.exp(m - m_new)
                l = alpha * l + jnp.sum(p, axis=-1, keepdims=True)
                acc = alpha * acc + _dot(p.astype(BF16), v)
                return m_new, l, acc

            def kv_block(j, c):
                c0 = pl.multiple_of(j * t, t)
                s = _dot_nt(q, k_ref[pl.ds(c0, t), ql])
                return step(s, *c, v_ref[pl.ds(c0, t), vl])

            init = (jnp.full((t, 1), NEG, F32), jnp.zeros((t, 1), F32), jnp.zeros((t, MLA_V), F32))
            m, l, acc = lax.fori_loop(0, i, kv_block, init)
            s = jnp.where(diag_ok, _dot_nt(q, k_ref[pl.ds(r0, t), ql]), NEG)
            m, l, acc = step(s, m, l, acc, v_ref[pl.ds(r0, t), vl])
            o_ref[pl.ds(r0, t), vl] = (acc / l).astype(o_ref.dtype)
            return carry

        lax.fori_loop(0, seq // t, q_block, 0)


def _mla_attn(q, k, v, batch, seq):
    return pl.pallas_call(
        functools.partial(_mla_attn_kernel, seq),
        grid=(batch, MLA_HEADS // 2),
        in_specs=[pl.BlockSpec((None, seq, 2 * MLA_PAD), lambda b, h: (b, 0, h)),
                  pl.BlockSpec((None, seq, 2 * MLA_PAD), lambda b, h: (b, 0, h)),
                  pl.BlockSpec((None, seq, 2 * MLA_V), lambda b, h: (b, 0, h))],
        out_specs=pl.BlockSpec((None, seq, 2 * MLA_V), lambda b, h: (b, 0, h)),
        out_shape=jax.ShapeDtypeStruct((batch, seq, MLA_HEADS * MLA_V), BF16),
        compiler_params=_cparams(2),
        name="mla_attn",
    )(q, k, v)


def _head_norm(x, gain):
    return _rms(x) * gain


def _norm_kv_into(k, v, gk, kn_ref, vn_ref, g, pad):
    if pad:
        kn_ref[g, :pad, :] = jnp.zeros((pad, HEAD_DIM), BF16)
        vn_ref[g, :pad, :] = jnp.zeros((pad, HEAD_DIM), BF16)
    kn_ref[g, pad:, :] = _head_norm(k, gk).astype(BF16)
    vn_ref[g, pad:, :] = v.astype(BF16)


def _stack_q(qt, g, reps, gq):
    parts = []
    for r in range(reps):
        c0 = (g * reps + r) * HEAD_DIM
        parts.append(_head_norm(qt[:, c0:c0 + HEAD_DIM], gq) * (HEAD_DIM ** -0.5))
    return parts[0] if reps == 1 else jnp.concatenate(parts, axis=0)


def _window_softmax(qs, kwin, vwin, bias, sink=None):
    s = _dot_nt(qs, kwin) + bias
    m = jnp.max(s, axis=-1, keepdims=True)
    if sink is not None:
        m = jnp.maximum(m, sink)
    p = jnp.exp(s - m)
    l = jnp.sum(p, axis=-1, keepdims=True)
    if sink is not None:
        l = l + jnp.exp(sink - m)
    o = _dot(p.astype(BF16), vwin) / l
    return o, m + jnp.log(l)


def _banded_kernel(length, n_kv, reps, n_prev, with_sink, with_lse, *refs):
    q_ref, k_ref, v_ref, gq_ref, gk_ref, bm_ref = refs[:6]
    refs = refs[6:]
    sink_ref = None
    if with_sink:
        sink_ref, refs = refs[0], refs[1:]
    o_ref, refs = refs[0], refs[1:]
    lse_ref = None
    if with_lse:
        lse_ref, refs = refs[0], refs[1:]
    kn_ref, vn_ref = refs
    pad = n_prev * QB
    kw = pad + QB
    gq, gk = gq_ref[...], gk_ref[...]
    for g in range(n_kv):
        sl = slice(g * HEAD_DIM, (g + 1) * HEAD_DIM)
        _norm_kv_into(k_ref[:, sl], v_ref[:, sl], gk, kn_ref, vn_ref, g, pad)
    col = lax.broadcasted_iota(jnp.int32, (1, kw), 1)

    def q_block(i, carry):
        r0 = pl.multiple_of(i * QB, QB)
        qt = q_ref[pl.ds(r0, QB), :]
        in_seq = col >= (n_prev - i) * QB
        for g in range(n_kv):
            qs = _stack_q(qt, g, reps, gq).astype(BF16)
            bias = jnp.where(in_seq, bm_ref[g], NEG)
            sink = sink_ref[g] if with_sink else None
            o, lse = _window_softmax(qs, kn_ref[g, pl.ds(r0, kw), :], vn_ref[g, pl.ds(r0, kw), :], bias, sink)
            for r in range(reps):
                c0 = (g * reps + r) * HEAD_DIM
                o_ref[pl.ds(r0, QB), c0:c0 + HEAD_DIM] = o[r * QB:(r + 1) * QB].astype(o_ref.dtype)
                if with_lse:
                    lse_ref[pl.ds(r0, QB), c0:c0 + HEAD_DIM] = jnp.broadcast_to(
                        lse[r * QB:(r + 1) * QB], (QB, HEAD_DIM))
        return carry

    lax.fori_loop(0, length // QB, q_block, 0)


def _banded_scratch(length, n_kv, n_prev):
    rows = n_prev * QB + length
    return [pltpu.VMEM((n_kv, rows, HEAD_DIM), BF16), pltpu.VMEM((n_kv, rows, HEAD_DIM), BF16)]


def _dilated_group(z_dil, batch, seq, group, dilation, gq, gk, bm):
    cols = z_dil.shape[-1]
    gw = DIL_HPG * HEAD_DIM
    per_res = cols // gw
    sd = seq // dilation
    zv = z_dil.reshape(batch, sd, dilation * cols)
    n_groups = len(DIL_PATTERNS)

    def spec(part):
        return pl.BlockSpec((None, sd, gw), lambda b, r: (b, 0, r * per_res + part * n_groups + group))

    out_spec = pl.BlockSpec((None, sd, gw), lambda b, r: (b, 0, r))
    shape = jax.ShapeDtypeStruct((batch, sd, dilation * gw), F32)
    out, lse = pl.pallas_call(
        functools.partial(_banded_kernel, sd, DIL_HPG, 1, 1, False, True),
        grid=(batch, dilation),
        in_specs=[spec(0), spec(1), spec(2), _resident(gq.shape), _resident(gk.shape), _resident(bm.shape)],
        out_specs=[out_spec, out_spec],
        out_shape=[shape, shape],
        scratch_shapes=_banded_scratch(sd, DIL_HPG, 1),
        compiler_params=_cparams(2),
        name="dilated_g%d" % group,
    )(zv, zv, zv, gq, gk, bm)
    return out.reshape(batch * seq, gw), lse.reshape(batch * seq, gw)


def _swa(z_swa, batch, seq, gq, gk, bm, sinks):
    qw = SWA_HEADS * HEAD_DIM
    kvw = SWA_KV_HEADS * HEAD_DIM
    reps = SWA_HEADS // SWA_KV_HEADS
    n_prev = -(-(SWA_WINDOW - 1) // QB)
    return pl.pallas_call(
        functools.partial(_banded_kernel, seq, SWA_KV_HEADS, reps, n_prev, True, False),
        grid=(batch,),
        in_specs=[pl.BlockSpec((None, seq, qw), lambda b: (b, 0, 0)),
                  pl.BlockSpec((None, seq, kvw), lambda b: (b, 0, qw // kvw)),
                  pl.BlockSpec((None, seq, kvw), lambda b: (b, 0, qw // kvw + 1)),
                  _resident(gq.shape), _resident(gk.shape), _resident(bm.shape), _resident(sinks.shape)],
        out_specs=pl.BlockSpec((None, seq, qw), lambda b: (b, 0, 0)),
        out_shape=jax.ShapeDtypeStruct((batch, seq, qw), BF16),
        scratch_shapes=_banded_scratch(seq, SWA_KV_HEADS, n_prev),
        compiler_params=_cparams(1),
        name="swa",
    )(z_swa, z_swa, z_swa, gq, gk, bm, sinks)


NSA_REPS = NSA_HEADS // NSA_KV_HEADS
NSA_N_SLC = 32
NSA_N_CMP_PAD = 128
NSA_WIN_PREV = -(-(NSA_WINDOW - 1) // QB)
_NQ = NSA_REPS * HEAD_DIM
_OFF_KC, _OFF_VC, _OFF_KS, _OFF_VS, _OFF_KW, _OFF_VW, _OFF_GATE = (_NQ + i * HEAD_DIM for i in range(7))


def _nsa_kernel(seq, z_ref, zc_ref, w1_ref, w2_ref, pos_ref, gq_ref, gk_ref, ovt_ref, expand_ref,
                bslc_ref, bwin_ref, o_ref, ks_ref, vs_ref, kw_ref, vw_ref, km_ref):
    n_chunks = seq // QB
    m_rows = NSA_REPS * QB
    gq = gq_ref[...]
    win_pad = NSA_WIN_PREV * QB
    win_kw = win_pad + QB

    half = NSA_CMP_LEN * HEAD_DIM // 2

    def compress(kv):
        x2 = zc_ref[kv]
        first = jnp.dot(x2, w1_ref[kv, :half, :], precision=_HI, preferred_element_type=F32)
        second = jnp.dot(x2, w1_ref[kv, half:, :], precision=_HI, preferred_element_type=F32)
        pos_rows = jnp.broadcast_to(pos_ref[kv], (8, NSA_CMP_LEN * HEAD_DIM))
        pos_term = jnp.dot(pos_rows, w1_ref[kv], precision=_HI, preferred_element_type=F32)[0:1]
        hidden = first + pltpu.roll(second, NSA_N_CMP_PAD - 1, 0) + pos_term
        return jnp.dot(jax.nn.gelu(hidden), w2_ref[kv], precision=_HI, preferred_element_type=F32)

    kc = _head_norm(compress(0), gk_ref[0])
    vc = compress(1).astype(BF16)

    ks_ref[...] = _head_norm(z_ref[:, _OFF_KS:_OFF_KS + HEAD_DIM], gk_ref[1]).astype(BF16)
    vs_ref[...] = z_ref[:, _OFF_VS:_OFF_VS + HEAD_DIM].astype(BF16)
    kw_ref[:win_pad, :] = jnp.zeros((win_pad, HEAD_DIM), BF16)
    vw_ref[:win_pad, :] = jnp.zeros((win_pad, HEAD_DIM), BF16)
    kw_ref[win_pad:, :] = _head_norm(z_ref[:, _OFF_KW:_OFF_KW + HEAD_DIM], gk_ref[2]).astype(BF16)
    vw_ref[win_pad:, :] = z_ref[:, _OFF_VW:_OFF_VW + HEAD_DIM].astype(BF16)

    cmp_end = lax.broadcasted_iota(jnp.int32, (m_rows, NSA_N_CMP_PAD), 1) * NSA_CMP_STRIDE + (NSA_CMP_LEN - 1)
    cmp_real = lax.broadcasted_iota(jnp.int32, (m_rows, NSA_N_CMP_PAD), 1) < NSA_N_CMP_PAD - 1
    row_in_blk = lax.broadcasted_iota(jnp.int32, (m_rows, NSA_N_CMP_PAD), 0) % QB
    blk_id = lax.broadcasted_iota(jnp.int32, (NSA_N_SLC, QB), 0)
    q_lane = lax.broadcasted_iota(jnp.int32, (NSA_N_SLC, QB), 1)
    win_col = lax.broadcasted_iota(jnp.int32, (1, win_kw), 1)

    def q_block(i, carry):
        r0 = pl.multiple_of(i * QB, QB)
        qf = _stack_q(z_ref[pl.ds(r0, QB), :_NQ], 0, NSA_REPS, gq)
        qs = qf.astype(BF16)

        ok = (cmp_end <= row_in_blk + i * QB) & cmp_real
        sc = jnp.where(ok, _dot_nt(qf, kc, precision=_HI), NEG)
        e = jnp.where(ok, jnp.exp(sc - jnp.max(sc, axis=-1, keepdims=True)), 0.0)
        p = e / jnp.maximum(jnp.sum(e, axis=-1, keepdims=True), TINY)
        o_cmp = _dot(p.astype(BF16), vc)

        p_sum = p[0:QB] + p[QB:2 * QB] + p[2 * QB:3 * QB] + p[3 * QB:4 * QB]
        imp = _dot_nt(ovt_ref[...], p_sum, precision=_HI)
        tb = lax.shift_right_logical(q_lane + i * QB, 6)
        forced = (blk_id == 0) | (blk_id == tb) | (blk_id == tb - 1)
        score = jnp.where(blk_id <= tb, imp + jnp.where(forced, NSA_FORCE, 0.0), -NSA_FORCE)
        rank = jnp.zeros((NSA_N_SLC, QB), F32)
        for other in range(NSA_N_SLC):
            s_o = score[other:other + 1, :]
            beats = (s_o > score) | ((s_o == score) & (blk_id > other))
            rank = rank + jnp.where(beats, 1.0, 0.0)
        sel_t = jnp.where(rank < NSA_TOP_N, 1.0, 0.0)
        sel = jnp.concatenate([sel_t, jnp.zeros((QB - NSA_N_SLC, QB), F32)], axis=0).T
        key_ok = _dot(sel.astype(BF16), expand_ref[...])
        for c in range(n_chunks):
            km_ref[c] = (key_ok[:, c * QB:(c + 1) * QB] - 1.0) * (-NEG)

        def kv_chunk(j, c):
            m, l, acc = c
            c0 = pl.multiple_of(j * QB, QB)
            km = km_ref[j]
            s = _dot_nt(qs, ks_ref[pl.ds(c0, QB), :]) + bslc_ref[i - j] + jnp.concatenate([km] * NSA_REPS, axis=0)
            m_new = jnp.maximum(m, jnp.max(s, axis=-1, keepdims=True))
            pj = jnp.exp(s - m_new)
            alpha = jnp.exp(m - m_new)
            l = alpha * l + jnp.sum(pj, axis=-1, keepdims=True)
            acc = alpha * acc + _dot(pj.astype(BF16), vs_ref[pl.ds(c0, QB), :])
            return m_new, l, acc

        init = (jnp.full((m_rows, 1), NEG, F32), jnp.zeros((m_rows, 1), F32), jnp.zeros((m_rows, HEAD_DIM), F32))
        _, l, acc = lax.fori_loop(0, i + 1, kv_chunk, init)
        o_slc = acc / l

        bias = jnp.where(win_col >= (NSA_WIN_PREV - i) * QB, bwin_ref[...], NEG)
        o_win, _ = _window_softmax(qs, kw_ref[pl.ds(r0, win_kw), :], vw_ref[pl.ds(r0, win_kw), :], bias)

        gate = jax.nn.sigmoid(z_ref[pl.ds(r0, QB), _OFF_GATE:_OFF_GATE + 128])
        for r in range(NSA_REPS):
            rows = slice(r * QB, (r + 1) * QB)
            out = (gate[:, 3 * r:3 * r + 1] * o_cmp[rows] + gate[:, 3 * r + 1:3 * r + 2] * o_slc[rows]
                   + gate[:, 3 * r + 2:3 * r + 3] * o_win[rows])
            o_ref[pl.ds(r0, QB), r * HEAD_DIM:(r + 1) * HEAD_DIM] = out.astype(o_ref.dtype)
        return carry

    lax.fori_loop(0, n_chunks, q_block, 0)


def _nsa(z_nsa, zc, w1, w2, pos, gq, gk, ovt, expand, bslc, bwin, batch, seq):
    gw = NSA_GROUP_COLS
    return pl.pallas_call(
        functools.partial(_nsa_kernel, seq),
        grid=(batch, NSA_KV_HEADS),
        in_specs=[pl.BlockSpec((None, seq, gw), lambda b, g: (b, 0, g)),
                  pl.BlockSpec((None, None) + zc.shape[2:], lambda b, g: (b, g, 0, 0, 0)),
                  _resident(w1.shape), _resident(w2.shape), _resident(pos.shape),
                  _resident(gq.shape), _resident(gk.shape), _resident(ovt.shape), _resident(expand.shape),
                  pl.BlockSpec((None,) + bslc.shape[1:], lambda b, g: (g, 0, 0, 0)),
                  pl.BlockSpec((None,) + bwin.shape[1:], lambda b, g: (g, 0, 0))],
        out_specs=pl.BlockSpec((None, seq, _NQ), lambda b, g: (b, 0, g)),
        out_shape=jax.ShapeDtypeStruct((batch, seq, NSA_HEADS * HEAD_DIM), BF16),
        scratch_shapes=[pltpu.VMEM((seq, HEAD_DIM), BF16), pltpu.VMEM((seq, HEAD_DIM), BF16),
                        pltpu.VMEM((NSA_WIN_PREV * QB + seq, HEAD_DIM), BF16),
                        pltpu.VMEM((NSA_WIN_PREV * QB + seq, HEAD_DIM), BF16),
                        pltpu.VMEM((seq // QB, QB, QB), F32)],
        compiler_params=_cparams(2),
        name="nsa",
    )(z_nsa, zc, w1, w2, pos, gq, gk, ovt, expand, bslc, bwin)


def _t5_bucket(dist):
    n = jnp.maximum(dist, 0)
    max_exact = NUM_BUCKETS // 2
    nf = jnp.maximum(n, 1).astype(F32)
    large = max_exact + (jnp.log(nf / max_exact) / math.log(MAX_DISTANCE / max_exact)
                         * (NUM_BUCKETS - max_exact)).astype(jnp.int32)
    large = jnp.minimum(large, NUM_BUCKETS - 1)
    return jnp.where(n < max_exact, n, large)


def _band_bias(bias_cols, window, n_prev, dist_scale, n_kv, reps):
    kw = (n_prev + 1) * QB
    delta = np.arange(QB)[:, None] + n_prev * QB - np.arange(kw)[None, :]
    band = jnp.asarray((delta >= 0) & (delta < window))
    bias = bias_cols.astype(F32)[_t5_bucket(jnp.asarray(delta * dist_scale))]
    tile = jnp.where(band[None], jnp.transpose(bias, (2, 0, 1)), NEG)
    return tile.reshape(n_kv, reps * QB, kw)


def _slc_bias(bias_cols, seq):
    n_chunks = seq // QB
    dist = (np.arange(n_chunks)[:, None, None] * QB + np.arange(QB)[None, :, None]
            - np.arange(QB)[None, None, :])
    bias = bias_cols.astype(F32)[_t5_bucket(jnp.asarray(dist))]
    tile = jnp.where(jnp.asarray(dist >= 0)[None], jnp.transpose(bias, (3, 0, 1, 2)), NEG)
    tile = tile.reshape(NSA_KV_HEADS, NSA_REPS, n_chunks, QB, QB)
    return jnp.transpose(tile, (0, 2, 1, 3, 4)).reshape(NSA_KV_HEADS, n_chunks, NSA_REPS * QB, QB)


def _rope_tables(seq):
    half = MLA_ROPE // 2
    inv = jnp.power(ROPE_THETA, -jnp.arange(half, dtype=F32) / half)
    ang = jnp.arange(seq, dtype=F32)[:, None] * inv[None, :]
    cos, sin = jnp.cos(ang), jnp.sin(ang)
    zeros = lambda w: jnp.zeros((seq, w), F32)
    tail = MLA_PAD - MLA_QK_DIM
    rope_c = jnp.concatenate([jnp.ones((seq, MLA_NOPE), F32), cos, cos, zeros(tail)], axis=1)
    rope_m = jnp.concatenate([zeros(MLA_NOPE), -sin, zeros(half), zeros(tail)], axis=1)
    rope_p = jnp.concatenate([zeros(MLA_NOPE), zeros(half), sin, zeros(tail)], axis=1)
    return rope_c, rope_m, rope_p


def _pad_cols(w, width):
    return jnp.pad(w, ((0, 0), (0, width - w.shape[1])))


def _nsa_column_order():
    g_cols = NSA_KV_HEADS * HEAD_DIM
    q_cols = NSA_HEADS * HEAD_DIM
    order = []
    for g in range(NSA_KV_HEADS):
        cols = list(range(g * _NQ, (g + 1) * _NQ))
        for part in range(6):
            start = q_cols + part * g_cols + g * HEAD_DIM
            cols += list(range(start, start + HEAD_DIM))
        gate0 = q_cols + 6 * g_cols + g * NSA_REPS * 3
        cols += list(range(gate0, gate0 + NSA_REPS * 3))
        order.append(cols)
    return order


def _nsa_mixer(z_nsa3, rel_bias, q_norm, k_norm, cmp_pos, cmp_w1, cmp_w2, batch, seq):
    zc = z_nsa3.reshape(batch, seq // NSA_CMP_STRIDE, NSA_CMP_STRIDE, NSA_KV_HEADS, NSA_GROUP_COLS)
    zc = zc[..., _OFF_KC:_OFF_KC + 2 * HEAD_DIM].reshape(
        batch, seq // NSA_CMP_STRIDE, NSA_CMP_STRIDE, NSA_KV_HEADS, 2, HEAD_DIM)
    zc = jnp.transpose(zc, (0, 3, 4, 1, 2, 5)).reshape(
        batch, NSA_KV_HEADS, 2, seq // NSA_CMP_STRIDE, NSA_CMP_STRIDE * HEAD_DIM)
    n_cmp = (seq - NSA_CMP_LEN) // NSA_CMP_STRIDE + 1
    ci = np.arange(NSA_N_CMP_PAD)[:, None] * NSA_CMP_STRIDE
    sj = np.arange(NSA_N_SLC)[None, :] * NSA_SLC_BLOCK
    overlap = ((ci < sj + NSA_SLC_BLOCK) & (ci + NSA_CMP_LEN > sj) & (np.arange(NSA_N_CMP_PAD)[:, None] < n_cmp))
    ovt = jnp.asarray(overlap.T.astype(np.float32))
    expand = jnp.asarray((np.arange(QB)[:, None] == np.arange(seq)[None, :] // NSA_SLC_BLOCK)
                         .astype(np.float32)).astype(BF16)
    nsa_cols = rel_bias[:, NSA_BIAS_COL0:NSA_BIAS_COL0 + NSA_HEADS]
    bslc = _slc_bias(nsa_cols, seq)
    bwin = _band_bias(nsa_cols, NSA_WINDOW, NSA_WIN_PREV, 1, NSA_KV_HEADS, NSA_REPS)
    return _nsa(z_nsa3, zc, cmp_w1, cmp_w2, cmp_pos.reshape(2, 1, NSA_CMP_LEN * HEAD_DIM), q_norm.reshape(1, HEAD_DIM),
                k_norm.reshape(3, 1, HEAD_DIM), ovt, expand, bslc, bwin, batch, seq)


def kernel(x, rel_bias, ffn1_norm, ffn1_w_gate, ffn1_w_up, ffn1_w_down, mix_norm, ffn2_norm, ffn2_w_gate,
           ffn2_w_up, ffn2_w_down, ab_w_in, mla_q_a_norm, mla_w_q_b, mla_kv_a_norm, mla_w_kv_b, mla_q_norm,
           mla_k_norm, dil_q_norm, dil_k_norm, ab_w_out, cd_w_in, swa_q_norm, swa_k_norm, swa_sinks,
           nsa_q_norm, nsa_k_norm, nsa_cmp_pos, nsa_cmp_w1, nsa_cmp_w2, cd_w_out):
    batch, seq, _ = x.shape
    n = batch * seq
    assert seq % (16 * QB) == 0 and n % TM == 0 and seq % TM == 0
    bf = lambda a: a.astype(BF16)
    xf = x.reshape(n, D_MODEL)

    xf = _ffn(xf, ffn1_norm[0], bf(ffn1_w_gate[0]), bf(ffn1_w_up[0]), bf(ffn1_w_down[0]))

    w_in = ab_w_in[0]
    mla_cols = MLA_Q_RANK + MLA_KV_RANK
    w_krope = jnp.pad(w_in[:, mla_cols:mla_cols + MLA_ROPE], ((0, 0), (MLA_NOPE, MLA_PAD - MLA_QK_DIM)))
    w_mla = jnp.concatenate([w_in[:, :mla_cols], w_krope], axis=1)
    z_mla, z_dil = _norm_proj(xf, mix_norm[0], [bf(w_mla), bf(w_in[:, mla_cols + MLA_ROPE:])])

    wq = _pad_cols(mla_w_q_b[0].reshape(MLA_Q_RANK * MLA_HEADS, MLA_QK_DIM), MLA_PAD)
    wq = wq.reshape(MLA_Q_RANK, MLA_HEADS * MLA_PAD)
    wkv = mla_w_kv_b[0].reshape(MLA_KV_RANK, MLA_HEADS, MLA_NOPE + MLA_V)
    wk = _pad_cols(wkv[:, :, :MLA_NOPE].reshape(MLA_KV_RANK * MLA_HEADS, MLA_NOPE), MLA_PAD)
    wk = wk.reshape(MLA_KV_RANK, MLA_HEADS * MLA_PAD)
    wv = wkv[:, :, MLA_NOPE:].reshape(MLA_KV_RANK, MLA_HEADS * MLA_V)
    rope_c, rope_m, rope_p = _rope_tables(seq)
    q_mla, k_mla, v_mla = _mla_prep(
        z_mla, mla_q_a_norm[0].reshape(1, -1), mla_kv_a_norm[0].reshape(1, -1), bf(wq), bf(wk), bf(wv),
        _pad_cols(mla_q_norm[0].reshape(1, -1), MLA_PAD), _pad_cols(mla_k_norm[0].reshape(1, -1), MLA_PAD),
        rope_c, rope_m, rope_p, seq)
    o_a = _mla_attn(q_mla.reshape(batch, seq, -1), k_mla.reshape(batch, seq, -1),
                    v_mla.reshape(batch, seq, -1), batch, seq).reshape(n, MLA_HEADS * MLA_V)

    z_dil3 = z_dil.reshape(batch, seq, -1)
    gq, gk = dil_q_norm[0].reshape(1, HEAD_DIM), dil_k_norm[0].reshape(1, HEAD_DIM)
    dil_outs, dil_lses = [], []
    for grp, (window, dilation) in enumerate(DIL_PATTERNS):
        bm = _band_bias(rel_bias[:, grp * DIL_HPG:(grp + 1) * DIL_HPG], window // dilation + 1, 1, dilation,
                        DIL_HPG, 1)
        o, lse = _dilated_group(z_dil3, batch, seq, grp, dilation, gq, gk, bm)
        dil_outs.append(o)
        dil_lses.append(lse)
    w_out = ab_w_out[0]
    xf = _out_ab(xf, o_a, dil_outs, dil_lses, bf(w_out[:MLA_HEADS * MLA_V]), bf(w_out[MLA_HEADS * MLA_V:]))
    xf = _ffn(xf, ffn2_norm[0], bf(ffn2_w_gate[0]), bf(ffn2_w_up[0]), bf(ffn2_w_down[0]))

    xf = _ffn(xf, ffn1_norm[1], bf(ffn1_w_gate[1]), bf(ffn1_w_up[1]), bf(ffn1_w_down[1]))
    w_in = cd_w_in[0]
    swa_cols = (SWA_HEADS + 2 * SWA_KV_HEADS) * HEAD_DIM
    w_nsa_src = w_in[:, swa_cols:]
    w_nsa = jnp.concatenate([_pad_cols(w_nsa_src[:, np.asarray(cols)], NSA_GROUP_COLS)
                             for cols in _nsa_column_order()], axis=1)
    z_swa, z_nsa = _norm_proj(xf, mix_norm[1], [bf(w_in[:, :swa_cols]), bf(w_nsa)])

    swa_reps = SWA_HEADS // SWA_KV_HEADS
    swa_prev = -(-(SWA_WINDOW - 1) // QB)
    bm_swa = _band_bias(rel_bias[:, :SWA_HEADS], SWA_WINDOW, swa_prev, 1, SWA_KV_HEADS, swa_reps)
    sinks = jnp.broadcast_to(swa_sinks[0].astype(F32).reshape(SWA_KV_HEADS, swa_reps, 1, 1),
                             (SWA_KV_HEADS, swa_reps, QB, 1)).reshape(SWA_KV_HEADS, swa_reps * QB, 1)
    o_c = _swa(z_swa.reshape(batch, seq, -1), batch, seq, swa_q_norm[0].reshape(1, HEAD_DIM),
               swa_k_norm[0].reshape(1, HEAD_DIM), bm_swa, sinks)

    o_d = _nsa_mixer(z_nsa.reshape(batch, seq, NSA_KV_HEADS * NSA_GROUP_COLS), rel_bias, nsa_q_norm[0],
                     nsa_k_norm[0], nsa_cmp_pos[0], nsa_cmp_w1[0], nsa_cmp_w2[0], batch, seq)

    w_out = cd_w_out[0]
    xf = _out_cd(xf, o_c.reshape(n, -1), o_d.reshape(n, -1), bf(w_out[:SWA_HEADS * HEAD_DIM]),
                 bf(w_out[SWA_HEADS * HEAD_DIM:]))
    xf = _ffn(xf, ffn2_norm[1], bf(ffn2_w_gate[1]), bf(ffn2_w_up[1]), bf(ffn2_w_down[1]))
    return xf.reshape(batch, seq, D_MODEL)
```

```python
import functools
import math

import numpy as np
import jax
import jax.numpy as jnp
from jax import lax
from jax.experimental import pallas as pl
from jax.experimental.pallas import tpu as pltpu

F32 = jnp.float32
BF16 = jnp.bfloat16

EPS = 1e-6
NEG = -1e30
TINY = 1e-30
D_MODEL = 1024
D_FF = 2816
NUM_BUCKETS = 32
MAX_DISTANCE = 2048
HEAD_DIM = 64
QB = 128

MLA_HEADS = 8
MLA_Q_RANK = 256
MLA_KV_RANK = 128
MLA_NOPE = 64
MLA_ROPE = 32
MLA_V = 64
MLA_QK_DIM = MLA_NOPE + MLA_ROPE
MLA_PAD = 128
ROPE_THETA = 10000.0

DIL_PATTERNS = ((128, 1), (512, 4), (2048, 16))
DIL_HPG = 4
SWA_HEADS = 8
SWA_KV_HEADS = 2
SWA_WINDOW = 128
NSA_HEADS = 8
NSA_KV_HEADS = 2
NSA_CMP_LEN = 32
NSA_CMP_STRIDE = 16
NSA_CMP_HIDDEN = 128
NSA_SLC_BLOCK = 64
NSA_TOP_N = 16
NSA_WINDOW = 512
NSA_FORCE = 1e6
NSA_BIAS_COL0 = 8
NSA_GROUP_COLS = 768

VMEM_LIMIT = 56 * 1024 * 1024
TM = 512

_NT = (((1,), (1,)), ((), ()))
_HI = lax.Precision.HIGHEST


def _cparams(n_axes):
    return pltpu.CompilerParams(dimension_semantics=("arbitrary",) * n_axes,
                                vmem_limit_bytes=VMEM_LIMIT)


def _resident(shape):
    nd = len(shape)
    return pl.BlockSpec(shape, lambda *_: (0,) * nd, pipeline_mode=pl.Buffered(1))


def _rms(x):
    return x * lax.rsqrt(jnp.mean(x * x, axis=-1, keepdims=True) + EPS)


def _dot(a, b):
    return jnp.dot(a, b, preferred_element_type=F32)


def _dot_nt(a, b, precision=None):
    return lax.dot_general(a, b, _NT, preferred_element_type=F32, precision=precision)


FF_CHUNK = 256


def _ffn_kernel(x_ref, g_ref, wg_ref, wu_ref, wd_ref, o_ref):
    x = x_ref[...]
    hb = (_rms(x) * g_ref[...]).astype(BF16)
    acc = jnp.zeros(x.shape, F32)
    for c in range(D_FF // FF_CHUNK):
        sl = slice(c * FF_CHUNK, (c + 1) * FF_CHUNK)
        gate = _dot(hb, wg_ref[:, sl])
        up = _dot(hb, wu_ref[:, sl])
        act = (gate * jax.nn.sigmoid(gate) * up).astype(BF16)
        acc = acc + _dot(act, wd_ref[sl, :])
    o_ref[...] = x + 0.5 * acc


def _ffn(x, g, wg, wu, wd):
    n = x.shape[0]
    return pl.pallas_call(
        _ffn_kernel,
        grid=(n // TM,),
        in_specs=[pl.BlockSpec((TM, D_MODEL), lambda i: (i, 0)),
                  _resident((1, D_MODEL)),
                  _resident((D_MODEL, D_FF)), _resident((D_MODEL, D_FF)), _resident((D_FF, D_MODEL))],
        out_specs=pl.BlockSpec((TM, D_MODEL), lambda i: (i, 0)),
        out_shape=jax.ShapeDtypeStruct((n, D_MODEL), F32),
        compiler_params=_cparams(1),
        name="ffn",
    )(x, g.reshape(1, D_MODEL), wg, wu, wd)


N_CHUNK = 512


def _norm_proj_kernel(n_out, x_ref, g_ref, *refs):
    w_refs, o_refs = refs[:n_out], refs[n_out:]
    hb = (_rms(x_ref[...]) * g_ref[...]).astype(BF16)
    for w_ref, o_ref in zip(w_refs, o_refs):
        width = w_ref.shape[1]
        for c0 in range(0, width, N_CHUNK):
            sl = slice(c0, min(c0 + N_CHUNK, width))
            o_ref[:, sl] = _dot(hb, w_ref[:, sl])


def _norm_proj(x, g, ws):
    n = x.shape[0]
    return pl.pallas_call(
        functools.partial(_norm_proj_kernel, len(ws)),
        grid=(n // TM,),
        in_specs=[pl.BlockSpec((TM, D_MODEL), lambda i: (i, 0)), _resident((1, D_MODEL))]
        + [_resident(w.shape) for w in ws],
        out_specs=[pl.BlockSpec((TM, w.shape[1]), lambda i: (i, 0)) for w in ws],
        out_shape=[jax.ShapeDtypeStruct((n, w.shape[1]), F32) for w in ws],
        compiler_params=_cparams(1),
        name="norm_proj",
    )(x, g.reshape(1, D_MODEL), *ws)


def _out_ab_kernel(x_ref, oa_ref, o0_ref, o1_ref, o2_ref, l0_ref, l1_ref, l2_ref, wa_ref, wb_ref, out_ref):
    l0, l1, l2 = l0_ref[...], l1_ref[...], l2_ref[...]
    m = jnp.maximum(jnp.maximum(l0, l1), l2)
    e0, e1, e2 = jnp.exp(l0 - m), jnp.exp(l1 - m), jnp.exp(l2 - m)
    ob = (e0 * o0_ref[...] + e1 * o1_ref[...] + e2 * o2_ref[...]) / (e0 + e1 + e2)
    out_ref[...] = x_ref[...] + _dot(oa_ref[...], wa_ref[...]) + _dot(ob.astype(BF16), wb_ref[...])


def _out_ab(x, oa, dil_outs, dil_lses, wa, wb):
    n = x.shape[0]
    dw = DIL_HPG * HEAD_DIM
    tile = lambda w: pl.BlockSpec((TM, w), lambda i: (i, 0))
    return pl.pallas_call(
        _out_ab_kernel,
        grid=(n // TM,),
        in_specs=[tile(D_MODEL), tile(oa.shape[1])] + [tile(dw)] * 6 + [_resident(wa.shape), _resident(wb.shape)],
        out_specs=tile(D_MODEL),
        out_shape=jax.ShapeDtypeStruct((n, D_MODEL), F32),
        compiler_params=_cparams(1),
        name="out_ab",
    )(x, oa, *dil_outs, *dil_lses, wa, wb)


def _out_cd_kernel(x_ref, oc_ref, od_ref, wc_ref, wd_ref, out_ref):
    out_ref[...] = x_ref[...] + _dot(oc_ref[...], wc_ref[...]) + _dot(od_ref[...], wd_ref[...])


def _out_cd(x, oc, od, wc, wd):
    n = x.shape[0]
    tile = lambda w: pl.BlockSpec((TM, w), lambda i: (i, 0))
    return pl.pallas_call(
        _out_cd_kernel,
        grid=(n // TM,),
        in_specs=[tile(D_MODEL), tile(oc.shape[1]), tile(od.shape[1]), _resident(wc.shape), _resident(wd.shape)],
        out_specs=tile(D_MODEL),
        out_shape=jax.ShapeDtypeStruct((n, D_MODEL), F32),
        compiler_params=_cparams(1),
        name="out_cd",
    )(x, oc, od, wc, wd)


def _mla_prep_kernel(z_ref, qan_ref, kvan_ref, wq_ref, wk_ref, wv_ref, gq_ref, gk_ref,
                     rc_ref, rm_ref, rp_ref, q_ref, k_ref, v_ref):
    cq = (_rms(z_ref[:, :MLA_Q_RANK]) * qan_ref[...]).astype(BF16)
    ckv = (_rms(z_ref[:, MLA_Q_RANK:MLA_Q_RANK + MLA_KV_RANK]) * kvan_ref[...]).astype(BF16)
    kr = z_ref[:, MLA_Q_RANK + MLA_KV_RANK:]
    rc, rm, rp = rc_ref[...], rm_ref[...], rp_ref[...]
    scale = MLA_QK_DIM ** -0.5

    def head_norm_rope(x, gain):
        x = x * lax.rsqrt(jnp.sum(x * x, axis=-1, keepdims=True) * (1.0 / MLA_QK_DIM) + EPS) * gain
        half = MLA_ROPE // 2
        return x * rc + pltpu.roll(x, MLA_PAD - half, 1) * rm + pltpu.roll(x, half, 1) * rp

    for h in range(MLA_HEADS):
        sl = slice(h * MLA_PAD, (h + 1) * MLA_PAD)
        qh = head_norm_rope(_dot(cq, wq_ref[:, sl]), gq_ref[...])
        q_ref[:, sl] = (qh * scale).astype(BF16)
        kh = head_norm_rope(_dot(ckv, wk_ref[:, sl]) + kr, gk_ref[...])
        k_ref[:, sl] = kh.astype(BF16)
    v_ref[...] = _dot(ckv, wv_ref[...]).astype(BF16)


def _mla_prep(z, qan, kvan, wq, wk, wv, gq, gk, rope_c, rope_m, rope_p, seq):
    n = z.shape[0]
    per_seq = seq // TM
    tile = lambda w: pl.BlockSpec((TM, w), lambda i: (i, 0))
    rope_spec = pl.BlockSpec((TM, MLA_PAD), lambda i: (i % per_seq, 0))
    hw = MLA_HEADS * MLA_PAD
    return pl.pallas_call(
        _mla_prep_kernel,
        grid=(n // TM,),
        in_specs=[tile(z.shape[1]), _resident(qan.shape), _resident(kvan.shape), _resident(wq.shape),
                  _resident(wk.shape), _resident(wv.shape), _resident(gq.shape), _resident(gk.shape),
                  rope_spec, rope_spec, rope_spec],
        out_specs=[tile(hw), tile(hw), tile(MLA_HEADS * MLA_V)],
        out_shape=[jax.ShapeDtypeStruct((n, hw), BF16), jax.ShapeDtypeStruct((n, hw), BF16),
                   jax.ShapeDtypeStruct((n, MLA_HEADS * MLA_V), BF16)],
        compiler_params=_cparams(1),
        name="mla_prep",
    )(z, qan, kvan, wq, wk, wv, gq, gk, rope_c, rope_m, rope_p)


MLA_TQ = 256


def _mla_attn_kernel(seq, q_ref, k_ref, v_ref, o_ref):
    t = MLA_TQ
    row = lax.broadcasted_iota(jnp.int32, (t, t), 0)
    col = lax.broadcasted_iota(jnp.int32, (t, t), 1)
    diag_ok = col <= row
    for hh in range(2):
        ql = slice(hh * MLA_PAD, (hh + 1) * MLA_PAD)
        vl = slice(hh * MLA_V, (hh + 1) * MLA_V)

        def q_block(i, carry):
            r0 = pl.multiple_of(i * t, t)
            q = q_ref[pl.ds(r0, t), ql]

            def step(s, m, l, acc, v):
                m_new = jnp.maximum(m, jnp.max(s, axis=-1, keepdims=True))
                p = jnp.exp(s - m_new)
                alpha = jnp.exp(m - m_new)
                l = alpha * l + jnp.sum(p, axis=-1, keepdims=True)
                acc = alpha * acc + _dot(p.astype(BF16), v)
                return m_new, l, acc

            def kv_block(j, c):
                c0 = pl.multiple_of(j * t, t)
                s = _dot_nt(q, k_ref[pl.ds(c0, t), ql])
                return step(s, *c, v_ref[pl.ds(c0, t), vl])

            init = (jnp.full((t, 1), NEG, F32), jnp.zeros((t, 1), F32), jnp.zeros((t, MLA_V), F32))
            m, l, acc = lax.fori_loop(0, i, kv_block, init)
            s = jnp.where(diag_ok, _dot_nt(q, k_ref[pl.ds(r0, t), ql]), NEG)
            m, l, acc = step(s, m, l, acc, v_ref[pl.ds(r0, t), vl])
            o_ref[pl.ds(r0, t), vl] = (acc / l).astype(o_ref.dtype)
            return carry

        lax.fori_loop(0, seq // t, q_block, 0)


def _mla_attn(q, k, v, batch, seq):
    return pl.pallas_call(
        functools.partial(_mla_attn_kernel, seq),
        grid=(batch, MLA_HEADS // 2),
        in_specs=[pl.BlockSpec((None, seq, 2 * MLA_PAD), lambda b, h: (b, 0, h)),
                  pl.BlockSpec((None, seq, 2 * MLA_PAD), lambda b, h: (b, 0, h)),
                  pl.BlockSpec((None, seq, 2 * MLA_V), lambda b, h: (b, 0, h))],
        out_specs=pl.BlockSpec((None, seq, 2 * MLA_V), lambda b, h: (b, 0, h)),
        out_shape=jax.ShapeDtypeStruct((batch, seq, MLA_HEADS * MLA_V), BF16),
        compiler_params=_cparams(2),
        name="mla_attn",
    )(q, k, v)


def _head_norm(x, gain):
    return _rms(x) * gain


def _norm_kv_into(k, v, gk, kn_ref, vn_ref, g, pad):
    if pad:
        kn_ref[g, :pad, :] = jnp.zeros((pad, HEAD_DIM), BF16)
        vn_ref[g, :pad, :] = jnp.zeros((pad, HEAD_DIM), BF16)
    kn_ref[g, pad:, :] = _head_norm(k, gk).astype(BF16)
    vn_ref[g, pad:, :] = v.astype(BF16)


def _stack_q(qt, g, reps, gq):
    parts = []
    for r in range(reps):
        c0 = (g * reps + r) * HEAD_DIM
        parts.append(_head_norm(qt[:, c0:c0 + HEAD_DIM], gq) * (HEAD_DIM ** -0.5))
    return parts[0] if reps == 1 else jnp.concatenate(parts, axis=0)


def _window_softmax(qs, kwin, vwin, bias, sink=None):
    s = _dot_nt(qs, kwin) + bias
    m = jnp.max(s, axis=-1, keepdims=True)
    if sink is not None:
        m = jnp.maximum(m, sink)
    p = jnp.exp(s - m)
    l = jnp.sum(p, axis=-1, keepdims=True)
    if sink is not None:
        l = l + jnp.exp(sink - m)
    o = _dot(p.astype(BF16), vwin) / l
    return o, m + jnp.log(l)


def _banded_kernel(seq, dilation, n_kv, reps, n_prev, with_sink, with_lse, *refs):
    q_ref, k_ref, v_ref, gq_ref, gk_ref, bm_ref = refs[:6]
    refs = refs[6:]
    sink_ref = None
    if with_sink:
        sink_ref, refs = refs[0], refs[1:]
    o_ref, refs = refs[0], refs[1:]
    lse_ref = None
    if with_lse:
        lse_ref, refs = refs[0], refs[1:]
    kn_ref, vn_ref, ob_ref, lb_ref = refs
    length = seq // dilation
    pad = n_prev * QB
    kw = pad + QB
    gq, gk = gq_ref[...], gk_ref[...]
    col = lax.broadcasted_iota(jnp.int32, (1, kw), 1)

    def rows(start, size):
        return pl.ds(start, size) if dilation == 1 else pl.ds(start, size, stride=dilation)

    def residue(res, carry):
        kall, vall = k_ref[rows(res, length), :], v_ref[rows(res, length), :]
        for g in range(n_kv):
            sl = slice(g * HEAD_DIM, (g + 1) * HEAD_DIM)
            _norm_kv_into(kall[:, sl], vall[:, sl], gk, kn_ref, vn_ref, g, pad)

        def q_block(i, c):
            r0 = pl.multiple_of(i * QB, QB)
            tok = rows(res + r0 * dilation, QB)
            qt = q_ref[tok, :]
            in_seq = col >= (n_prev - i) * QB
            for g in range(n_kv):
                qs = _stack_q(qt, g, reps, gq).astype(BF16)
                bias = jnp.where(in_seq, bm_ref[g], NEG)
                sink = sink_ref[g] if with_sink else None
                o, lse = _window_softmax(qs, kn_ref[g, pl.ds(r0, kw), :], vn_ref[g, pl.ds(r0, kw), :], bias, sink)
                for r in range(reps):
                    c0 = (g * reps + r) * HEAD_DIM
                    ob_ref[:, c0:c0 + HEAD_DIM] = o[r * QB:(r + 1) * QB]
                    if with_lse:
                        lb_ref[:, c0:c0 + HEAD_DIM] = jnp.broadcast_to(lse[r * QB:(r + 1) * QB], (QB, HEAD_DIM))
            o_ref[tok, :] = ob_ref[...].astype(o_ref.dtype)
            if with_lse:
                lse_ref[tok, :] = lb_ref[...]
            return c

        lax.fori_loop(0, length // QB, q_block, 0)
        return carry

    if dilation == 1:
        residue(0, 0)
    else:
        lax.fori_loop(0, dilation, residue, 0)


def _banded_scratch(length, n_kv, reps, n_prev):
    rows = n_prev * QB + length
    width = n_kv * reps * HEAD_DIM
    return [pltpu.VMEM((n_kv, rows, HEAD_DIM), BF16), pltpu.VMEM((n_kv, rows, HEAD_DIM), BF16),
            pltpu.VMEM((QB, width), F32), pltpu.VMEM((QB, width), F32)]


def _dilated_group(z_dil, batch, seq, group, dilation, gq, gk, bm):
    gw = DIL_HPG * HEAD_DIM
    n_groups = len(DIL_PATTERNS)
    hps = 2
    halves = DIL_HPG // hps

    def spec(part):
        return pl.BlockSpec((None, seq, hps * HEAD_DIM), lambda b, h: (b, 0, (part * n_groups + group) * halves + h))

    out_spec = pl.BlockSpec((None, seq, hps * HEAD_DIM), lambda b, h: (b, 0, h))
    shape = jax.ShapeDtypeStruct((batch, seq, gw), F32)
    out, lse = pl.pallas_call(
        functools.partial(_banded_kernel, seq, dilation, hps, 1, 1, False, True),
        grid=(batch, halves),
        in_specs=[spec(0), spec(1), spec(2), _resident(gq.shape), _resident(gk.shape),
                  pl.BlockSpec((hps,) + bm.shape[1:], lambda b, h: (h, 0, 0))],
        out_specs=[out_spec, out_spec],
        out_shape=[shape, shape],
        scratch_shapes=_banded_scratch(seq // dilation, hps, 1, 1),
        compiler_params=_cparams(2),
        name="dilated_g%d" % group,
    )(z_dil, z_dil, z_dil, gq, gk, bm)
    return out.reshape(batch * seq, gw), lse.reshape(batch * seq, gw)


def _swa(z_swa, batch, seq, gq, gk, bm, sinks):
    qw = SWA_HEADS * HEAD_DIM
    kvw = SWA_KV_HEADS * HEAD_DIM
    reps = SWA_HEADS // SWA_KV_HEADS
    n_prev = -(-(SWA_WINDOW - 1) // QB)
    return pl.pallas_call(
        functools.partial(_banded_kernel, seq, 1, SWA_KV_HEADS, reps, n_prev, True, False),
        grid=(batch,),
        in_specs=[pl.BlockSpec((None, seq, qw), lambda b: (b, 0, 0)),
                  pl.BlockSpec((None, seq, kvw), lambda b: (b, 0, qw // kvw)),
                  pl.BlockSpec((None, seq, kvw), lambda b: (b, 0, qw // kvw + 1)),
                  _resident(gq.shape), _resident(gk.shape), _resident(bm.shape), _resident(sinks.shape)],
        out_specs=pl.BlockSpec((None, seq, qw), lambda b: (b, 0, 0)),
        out_shape=jax.ShapeDtypeStruct((batch, seq, qw), BF16),
        scratch_shapes=_banded_scratch(seq, SWA_KV_HEADS, reps, n_prev),
        compiler_params=_cparams(1),
        name="swa",
    )(z_swa, z_swa, z_swa, gq, gk, bm, sinks)


NSA_REPS = NSA_HEADS // NSA_KV_HEADS
NSA_N_SLC = 32
NSA_N_CMP_PAD = 128
NSA_WIN_PREV = -(-(NSA_WINDOW - 1) // QB)
_NQ = NSA_REPS * HEAD_DIM
_OFF_KC, _OFF_VC, _OFF_KS, _OFF_VS, _OFF_KW, _OFF_VW, _OFF_GATE = (_NQ + i * HEAD_DIM for i in range(7))


def _nsa_kernel(seq, z_ref, zc_ref, w1_ref, w2_ref, pos_ref, gq_ref, gk_ref, ovt_ref, expand_ref,
                bslc_ref, bwin_ref, o_ref, ks_ref, vs_ref, kw_ref, vw_ref, km_ref):
    n_chunks = seq // QB
    m_rows = NSA_REPS * QB
    gq = gq_ref[...]
    win_pad = NSA_WIN_PREV * QB
    win_kw = win_pad + QB

    half = NSA_CMP_LEN * HEAD_DIM // 2

    def compress(kv):
        x2 = zc_ref[kv]
        first = jnp.dot(x2, w1_ref[kv, :half, :], precision=_HI, preferred_element_type=F32)
        second = jnp.dot(x2, w1_ref[kv, half:, :], precision=_HI, preferred_element_type=F32)
        pos_rows = jnp.broadcast_to(pos_ref[kv], (8, NSA_CMP_LEN * HEAD_DIM))
        pos_term = jnp.dot(pos_rows, w1_ref[kv], precision=_HI, preferred_element_type=F32)[0:1]
        hidden = first + pltpu.roll(second, NSA_N_CMP_PAD - 1, 0) + pos_term
        return jnp.dot(jax.nn.gelu(hidden), w2_ref[kv], precision=_HI, preferred_element_type=F32)

    kc = _head_norm(compress(0), gk_ref[0])
    vc = compress(1).astype(BF16)

    ks_ref[...] = _head_norm(z_ref[:, _OFF_KS:_OFF_KS + HEAD_DIM], gk_ref[1]).astype(BF16)
    vs_ref[...] = z_ref[:, _OFF_VS:_OFF_VS + HEAD_DIM].astype(BF16)
    kw_ref[:win_pad, :] = jnp.zeros((win_pad, HEAD_DIM), BF16)
    vw_ref[:win_pad, :] = jnp.zeros((win_pad, HEAD_DIM), BF16)
    kw_ref[win_pad:, :] = _head_norm(z_ref[:, _OFF_KW:_OFF_KW + HEAD_DIM], gk_ref[2]).astype(BF16)
    vw_ref[win_pad:, :] = z_ref[:, _OFF_VW:_OFF_VW + HEAD_DIM].astype(BF16)

    cmp_end = lax.broadcasted_iota(jnp.int32, (m_rows, NSA_N_CMP_PAD), 1) * NSA_CMP_STRIDE + (NSA_CMP_LEN - 1)
    cmp_real = lax.broadcasted_iota(jnp.int32, (m_rows, NSA_N_CMP_PAD), 1) < NSA_N_CMP_PAD - 1
    row_in_blk = lax.broadcasted_iota(jnp.int32, (m_rows, NSA_N_CMP_PAD), 0) % QB
    blk_id = lax.broadcasted_iota(jnp.int32, (NSA_N_SLC, QB), 0)
    q_lane = lax.broadcasted_iota(jnp.int32, (NSA_N_SLC, QB), 1)
    win_col = lax.broadcasted_iota(jnp.int32, (1, win_kw), 1)

    def q_block(i, carry):
        r0 = pl.multiple_of(i * QB, QB)
        qf = _stack_q(z_ref[pl.ds(r0, QB), :_NQ], 0, NSA_REPS, gq)
        qs = qf.astype(BF16)

        ok = (cmp_end <= row_in_blk + i * QB) & cmp_real
        sc = jnp.where(ok, _dot_nt(qf, kc, precision=_HI), NEG)
        e = jnp.where(ok, jnp.exp(sc - jnp.max(sc, axis=-1, keepdims=True)), 0.0)
        p = e / jnp.maximum(jnp.sum(e, axis=-1, keepdims=True), TINY)
        o_cmp = _dot(p.astype(BF16), vc)

        p_sum = p[0:QB] + p[QB:2 * QB] + p[2 * QB:3 * QB] + p[3 * QB:4 * QB]
        imp = _dot_nt(ovt_ref[...], p_sum, precision=_HI)
        tb = lax.shift_right_logical(q_lane + i * QB, 6)
        forced = (blk_id == 0) | (blk_id == tb) | (blk_id == tb - 1)
        score = jnp.where(blk_id <= tb, imp + jnp.where(forced, NSA_FORCE, 0.0), -NSA_FORCE)
        rank = jnp.zeros((NSA_N_SLC, QB), F32)
        for other in range(NSA_N_SLC):
            s_o = score[other:other + 1, :]
            beats = (s_o > score) | ((s_o == score) & (blk_id > other))
            rank = rank + jnp.where(beats, 1.0, 0.0)
        sel_t = jnp.where(rank < NSA_TOP_N, 1.0, 0.0)
        sel = jnp.concatenate([sel_t, jnp.zeros((QB - NSA_N_SLC, QB), F32)], axis=0).T
        key_ok = _dot(sel.astype(BF16), expand_ref[...])
        for c in range(n_chunks):
            km_ref[c] = (key_ok[:, c * QB:(c + 1) * QB] - 1.0) * (-NEG)

        def kv_chunk(j, c):
            m, l, acc = c
            c0 = pl.multiple_of(j * QB, QB)
            km = km_ref[j]
            s = _dot_nt(qs, ks_ref[pl.ds(c0, QB), :]) + bslc_ref[i - j] + jnp.concatenate([km] * NSA_REPS, axis=0)
            m_new = jnp.maximum(m, jnp.max(s, axis=-1, keepdims=True))
            pj = jnp.exp(s - m_new)
            alpha = jnp.exp(m - m_new)
            l = alpha * l + jnp.sum(pj, axis=-1, keepdims=True)
            acc = alpha * acc + _dot(pj.astype(BF16), vs_ref[pl.ds(c0, QB), :])
            return m_new, l, acc

        init = (jnp.full((m_rows, 1), NEG, F32), jnp.zeros((m_rows, 1), F32), jnp.zeros((m_rows, HEAD_DIM), F32))
        _, l, acc = lax.fori_loop(0, i + 1, kv_chunk, init)
        o_slc = acc / l

        bias = jnp.where(win_col >= (NSA_WIN_PREV - i) * QB, bwin_ref[...], NEG)
        o_win, _ = _window_softmax(qs, kw_ref[pl.ds(r0, win_kw), :], vw_ref[pl.ds(r0, win_kw), :], bias)

        gate = jax.nn.sigmoid(z_ref[pl.ds(r0, QB), _OFF_GATE:_OFF_GATE + 128])
        for r in range(NSA_REPS):
            rows = slice(r * QB, (r + 1) * QB)
            out = (gate[:, 3 * r:3 * r + 1] * o_cmp[rows] + gate[:, 3 * r + 1:3 * r + 2] * o_slc[rows]
                   + gate[:, 3 * r + 2:3 * r + 3] * o_win[rows])
            o_ref[pl.ds(r0, QB), r * HEAD_DIM:(r + 1) * HEAD_DIM] = out.astype(o_ref.dtype)
        return carry

    lax.fori_loop(0, n_chunks, q_block, 0)


def _nsa(z_nsa, zc, w1, w2, pos, gq, gk, ovt, expand, bslc, bwin, batch, seq):
    gw = NSA_GROUP_COLS
    return pl.pallas_call(
        functools.partial(_nsa_kernel, seq),
        grid=(batch, NSA_KV_HEADS),
        in_specs=[pl.BlockSpec((None, seq, gw), lambda b, g: (b, 0, g)),
                  pl.BlockSpec((None, None) + zc.shape[2:], lambda b, g: (b, g, 0, 0, 0)),
                  _resident(w1.shape), _resident(w2.shape), _resident(pos.shape),
                  _resident(gq.shape), _resident(gk.shape), _resident(ovt.shape), _resident(expand.shape),
                  pl.BlockSpec((None,) + bslc.shape[1:], lambda b, g: (g, 0, 0, 0)),
                  pl.BlockSpec((None,) + bwin.shape[1:], lambda b, g: (g, 0, 0))],
        out_specs=pl.BlockSpec((None, seq, _NQ), lambda b, g: (b, 0, g)),
        out_shape=jax.ShapeDtypeStruct((batch, seq, NSA_HEADS * HEAD_DIM), BF16),
        scratch_shapes=[pltpu.VMEM((seq, HEAD_DIM), BF16), pltpu.VMEM((seq, HEAD_DIM), BF16),
                        pltpu.VMEM((NSA_WIN_PREV * QB + seq, HEAD_DIM), BF16),
                        pltpu.VMEM((NSA_WIN_PREV * QB + seq, HEAD_DIM), BF16),
                        pltpu.VMEM((seq // QB, QB, QB), F32)],
        compiler_params=_cparams(2),
        name="nsa",
    )(z_nsa, zc, w1, w2, pos, gq, gk, ovt, expand, bslc, bwin)


def _t5_bucket(dist):
    n = jnp.maximum(dist, 0)
    max_exact = NUM_BUCKETS // 2
    nf = jnp.maximum(n, 1).astype(F32)
    large = max_exact + (jnp.log(nf / max_exact) / math.log(MAX_DISTANCE / max_exact)
                         * (NUM_BUCKETS - max_exact)).astype(jnp.int32)
    large = jnp.minimum(large, NUM_BUCKETS - 1)
    return jnp.where(n < max_exact, n, large)


def _toeplitz(u, rows, cols):
    lead = u.shape[:-1]
    lu = rows + cols - 1
    assert u.shape[-1] == lu
    padded = jnp.pad(u, [(0, 0)] * len(lead) + [(0, 1)])
    flat = jnp.broadcast_to(padded[..., None, :], lead + (rows, lu + 1)).reshape(lead + (rows * (lu + 1),))
    return flat[..., :rows * lu].reshape(lead + (rows, lu))[..., rows - 1:]


def _bias_by_distance(bias_cols, delta, valid, dist_scale=1):
    bucket = _t5_bucket(jnp.asarray(np.maximum(delta, 0) * dist_scale, dtype=jnp.int32))
    return jnp.where(jnp.asarray(valid)[None, :], bias_cols.astype(F32)[bucket].T, NEG)


def _band_bias(bias_cols, window, n_prev, dist_scale, n_kv, reps):
    kw = (n_prev + 1) * QB
    delta = n_prev * QB + QB - 1 - np.arange(kw + QB - 1)
    u = _bias_by_distance(bias_cols, delta, (delta >= 0) & (delta < window), dist_scale)
    return _toeplitz(u, QB, kw).reshape(n_kv, reps * QB, kw)


def _slc_bias(bias_cols, seq):
    n_chunks = seq // QB
    delta = seq - 1 - np.arange(seq + QB - 1)
    strip = _toeplitz(_bias_by_distance(bias_cols, delta, delta >= 0), QB, seq)
    tile = jnp.flip(strip.reshape(NSA_KV_HEADS, NSA_REPS, QB, n_chunks, QB), axis=3)
    return jnp.transpose(tile, (0, 3, 1, 2, 4)).reshape(NSA_KV_HEADS, n_chunks, NSA_REPS * QB, QB)


def _rope_tables(seq):
    half = MLA_ROPE // 2
    inv = jnp.power(ROPE_THETA, -jnp.arange(half, dtype=F32) / half)
    ang = jnp.arange(seq, dtype=F32)[:, None] * inv[None, :]
    cos, sin = jnp.cos(ang), jnp.sin(ang)
    zeros = lambda w: jnp.zeros((seq, w), F32)
    tail = MLA_PAD - MLA_QK_DIM
    rope_c = jnp.concatenate([jnp.ones((seq, MLA_NOPE), F32), cos, cos, zeros(tail)], axis=1)
    rope_m = jnp.concatenate([zeros(MLA_NOPE), -sin, zeros(half), zeros(tail)], axis=1)
    rope_p = jnp.concatenate([zeros(MLA_NOPE), zeros(half), sin, zeros(tail)], axis=1)
    return rope_c, rope_m, rope_p


def _pad_cols(w, width):
    return jnp.pad(w, ((0, 0), (0, width - w.shape[1])))


def _nsa_column_order():
    g_cols = NSA_KV_HEADS * HEAD_DIM
    q_cols = NSA_HEADS * HEAD_DIM
    order = []
    for g in range(NSA_KV_HEADS):
        cols = list(range(g * _NQ, (g + 1) * _NQ))
        for part in range(6):
            start = q_cols + part * g_cols + g * HEAD_DIM
            cols += list(range(start, start + HEAD_DIM))
        gate0 = q_cols + 6 * g_cols + g * NSA_REPS * 3
        cols += list(range(gate0, gate0 + NSA_REPS * 3))
        order.append(cols)
    return order


def _nsa_mixer(z_nsa3, rel_bias, q_norm, k_norm, cmp_pos, cmp_w1, cmp_w2, batch, seq):
    zc = z_nsa3.reshape(batch, seq // NSA_CMP_STRIDE, NSA_CMP_STRIDE, NSA_KV_HEADS, NSA_GROUP_COLS)
    zc = zc[..., _OFF_KC:_OFF_KC + 2 * HEAD_DIM].reshape(
        batch, seq // NSA_CMP_STRIDE, NSA_CMP_STRIDE, NSA_KV_HEADS, 2, HEAD_DIM)
    zc = jnp.transpose(zc, (0, 3, 4, 1, 2, 5)).reshape(
        batch, NSA_KV_HEADS, 2, seq // NSA_CMP_STRIDE, NSA_CMP_STRIDE * HEAD_DIM)
    n_cmp = (seq - NSA_CMP_LEN) // NSA_CMP_STRIDE + 1
    ci = np.arange(NSA_N_CMP_PAD)[:, None] * NSA_CMP_STRIDE
    sj = np.arange(NSA_N_SLC)[None, :] * NSA_SLC_BLOCK
    overlap = ((ci < sj + NSA_SLC_BLOCK) & (ci + NSA_CMP_LEN > sj) & (np.arange(NSA_N_CMP_PAD)[:, None] < n_cmp))
    ovt = jnp.asarray(overlap.T.astype(np.float32))
    expand = jnp.asarray((np.arange(QB)[:, None] == np.arange(seq)[None, :] // NSA_SLC_BLOCK)
                         .astype(np.float32)).astype(BF16)
    nsa_cols = rel_bias[:, NSA_BIAS_COL0:NSA_BIAS_COL0 + NSA_HEADS]
    bslc = _slc_bias(nsa_cols, seq)
    bwin = _band_bias(nsa_cols, NSA_WINDOW, NSA_WIN_PREV, 1, NSA_KV_HEADS, NSA_REPS)
    return _nsa(z_nsa3, zc, cmp_w1, cmp_w2, cmp_pos.reshape(2, 1, NSA_CMP_LEN * HEAD_DIM), q_norm.reshape(1, HEAD_DIM),
                k_norm.reshape(3, 1, HEAD_DIM), ovt, expand, bslc, bwin, batch, seq)


def kernel(x, rel_bias, ffn1_norm, ffn1_w_gate, ffn1_w_up, ffn1_w_down, mix_norm, ffn2_norm, ffn2_w_gate,
           ffn2_w_up, ffn2_w_down, ab_w_in, mla_q_a_norm, mla_w_q_b, mla_kv_a_norm, mla_w_kv_b, mla_q_norm,
           mla_k_norm, dil_q_norm, dil_k_norm, ab_w_out, cd_w_in, swa_q_norm, swa_k_norm, swa_sinks,
           nsa_q_norm, nsa_k_norm, nsa_cmp_pos, nsa_cmp_w1, nsa_cmp_w2, cd_w_out):
    batch, seq, _ = x.shape
    n = batch * seq
    assert seq % (16 * QB) == 0 and n % TM == 0 and seq % TM == 0
    bf = lambda a: a.astype(BF16)
    xf = x.reshape(n, D_MODEL)

    xf = _ffn(xf, ffn1_norm[0], bf(ffn1_w_gate[0]), bf(ffn1_w_up[0]), bf(ffn1_w_down[0]))

    w_in = ab_w_in[0]
    mla_cols = MLA_Q_RANK + MLA_KV_RANK
    w_krope = jnp.pad(w_in[:, mla_cols:mla_cols + MLA_ROPE], ((0, 0), (MLA_NOPE, MLA_PAD - MLA_QK_DIM)))
    w_mla = jnp.concatenate([w_in[:, :mla_cols], w_krope], axis=1)
    z_mla, z_dil = _norm_proj(xf, mix_norm[0], [bf(w_mla), bf(w_in[:, mla_cols + MLA_ROPE:])])

    wq = _pad_cols(mla_w_q_b[0].reshape(MLA_Q_RANK * MLA_HEADS, MLA_QK_DIM), MLA_PAD)
    wq = wq.reshape(MLA_Q_RANK, MLA_HEADS * MLA_PAD)
    wkv = mla_w_kv_b[0].reshape(MLA_KV_RANK, MLA_HEADS, MLA_NOPE + MLA_V)
    wk = _pad_cols(wkv[:, :, :MLA_NOPE].reshape(MLA_KV_RANK * MLA_HEADS, MLA_NOPE), MLA_PAD)
    wk = wk.reshape(MLA_KV_RANK, MLA_HEADS * MLA_PAD)
    wv = wkv[:, :, MLA_NOPE:].reshape(MLA_KV_RANK, MLA_HEADS * MLA_V)
    rope_c, rope_m, rope_p = _rope_tables(seq)
    q_mla, k_mla, v_mla = _mla_prep(
        z_mla, mla_q_a_norm[0].reshape(1, -1), mla_kv_a_norm[0].reshape(1, -1), bf(wq), bf(wk), bf(wv),
        _pad_cols(mla_q_norm[0].reshape(1, -1), MLA_PAD), _pad_cols(mla_k_norm[0].reshape(1, -1), MLA_PAD),
        rope_c, rope_m, rope_p, seq)
    o_a = _mla_attn(q_mla.reshape(batch, seq, -1), k_mla.reshape(batch, seq, -1),
                    v_mla.reshape(batch, seq, -1), batch, seq).reshape(n, MLA_HEADS * MLA_V)

    z_dil3 = z_dil.reshape(batch, seq, -1)
    gq, gk = dil_q_norm[0].reshape(1, HEAD_DIM), dil_k_norm[0].reshape(1, HEAD_DIM)
    dil_outs, dil_lses = [], []
    for grp, (window, dilation) in enumerate(DIL_PATTERNS):
        bm = _band_bias(rel_bias[:, grp * DIL_HPG:(grp + 1) * DIL_HPG], window // dilation + 1, 1, dilation,
                        DIL_HPG, 1)
        o, lse = _dilated_group(z_dil3, batch, seq, grp, dilation, gq, gk, bm)
        dil_outs.append(o)
        dil_lses.append(lse)
    w_out = ab_w_out[0]
    xf = _out_ab(xf, o_a, dil_outs, dil_lses, bf(w_out[:MLA_HEADS * MLA_V]), bf(w_out[MLA_HEADS * MLA_V:]))
    xf = _ffn(xf, ffn2_norm[0], bf(ffn2_w_gate[0]), bf(ffn2_w_up[0]), bf(ffn2_w_down[0]))

    xf = _ffn(xf, ffn1_norm[1], bf(ffn1_w_gate[1]), bf(ffn1_w_up[1]), bf(ffn1_w_down[1]))
    w_in = cd_w_in[0]
    swa_cols = (SWA_HEADS + 2 * SWA_KV_HEADS) * HEAD_DIM
    w_nsa_src = w_in[:, swa_cols:]
    w_nsa = jnp.concatenate([_pad_cols(w_nsa_src[:, np.asarray(cols)], NSA_GROUP_COLS)
                             for cols in _nsa_column_order()], axis=1)
    z_swa, z_nsa = _norm_proj(xf, mix_norm[1], [bf(w_in[:, :swa_cols]), bf(w_nsa)])

    swa_reps = SWA_HEADS // SWA_KV_HEADS
    swa_prev = -(-(SWA_WINDOW - 1) // QB)
    bm_swa = _band_bias(rel_bias[:, :SWA_HEADS], SWA_WINDOW, swa_prev, 1, SWA_KV_HEADS, swa_reps)
    sinks = jnp.broadcast_to(swa_sinks[0].astype(F32).reshape(SWA_KV_HEADS, swa_reps, 1, 1),
                             (SWA_KV_HEADS, swa_reps, QB, 1)).reshape(SWA_KV_HEADS, swa_reps * QB, 1)
    o_c = _swa(z_swa.reshape(batch, seq, -1), batch, seq, swa_q_norm[0].reshape(1, HEAD_DIM),
               swa_k_norm[0].reshape(1, HEAD_DIM), bm_swa, sinks)

    o_d = _nsa_mixer(z_nsa.reshape(batch, seq, NSA_KV_HEADS * NSA_GROUP_COLS), rel_bias, nsa_q_norm[0],
                     nsa_k_norm[0], nsa_cmp_pos[0], nsa_cmp_w1[0], nsa_cmp_w2[0], batch, seq)

    w_out = cd_w_out[0]
    xf = _out_cd(xf, o_c.reshape(n, -1), o_d.reshape(n, -1), bf(w_out[:SWA_HEADS * HEAD_DIM]),
                 bf(w_out[SWA_HEADS * HEAD_DIM:]))
    xf = _ffn(xf, ffn2_norm[1], bf(ffn2_w_gate[1]), bf(ffn2_w_up[1]), bf(ffn2_w_down[1]))
    return xf.reshape(batch, seq, D_MODEL)
```

```python
import functools
import math

import numpy as np
import jax
import jax.numpy as jnp
from jax import lax
from jax.experimental import pallas as pl
from jax.experimental.pallas import tpu as pltpu

F32 = jnp.float32
BF16 = jnp.bfloat16

EPS = 1e-6
NEG = -1e30
TINY = 1e-30
D_MODEL = 1024
D_FF = 2816
NUM_BUCKETS = 32
MAX_DISTANCE = 2048
HEAD_DIM = 64
QB = 128

MLA_HEADS = 8
MLA_Q_RANK = 256
MLA_KV_RANK = 128
MLA_NOPE = 64
MLA_ROPE = 32
MLA_V = 64
MLA_QK_DIM = MLA_NOPE + MLA_ROPE
MLA_PAD = 128
ROPE_THETA = 10000.0

DIL_PATTERNS = ((128, 1), (512, 4), (2048, 16))
DIL_HPG = 4
SWA_HEADS = 8
SWA_KV_HEADS = 2
SWA_WINDOW = 128
NSA_HEADS = 8
NSA_KV_HEADS = 2
NSA_CMP_LEN = 32
NSA_CMP_STRIDE = 16
NSA_CMP_HIDDEN = 128
NSA_SLC_BLOCK = 64
NSA_TOP_N = 16
NSA_WINDOW = 512
NSA_FORCE = 1e6
NSA_BIAS_COL0 = 8
NSA_GROUP_COLS = 768

VMEM_LIMIT = 56 * 1024 * 1024
TM = 512

_NT = (((1,), (1,)), ((), ()))
_HI = lax.Precision.HIGHEST


def _cparams(n_axes):
    return pltpu.CompilerParams(dimension_semantics=("arbitrary",) * n_axes,
                                vmem_limit_bytes=VMEM_LIMIT)


def _resident(shape):
    nd = len(shape)
    return pl.BlockSpec(shape, lambda *_: (0,) * nd, pipeline_mode=pl.Buffered(1))


def _rms(x):
    return x * lax.rsqrt(jnp.mean(x * x, axis=-1, keepdims=True) + EPS)


def _dot(a, b):
    return jnp.dot(a, b, preferred_element_type=F32)


def _dot_nt(a, b, precision=None):
    return lax.dot_general(a, b, _NT, preferred_element_type=F32, precision=precision)


FF_CHUNK = 256


def _ffn_kernel(x_ref, g_ref, wg_ref, wu_ref, wd_ref, o_ref):
    x = x_ref[...]
    hb = (_rms(x) * g_ref[...]).astype(BF16)
    acc = jnp.zeros(x.shape, F32)
    for c in range(D_FF // FF_CHUNK):
        sl = slice(c * FF_CHUNK, (c + 1) * FF_CHUNK)
        gate = _dot(hb, wg_ref[:, sl])
        up = _dot(hb, wu_ref[:, sl])
        act = (gate * jax.nn.sigmoid(gate) * up).astype(BF16)
        acc = acc + _dot(act, wd_ref[sl, :])
    o_ref[...] = x + 0.5 * acc


def _ffn(x, g, wg, wu, wd):
    n = x.shape[0]
    return pl.pallas_call(
        _ffn_kernel,
        grid=(n // TM,),
        in_specs=[pl.BlockSpec((TM, D_MODEL), lambda i: (i, 0)),
                  _resident((1, D_MODEL)),
                  _resident((D_MODEL, D_FF)), _resident((D_MODEL, D_FF)), _resident((D_FF, D_MODEL))],
        out_specs=pl.BlockSpec((TM, D_MODEL), lambda i: (i, 0)),
        out_shape=jax.ShapeDtypeStruct((n, D_MODEL), F32),
        compiler_params=_cparams(1),
        name="ffn",
    )(x, g.reshape(1, D_MODEL), wg, wu, wd)


N_CHUNK = 512


def _norm_proj_kernel(n_out, x_ref, g_ref, *refs):
    w_refs, o_refs = refs[:n_out], refs[n_out:]
    hb = (_rms(x_ref[...]) * g_ref[...]).astype(BF16)
    for w_ref, o_ref in zip(w_refs, o_refs):
        width = w_ref.shape[1]
        for c0 in range(0, width, N_CHUNK):
            sl = slice(c0, min(c0 + N_CHUNK, width))
            o_ref[:, sl] = _dot(hb, w_ref[:, sl])


def _norm_proj(x, g, ws):
    n = x.shape[0]
    return pl.pallas_call(
        functools.partial(_norm_proj_kernel, len(ws)),
        grid=(n // TM,),
        in_specs=[pl.BlockSpec((TM, D_MODEL), lambda i: (i, 0)), _resident((1, D_MODEL))]
        + [_resident(w.shape) for w in ws],
        out_specs=[pl.BlockSpec((TM, w.shape[1]), lambda i: (i, 0)) for w in ws],
        out_shape=[jax.ShapeDtypeStruct((n, w.shape[1]), F32) for w in ws],
        compiler_params=_cparams(1),
        name="norm_proj",
    )(x, g.reshape(1, D_MODEL), *ws)


def _out_ab_kernel(x_ref, oa_ref, o0_ref, o1_ref, o2_ref, l0_ref, l1_ref, l2_ref, wa_ref, wb_ref, out_ref):
    l0, l1, l2 = l0_ref[...], l1_ref[...], l2_ref[...]
    m = jnp.maximum(jnp.maximum(l0, l1), l2)
    e0, e1, e2 = jnp.exp(l0 - m), jnp.exp(l1 - m), jnp.exp(l2 - m)
    ob = (e0 * o0_ref[...] + e1 * o1_ref[...] + e2 * o2_ref[...]) / (e0 + e1 + e2)
    out_ref[...] = x_ref[...] + _dot(oa_ref[...], wa_ref[...]) + _dot(ob.astype(BF16), wb_ref[...])


def _out_ab(x, oa, dil_outs, dil_lses, wa, wb):
    n = x.shape[0]
    dw = DIL_HPG * HEAD_DIM
    tile = lambda w: pl.BlockSpec((TM, w), lambda i: (i, 0))
    return pl.pallas_call(
        _out_ab_kernel,
        grid=(n // TM,),
        in_specs=[tile(D_MODEL), tile(oa.shape[1])] + [tile(dw)] * 6 + [_resident(wa.shape), _resident(wb.shape)],
        out_specs=tile(D_MODEL),
        out_shape=jax.ShapeDtypeStruct((n, D_MODEL), F32),
        compiler_params=_cparams(1),
        name="out_ab",
    )(x, oa, *dil_outs, *dil_lses, wa, wb)


def _out_cd_kernel(x_ref, oc_ref, od_ref, wc_ref, wd_ref, out_ref):
    out_ref[...] = x_ref[...] + _dot(oc_ref[...], wc_ref[...]) + _dot(od_ref[...], wd_ref[...])


def _out_cd(x, oc, od, wc, wd):
    n = x.shape[0]
    tile = lambda w: pl.BlockSpec((TM, w), lambda i: (i, 0))
    return pl.pallas_call(
        _out_cd_kernel,
        grid=(n // TM,),
        in_specs=[tile(D_MODEL), tile(oc.shape[1]), tile(od.shape[1]), _resident(wc.shape), _resident(wd.shape)],
        out_specs=tile(D_MODEL),
        out_shape=jax.ShapeDtypeStruct((n, D_MODEL), F32),
        compiler_params=_cparams(1),
        name="out_cd",
    )(x, oc, od, wc, wd)


def _mla_prep_kernel(z_ref, qan_ref, kvan_ref, wq_ref, wk_ref, wv_ref, gq_ref, gk_ref,
                     rc_ref, rm_ref, rp_ref, q_ref, k_ref, vt_ref):
    cq = (_rms(z_ref[:, :MLA_Q_RANK]) * qan_ref[...]).astype(BF16)
    ckv = (_rms(z_ref[:, MLA_Q_RANK:MLA_Q_RANK + MLA_KV_RANK]) * kvan_ref[...]).astype(BF16)
    kr = z_ref[:, MLA_Q_RANK + MLA_KV_RANK:]
    rc, rm, rp = rc_ref[...], rm_ref[...], rp_ref[...]
    scale = MLA_QK_DIM ** -0.5

    def head_norm_rope(x, gain):
        x = x * lax.rsqrt(jnp.sum(x * x, axis=-1, keepdims=True) * (1.0 / MLA_QK_DIM) + EPS) * gain
        half = MLA_ROPE // 2
        return x * rc + pltpu.roll(x, MLA_PAD - half, 1) * rm + pltpu.roll(x, half, 1) * rp

    for h in range(MLA_HEADS):
        sl = slice(h * MLA_PAD, (h + 1) * MLA_PAD)
        qh = head_norm_rope(_dot(cq, wq_ref[:, sl]), gq_ref[...])
        q_ref[:, sl] = (qh * scale).astype(BF16)
        kh = head_norm_rope(_dot(ckv, wk_ref[:, sl]) + kr, gk_ref[...])
        k_ref[:, sl] = kh.astype(BF16)
    v = _dot(ckv, wv_ref[...])
    for c in range(TM // MLA_TQ):
        vt_ref[c] = v[c * MLA_TQ:(c + 1) * MLA_TQ].T.astype(BF16)


MLA_TQ = 256


def _mla_prep(z, qan, kvan, wq, wk, wv, gq, gk, rope_c, rope_m, rope_p, seq):
    n = z.shape[0]
    per_seq = seq // TM
    tile = lambda w: pl.BlockSpec((TM, w), lambda i: (i, 0))
    rope_spec = pl.BlockSpec((TM, MLA_PAD), lambda i: (i % per_seq, 0))
    hw = MLA_HEADS * MLA_PAD
    vw = MLA_HEADS * MLA_V
    return pl.pallas_call(
        _mla_prep_kernel,
        grid=(n // TM,),
        in_specs=[tile(z.shape[1]), _resident(qan.shape), _resident(kvan.shape), _resident(wq.shape),
                  _resident(wk.shape), _resident(wv.shape), _resident(gq.shape), _resident(gk.shape),
                  rope_spec, rope_spec, rope_spec],
        out_specs=[tile(hw), tile(hw), pl.BlockSpec((TM // MLA_TQ, vw, MLA_TQ), lambda i: (i, 0, 0))],
        out_shape=[jax.ShapeDtypeStruct((n, hw), BF16), jax.ShapeDtypeStruct((n, hw), BF16),
                   jax.ShapeDtypeStruct((n // MLA_TQ, vw, MLA_TQ), BF16)],
        compiler_params=_cparams(1),
        name="mla_prep",
    )(z, qan, kvan, wq, wk, wv, gq, gk, rope_c, rope_m, rope_p)


def _flash_step_t(s, m, l, acc, vt):
    m_new = jnp.maximum(m, jnp.max(s, axis=0, keepdims=True))
    p = jnp.exp(s - m_new)
    alpha = jnp.exp(m - m_new)
    l = alpha * l + jnp.sum(p, axis=0, keepdims=True)
    acc = alpha * acc + _dot(vt, p.astype(BF16))
    return m_new, l, acc


def _mla_attn_kernel(seq, q_ref, k_ref, vt_ref, o_ref):
    t = MLA_TQ
    key = lax.broadcasted_iota(jnp.int32, (t, t), 0)
    qry = lax.broadcasted_iota(jnp.int32, (t, t), 1)
    diag_ok = key <= qry
    heads = [(slice(hh * MLA_PAD, (hh + 1) * MLA_PAD), slice(hh * MLA_V, (hh + 1) * MLA_V)) for hh in range(2)]

    init = (jnp.full((1, t), NEG, F32), jnp.zeros((1, t), F32), jnp.zeros((MLA_V, t), F32))
    for i in range(seq // t):
        rows = slice(i * t, (i + 1) * t)
        qs = [q_ref[rows, ql] for ql, _ in heads]

        def scores(h, j):
            s = _dot_nt(k_ref[j * t:(j + 1) * t, heads[h][0]], qs[h])
            return jnp.where(diag_ok, s, NEG) if j == i else s

        cur = [scores(h, 0) for h in range(2)]
        states = [init, init]
        for j in range(i + 1):
            for h in range(2):
                nxt = scores(h, j + 1) if j < i else None
                states[h] = _flash_step_t(cur[h], *states[h], vt_ref[j, heads[h][1], :])
                cur[h] = nxt
        outs = [acc * (1.0 / l) for _, l, acc in states]
        o_ref[rows, :] = jnp.concatenate(outs, axis=0).T.astype(o_ref.dtype)


def _mla_attn(q, k, vt, batch, seq):
    return pl.pallas_call(
        functools.partial(_mla_attn_kernel, seq),
        grid=(batch, MLA_HEADS // 2),
        in_specs=[pl.BlockSpec((None, seq, 2 * MLA_PAD), lambda b, h: (b, 0, h)),
                  pl.BlockSpec((None, seq, 2 * MLA_PAD), lambda b, h: (b, 0, h)),
                  pl.BlockSpec((None, seq // MLA_TQ, 2 * MLA_V, MLA_TQ), lambda b, h: (b, 0, h, 0))],
        out_specs=pl.BlockSpec((None, seq, 2 * MLA_V), lambda b, h: (b, 0, h)),
        out_shape=jax.ShapeDtypeStruct((batch, seq, MLA_HEADS * MLA_V), BF16),
        compiler_params=_cparams(2),
        name="mla_attn",
    )(q, k, vt)


def _head_norm(x, gain):
    return _rms(x) * gain


def _heads_t(qt, n_heads, gain):
    tiles = []
    for c0 in range(0, n_heads * HEAD_DIM, 2 * HEAD_DIM):
        pair = qt[:, c0:c0 + 2 * HEAD_DIM].T
        for h in range(2):
            x = pair[h * HEAD_DIM:(h + 1) * HEAD_DIM]
            tiles.append(x * lax.rsqrt(jnp.mean(x * x, axis=0, keepdims=True) + EPS) * gain)
    return tiles


def _q_gain(gq_ref):
    return jnp.broadcast_to(gq_ref[...] * (HEAD_DIM ** -0.5), (HEAD_DIM, QB))


BAND_KV = 2


def _banded_kernel(seq, dilation, reps, n_prev, with_sink, with_lse, group_size, *refs):
    q_ref, k_ref, v_ref, gq_ref, gk_ref, bm_ref = refs[:6]
    refs = refs[6:]
    sink_ref = None
    if with_sink:
        sink_ref, refs = refs[0], refs[1:]
    o_ref, refs = refs[0], refs[1:]
    lse_ref = None
    if with_lse:
        lse_ref, refs = refs[0], refs[1:]
    kn_ref, vt_ref = refs
    length = seq // dilation
    n_blocks = length // QB
    pad = n_prev * QB
    kw = pad + QB
    gq, gk = _q_gain(gq_ref), gk_ref[...]
    win_row = lax.broadcasted_iota(jnp.int32, (kw, 1), 0)

    def rows(start, size):
        return pl.ds(start, size) if dilation == 1 else pl.ds(start, size, stride=dilation)

    for res in range(dilation):
        kall, vall = k_ref[rows(res, length), :], v_ref[rows(res, length), :]
        for g in range(BAND_KV):
            kn_ref[res, g, :pad, :] = jnp.zeros((pad, HEAD_DIM), BF16)
            kn_ref[res, g, pad:, :] = _head_norm(kall[:, g * HEAD_DIM:(g + 1) * HEAD_DIM], gk).astype(BF16)
        for j in range(n_prev):
            vt_ref[res, j] = jnp.zeros((BAND_KV * HEAD_DIM, QB), BF16)
        for j in range(n_blocks):
            vt_ref[res, n_prev + j] = vall[j * QB:(j + 1) * QB].T.astype(BF16)

    def attend(res, i, g, s):
        bias = bm_ref[g]
        if i < n_prev:
            bias = jnp.where(win_row >= (n_prev - i) * QB, bias, NEG)
        s = s + bias
        m = jnp.max(s, axis=0, keepdims=True)
        if with_sink:
            m = jnp.maximum(m, sink_ref[g])
        p = jnp.exp(s - m)
        l = jnp.sum(p, axis=0, keepdims=True)
        if with_sink:
            l = l + jnp.exp(sink_ref[g] - m)
        p = p.astype(BF16)
        vl = slice(g * HEAD_DIM, (g + 1) * HEAD_DIM)
        o = _dot(vt_ref[res, i, vl, :], p[0:QB])
        for w in range(1, n_prev + 1):
            o = o + _dot(vt_ref[res, i + w, vl, :], p[w * QB:(w + 1) * QB])
        return o * (1.0 / l), m + jnp.log(l)

    items = [(res, i) for res in range(dilation) for i in range(n_blocks)]
    for at in range(0, len(items), group_size):
        group = items[at:at + group_size]
        toks = [rows(res + i * QB * dilation, QB) for res, i in group]
        heads = [_heads_t(q_ref[tok, :], BAND_KV * reps, gq) for tok in toks]
        qss = [[jnp.concatenate(h[g * reps:(g + 1) * reps], axis=1).astype(BF16) for g in range(BAND_KV)]
               for h in heads]
        scores = [[_dot(kn_ref[res, g, i * QB:i * QB + kw, :], qs[g]) for g in range(BAND_KV)]
                  for (res, i), qs in zip(group, qss)]
        for (res, i), tok, sc in zip(group, toks, scores):
            outs, lses = [], []
            for g in range(BAND_KV):
                o, lse = attend(res, i, g, sc[g])
                for r in range(reps):
                    outs.append(o[:, r * QB:(r + 1) * QB])
                    if with_lse:
                        lses.append(jnp.broadcast_to(lse[:, r * QB:(r + 1) * QB], (HEAD_DIM, QB)))
            o_ref[tok, :] = jnp.concatenate(outs, axis=0).T.astype(o_ref.dtype)
            if with_lse:
                lse_ref[tok, :] = jnp.concatenate(lses, axis=0).T


def _banded_scratch(seq, dilation, n_prev):
    length = seq // dilation
    return [pltpu.VMEM((dilation, BAND_KV, n_prev * QB + length, HEAD_DIM), BF16),
            pltpu.VMEM((dilation, n_prev + length // QB, BAND_KV * HEAD_DIM, QB), BF16)]


def _dilated_group(z_dil, batch, seq, group, dilation, gq, gk, bm):
    gw = DIL_HPG * HEAD_DIM
    n_groups = len(DIL_PATTERNS)
    hps = BAND_KV
    halves = DIL_HPG // hps
    bm_t = jnp.transpose(bm, (0, 2, 1))

    def spec(part):
        return pl.BlockSpec((None, seq, hps * HEAD_DIM), lambda b, h: (b, 0, (part * n_groups + group) * halves + h))

    out_spec = pl.BlockSpec((None, seq, hps * HEAD_DIM), lambda b, h: (b, 0, h))
    shape = jax.ShapeDtypeStruct((batch, seq, gw), F32)
    out, lse = pl.pallas_call(
        functools.partial(_banded_kernel, seq, dilation, 1, 1, False, True, 4),
        grid=(batch, halves),
        in_specs=[spec(0), spec(1), spec(2), _resident(gq.shape), _resident(gk.shape),
                  pl.BlockSpec((hps,) + bm_t.shape[1:], lambda b, h: (h, 0, 0))],
        out_specs=[out_spec, out_spec],
        out_shape=[shape, shape],
        scratch_shapes=_banded_scratch(seq, dilation, 1),
        compiler_params=_cparams(2),
        name="dilated_g%d" % group,
    )(z_dil, z_dil, z_dil, gq, gk, bm_t)
    return out.reshape(batch * seq, gw), lse.reshape(batch * seq, gw)


def _swa(z_swa, batch, seq, gq, gk, bm, sinks):
    qw = SWA_HEADS * HEAD_DIM
    kvw = SWA_KV_HEADS * HEAD_DIM
    reps = SWA_HEADS // SWA_KV_HEADS
    n_prev = -(-(SWA_WINDOW - 1) // QB)
    assert SWA_KV_HEADS == BAND_KV
    bm_t = jnp.transpose(bm, (0, 2, 1))
    return pl.pallas_call(
        functools.partial(_banded_kernel, seq, 1, reps, n_prev, True, False, 2),
        grid=(batch,),
        in_specs=[pl.BlockSpec((None, seq, qw), lambda b: (b, 0, 0)),
                  pl.BlockSpec((None, seq, kvw), lambda b: (b, 0, qw // kvw)),
                  pl.BlockSpec((None, seq, kvw), lambda b: (b, 0, qw // kvw + 1)),
                  _resident(gq.shape), _resident(gk.shape), _resident(bm_t.shape), _resident(sinks.shape)],
        out_specs=pl.BlockSpec((None, seq, qw), lambda b: (b, 0, 0)),
        out_shape=jax.ShapeDtypeStruct((batch, seq, qw), BF16),
        scratch_shapes=_banded_scratch(seq, 1, n_prev),
        compiler_params=_cparams(1),
        name="swa",
    )(z_swa, z_swa, z_swa, gq, gk, bm_t, sinks)


NSA_REPS = NSA_HEADS // NSA_KV_HEADS
NSA_N_SLC = 32
NSA_N_CMP_PAD = 128
NSA_WIN_PREV = -(-(NSA_WINDOW - 1) // QB)
_NQ = NSA_REPS * HEAD_DIM
_OFF_KC, _OFF_VC, _OFF_KS, _OFF_VS, _OFF_KW, _OFF_VW, _OFF_GATE = (_NQ + i * HEAD_DIM for i in range(7))


NSA_TIERS = 4


def _nsa_kernel(seq, z_ref, zc_ref, w1_ref, w2_ref, w2vt_ref, pos_ref, gq_ref, gk_ref, ovt_ref, expand_ref,
                bslc_ref, bwin_ref, o_ref, ks_ref, vst_ref, kw_ref, vwt_ref):
    n_chunks = seq // QB
    lanes = NSA_REPS * QB
    gq = _q_gain(gq_ref)
    win_pad = NSA_WIN_PREV * QB
    win_kw = win_pad + QB

    half = NSA_CMP_LEN * HEAD_DIM // 2

    def hidden(kv):
        x2 = zc_ref[kv]
        first = jnp.dot(x2, w1_ref[kv, :half, :], precision=_HI, preferred_element_type=F32)
        second = jnp.dot(x2, w1_ref[kv, half:, :], precision=_HI, preferred_element_type=F32)
        pos_rows = jnp.broadcast_to(pos_ref[kv], (8, NSA_CMP_LEN * HEAD_DIM))
        pos_term = jnp.dot(pos_rows, w1_ref[kv], precision=_HI, preferred_element_type=F32)[0:1]
        return jax.nn.gelu(first + pltpu.roll(second, NSA_N_CMP_PAD - 1, 0) + pos_term)

    kc = _head_norm(jnp.dot(hidden(0), w2_ref[0], precision=_HI, preferred_element_type=F32), gk_ref[0])
    vct = _dot_nt(w2vt_ref[...], hidden(1), precision=_HI).astype(BF16)

    ks_ref[...] = _head_norm(z_ref[:, _OFF_KS:_OFF_KS + HEAD_DIM], gk_ref[1]).astype(BF16)
    kw_ref[:win_pad, :] = jnp.zeros((win_pad, HEAD_DIM), BF16)
    kw_ref[win_pad:, :] = _head_norm(z_ref[:, _OFF_KW:_OFF_KW + HEAD_DIM], gk_ref[2]).astype(BF16)
    for j in range(NSA_WIN_PREV):
        vwt_ref[j] = jnp.zeros((HEAD_DIM, QB), BF16)
    for j in range(n_chunks):
        rows = slice(j * QB, (j + 1) * QB)
        vst_ref[j] = z_ref[rows, _OFF_KS:_OFF_KS + 2 * HEAD_DIM].T[HEAD_DIM:].astype(BF16)
        vwt_ref[NSA_WIN_PREV + j] = z_ref[rows, _OFF_KW:_OFF_KW + 2 * HEAD_DIM].T[HEAD_DIM:].astype(BF16)

    cmp_id = lax.broadcasted_iota(jnp.int32, (NSA_N_CMP_PAD, lanes), 0)
    cmp_end = cmp_id * NSA_CMP_STRIDE + (NSA_CMP_LEN - 1)
    cmp_real = cmp_id < NSA_N_CMP_PAD - 1
    q_in_blk = lax.broadcasted_iota(jnp.int32, (NSA_N_CMP_PAD, lanes), 1) & (QB - 1)
    blk_id = lax.broadcasted_iota(jnp.int32, (NSA_N_SLC, QB), 0)
    q_lane = lax.broadcasted_iota(jnp.int32, (NSA_N_SLC, QB), 1)
    win_row = lax.broadcasted_iota(jnp.int32, (win_kw, 1), 0)
    init = (jnp.full((1, lanes), NEG, F32), jnp.zeros((1, lanes), F32), jnp.zeros((HEAD_DIM, lanes), F32))

    def q_block(n_keys, mask_pad, i, carry):
        r0 = pl.multiple_of(i * QB, QB)
        qf = jnp.concatenate(_heads_t(z_ref[pl.ds(r0, QB), :_NQ], NSA_REPS, gq), axis=1)
        qs = qf.astype(BF16)

        sc = jnp.dot(kc, qf, precision=_HI, preferred_element_type=F32)
        s_win = _dot(kw_ref[pl.ds(r0, win_kw), :], qs)

        def slc_scores(j):
            return _dot(ks_ref[j * QB:(j + 1) * QB, :], qs)

        cur = slc_scores(0)

        ok = (cmp_end <= q_in_blk + i * QB) & cmp_real
        sc = jnp.where(ok, sc, NEG)
        e = jnp.where(ok, jnp.exp(sc - jnp.max(sc, axis=0, keepdims=True)), 0.0)
        p = e * (1.0 / jnp.maximum(jnp.sum(e, axis=0, keepdims=True), TINY))
        o_cmp = _dot(vct, p.astype(BF16))

        p_sum = p[:, 0:QB] + p[:, QB:2 * QB] + p[:, 2 * QB:3 * QB] + p[:, 3 * QB:4 * QB]
        imp = jnp.dot(ovt_ref[...], p_sum, precision=_HI, preferred_element_type=F32)

        bias = bwin_ref[...]
        if mask_pad:
            bias = jnp.where(win_row >= (NSA_WIN_PREV - i) * QB, bias, NEG)
        s_win = s_win + bias
        pw = jnp.exp(s_win - jnp.max(s_win, axis=0, keepdims=True))
        lw = jnp.sum(pw, axis=0, keepdims=True)
        pw = pw.astype(BF16)
        o_win = _dot(vwt_ref[i], pw[0:QB])
        for c in range(1, NSA_WIN_PREV + 1):
            o_win = o_win + _dot(vwt_ref[i + c], pw[c * QB:(c + 1) * QB])
        o_win = o_win * (1.0 / lw)

        tb = lax.shift_right_logical(q_lane + i * QB, 6)
        forced = (blk_id == 0) | (blk_id == tb) | (blk_id == tb - 1)
        score = jnp.where(blk_id <= tb, imp + jnp.where(forced, NSA_FORCE, 0.0), -NSA_FORCE)
        rank = jnp.zeros((NSA_N_SLC, QB), F32)
        for other in range(NSA_N_SLC):
            s_o = score[other:other + 1, :]
            beats = (s_o > score) | ((s_o == score) & (blk_id > other))
            rank = rank + jnp.where(beats, 1.0, 0.0)
        sel_t = jnp.where(rank < NSA_TOP_N, 1.0, 0.0).astype(BF16)
        key_ok = _dot(expand_ref[:n_keys * QB, :], sel_t)

        state = init
        for j in range(n_keys):
            nxt = slc_scores(j + 1) if j + 1 < n_keys else None
            km = (key_ok[j * QB:(j + 1) * QB] - 1.0) * (-NEG)
            s = cur + bslc_ref[jnp.maximum(i - j + 1, 0)] + jnp.concatenate([km] * NSA_REPS, axis=1)
            state = _flash_step_t(s, *state, vst_ref[j])
            cur = nxt
        o_slc = state[2] * (1.0 / state[1])

        gate = jax.nn.sigmoid(z_ref[pl.ds(r0, QB), _OFF_GATE:_OFF_GATE + QB]).T
        outs = []
        for r in range(NSA_REPS):
            qsl = slice(r * QB, (r + 1) * QB)
            outs.append(gate[3 * r:3 * r + 1] * o_cmp[:, qsl] + gate[3 * r + 1:3 * r + 2] * o_slc[:, qsl]
                        + gate[3 * r + 2:3 * r + 3] * o_win[:, qsl])
        o_ref[pl.ds(r0, QB), :] = jnp.concatenate(outs, axis=0).T.astype(o_ref.dtype)
        return carry

    per_tier = n_chunks // NSA_TIERS
    for tier in range(NSA_TIERS):
        lo, hi = tier * per_tier, (tier + 1) * per_tier
        lax.fori_loop(lo, hi, functools.partial(q_block, hi, lo < NSA_WIN_PREV), 0)


def _nsa(z_nsa, zc, w1, w2, w2vt, pos, gq, gk, ovt, expand, bslc, bwin, batch, seq):
    gw = NSA_GROUP_COLS
    n_chunks = seq // QB
    return pl.pallas_call(
        functools.partial(_nsa_kernel, seq),
        grid=(batch, NSA_KV_HEADS),
        in_specs=[pl.BlockSpec((None, seq, gw), lambda b, g: (b, 0, g)),
                  pl.BlockSpec((None, None) + zc.shape[2:], lambda b, g: (b, g, 0, 0, 0)),
                  _resident(w1.shape), _resident(w2.shape), _resident(w2vt.shape), _resident(pos.shape),
                  _resident(gq.shape), _resident(gk.shape), _resident(ovt.shape), _resident(expand.shape),
                  pl.BlockSpec((None,) + bslc.shape[1:], lambda b, g: (g, 0, 0, 0)),
                  pl.BlockSpec((None,) + bwin.shape[1:], lambda b, g: (g, 0, 0))],
        out_specs=pl.BlockSpec((None, seq, _NQ), lambda b, g: (b, 0, g)),
        out_shape=jax.ShapeDtypeStruct((batch, seq, NSA_HEADS * HEAD_DIM), BF16),
        scratch_shapes=[pltpu.VMEM((seq, HEAD_DIM), BF16),
                        pltpu.VMEM((n_chunks, HEAD_DIM, QB), BF16),
                        pltpu.VMEM((NSA_WIN_PREV * QB + seq, HEAD_DIM), BF16),
                        pltpu.VMEM((NSA_WIN_PREV + n_chunks, HEAD_DIM, QB), BF16)],
        compiler_params=_cparams(2),
        name="nsa",
    )(z_nsa, zc, w1, w2, w2vt, pos, gq, gk, ovt, expand, bslc, bwin)


def _t5_bucket(dist):
    n = jnp.maximum(dist, 0)
    max_exact = NUM_BUCKETS // 2
    nf = jnp.maximum(n, 1).astype(F32)
    large = max_exact + (jnp.log(nf / max_exact) / math.log(MAX_DISTANCE / max_exact)
                         * (NUM_BUCKETS - max_exact)).astype(jnp.int32)
    large = jnp.minimum(large, NUM_BUCKETS - 1)
    return jnp.where(n < max_exact, n, large)


def _toeplitz(u, rows, cols):
    lead = u.shape[:-1]
    lu = rows + cols - 1
    assert u.shape[-1] == lu
    padded = jnp.pad(u, [(0, 0)] * len(lead) + [(0, 1)])
    flat = jnp.broadcast_to(padded[..., None, :], lead + (rows, lu + 1)).reshape(lead + (rows * (lu + 1),))
    return flat[..., :rows * lu].reshape(lead + (rows, lu))[..., rows - 1:]


def _bias_by_distance(bias_cols, delta, valid, dist_scale=1):
    bucket = _t5_bucket(jnp.asarray(np.maximum(delta, 0) * dist_scale, dtype=jnp.int32))
    return jnp.where(jnp.asarray(valid)[None, :], bias_cols.astype(F32)[bucket].T, NEG)


def _band_bias(bias_cols, window, n_prev, dist_scale, n_kv, reps):
    kw = (n_prev + 1) * QB
    delta = n_prev * QB + QB - 1 - np.arange(kw + QB - 1)
    u = _bias_by_distance(bias_cols, delta, (delta >= 0) & (delta < window), dist_scale)
    return _toeplitz(u, QB, kw).reshape(n_kv, reps * QB, kw)


def _slc_bias_t(bias_cols, seq):
    n_chunks = seq // QB
    delta = seq - 1 - np.arange(seq + QB - 1)
    strip = _toeplitz(_bias_by_distance(bias_cols, delta, delta >= 0), QB, seq)
    tile = jnp.flip(strip.reshape(NSA_KV_HEADS, NSA_REPS, QB, n_chunks, QB), axis=3)
    tile = jnp.transpose(tile, (0, 3, 4, 1, 2)).reshape(NSA_KV_HEADS, n_chunks, QB, NSA_REPS * QB)
    return jnp.concatenate([jnp.full_like(tile[:, :1], NEG), tile], axis=1)


def _rope_tables(seq):
    half = MLA_ROPE // 2
    inv = jnp.power(ROPE_THETA, -jnp.arange(half, dtype=F32) / half)
    ang = jnp.arange(seq, dtype=F32)[:, None] * inv[None, :]
    cos, sin = jnp.cos(ang), jnp.sin(ang)
    zeros = lambda w: jnp.zeros((seq, w), F32)
    tail = MLA_PAD - MLA_QK_DIM
    rope_c = jnp.concatenate([jnp.ones((seq, MLA_NOPE), F32), cos, cos, zeros(tail)], axis=1)
    rope_m = jnp.concatenate([zeros(MLA_NOPE), -sin, zeros(half), zeros(tail)], axis=1)
    rope_p = jnp.concatenate([zeros(MLA_NOPE), zeros(half), sin, zeros(tail)], axis=1)
    return rope_c, rope_m, rope_p


def _pad_cols(w, width):
    return jnp.pad(w, ((0, 0), (0, width - w.shape[1])))


def _nsa_column_order():
    g_cols = NSA_KV_HEADS * HEAD_DIM
    q_cols = NSA_HEADS * HEAD_DIM
    order = []
    for g in range(NSA_KV_HEADS):
        cols = list(range(g * _NQ, (g + 1) * _NQ))
        for part in range(6):
            start = q_cols + part * g_cols + g * HEAD_DIM
            cols += list(range(start, start + HEAD_DIM))
        gate0 = q_cols + 6 * g_cols + g * NSA_REPS * 3
        cols += list(range(gate0, gate0 + NSA_REPS * 3))
        order.append(cols)
    return order


def _nsa_cmp_columns():
    q_cols = NSA_HEADS * HEAD_DIM
    g_cols = NSA_KV_HEADS * HEAD_DIM
    cols = []
    for g in range(NSA_KV_HEADS):
        for part in range(2):
            start = q_cols + part * g_cols + g * HEAD_DIM
            cols += list(range(start, start + HEAD_DIM))
    return cols


def _nsa_mixer(z_nsa3, z_cmp, rel_bias, q_norm, k_norm, cmp_pos, cmp_w1, cmp_w2, batch, seq):
    rows16 = seq // NSA_CMP_STRIDE
    zc = z_cmp.reshape(batch, rows16, NSA_CMP_STRIDE, NSA_KV_HEADS, 2, HEAD_DIM)
    zc = jnp.transpose(zc, (0, 3, 4, 1, 2, 5)).reshape(batch, NSA_KV_HEADS, 2, rows16, NSA_CMP_STRIDE * HEAD_DIM)
    n_cmp = (seq - NSA_CMP_LEN) // NSA_CMP_STRIDE + 1
    ci = np.arange(NSA_N_CMP_PAD)[:, None] * NSA_CMP_STRIDE
    sj = np.arange(NSA_N_SLC)[None, :] * NSA_SLC_BLOCK
    overlap = ((ci < sj + NSA_SLC_BLOCK) & (ci + NSA_CMP_LEN > sj) & (np.arange(NSA_N_CMP_PAD)[:, None] < n_cmp))
    ovt = jnp.asarray(overlap.T.astype(np.float32))
    expand = jnp.asarray((np.arange(seq)[:, None] // NSA_SLC_BLOCK == np.arange(NSA_N_SLC)[None, :])
                         .astype(np.float32)).astype(BF16)
    nsa_cols = rel_bias[:, NSA_BIAS_COL0:NSA_BIAS_COL0 + NSA_HEADS]
    bslc = _slc_bias_t(nsa_cols, seq)
    bwin = jnp.transpose(_band_bias(nsa_cols, NSA_WINDOW, NSA_WIN_PREV, 1, NSA_KV_HEADS, NSA_REPS), (0, 2, 1))
    return _nsa(z_nsa3, zc, cmp_w1, cmp_w2, cmp_w2[1].T, cmp_pos.reshape(2, 1, NSA_CMP_LEN * HEAD_DIM),
                q_norm.reshape(HEAD_DIM, 1), k_norm.reshape(3, 1, HEAD_DIM), ovt, expand, bslc, bwin, batch, seq)


def kernel(x, rel_bias, ffn1_norm, ffn1_w_gate, ffn1_w_up, ffn1_w_down, mix_norm, ffn2_norm, ffn2_w_gate,
           ffn2_w_up, ffn2_w_down, ab_w_in, mla_q_a_norm, mla_w_q_b, mla_kv_a_norm, mla_w_kv_b, mla_q_norm,
           mla_k_norm, dil_q_norm, dil_k_norm, ab_w_out, cd_w_in, swa_q_norm, swa_k_norm, swa_sinks,
           nsa_q_norm, nsa_k_norm, nsa_cmp_pos, nsa_cmp_w1, nsa_cmp_w2, cd_w_out):
    batch, seq, _ = x.shape
    n = batch * seq
    assert seq % (16 * QB) == 0 and n % TM == 0 and seq % TM == 0
    bf = lambda a: a.astype(BF16)
    xf = x.reshape(n, D_MODEL)

    xf = _ffn(xf, ffn1_norm[0], bf(ffn1_w_gate[0]), bf(ffn1_w_up[0]), bf(ffn1_w_down[0]))

    w_in = ab_w_in[0]
    mla_cols = MLA_Q_RANK + MLA_KV_RANK
    w_krope = jnp.pad(w_in[:, mla_cols:mla_cols + MLA_ROPE], ((0, 0), (MLA_NOPE, MLA_PAD - MLA_QK_DIM)))
    w_mla = jnp.concatenate([w_in[:, :mla_cols], w_krope], axis=1)
    z_mla, z_dil = _norm_proj(xf, mix_norm[0], [bf(w_mla), bf(w_in[:, mla_cols + MLA_ROPE:])])

    wq = _pad_cols(mla_w_q_b[0].reshape(MLA_Q_RANK * MLA_HEADS, MLA_QK_DIM), MLA_PAD)
    wq = wq.reshape(MLA_Q_RANK, MLA_HEADS * MLA_PAD)
    wkv = mla_w_kv_b[0].reshape(MLA_KV_RANK, MLA_HEADS, MLA_NOPE + MLA_V)
    wk = _pad_cols(wkv[:, :, :MLA_NOPE].reshape(MLA_KV_RANK * MLA_HEADS, MLA_NOPE), MLA_PAD)
    wk = wk.reshape(MLA_KV_RANK, MLA_HEADS * MLA_PAD)
    wv = wkv[:, :, MLA_NOPE:].reshape(MLA_KV_RANK, MLA_HEADS * MLA_V)
    rope_c, rope_m, rope_p = _rope_tables(seq)
    q_mla, k_mla, vt_mla = _mla_prep(
        z_mla, mla_q_a_norm[0].reshape(1, -1), mla_kv_a_norm[0].reshape(1, -1), bf(wq), bf(wk), bf(wv),
        _pad_cols(mla_q_norm[0].reshape(1, -1), MLA_PAD), _pad_cols(mla_k_norm[0].reshape(1, -1), MLA_PAD),
        rope_c, rope_m, rope_p, seq)
    o_a = _mla_attn(q_mla.reshape(batch, seq, -1), k_mla.reshape(batch, seq, -1),
                    vt_mla.reshape(batch, seq // MLA_TQ, MLA_HEADS * MLA_V, MLA_TQ),
                    batch, seq).reshape(n, MLA_HEADS * MLA_V)

    z_dil3 = z_dil.reshape(batch, seq, -1)
    gq, gk = dil_q_norm[0].reshape(HEAD_DIM, 1), dil_k_norm[0].reshape(1, HEAD_DIM)
    dil_outs, dil_lses = [], []
    for grp, (window, dilation) in enumerate(DIL_PATTERNS):
        bm = _band_bias(rel_bias[:, grp * DIL_HPG:(grp + 1) * DIL_HPG], window // dilation + 1, 1, dilation,
                        DIL_HPG, 1)
        o, lse = _dilated_group(z_dil3, batch, seq, grp, dilation, gq, gk, bm)
        dil_outs.append(o)
        dil_lses.append(lse)
    w_out = ab_w_out[0]
    xf = _out_ab(xf, o_a, dil_outs, dil_lses, bf(w_out[:MLA_HEADS * MLA_V]), bf(w_out[MLA_HEADS * MLA_V:]))
    xf = _ffn(xf, ffn2_norm[0], bf(ffn2_w_gate[0]), bf(ffn2_w_up[0]), bf(ffn2_w_down[0]))

    xf = _ffn(xf, ffn1_norm[1], bf(ffn1_w_gate[1]), bf(ffn1_w_up[1]), bf(ffn1_w_down[1]))
    w_in = cd_w_in[0]
    swa_cols = (SWA_HEADS + 2 * SWA_KV_HEADS) * HEAD_DIM
    w_nsa_src = w_in[:, swa_cols:]
    w_nsa = jnp.concatenate([_pad_cols(w_nsa_src[:, np.asarray(cols)], NSA_GROUP_COLS)
                             for cols in _nsa_column_order()], axis=1)
    w_cmp = w_nsa_src[:, np.asarray(_nsa_cmp_columns())]
    z_swa, z_nsa, z_cmp = _norm_proj(xf, mix_norm[1], [bf(w_in[:, :swa_cols]), bf(w_nsa), bf(w_cmp)])

    swa_reps = SWA_HEADS // SWA_KV_HEADS
    swa_prev = -(-(SWA_WINDOW - 1) // QB)
    bm_swa = _band_bias(rel_bias[:, :SWA_HEADS], SWA_WINDOW, swa_prev, 1, SWA_KV_HEADS, swa_reps)
    sinks = jnp.broadcast_to(swa_sinks[0].astype(F32).reshape(SWA_KV_HEADS, 1, swa_reps, 1),
                             (SWA_KV_HEADS, 1, swa_reps, QB)).reshape(SWA_KV_HEADS, 1, swa_reps * QB)
    o_c = _swa(z_swa.reshape(batch, seq, -1), batch, seq, swa_q_norm[0].reshape(HEAD_DIM, 1),
               swa_k_norm[0].reshape(1, HEAD_DIM), bm_swa, sinks)

    o_d = _nsa_mixer(z_nsa.reshape(batch, seq, NSA_KV_HEADS * NSA_GROUP_COLS), z_cmp, rel_bias, nsa_q_norm[0],
                     nsa_k_norm[0], nsa_cmp_pos[0], nsa_cmp_w1[0], nsa_cmp_w2[0], batch, seq)

    w_out = cd_w_out[0]
    xf = _out_cd(xf, o_c.reshape(n, -1), o_d.reshape(n, -1), bf(w_out[:SWA_HEADS * HEAD_DIM]),
                 bf(w_out[SWA_HEADS * HEAD_DIM:]))
    xf = _ffn(xf, ffn2_norm[1], bf(ffn2_w_gate[1]), bf(ffn2_w_up[1]), bf(ffn2_w_down[1]))
    return xf.reshape(batch, seq, D_MODEL)
```

```python
import functools
import math

import numpy as np
import jax
import jax.numpy as jnp
from jax import lax
from jax.experimental import pallas as pl
from jax.experimental.pallas import tpu as pltpu

F32 = jnp.float32
BF16 = jnp.bfloat16

EPS = 1e-6
NEG = -1e30
TINY = 1e-30
D_MODEL = 1024
D_FF = 2816
NUM_BUCKETS = 32
MAX_DISTANCE = 2048
HEAD_DIM = 64
QB = 128

MLA_HEADS = 8
MLA_Q_RANK = 256
MLA_KV_RANK = 128
MLA_NOPE = 64
MLA_ROPE = 32
MLA_V = 64
MLA_QK_DIM = MLA_NOPE + MLA_ROPE
MLA_PAD = 128
ROPE_THETA = 10000.0

DIL_PATTERNS = ((128, 1), (512, 4), (2048, 16))
DIL_HPG = 4
SWA_HEADS = 8
SWA_KV_HEADS = 2
SWA_WINDOW = 128
NSA_HEADS = 8
NSA_KV_HEADS = 2
NSA_CMP_LEN = 32
NSA_CMP_STRIDE = 16
NSA_CMP_HIDDEN = 128
NSA_SLC_BLOCK = 64
NSA_TOP_N = 16
NSA_WINDOW = 512
NSA_FORCE = 1e6
NSA_BIAS_COL0 = 8
NSA_GROUP_COLS = 768

VMEM_LIMIT = 56 * 1024 * 1024
TM = 512

_NT = (((1,), (1,)), ((), ()))
_HI = lax.Precision.HIGHEST


def _cparams(n_axes):
    return pltpu.CompilerParams(dimension_semantics=("arbitrary",) * n_axes,
                                vmem_limit_bytes=VMEM_LIMIT)


def _resident(shape):
    nd = len(shape)
    return pl.BlockSpec(shape, lambda *_: (0,) * nd, pipeline_mode=pl.Buffered(1))


def _rms(x):
    return x * lax.rsqrt(jnp.mean(x * x, axis=-1, keepdims=True) + EPS)


def _dot(a, b):
    return jnp.dot(a, b, preferred_element_type=F32)


def _dot_nt(a, b, precision=None):
    return lax.dot_general(a, b, _NT, preferred_element_type=F32, precision=precision)


FF_CHUNK = 256


N_CHUNK = 512


def _swiglu_half_step(x, g_ref, wg_ref, wu_ref, wd_ref):
    hb = (_rms(x) * g_ref[...]).astype(BF16)
    acc = jnp.zeros(x.shape, F32)
    for c in range(D_FF // FF_CHUNK):
        sl = slice(c * FF_CHUNK, (c + 1) * FF_CHUNK)
        gate = _dot(hb, wg_ref[:, sl])
        up = _dot(hb, wu_ref[:, sl])
        act = (gate * jax.nn.sigmoid(gate) * up).astype(BF16)
        acc = acc + _dot(act, wd_ref[sl, :])
    return x + 0.5 * acc


def _ffn_proj_kernel(n_out, x_ref, g_ref, wg_ref, wu_ref, wd_ref, gm_ref, *refs):
    w_refs, x_out_ref, z_refs = refs[:n_out], refs[n_out], refs[n_out + 1:]
    x = _swiglu_half_step(x_ref[...], g_ref, wg_ref, wu_ref, wd_ref)
    x_out_ref[...] = x
    hb = (_rms(x) * gm_ref[...]).astype(BF16)
    for w_ref, z_ref in zip(w_refs, z_refs):
        width = w_ref.shape[1]
        for c0 in range(0, width, N_CHUNK):
            sl = slice(c0, min(c0 + N_CHUNK, width))
            z_ref[:, sl] = _dot(hb, w_ref[:, sl])


def _tile(width):
    return pl.BlockSpec((TM, width), lambda i: (i, 0))


def _ffn_proj(x, g, wg, wu, wd, g_mix, ws):
    n = x.shape[0]
    ffn_w = [_resident((D_MODEL, D_FF)), _resident((D_MODEL, D_FF)), _resident((D_FF, D_MODEL))]
    return pl.pallas_call(
        functools.partial(_ffn_proj_kernel, len(ws)),
        grid=(n // TM,),
        in_specs=[_tile(D_MODEL), _resident((1, D_MODEL))] + ffn_w + [_resident((1, D_MODEL))]
        + [_resident(w.shape) for w in ws],
        out_specs=[_tile(D_MODEL)] + [_tile(w.shape[1]) for w in ws],
        out_shape=[jax.ShapeDtypeStruct((n, D_MODEL), F32)]
        + [jax.ShapeDtypeStruct((n, w.shape[1]), F32) for w in ws],
        compiler_params=_cparams(1),
        name="ffn_proj",
    )(x, g.reshape(1, D_MODEL), wg, wu, wd, g_mix.reshape(1, D_MODEL), *ws)


def _out_ab_ffn_kernel(x_ref, oa_ref, o0_ref, o1_ref, o2_ref, l0_ref, l1_ref, l2_ref, wa_ref, wb_ref,
                       g_ref, wg_ref, wu_ref, wd_ref, out_ref):
    l0, l1, l2 = l0_ref[...], l1_ref[...], l2_ref[...]
    m = jnp.maximum(jnp.maximum(l0, l1), l2)
    e0, e1, e2 = jnp.exp(l0 - m), jnp.exp(l1 - m), jnp.exp(l2 - m)
    ob = (e0 * o0_ref[...] + e1 * o1_ref[...] + e2 * o2_ref[...]) / (e0 + e1 + e2)
    x = x_ref[...] + _dot(oa_ref[...], wa_ref[...]) + _dot(ob.astype(BF16), wb_ref[...])
    out_ref[...] = _swiglu_half_step(x, g_ref, wg_ref, wu_ref, wd_ref)


def _out_ab_ffn(x, oa, dil_outs, dil_lses, wa, wb, g, wg, wu, wd):
    n = x.shape[0]
    dw = DIL_HPG * HEAD_DIM
    ffn_w = [_resident((D_MODEL, D_FF)), _resident((D_MODEL, D_FF)), _resident((D_FF, D_MODEL))]
    return pl.pallas_call(
        _out_ab_ffn_kernel,
        grid=(n // TM,),
        in_specs=[_tile(D_MODEL), _tile(oa.shape[1])] + [_tile(dw)] * 6 + [_resident(wa.shape), _resident(wb.shape),
                                                                        _resident((1, D_MODEL))] + ffn_w,
        out_specs=_tile(D_MODEL),
        out_shape=jax.ShapeDtypeStruct((n, D_MODEL), F32),
        compiler_params=_cparams(1),
        name="out_ab_ffn",
    )(x, oa, *dil_outs, *dil_lses, wa, wb, g.reshape(1, D_MODEL), wg, wu, wd)


def _out_cd_ffn_kernel(x_ref, oc_ref, od_ref, wc_ref, wdd_ref, g_ref, wg_ref, wu_ref, wd_ref, out_ref):
    x = x_ref[...] + _dot(oc_ref[...], wc_ref[...]) + _dot(od_ref[...], wdd_ref[...])
    out_ref[...] = _swiglu_half_step(x, g_ref, wg_ref, wu_ref, wd_ref)


def _out_cd_ffn(x, oc, od, wc, wdd, g, wg, wu, wd):
    n = x.shape[0]
    ffn_w = [_resident((D_MODEL, D_FF)), _resident((D_MODEL, D_FF)), _resident((D_FF, D_MODEL))]
    return pl.pallas_call(
        _out_cd_ffn_kernel,
        grid=(n // TM,),
        in_specs=[_tile(D_MODEL), _tile(oc.shape[1]), _tile(od.shape[1]), _resident(wc.shape), _resident(wdd.shape),
                  _resident((1, D_MODEL))] + ffn_w,
        out_specs=_tile(D_MODEL),
        out_shape=jax.ShapeDtypeStruct((n, D_MODEL), F32),
        compiler_params=_cparams(1),
        name="out_cd_ffn",
    )(x, oc, od, wc, wdd, g.reshape(1, D_MODEL), wg, wu, wd)


def _mla_prep_kernel(z_ref, qan_ref, kvan_ref, wq_ref, wk_ref, wv_ref, gq_ref, gk_ref,
                     rc_ref, rm_ref, rp_ref, q_ref, k_ref, vt_ref):
    cq = (_rms(z_ref[:, :MLA_Q_RANK]) * qan_ref[...]).astype(BF16)
    ckv = (_rms(z_ref[:, MLA_Q_RANK:MLA_Q_RANK + MLA_KV_RANK]) * kvan_ref[...]).astype(BF16)
    kr = z_ref[:, MLA_Q_RANK + MLA_KV_RANK:]
    rc, rm, rp = rc_ref[...], rm_ref[...], rp_ref[...]
    scale = MLA_QK_DIM ** -0.5

    def head_norm_rope(x, gain):
        x = x * lax.rsqrt(jnp.sum(x * x, axis=-1, keepdims=True) * (1.0 / MLA_QK_DIM) + EPS) * gain
        half = MLA_ROPE // 2
        return x * rc + pltpu.roll(x, MLA_PAD - half, 1) * rm + pltpu.roll(x, half, 1) * rp

    for h in range(MLA_HEADS):
        sl = slice(h * MLA_PAD, (h + 1) * MLA_PAD)
        qh = head_norm_rope(_dot(cq, wq_ref[:, sl]), gq_ref[...])
        q_ref[:, sl] = (qh * scale).astype(BF16)
        kh = head_norm_rope(_dot(ckv, wk_ref[:, sl]) + kr, gk_ref[...])
        k_ref[:, sl] = kh.astype(BF16)
    v = _dot(ckv, wv_ref[...])
    for c in range(TM // MLA_TQ):
        vt_ref[c] = v[c * MLA_TQ:(c + 1) * MLA_TQ].T.astype(BF16)


MLA_TQ = 256


def _mla_prep(z, qan, kvan, wq, wk, wv, gq, gk, rope_c, rope_m, rope_p, seq):
    n = z.shape[0]
    per_seq = seq // TM
    tile = lambda w: pl.BlockSpec((TM, w), lambda i: (i, 0))
    rope_spec = pl.BlockSpec((TM, MLA_PAD), lambda i: (i % per_seq, 0))
    hw = MLA_HEADS * MLA_PAD
    vw = MLA_HEADS * MLA_V
    return pl.pallas_call(
        _mla_prep_kernel,
        grid=(n // TM,),
        in_specs=[tile(z.shape[1]), _resident(qan.shape), _resident(kvan.shape), _resident(wq.shape),
                  _resident(wk.shape), _resident(wv.shape), _resident(gq.shape), _resident(gk.shape),
                  rope_spec, rope_spec, rope_spec],
        out_specs=[tile(hw), tile(hw), pl.BlockSpec((TM // MLA_TQ, vw, MLA_TQ), lambda i: (i, 0, 0))],
        out_shape=[jax.ShapeDtypeStruct((n, hw), BF16), jax.ShapeDtypeStruct((n, hw), BF16),
                   jax.ShapeDtypeStruct((n // MLA_TQ, vw, MLA_TQ), BF16)],
        compiler_params=_cparams(1),
        name="mla_prep",
    )(z, qan, kvan, wq, wk, wv, gq, gk, rope_c, rope_m, rope_p)


def _flash_step_t(s, m, l, acc, vt):
    m_new = jnp.maximum(m, jnp.max(s, axis=0, keepdims=True))
    p = jnp.exp(s - m_new)
    alpha = jnp.exp(m - m_new)
    l = alpha * l + jnp.sum(p, axis=0, keepdims=True)
    acc = alpha * acc + _dot(vt, p.astype(BF16))
    return m_new, l, acc


def _mla_attn_kernel(seq, q_ref, k_ref, vt_ref, o_ref):
    t = MLA_TQ
    key = lax.broadcasted_iota(jnp.int32, (t, t), 0)
    qry = lax.broadcasted_iota(jnp.int32, (t, t), 1)
    diag_ok = key <= qry
    heads = [(slice(hh * MLA_PAD, (hh + 1) * MLA_PAD), slice(hh * MLA_V, (hh + 1) * MLA_V)) for hh in range(2)]

    init = (jnp.full((1, t), NEG, F32), jnp.zeros((1, t), F32), jnp.zeros((MLA_V, t), F32))
    for i in range(seq // t):
        rows = slice(i * t, (i + 1) * t)
        qs = [q_ref[rows, ql] for ql, _ in heads]

        def scores(h, j):
            s = _dot_nt(k_ref[j * t:(j + 1) * t, heads[h][0]], qs[h])
            return jnp.where(diag_ok, s, NEG) if j == i else s

        cur = [scores(h, 0) for h in range(2)]
        states = [init, init]
        for j in range(i + 1):
            for h in range(2):
                nxt = scores(h, j + 1) if j < i else None
                states[h] = _flash_step_t(cur[h], *states[h], vt_ref[j, heads[h][1], :])
                cur[h] = nxt
        outs = [acc * (1.0 / l) for _, l, acc in states]
        o_ref[rows, :] = jnp.concatenate(outs, axis=0).T.astype(o_ref.dtype)


def _mla_attn(q, k, vt, batch, seq):
    return pl.pallas_call(
        functools.partial(_mla_attn_kernel, seq),
        grid=(batch, MLA_HEADS // 2),
        in_specs=[pl.BlockSpec((None, seq, 2 * MLA_PAD), lambda b, h: (b, 0, h)),
                  pl.BlockSpec((None, seq, 2 * MLA_PAD), lambda b, h: (b, 0, h)),
                  pl.BlockSpec((None, seq // MLA_TQ, 2 * MLA_V, MLA_TQ), lambda b, h: (b, 0, h, 0))],
        out_specs=pl.BlockSpec((None, seq, 2 * MLA_V), lambda b, h: (b, 0, h)),
        out_shape=jax.ShapeDtypeStruct((batch, seq, MLA_HEADS * MLA_V), BF16),
        compiler_params=_cparams(2),
        name="mla_attn",
    )(q, k, vt)


def _head_norm(x, gain):
    return _rms(x) * gain


def _heads_t(qt, n_heads, gain):
    tiles = []
    for c0 in range(0, n_heads * HEAD_DIM, 2 * HEAD_DIM):
        pair = qt[:, c0:c0 + 2 * HEAD_DIM].T
        for h in range(2):
            x = pair[h * HEAD_DIM:(h + 1) * HEAD_DIM]
            tiles.append(x * lax.rsqrt(jnp.mean(x * x, axis=0, keepdims=True) + EPS) * gain)
    return tiles


def _q_gain(gq_ref):
    return jnp.broadcast_to(gq_ref[...] * (HEAD_DIM ** -0.5), (HEAD_DIM, QB))


BAND_KV = 2


def _banded_kernel(seq, dilation, reps, n_prev, with_sink, with_lse, group_size, *refs):
    q_ref, k_ref, v_ref, gq_ref, gk_ref, bm_ref = refs[:6]
    refs = refs[6:]
    sink_ref = None
    if with_sink:
        sink_ref, refs = refs[0], refs[1:]
    o_ref, refs = refs[0], refs[1:]
    lse_ref = None
    if with_lse:
        lse_ref, refs = refs[0], refs[1:]
    kn_ref, vt_ref = refs
    length = seq // dilation
    n_blocks = length // QB
    pad = n_prev * QB
    kw = pad + QB
    gq, gk = _q_gain(gq_ref), gk_ref[...]
    win_row = lax.broadcasted_iota(jnp.int32, (kw, 1), 0)

    def rows(start, size):
        return pl.ds(start, size) if dilation == 1 else pl.ds(start, size, stride=dilation)

    for res in range(dilation):
        kall, vall = k_ref[rows(res, length), :], v_ref[rows(res, length), :]
        for g in range(BAND_KV):
            kn_ref[res, g, :pad, :] = jnp.zeros((pad, HEAD_DIM), BF16)
            kn_ref[res, g, pad:, :] = _head_norm(kall[:, g * HEAD_DIM:(g + 1) * HEAD_DIM], gk).astype(BF16)
        for j in range(n_prev):
            vt_ref[res, j] = jnp.zeros((BAND_KV * HEAD_DIM, QB), BF16)
        for j in range(n_blocks):
            vt_ref[res, n_prev + j] = vall[j * QB:(j + 1) * QB].T.astype(BF16)

    def attend(res, i, g, s):
        bias = bm_ref[g]
        if i < n_prev:
            bias = jnp.where(win_row >= (n_prev - i) * QB, bias, NEG)
        s = s + bias
        m = jnp.max(s, axis=0, keepdims=True)
        if with_sink:
            m = jnp.maximum(m, sink_ref[g])
        p = jnp.exp(s - m)
        l = jnp.sum(p, axis=0, keepdims=True)
        if with_sink:
            l = l + jnp.exp(sink_ref[g] - m)
        p = p.astype(BF16)
        vl = slice(g * HEAD_DIM, (g + 1) * HEAD_DIM)
        o = _dot(vt_ref[res, i, vl, :], p[0:QB])
        for w in range(1, n_prev + 1):
            o = o + _dot(vt_ref[res, i + w, vl, :], p[w * QB:(w + 1) * QB])
        return o * (1.0 / l), m + jnp.log(l)

    items = [(res, i) for res in range(dilation) for i in range(n_blocks)]
    for at in range(0, len(items), group_size):
        group = items[at:at + group_size]
        toks = [rows(res + i * QB * dilation, QB) for res, i in group]
        heads = [_heads_t(q_ref[tok, :], BAND_KV * reps, gq) for tok in toks]
        qss = [[jnp.concatenate(h[g * reps:(g + 1) * reps], axis=1).astype(BF16) for g in range(BAND_KV)]
               for h in heads]
        scores = [[_dot(kn_ref[res, g, i * QB:i * QB + kw, :], qs[g]) for g in range(BAND_KV)]
                  for (res, i), qs in zip(group, qss)]
        for (res, i), tok, sc in zip(group, toks, scores):
            outs, lses = [], []
            for g in range(BAND_KV):
                o, lse = attend(res, i, g, sc[g])
                for r in range(reps):
                    outs.append(o[:, r * QB:(r + 1) * QB])
                    if with_lse:
                        lses.append(jnp.broadcast_to(lse[:, r * QB:(r + 1) * QB], (HEAD_DIM, QB)))
            o_ref[tok, :] = jnp.concatenate(outs, axis=0).T.astype(o_ref.dtype)
            if with_lse:
                lse_ref[tok, :] = jnp.concatenate(lses, axis=0).T


def _banded_scratch(seq, dilation, n_prev):
    length = seq // dilation
    return [pltpu.VMEM((dilation, BAND_KV, n_prev * QB + length, HEAD_DIM), BF16),
            pltpu.VMEM((dilation, n_prev + length // QB, BAND_KV * HEAD_DIM, QB), BF16)]


def _dilated_group(z_dil, batch, seq, group, dilation, gq, gk, bm):
    gw = DIL_HPG * HEAD_DIM
    n_groups = len(DIL_PATTERNS)
    hps = BAND_KV
    halves = DIL_HPG // hps
    bm_t = jnp.transpose(bm, (0, 2, 1))

    def spec(part):
        return pl.BlockSpec((None, seq, hps * HEAD_DIM), lambda b, h: (b, 0, (part * n_groups + group) * halves + h))

    out_spec = pl.BlockSpec((None, seq, hps * HEAD_DIM), lambda b, h: (b, 0, h))
    shape = jax.ShapeDtypeStruct((batch, seq, gw), F32)
    out, lse = pl.pallas_call(
        functools.partial(_banded_kernel, seq, dilation, 1, 1, False, True, 4),
        grid=(batch, halves),
        in_specs=[spec(0), spec(1), spec(2), _resident(gq.shape), _resident(gk.shape),
                  pl.BlockSpec((hps,) + bm_t.shape[1:], lambda b, h: (h, 0, 0))],
        out_specs=[out_spec, out_spec],
        out_shape=[shape, shape],
        scratch_shapes=_banded_scratch(seq, dilation, 1),
        compiler_params=_cparams(2),
        name="dilated_g%d" % group,
    )(z_dil, z_dil, z_dil, gq, gk, bm_t)
    return out.reshape(batch * seq, gw), lse.reshape(batch * seq, gw)


def _swa(z_swa, batch, seq, gq, gk, bm, sinks):
    qw = SWA_HEADS * HEAD_DIM
    kvw = SWA_KV_HEADS * HEAD_DIM
    reps = SWA_HEADS // SWA_KV_HEADS
    n_prev = -(-(SWA_WINDOW - 1) // QB)
    assert SWA_KV_HEADS == BAND_KV
    bm_t = jnp.transpose(bm, (0, 2, 1))
    return pl.pallas_call(
        functools.partial(_banded_kernel, seq, 1, reps, n_prev, True, False, 2),
        grid=(batch,),
        in_specs=[pl.BlockSpec((None, seq, qw), lambda b: (b, 0, 0)),
                  pl.BlockSpec((None, seq, kvw), lambda b: (b, 0, qw // kvw)),
                  pl.BlockSpec((None, seq, kvw), lambda b: (b, 0, qw // kvw + 1)),
                  _resident(gq.shape), _resident(gk.shape), _resident(bm_t.shape), _resident(sinks.shape)],
        out_specs=pl.BlockSpec((None, seq, qw), lambda b: (b, 0, 0)),
        out_shape=jax.ShapeDtypeStruct((batch, seq, qw), BF16),
        scratch_shapes=_banded_scratch(seq, 1, n_prev),
        compiler_params=_cparams(1),
        name="swa",
    )(z_swa, z_swa, z_swa, gq, gk, bm_t, sinks)


NSA_REPS = NSA_HEADS // NSA_KV_HEADS
NSA_N_SLC = 32
NSA_N_CMP_PAD = 128
NSA_WIN_PREV = -(-(NSA_WINDOW - 1) // QB)
_NQ = NSA_REPS * HEAD_DIM
_OFF_KC, _OFF_VC, _OFF_KS, _OFF_VS, _OFF_KW, _OFF_VW, _OFF_GATE = (_NQ + i * HEAD_DIM for i in range(7))


NSA_TIERS = 4
NSA_STEP = 2


def _nsa_kernel(seq, z_ref, zc_ref, w1_ref, w2_ref, w2vt_ref, pos_ref, gq_ref, gk_ref, ovt_ref,
                bslc_ref, bwin_ref, o_ref, ks_ref, vst_ref, kw_ref, vwt_ref):
    n_chunks = seq // QB
    lanes = NSA_REPS * QB
    gq = _q_gain(gq_ref)
    win_pad = NSA_WIN_PREV * QB
    win_kw = win_pad + QB

    half = NSA_CMP_LEN * HEAD_DIM // 2

    def hidden(kv):
        x2 = zc_ref[kv].astype(BF16)
        first = _dot(x2, w1_ref[kv, :half, :])
        second = _dot(x2, w1_ref[kv, half:, :])
        pos_rows = jnp.broadcast_to(pos_ref[kv], (8, NSA_CMP_LEN * HEAD_DIM)).astype(BF16)
        pos_term = _dot(pos_rows, w1_ref[kv])[0:1]
        return jax.nn.gelu(first + pltpu.roll(second, NSA_N_CMP_PAD - 1, 0) + pos_term).astype(BF16)

    kc = _head_norm(_dot(hidden(0), w2_ref[0]), gk_ref[0]).astype(BF16)
    vct = _dot_nt(w2vt_ref[...], hidden(1)).astype(BF16)

    ks_ref[...] = _head_norm(z_ref[:, _OFF_KS:_OFF_KS + HEAD_DIM], gk_ref[1]).astype(BF16)
    kw_ref[:win_pad, :] = jnp.zeros((win_pad, HEAD_DIM), BF16)
    kw_ref[win_pad:, :] = _head_norm(z_ref[:, _OFF_KW:_OFF_KW + HEAD_DIM], gk_ref[2]).astype(BF16)
    for j in range(NSA_WIN_PREV):
        vwt_ref[j] = jnp.zeros((HEAD_DIM, QB), BF16)
    for j in range(n_chunks):
        rows = slice(j * QB, (j + 1) * QB)
        vst_ref[j // NSA_STEP, :, (j % NSA_STEP) * QB:(j % NSA_STEP + 1) * QB] = (
            z_ref[rows, _OFF_KS:_OFF_KS + 2 * HEAD_DIM].T[HEAD_DIM:].astype(BF16))
        vwt_ref[NSA_WIN_PREV + j] = z_ref[rows, _OFF_KW:_OFF_KW + 2 * HEAD_DIM].T[HEAD_DIM:].astype(BF16)

    cmp_id = lax.broadcasted_iota(jnp.int32, (NSA_N_CMP_PAD, lanes), 0)
    cmp_end = cmp_id * NSA_CMP_STRIDE + (NSA_CMP_LEN - 1)
    cmp_real = cmp_id < NSA_N_CMP_PAD - 1
    q_in_blk = lax.broadcasted_iota(jnp.int32, (NSA_N_CMP_PAD, lanes), 1) & (QB - 1)
    blk_id = lax.broadcasted_iota(jnp.int32, (NSA_N_SLC, QB), 0)
    q_lane = lax.broadcasted_iota(jnp.int32, (NSA_N_SLC, QB), 1)
    win_row = lax.broadcasted_iota(jnp.int32, (win_kw, 1), 0)
    init = (jnp.full((1, lanes), NEG, F32), jnp.zeros((1, lanes), F32), jnp.zeros((HEAD_DIM, lanes), F32))

    def q_block(n_keys, mask_pad, i, carry):
        r0 = pl.multiple_of(i * QB, QB)
        qf = jnp.concatenate(_heads_t(z_ref[pl.ds(r0, QB), :_NQ], NSA_REPS, gq), axis=1)
        qs = qf.astype(BF16)

        sc = _dot(kc, qs)
        s_win = _dot(kw_ref[pl.ds(r0, win_kw), :], qs)
        step_keys = NSA_STEP * QB

        def slc_scores(j):
            return _dot(ks_ref[j * step_keys:(j + 1) * step_keys, :], qs)

        cur = slc_scores(0)

        ok = (cmp_end <= q_in_blk + i * QB) & cmp_real
        sc = jnp.where(ok, sc, NEG)
        e = jnp.where(ok, jnp.exp(sc - jnp.max(sc, axis=0, keepdims=True)), 0.0)
        p = e * (1.0 / jnp.maximum(jnp.sum(e, axis=0, keepdims=True), TINY))
        o_cmp = _dot(vct, p.astype(BF16))

        p_sum = p[:, 0:QB] + p[:, QB:2 * QB] + p[:, 2 * QB:3 * QB] + p[:, 3 * QB:4 * QB]
        imp = jnp.dot(ovt_ref[...], p_sum, precision=_HI, preferred_element_type=F32)

        bias = bwin_ref[...]
        if mask_pad:
            bias = jnp.where(win_row >= (NSA_WIN_PREV - i) * QB, bias, NEG)
        s_win = s_win + bias
        pw = jnp.exp(s_win - jnp.max(s_win, axis=0, keepdims=True))
        lw = jnp.sum(pw, axis=0, keepdims=True)
        pw = pw.astype(BF16)
        o_win = _dot(vwt_ref[i], pw[0:QB])
        for c in range(1, NSA_WIN_PREV + 1):
            o_win = o_win + _dot(vwt_ref[i + c], pw[c * QB:(c + 1) * QB])
        o_win = o_win * (1.0 / lw)

        tb = lax.shift_right_logical(q_lane + i * QB, 6)
        forced = (blk_id == 0) | (blk_id == tb) | (blk_id == tb - 1)
        score = jnp.where(blk_id <= tb, imp + jnp.where(forced, NSA_FORCE, 0.0), -NSA_FORCE)
        rank = jnp.zeros((NSA_N_SLC, QB), F32)
        for other in range(NSA_N_SLC):
            s_o = score[other:other + 1, :]
            beats = (s_o > score) | ((s_o == score) & (blk_id > other))
            rank = rank + jnp.where(beats, 1.0, 0.0)
        drop = jnp.where(rank < NSA_TOP_N, 0.0, NEG)

        state = init
        blocks_per_step = step_keys // NSA_SLC_BLOCK
        for j in range(n_keys // NSA_STEP):
            nxt = slc_scores(j + 1) if (j + 1) * NSA_STEP < n_keys else None
            bias = jnp.concatenate([bslc_ref[jnp.maximum(i - (j * NSA_STEP + c) + 1, 0)] for c in range(NSA_STEP)],
                                   axis=0)
            km = jnp.concatenate([jnp.broadcast_to(drop[b:b + 1], (NSA_SLC_BLOCK, QB))
                                  for b in range(j * blocks_per_step, (j + 1) * blocks_per_step)], axis=0)
            s = cur + bias + jnp.concatenate([km] * NSA_REPS, axis=1)
            state = _flash_step_t(s, *state, vst_ref[j])
            cur = nxt
        o_slc = state[2] * (1.0 / state[1])

        gate = jax.nn.sigmoid(z_ref[pl.ds(r0, QB), _OFF_GATE:_OFF_GATE + QB]).T
        outs = []
        for r in range(NSA_REPS):
            qsl = slice(r * QB, (r + 1) * QB)
            outs.append(gate[3 * r:3 * r + 1] * o_cmp[:, qsl] + gate[3 * r + 1:3 * r + 2] * o_slc[:, qsl]
                        + gate[3 * r + 2:3 * r + 3] * o_win[:, qsl])
        o_ref[pl.ds(r0, QB), :] = jnp.concatenate(outs, axis=0).T.astype(o_ref.dtype)
        return carry

    per_tier = n_chunks // NSA_TIERS
    for tier in range(NSA_TIERS):
        lo, hi = tier * per_tier, (tier + 1) * per_tier
        lax.fori_loop(lo, hi, functools.partial(q_block, hi, lo < NSA_WIN_PREV), 0)


def _nsa(z_nsa, zc, w1, w2, w2vt, pos, gq, gk, ovt, bslc, bwin, batch, seq):
    gw = NSA_GROUP_COLS
    n_chunks = seq // QB
    return pl.pallas_call(
        functools.partial(_nsa_kernel, seq),
        grid=(batch, NSA_KV_HEADS),
        in_specs=[pl.BlockSpec((None, seq, gw), lambda b, g: (b, 0, g)),
                  pl.BlockSpec((None, None) + zc.shape[2:], lambda b, g: (b, g, 0, 0, 0)),
                  _resident(w1.shape), _resident(w2.shape), _resident(w2vt.shape), _resident(pos.shape),
                  _resident(gq.shape), _resident(gk.shape), _resident(ovt.shape),
                  pl.BlockSpec((None,) + bslc.shape[1:], lambda b, g: (g, 0, 0, 0)),
                  pl.BlockSpec((None,) + bwin.shape[1:], lambda b, g: (g, 0, 0))],
        out_specs=pl.BlockSpec((None, seq, _NQ), lambda b, g: (b, 0, g)),
        out_shape=jax.ShapeDtypeStruct((batch, seq, NSA_HEADS * HEAD_DIM), BF16),
        scratch_shapes=[pltpu.VMEM((seq, HEAD_DIM), BF16),
                        pltpu.VMEM((n_chunks // NSA_STEP, HEAD_DIM, NSA_STEP * QB), BF16),
                        pltpu.VMEM((NSA_WIN_PREV * QB + seq, HEAD_DIM), BF16),
                        pltpu.VMEM((NSA_WIN_PREV + n_chunks, HEAD_DIM, QB), BF16)],
        compiler_params=_cparams(2),
        name="nsa",
    )(z_nsa, zc, w1, w2, w2vt, pos, gq, gk, ovt, bslc, bwin)


def _t5_bucket(dist):
    n = jnp.maximum(dist, 0)
    max_exact = NUM_BUCKETS // 2
    nf = jnp.maximum(n, 1).astype(F32)
    large = max_exact + (jnp.log(nf / max_exact) / math.log(MAX_DISTANCE / max_exact)
                         * (NUM_BUCKETS - max_exact)).astype(jnp.int32)
    large = jnp.minimum(large, NUM_BUCKETS - 1)
    return jnp.where(n < max_exact, n, large)


def _toeplitz(u, rows, cols):
    lead = u.shape[:-1]
    lu = rows + cols - 1
    assert u.shape[-1] == lu
    padded = jnp.pad(u, [(0, 0)] * len(lead) + [(0, 1)])
    flat = jnp.broadcast_to(padded[..., None, :], lead + (rows, lu + 1)).reshape(lead + (rows * (lu + 1),))
    return flat[..., :rows * lu].reshape(lead + (rows, lu))[..., rows - 1:]


def _bias_by_distance(bias_cols, delta, valid, dist_scale=1):
    bucket = _t5_bucket(jnp.asarray(np.maximum(delta, 0) * dist_scale, dtype=jnp.int32))
    return jnp.where(jnp.asarray(valid)[None, :], bias_cols.astype(F32)[bucket].T, NEG)


def _band_bias(bias_cols, window, n_prev, dist_scale, n_kv, reps):
    kw = (n_prev + 1) * QB
    delta = n_prev * QB + QB - 1 - np.arange(kw + QB - 1)
    u = _bias_by_distance(bias_cols, delta, (delta >= 0) & (delta < window), dist_scale)
    return _toeplitz(u, QB, kw).reshape(n_kv, reps * QB, kw)


def _slc_bias_t(bias_cols, seq):
    n_chunks = seq // QB
    delta = seq - 1 - np.arange(seq + QB - 1)
    strip = _toeplitz(_bias_by_distance(bias_cols, delta, delta >= 0), QB, seq)
    tile = jnp.flip(strip.reshape(NSA_KV_HEADS, NSA_REPS, QB, n_chunks, QB), axis=3)
    tile = jnp.transpose(tile, (0, 3, 4, 1, 2)).reshape(NSA_KV_HEADS, n_chunks, QB, NSA_REPS * QB)
    return jnp.concatenate([jnp.full_like(tile[:, :1], NEG), tile], axis=1)


def _rope_tables(seq):
    half = MLA_ROPE // 2
    inv = jnp.power(ROPE_THETA, -jnp.arange(half, dtype=F32) / half)
    ang = jnp.arange(seq, dtype=F32)[:, None] * inv[None, :]
    cos, sin = jnp.cos(ang), jnp.sin(ang)
    zeros = lambda w: jnp.zeros((seq, w), F32)
    tail = MLA_PAD - MLA_QK_DIM
    rope_c = jnp.concatenate([jnp.ones((seq, MLA_NOPE), F32), cos, cos, zeros(tail)], axis=1)
    rope_m = jnp.concatenate([zeros(MLA_NOPE), -sin, zeros(half), zeros(tail)], axis=1)
    rope_p = jnp.concatenate([zeros(MLA_NOPE), zeros(half), sin, zeros(tail)], axis=1)
    return rope_c, rope_m, rope_p


def _pad_cols(w, width):
    return jnp.pad(w, ((0, 0), (0, width - w.shape[1])))


def _nsa_column_order():
    g_cols = NSA_KV_HEADS * HEAD_DIM
    q_cols = NSA_HEADS * HEAD_DIM
    order = []
    for g in range(NSA_KV_HEADS):
        cols = list(range(g * _NQ, (g + 1) * _NQ))
        for part in range(6):
            start = q_cols + part * g_cols + g * HEAD_DIM
            cols += list(range(start, start + HEAD_DIM))
        gate0 = q_cols + 6 * g_cols + g * NSA_REPS * 3
        cols += list(range(gate0, gate0 + NSA_REPS * 3))
        order.append(cols)
    return order


def _nsa_cmp_columns():
    q_cols = NSA_HEADS * HEAD_DIM
    g_cols = NSA_KV_HEADS * HEAD_DIM
    cols = []
    for g in range(NSA_KV_HEADS):
        for part in range(2):
            start = q_cols + part * g_cols + g * HEAD_DIM
            cols += list(range(start, start + HEAD_DIM))
    return cols


def _nsa_mixer(z_nsa3, z_cmp, rel_bias, q_norm, k_norm, cmp_pos, cmp_w1, cmp_w2, batch, seq):
    rows16 = seq // NSA_CMP_STRIDE
    zc = z_cmp.reshape(batch, rows16, NSA_CMP_STRIDE, NSA_KV_HEADS, 2, HEAD_DIM)
    zc = jnp.transpose(zc, (0, 3, 4, 1, 2, 5)).reshape(batch, NSA_KV_HEADS, 2, rows16, NSA_CMP_STRIDE * HEAD_DIM)
    n_cmp = (seq - NSA_CMP_LEN) // NSA_CMP_STRIDE + 1
    ci = np.arange(NSA_N_CMP_PAD)[:, None] * NSA_CMP_STRIDE
    sj = np.arange(NSA_N_SLC)[None, :] * NSA_SLC_BLOCK
    overlap = ((ci < sj + NSA_SLC_BLOCK) & (ci + NSA_CMP_LEN > sj) & (np.arange(NSA_N_CMP_PAD)[:, None] < n_cmp))
    ovt = jnp.asarray(overlap.T.astype(np.float32))
    nsa_cols = rel_bias[:, NSA_BIAS_COL0:NSA_BIAS_COL0 + NSA_HEADS]
    bslc = _slc_bias_t(nsa_cols, seq)
    bwin = jnp.transpose(_band_bias(nsa_cols, NSA_WINDOW, NSA_WIN_PREV, 1, NSA_KV_HEADS, NSA_REPS), (0, 2, 1))
    return _nsa(z_nsa3, zc, cmp_w1.astype(BF16), cmp_w2.astype(BF16), cmp_w2[1].T.astype(BF16),
                cmp_pos.reshape(2, 1, NSA_CMP_LEN * HEAD_DIM), q_norm.reshape(HEAD_DIM, 1),
                k_norm.reshape(3, 1, HEAD_DIM), ovt, bslc, bwin, batch, seq)


def kernel(x, rel_bias, ffn1_norm, ffn1_w_gate, ffn1_w_up, ffn1_w_down, mix_norm, ffn2_norm, ffn2_w_gate,
           ffn2_w_up, ffn2_w_down, ab_w_in, mla_q_a_norm, mla_w_q_b, mla_kv_a_norm, mla_w_kv_b, mla_q_norm,
           mla_k_norm, dil_q_norm, dil_k_norm, ab_w_out, cd_w_in, swa_q_norm, swa_k_norm, swa_sinks,
           nsa_q_norm, nsa_k_norm, nsa_cmp_pos, nsa_cmp_w1, nsa_cmp_w2, cd_w_out):
    batch, seq, _ = x.shape
    n = batch * seq
    assert seq % (16 * QB) == 0 and n % TM == 0 and seq % TM == 0
    bf = lambda a: a.astype(BF16)
    xf = x.reshape(n, D_MODEL)

    w_in = ab_w_in[0]
    mla_cols = MLA_Q_RANK + MLA_KV_RANK
    w_krope = jnp.pad(w_in[:, mla_cols:mla_cols + MLA_ROPE], ((0, 0), (MLA_NOPE, MLA_PAD - MLA_QK_DIM)))
    w_mla = jnp.concatenate([w_in[:, :mla_cols], w_krope], axis=1)
    xf, z_mla, z_dil = _ffn_proj(xf, ffn1_norm[0], bf(ffn1_w_gate[0]), bf(ffn1_w_up[0]), bf(ffn1_w_down[0]),
                                 mix_norm[0], [bf(w_mla), bf(w_in[:, mla_cols + MLA_ROPE:])])

    wq = _pad_cols(mla_w_q_b[0].reshape(MLA_Q_RANK * MLA_HEADS, MLA_QK_DIM), MLA_PAD)
    wq = wq.reshape(MLA_Q_RANK, MLA_HEADS * MLA_PAD)
    wkv = mla_w_kv_b[0].reshape(MLA_KV_RANK, MLA_HEADS, MLA_NOPE + MLA_V)
    wk = _pad_cols(wkv[:, :, :MLA_NOPE].reshape(MLA_KV_RANK * MLA_HEADS, MLA_NOPE), MLA_PAD)
    wk = wk.reshape(MLA_KV_RANK, MLA_HEADS * MLA_PAD)
    wv = wkv[:, :, MLA_NOPE:].reshape(MLA_KV_RANK, MLA_HEADS * MLA_V)
    rope_c, rope_m, rope_p = _rope_tables(seq)
    q_mla, k_mla, vt_mla = _mla_prep(
        z_mla, mla_q_a_norm[0].reshape(1, -1), mla_kv_a_norm[0].reshape(1, -1), bf(wq), bf(wk), bf(wv),
        _pad_cols(mla_q_norm[0].reshape(1, -1), MLA_PAD), _pad_cols(mla_k_norm[0].reshape(1, -1), MLA_PAD),
        rope_c, rope_m, rope_p, seq)
    o_a = _mla_attn(q_mla.reshape(batch, seq, -1), k_mla.reshape(batch, seq, -1),
                    vt_mla.reshape(batch, seq // MLA_TQ, MLA_HEADS * MLA_V, MLA_TQ),
                    batch, seq).reshape(n, MLA_HEADS * MLA_V)

    z_dil3 = z_dil.reshape(batch, seq, -1)
    gq, gk = dil_q_norm[0].reshape(HEAD_DIM, 1), dil_k_norm[0].reshape(1, HEAD_DIM)
    dil_outs, dil_lses = [], []
    for grp, (window, dilation) in enumerate(DIL_PATTERNS):
        bm = _band_bias(rel_bias[:, grp * DIL_HPG:(grp + 1) * DIL_HPG], window // dilation + 1, 1, dilation,
                        DIL_HPG, 1)
        o, lse = _dilated_group(z_dil3, batch, seq, grp, dilation, gq, gk, bm)
        dil_outs.append(o)
        dil_lses.append(lse)
    w_out = ab_w_out[0]
    xf = _out_ab_ffn(xf, o_a, dil_outs, dil_lses, bf(w_out[:MLA_HEADS * MLA_V]), bf(w_out[MLA_HEADS * MLA_V:]),
                     ffn2_norm[0], bf(ffn2_w_gate[0]), bf(ffn2_w_up[0]), bf(ffn2_w_down[0]))

    w_in = cd_w_in[0]
    swa_cols = (SWA_HEADS + 2 * SWA_KV_HEADS) * HEAD_DIM
    w_nsa_src = w_in[:, swa_cols:]
    w_nsa = jnp.concatenate([_pad_cols(w_nsa_src[:, np.asarray(cols)], NSA_GROUP_COLS)
                             for cols in _nsa_column_order()], axis=1)
    w_cmp = w_nsa_src[:, np.asarray(_nsa_cmp_columns())]
    xf, z_swa, z_nsa, z_cmp = _ffn_proj(xf, ffn1_norm[1], bf(ffn1_w_gate[1]), bf(ffn1_w_up[1]), bf(ffn1_w_down[1]),
                                        mix_norm[1], [bf(w_in[:, :swa_cols]), bf(w_nsa), bf(w_cmp)])

    swa_reps = SWA_HEADS // SWA_KV_HEADS
    swa_prev = -(-(SWA_WINDOW - 1) // QB)
    bm_swa = _band_bias(rel_bias[:, :SWA_HEADS], SWA_WINDOW, swa_prev, 1, SWA_KV_HEADS, swa_reps)
    sinks = jnp.broadcast_to(swa_sinks[0].astype(F32).reshape(SWA_KV_HEADS, 1, swa_reps, 1),
                             (SWA_KV_HEADS, 1, swa_reps, QB)).reshape(SWA_KV_HEADS, 1, swa_reps * QB)
    o_c = _swa(z_swa.reshape(batch, seq, -1), batch, seq, swa_q_norm[0].reshape(HEAD_DIM, 1),
               swa_k_norm[0].reshape(1, HEAD_DIM), bm_swa, sinks)

    o_d = _nsa_mixer(z_nsa.reshape(batch, seq, NSA_KV_HEADS * NSA_GROUP_COLS), z_cmp, rel_bias, nsa_q_norm[0],
                     nsa_k_norm[0], nsa_cmp_pos[0], nsa_cmp_w1[0], nsa_cmp_w2[0], batch, seq)

    w_out = cd_w_out[0]
    xf = _out_cd_ffn(xf, o_c.reshape(n, -1), o_d.reshape(n, -1), bf(w_out[:SWA_HEADS * HEAD_DIM]),
                     bf(w_out[SWA_HEADS * HEAD_DIM:]), ffn2_norm[1], bf(ffn2_w_gate[1]), bf(ffn2_w_up[1]),
                     bf(ffn2_w_down[1]))
    return xf.reshape(batch, seq, D_MODEL)
```

```python
import functools
import math

import numpy as np
import jax
import jax.numpy as jnp
from jax import lax
from jax.experimental import pallas as pl
from jax.experimental.pallas import tpu as pltpu

F32 = jnp.float32
BF16 = jnp.bfloat16

EPS = 1e-6
NEG = -1e30
TINY = 1e-30
LOG2E = math.log2(math.e)
LN2 = math.log(2.0)
D_MODEL = 1024
D_FF = 2816
NUM_BUCKETS = 32
MAX_DISTANCE = 2048
HEAD_DIM = 64
QB = 128

MLA_HEADS = 8
MLA_Q_RANK = 256
MLA_KV_RANK = 128
MLA_NOPE = 64
MLA_ROPE = 32
MLA_V = 64
MLA_QK_DIM = MLA_NOPE + MLA_ROPE
MLA_PAD = 128
ROPE_THETA = 10000.0

DIL_PATTERNS = ((128, 1), (512, 4), (2048, 16))
DIL_HPG = 4
SWA_HEADS = 8
SWA_KV_HEADS = 2
SWA_WINDOW = 128
NSA_HEADS = 8
NSA_KV_HEADS = 2
NSA_CMP_LEN = 32
NSA_CMP_STRIDE = 16
NSA_CMP_HIDDEN = 128
NSA_SLC_BLOCK = 64
NSA_TOP_N = 16
NSA_WINDOW = 512
NSA_FORCE = 1e6
NSA_BIAS_COL0 = 8
NSA_GROUP_COLS = 768

VMEM_LIMIT = 56 * 1024 * 1024
TM = 512

_NT = (((1,), (1,)), ((), ()))
_HI = lax.Precision.HIGHEST


def _cparams(n_axes):
    return pltpu.CompilerParams(dimension_semantics=("arbitrary",) * n_axes,
                                vmem_limit_bytes=VMEM_LIMIT)


def _resident(shape):
    nd = len(shape)
    return pl.BlockSpec(shape, lambda *_: (0,) * nd, pipeline_mode=pl.Buffered(1))


def _rms(x):
    return x * lax.rsqrt(jnp.mean(x * x, axis=-1, keepdims=True) + EPS)


def _dot(a, b):
    return jnp.dot(a, b, preferred_element_type=F32)


def _dot_nt(a, b, precision=None):
    return lax.dot_general(a, b, _NT, preferred_element_type=F32, precision=precision)


FF_CHUNK = 256


N_CHUNK = 512


def _swiglu_half_step(x, g_ref, wg_ref, wu_ref, wd_ref):
    hb = (_rms(x) * g_ref[...]).astype(BF16)
    acc = jnp.zeros(x.shape, F32)
    for c in range(D_FF // FF_CHUNK):
        sl = slice(c * FF_CHUNK, (c + 1) * FF_CHUNK)
        gate = _dot(hb, wg_ref[:, sl])
        up = _dot(hb, wu_ref[:, sl])
        act = (gate * jax.nn.sigmoid(gate) * up).astype(BF16)
        acc = acc + _dot(act, wd_ref[sl, :])
    return x + 0.5 * acc


def _ffn_proj_kernel(n_out, x_ref, g_ref, wg_ref, wu_ref, wd_ref, gm_ref, *refs):
    w_refs, x_out_ref, z_refs = refs[:n_out], refs[n_out], refs[n_out + 1:]
    x = _swiglu_half_step(x_ref[...], g_ref, wg_ref, wu_ref, wd_ref)
    x_out_ref[...] = x
    hb = (_rms(x) * gm_ref[...]).astype(BF16)
    for w_ref, z_ref in zip(w_refs, z_refs):
        width = w_ref.shape[1]
        for c0 in range(0, width, N_CHUNK):
            sl = slice(c0, min(c0 + N_CHUNK, width))
            z_ref[:, sl] = _dot(hb, w_ref[:, sl])


def _tile(width):
    return pl.BlockSpec((TM, width), lambda i: (i, 0))


def _ffn_proj(x, g, wg, wu, wd, g_mix, ws):
    n = x.shape[0]
    ffn_w = [_resident((D_MODEL, D_FF)), _resident((D_MODEL, D_FF)), _resident((D_FF, D_MODEL))]
    return pl.pallas_call(
        functools.partial(_ffn_proj_kernel, len(ws)),
        grid=(n // TM,),
        in_specs=[_tile(D_MODEL), _resident((1, D_MODEL))] + ffn_w + [_resident((1, D_MODEL))]
        + [_resident(w.shape) for w in ws],
        out_specs=[_tile(D_MODEL)] + [_tile(w.shape[1]) for w in ws],
        out_shape=[jax.ShapeDtypeStruct((n, D_MODEL), F32)]
        + [jax.ShapeDtypeStruct((n, w.shape[1]), F32) for w in ws],
        compiler_params=_cparams(1),
        name="ffn_proj",
    )(x, g.reshape(1, D_MODEL), wg, wu, wd, g_mix.reshape(1, D_MODEL), *ws)


def _out_ab_ffn_kernel(x_ref, oa_ref, o0_ref, o1_ref, o2_ref, l0_ref, l1_ref, l2_ref, wa_ref, wb_ref,
                       g_ref, wg_ref, wu_ref, wd_ref, out_ref):
    l0, l1, l2 = l0_ref[...], l1_ref[...], l2_ref[...]
    m = jnp.maximum(jnp.maximum(l0, l1), l2)
    e0, e1, e2 = jnp.exp(l0 - m), jnp.exp(l1 - m), jnp.exp(l2 - m)
    ob = (e0 * o0_ref[...] + e1 * o1_ref[...] + e2 * o2_ref[...]) / (e0 + e1 + e2)
    x = x_ref[...] + _dot(oa_ref[...], wa_ref[...]) + _dot(ob.astype(BF16), wb_ref[...])
    out_ref[...] = _swiglu_half_step(x, g_ref, wg_ref, wu_ref, wd_ref)


def _out_ab_ffn(x, oa, dil_outs, dil_lses, wa, wb, g, wg, wu, wd):
    n = x.shape[0]
    dw = DIL_HPG * HEAD_DIM
    ffn_w = [_resident((D_MODEL, D_FF)), _resident((D_MODEL, D_FF)), _resident((D_FF, D_MODEL))]
    return pl.pallas_call(
        _out_ab_ffn_kernel,
        grid=(n // TM,),
        in_specs=[_tile(D_MODEL), _tile(oa.shape[1])] + [_tile(dw)] * 6 + [_resident(wa.shape), _resident(wb.shape),
                                                                        _resident((1, D_MODEL))] + ffn_w,
        out_specs=_tile(D_MODEL),
        out_shape=jax.ShapeDtypeStruct((n, D_MODEL), F32),
        compiler_params=_cparams(1),
        name="out_ab_ffn",
    )(x, oa, *dil_outs, *dil_lses, wa, wb, g.reshape(1, D_MODEL), wg, wu, wd)


def _out_cd_ffn_kernel(x_ref, oc_ref, od_ref, wc_ref, wdd_ref, g_ref, wg_ref, wu_ref, wd_ref, out_ref):
    x = x_ref[...] + _dot(oc_ref[...], wc_ref[...]) + _dot(od_ref[...], wdd_ref[...])
    out_ref[...] = _swiglu_half_step(x, g_ref, wg_ref, wu_ref, wd_ref)


def _out_cd_ffn(x, oc, od, wc, wdd, g, wg, wu, wd):
    n = x.shape[0]
    ffn_w = [_resident((D_MODEL, D_FF)), _resident((D_MODEL, D_FF)), _resident((D_FF, D_MODEL))]
    return pl.pallas_call(
        _out_cd_ffn_kernel,
        grid=(n // TM,),
        in_specs=[_tile(D_MODEL), _tile(oc.shape[1]), _tile(od.shape[1]), _resident(wc.shape), _resident(wdd.shape),
                  _resident((1, D_MODEL))] + ffn_w,
        out_specs=_tile(D_MODEL),
        out_shape=jax.ShapeDtypeStruct((n, D_MODEL), F32),
        compiler_params=_cparams(1),
        name="out_cd_ffn",
    )(x, oc, od, wc, wdd, g.reshape(1, D_MODEL), wg, wu, wd)


def _mla_prep_kernel(z_ref, qan_ref, kvan_ref, wq_ref, wk_ref, wv_ref, gq_ref, gk_ref,
                     rc_ref, rm_ref, rp_ref, q_ref, k_ref, vt_ref):
    cq = (_rms(z_ref[:, :MLA_Q_RANK]) * qan_ref[...]).astype(BF16)
    ckv = (_rms(z_ref[:, MLA_Q_RANK:MLA_Q_RANK + MLA_KV_RANK]) * kvan_ref[...]).astype(BF16)
    kr = z_ref[:, MLA_Q_RANK + MLA_KV_RANK:]
    rc, rm, rp = rc_ref[...], rm_ref[...], rp_ref[...]
    scale = MLA_QK_DIM ** -0.5 * LOG2E

    def head_norm_rope(x, gain):
        x = x * lax.rsqrt(jnp.sum(x * x, axis=-1, keepdims=True) * (1.0 / MLA_QK_DIM) + EPS) * gain
        half = MLA_ROPE // 2
        return x * rc + pltpu.roll(x, MLA_PAD - half, 1) * rm + pltpu.roll(x, half, 1) * rp

    for h in range(MLA_HEADS):
        sl = slice(h * MLA_PAD, (h + 1) * MLA_PAD)
        qh = head_norm_rope(_dot(cq, wq_ref[:, sl]), gq_ref[...])
        q_ref[:, sl] = (qh * scale).astype(BF16)
        kh = head_norm_rope(_dot(ckv, wk_ref[:, sl]) + kr, gk_ref[...])
        k_ref[:, sl] = kh.astype(BF16)
    v = _dot(ckv, wv_ref[...])
    for c in range(TM // MLA_TQ):
        vt_ref[c] = v[c * MLA_TQ:(c + 1) * MLA_TQ].T.astype(BF16)


MLA_TQ = 256


def _mla_prep(z, qan, kvan, wq, wk, wv, gq, gk, rope_c, rope_m, rope_p, seq):
    n = z.shape[0]
    per_seq = seq // TM
    tile = lambda w: pl.BlockSpec((TM, w), lambda i: (i, 0))
    rope_spec = pl.BlockSpec((TM, MLA_PAD), lambda i: (i % per_seq, 0))
    hw = MLA_HEADS * MLA_PAD
    vw = MLA_HEADS * MLA_V
    return pl.pallas_call(
        _mla_prep_kernel,
        grid=(n // TM,),
        in_specs=[tile(z.shape[1]), _resident(qan.shape), _resident(kvan.shape), _resident(wq.shape),
                  _resident(wk.shape), _resident(wv.shape), _resident(gq.shape), _resident(gk.shape),
                  rope_spec, rope_spec, rope_spec],
        out_specs=[tile(hw), tile(hw), pl.BlockSpec((TM // MLA_TQ, vw, MLA_TQ), lambda i: (i, 0, 0))],
        out_shape=[jax.ShapeDtypeStruct((n, hw), BF16), jax.ShapeDtypeStruct((n, hw), BF16),
                   jax.ShapeDtypeStruct((n // MLA_TQ, vw, MLA_TQ), BF16)],
        compiler_params=_cparams(1),
        name="mla_prep",
    )(z, qan, kvan, wq, wk, wv, gq, gk, rope_c, rope_m, rope_p)


def _flash_step_t(s, m, l, acc, pv, vt):
    m_new = jnp.maximum(m, jnp.max(s, axis=0, keepdims=True))
    p = jnp.exp2(s - m_new)
    alpha = jnp.exp2(m - m_new)
    l = alpha * l + jnp.sum(p, axis=0, keepdims=True)
    acc = alpha * (acc + pv)
    return m_new, l, acc, _dot(vt, p.astype(BF16))


def _flash_init(dv, n_queries):
    return (jnp.full((1, n_queries), NEG, F32), jnp.zeros((1, n_queries), F32), jnp.zeros((dv, n_queries), F32),
            jnp.zeros((dv, n_queries), F32))


def _flash_finish(m, l, acc, pv):
    return (acc + pv) * (1.0 / l)


def _mla_attn_kernel(seq, q_ref, k_ref, vt_ref, o_ref):
    t = MLA_TQ
    key = lax.broadcasted_iota(jnp.int32, (t, t), 0)
    qry = lax.broadcasted_iota(jnp.int32, (t, t), 1)
    diag_ok = key <= qry
    heads = [(slice(hh * MLA_PAD, (hh + 1) * MLA_PAD), slice(hh * MLA_V, (hh + 1) * MLA_V)) for hh in range(2)]

    init = _flash_init(MLA_V, t)
    for i in range(seq // t):
        rows = slice(i * t, (i + 1) * t)
        qs = [q_ref[rows, ql] for ql, _ in heads]

        def scores(h, j):
            s = _dot_nt(k_ref[j * t:(j + 1) * t, heads[h][0]], qs[h])
            return jnp.where(diag_ok, s, NEG) if j == i else s

        cur = [scores(h, 0) for h in range(2)]
        states = [init, init]
        for j in range(i + 1):
            for h in range(2):
                nxt = scores(h, j + 1) if j < i else None
                states[h] = _flash_step_t(cur[h], *states[h], vt_ref[j, heads[h][1], :])
                cur[h] = nxt
        outs = [_flash_finish(*state) for state in states]
        o_ref[rows, :] = jnp.concatenate(outs, axis=0).T.astype(o_ref.dtype)


def _mla_attn(q, k, vt, batch, seq):
    return pl.pallas_call(
        functools.partial(_mla_attn_kernel, seq),
        grid=(batch, MLA_HEADS // 2),
        in_specs=[pl.BlockSpec((None, seq, 2 * MLA_PAD), lambda b, h: (b, 0, h)),
                  pl.BlockSpec((None, seq, 2 * MLA_PAD), lambda b, h: (b, 0, h)),
                  pl.BlockSpec((None, seq // MLA_TQ, 2 * MLA_V, MLA_TQ), lambda b, h: (b, 0, h, 0))],
        out_specs=pl.BlockSpec((None, seq, 2 * MLA_V), lambda b, h: (b, 0, h)),
        out_shape=jax.ShapeDtypeStruct((batch, seq, MLA_HEADS * MLA_V), BF16),
        compiler_params=_cparams(2),
        name="mla_attn",
    )(q, k, vt)


def _head_norm(x, gain):
    return _rms(x) * gain


def _heads_t(qt, n_heads, gain):
    tiles = []
    for c0 in range(0, n_heads * HEAD_DIM, 2 * HEAD_DIM):
        pair = qt[:, c0:c0 + 2 * HEAD_DIM].T
        for h in range(2):
            x = pair[h * HEAD_DIM:(h + 1) * HEAD_DIM]
            tiles.append(x * lax.rsqrt(jnp.mean(x * x, axis=0, keepdims=True) + EPS) * gain)
    return tiles


def _q_gain(gq_ref):
    return jnp.broadcast_to(gq_ref[...] * (HEAD_DIM ** -0.5 * LOG2E), (HEAD_DIM, QB))


BAND_KV = 2


def _banded_kernel(seq, dilation, reps, n_prev, with_sink, with_lse, group_size, *refs):
    q_ref, k_ref, v_ref, gq_ref, gk_ref, bm_ref = refs[:6]
    refs = refs[6:]
    sink_ref = None
    if with_sink:
        sink_ref, refs = refs[0], refs[1:]
    o_ref, refs = refs[0], refs[1:]
    lse_ref = None
    if with_lse:
        lse_ref, refs = refs[0], refs[1:]
    kn_ref, vt_ref = refs
    length = seq // dilation
    n_blocks = length // QB
    pad = n_prev * QB
    kw = pad + QB
    gq, gk = _q_gain(gq_ref), gk_ref[...]
    win_row = lax.broadcasted_iota(jnp.int32, (kw, 1), 0)

    def rows(start, size):
        return pl.ds(start, size) if dilation == 1 else pl.ds(start, size, stride=dilation)

    for res in range(dilation):
        kall, vall = k_ref[rows(res, length), :], v_ref[rows(res, length), :]
        for g in range(BAND_KV):
            kn_ref[res, g, :pad, :] = jnp.zeros((pad, HEAD_DIM), BF16)
            kn_ref[res, g, pad:, :] = _head_norm(kall[:, g * HEAD_DIM:(g + 1) * HEAD_DIM], gk).astype(BF16)
        for j in range(n_prev):
            vt_ref[res, j] = jnp.zeros((BAND_KV * HEAD_DIM, QB), BF16)
        for j in range(n_blocks):
            vt_ref[res, n_prev + j] = vall[j * QB:(j + 1) * QB].T.astype(BF16)

    def attend(res, i, g, s):
        bias = bm_ref[g]
        if i < n_prev:
            bias = jnp.where(win_row >= (n_prev - i) * QB, bias, NEG)
        s = s + bias
        m = jnp.max(s, axis=0, keepdims=True)
        if with_sink:
            m = jnp.maximum(m, sink_ref[g])
        p = jnp.exp2(s - m)
        l = jnp.sum(p, axis=0, keepdims=True)
        if with_sink:
            l = l + jnp.exp2(sink_ref[g] - m)
        p = p.astype(BF16)
        vl = slice(g * HEAD_DIM, (g + 1) * HEAD_DIM)
        o = _dot(vt_ref[res, i, vl, :], p[0:QB])
        for w in range(1, n_prev + 1):
            o = o + _dot(vt_ref[res, i + w, vl, :], p[w * QB:(w + 1) * QB])
        return o * (1.0 / l), m * LN2 + jnp.log(l)

    items = [(res, i) for res in range(dilation) for i in range(n_blocks)]
    for at in range(0, len(items), group_size):
        group = items[at:at + group_size]
        toks = [rows(res + i * QB * dilation, QB) for res, i in group]
        heads = [_heads_t(q_ref[tok, :], BAND_KV * reps, gq) for tok in toks]
        qss = [[jnp.concatenate(h[g * reps:(g + 1) * reps], axis=1).astype(BF16) for g in range(BAND_KV)]
               for h in heads]
        scores = [[_dot(kn_ref[res, g, i * QB:i * QB + kw, :], qs[g]) for g in range(BAND_KV)]
                  for (res, i), qs in zip(group, qss)]
        for (res, i), tok, sc in zip(group, toks, scores):
            outs, lses = [], []
            for g in range(BAND_KV):
                o, lse = attend(res, i, g, sc[g])
                for r in range(reps):
                    outs.append(o[:, r * QB:(r + 1) * QB])
                    if with_lse:
                        lses.append(jnp.broadcast_to(lse[:, r * QB:(r + 1) * QB], (HEAD_DIM, QB)))
            o_ref[tok, :] = jnp.concatenate(outs, axis=0).T.astype(o_ref.dtype)
            if with_lse:
                lse_ref[tok, :] = jnp.concatenate(lses, axis=0).T


def _banded_scratch(seq, dilation, n_prev):
    length = seq // dilation
    return [pltpu.VMEM((dilation, BAND_KV, n_prev * QB + length, HEAD_DIM), BF16),
            pltpu.VMEM((dilation, n_prev + length // QB, BAND_KV * HEAD_DIM, QB), BF16)]


def _dilated_group(z_dil, batch, seq, group, dilation, gq, gk, bm):
    gw = DIL_HPG * HEAD_DIM
    n_groups = len(DIL_PATTERNS)
    hps = BAND_KV
    halves = DIL_HPG // hps
    bm_t = jnp.transpose(bm, (0, 2, 1))

    def spec(part):
        return pl.BlockSpec((None, seq, hps * HEAD_DIM), lambda b, h: (b, 0, (part * n_groups + group) * halves + h))

    out_spec = pl.BlockSpec((None, seq, hps * HEAD_DIM), lambda b, h: (b, 0, h))
    shape = jax.ShapeDtypeStruct((batch, seq, gw), F32)
    out, lse = pl.pallas_call(
        functools.partial(_banded_kernel, seq, dilation, 1, 1, False, True, 4),
        grid=(batch, halves),
        in_specs=[spec(0), spec(1), spec(2), _resident(gq.shape), _resident(gk.shape),
                  pl.BlockSpec((hps,) + bm_t.shape[1:], lambda b, h: (h, 0, 0))],
        out_specs=[out_spec, out_spec],
        out_shape=[shape, shape],
        scratch_shapes=_banded_scratch(seq, dilation, 1),
        compiler_params=_cparams(2),
        name="dilated_g%d" % group,
    )(z_dil, z_dil, z_dil, gq, gk, bm_t)
    return out.reshape(batch * seq, gw), lse.reshape(batch * seq, gw)


def _swa(z_swa, batch, seq, gq, gk, bm, sinks):
    qw = SWA_HEADS * HEAD_DIM
    kvw = SWA_KV_HEADS * HEAD_DIM
    reps = SWA_HEADS // SWA_KV_HEADS
    n_prev = -(-(SWA_WINDOW - 1) // QB)
    assert SWA_KV_HEADS == BAND_KV
    bm_t = jnp.transpose(bm, (0, 2, 1))
    return pl.pallas_call(
        functools.partial(_banded_kernel, seq, 1, reps, n_prev, True, False, 2),
        grid=(batch,),
        in_specs=[pl.BlockSpec((None, seq, qw), lambda b: (b, 0, 0)),
                  pl.BlockSpec((None, seq, kvw), lambda b: (b, 0, qw // kvw)),
                  pl.BlockSpec((None, seq, kvw), lambda b: (b, 0, qw // kvw + 1)),
                  _resident(gq.shape), _resident(gk.shape), _resident(bm_t.shape), _resident(sinks.shape)],
        out_specs=pl.BlockSpec((None, seq, qw), lambda b: (b, 0, 0)),
        out_shape=jax.ShapeDtypeStruct((batch, seq, qw), BF16),
        scratch_shapes=_banded_scratch(seq, 1, n_prev),
        compiler_params=_cparams(1),
        name="swa",
    )(z_swa, z_swa, z_swa, gq, gk, bm_t, sinks)


NSA_REPS = NSA_HEADS // NSA_KV_HEADS
NSA_N_SLC = 32
NSA_N_CMP_PAD = 128
NSA_WIN_PREV = -(-(NSA_WINDOW - 1) // QB)
_NQ = NSA_REPS * HEAD_DIM
_OFF_KC, _OFF_VC, _OFF_KS, _OFF_VS, _OFF_KW, _OFF_VW, _OFF_GATE = (_NQ + i * HEAD_DIM for i in range(7))


NSA_TIERS = 4
NSA_STEP = 8
NSA_QPAIR = 2


def _nsa_kernel(seq, z_ref, zc_ref, w1_ref, w2_ref, w2vt_ref, pos_ref, gq_ref, gk_ref, ovt_ref,
                bslc_ref, bwin_ref, o_ref, ks_ref, vst_ref, kw_ref, vwt_ref):
    n_chunks = seq // QB
    lanes = NSA_REPS * QB
    gq = _q_gain(gq_ref)
    win_pad = NSA_WIN_PREV * QB
    win_kw = win_pad + QB

    half = NSA_CMP_LEN * HEAD_DIM // 2

    def hidden(kv):
        x2 = zc_ref[kv].astype(BF16)
        first = _dot(x2, w1_ref[kv, :half, :])
        second = _dot(x2, w1_ref[kv, half:, :])
        pos_rows = jnp.broadcast_to(pos_ref[kv], (8, NSA_CMP_LEN * HEAD_DIM)).astype(BF16)
        pos_term = _dot(pos_rows, w1_ref[kv])[0:1]
        return jax.nn.gelu(first + pltpu.roll(second, NSA_N_CMP_PAD - 1, 0) + pos_term).astype(BF16)

    kc = _head_norm(_dot(hidden(0), w2_ref[0]), gk_ref[0]).astype(BF16)
    vct = _dot_nt(w2vt_ref[...], hidden(1)).astype(BF16)

    ks_ref[...] = _head_norm(z_ref[:, _OFF_KS:_OFF_KS + HEAD_DIM], gk_ref[1]).astype(BF16)
    kw_ref[:win_pad, :] = jnp.zeros((win_pad, HEAD_DIM), BF16)
    kw_ref[win_pad:, :] = _head_norm(z_ref[:, _OFF_KW:_OFF_KW + HEAD_DIM], gk_ref[2]).astype(BF16)
    for j in range(NSA_WIN_PREV):
        vwt_ref[j] = jnp.zeros((HEAD_DIM, QB), BF16)
    for j in range(n_chunks):
        rows = slice(j * QB, (j + 1) * QB)
        vst_ref[:, rows] = z_ref[rows, _OFF_KS:_OFF_KS + 2 * HEAD_DIM].T[HEAD_DIM:].astype(BF16)
        vwt_ref[NSA_WIN_PREV + j] = z_ref[rows, _OFF_KW:_OFF_KW + 2 * HEAD_DIM].T[HEAD_DIM:].astype(BF16)

    cmp_id = lax.broadcasted_iota(jnp.int32, (NSA_N_CMP_PAD, lanes), 0)
    cmp_end = cmp_id * NSA_CMP_STRIDE + (NSA_CMP_LEN - 1)
    cmp_real = cmp_id < NSA_N_CMP_PAD - 1
    q_in_blk = lax.broadcasted_iota(jnp.int32, (NSA_N_CMP_PAD, lanes), 1) & (QB - 1)
    blk_id = lax.broadcasted_iota(jnp.int32, (NSA_N_SLC, QB), 0)
    q_lane = lax.broadcasted_iota(jnp.int32, (NSA_N_SLC, QB), 1)
    win_row = lax.broadcasted_iota(jnp.int32, (win_kw, 1), 0)
    init = _flash_init(HEAD_DIM, lanes)
    both = range(NSA_QPAIR)

    def q_blocks(n_keys, mask_pad, pair, carry):
        step = max(d for d in range(1, NSA_STEP + 1) if n_keys % d == 0)
        step_keys = step * QB
        blocks_per_step = step_keys // NSA_SLC_BLOCK
        blk = [pair * NSA_QPAIR + b for b in both]
        r0 = [pl.multiple_of(i * QB, QB) for i in blk]
        qs = [jnp.concatenate(_heads_t(z_ref[pl.ds(r, QB), :_NQ], NSA_REPS, gq), axis=1).astype(BF16)
              for r in r0]

        sc = [_dot(kc, q) for q in qs]
        s_win = [_dot(kw_ref[pl.ds(r, win_kw), :], q) for r, q in zip(r0, qs)]

        def slc_scores(b, j):
            return _dot(ks_ref[j * step_keys:(j + 1) * step_keys, :], qs[b])

        cur = [slc_scores(b, 0) for b in both]

        o_cmp, imp = [], []
        for b in both:
            ok = (cmp_end <= q_in_blk + blk[b] * QB) & cmp_real
            s = jnp.where(ok, sc[b], NEG)
            e = jnp.where(ok, jnp.exp2(s - jnp.max(s, axis=0, keepdims=True)), 0.0)
            p = e * (1.0 / jnp.maximum(jnp.sum(e, axis=0, keepdims=True), TINY))
            o_cmp.append(_dot(vct, p.astype(BF16)))
            p_sum = p[:, 0:QB] + p[:, QB:2 * QB] + p[:, 2 * QB:3 * QB] + p[:, 3 * QB:4 * QB]
            imp.append(jnp.dot(ovt_ref[...], p_sum, precision=_HI, preferred_element_type=F32))

        o_win = []
        for b in both:
            bias = bwin_ref[...]
            if mask_pad:
                bias = jnp.where(win_row >= (NSA_WIN_PREV - blk[b]) * QB, bias, NEG)
            s = s_win[b] + bias
            pw = jnp.exp2(s - jnp.max(s, axis=0, keepdims=True))
            lw = jnp.sum(pw, axis=0, keepdims=True)
            pw = pw.astype(BF16)
            o = _dot(vwt_ref[blk[b]], pw[0:QB])
            for c in range(1, NSA_WIN_PREV + 1):
                o = o + _dot(vwt_ref[blk[b] + c], pw[c * QB:(c + 1) * QB])
            o_win.append(o * (1.0 / lw))

        drop = []
        for b in both:
            tb = lax.shift_right_logical(q_lane + blk[b] * QB, 6)
            forced = (blk_id == 0) | (blk_id == tb) | (blk_id == tb - 1)
            score = jnp.where(blk_id <= tb, imp[b] + jnp.where(forced, NSA_FORCE, 0.0), -NSA_FORCE)
            rank = jnp.zeros((NSA_N_SLC, QB), F32)
            for other in range(NSA_N_SLC):
                s_o = score[other:other + 1, :]
                beats = (s_o > score) | ((s_o == score) & (blk_id > other))
                rank = rank + jnp.where(beats, 1.0, 0.0)
            drop.append(jnp.where(rank < NSA_TOP_N, 0.0, NEG))

        states = [init for _ in both]
        for j in range(n_keys // step):
            for b in both:
                nxt = slc_scores(b, j + 1) if (j + 1) * step < n_keys else None
                bias = jnp.concatenate(
                    [bslc_ref[jnp.maximum(blk[b] - (j * step + c) + 1, 0)] for c in range(step)], axis=0)
                km = jnp.concatenate([jnp.broadcast_to(drop[b][k:k + 1], (NSA_SLC_BLOCK, QB))
                                      for k in range(j * blocks_per_step, (j + 1) * blocks_per_step)], axis=0)
                s = cur[b] + bias + jnp.concatenate([km] * NSA_REPS, axis=1)
                states[b] = _flash_step_t(s, *states[b], vst_ref[:, j * step_keys:(j + 1) * step_keys])
                cur[b] = nxt

        for b in both:
            o_slc = _flash_finish(*states[b])
            gate = jax.nn.sigmoid(z_ref[pl.ds(r0[b], QB), _OFF_GATE:_OFF_GATE + QB]).T
            outs = []
            for r in range(NSA_REPS):
                qsl = slice(r * QB, (r + 1) * QB)
                outs.append(gate[3 * r:3 * r + 1] * o_cmp[b][:, qsl] + gate[3 * r + 1:3 * r + 2] * o_slc[:, qsl]
                            + gate[3 * r + 2:3 * r + 3] * o_win[b][:, qsl])
            o_ref[pl.ds(r0[b], QB), :] = jnp.concatenate(outs, axis=0).T.astype(o_ref.dtype)
        return carry

    per_tier = n_chunks // NSA_TIERS
    for tier in range(NSA_TIERS):
        lo, hi = tier * per_tier, (tier + 1) * per_tier
        lax.fori_loop(lo // NSA_QPAIR, hi // NSA_QPAIR, functools.partial(q_blocks, hi, lo < NSA_WIN_PREV), 0)


def _nsa(z_nsa, zc, w1, w2, w2vt, pos, gq, gk, ovt, bslc, bwin, batch, seq):
    gw = NSA_GROUP_COLS
    n_chunks = seq // QB
    return pl.pallas_call(
        functools.partial(_nsa_kernel, seq),
        grid=(batch, NSA_KV_HEADS),
        in_specs=[pl.BlockSpec((None, seq, gw), lambda b, g: (b, 0, g)),
                  pl.BlockSpec((None, None) + zc.shape[2:], lambda b, g: (b, g, 0, 0, 0)),
                  _resident(w1.shape), _resident(w2.shape), _resident(w2vt.shape), _resident(pos.shape),
                  _resident(gq.shape), _resident(gk.shape), _resident(ovt.shape),
                  pl.BlockSpec((None,) + bslc.shape[1:], lambda b, g: (g, 0, 0, 0)),
                  pl.BlockSpec((None,) + bwin.shape[1:], lambda b, g: (g, 0, 0))],
        out_specs=pl.BlockSpec((None, seq, _NQ), lambda b, g: (b, 0, g)),
        out_shape=jax.ShapeDtypeStruct((batch, seq, NSA_HEADS * HEAD_DIM), BF16),
        scratch_shapes=[pltpu.VMEM((seq, HEAD_DIM), BF16),
                        pltpu.VMEM((HEAD_DIM, seq), BF16),
                        pltpu.VMEM((NSA_WIN_PREV * QB + seq, HEAD_DIM), BF16),
                        pltpu.VMEM((NSA_WIN_PREV + n_chunks, HEAD_DIM, QB), BF16)],
        compiler_params=_cparams(2),
        name="nsa",
    )(z_nsa, zc, w1, w2, w2vt, pos, gq, gk, ovt, bslc, bwin)


def _t5_bucket(dist):
    n = jnp.maximum(dist, 0)
    max_exact = NUM_BUCKETS // 2
    nf = jnp.maximum(n, 1).astype(F32)
    large = max_exact + (jnp.log(nf / max_exact) / math.log(MAX_DISTANCE / max_exact)
                         * (NUM_BUCKETS - max_exact)).astype(jnp.int32)
    large = jnp.minimum(large, NUM_BUCKETS - 1)
    return jnp.where(n < max_exact, n, large)


def _toeplitz(u, rows, cols):
    lead = u.shape[:-1]
    lu = rows + cols - 1
    assert u.shape[-1] == lu
    padded = jnp.pad(u, [(0, 0)] * len(lead) + [(0, 1)])
    flat = jnp.broadcast_to(padded[..., None, :], lead + (rows, lu + 1)).reshape(lead + (rows * (lu + 1),))
    return flat[..., :rows * lu].reshape(lead + (rows, lu))[..., rows - 1:]


def _bias_by_distance(bias_cols, delta, valid, dist_scale=1):
    bucket = _t5_bucket(jnp.asarray(np.maximum(delta, 0) * dist_scale, dtype=jnp.int32))
    return jnp.where(jnp.asarray(valid)[None, :], bias_cols.astype(F32)[bucket].T * LOG2E, NEG)


def _band_bias(bias_cols, window, n_prev, dist_scale, n_kv, reps):
    kw = (n_prev + 1) * QB
    delta = n_prev * QB + QB - 1 - np.arange(kw + QB - 1)
    u = _bias_by_distance(bias_cols, delta, (delta >= 0) & (delta < window), dist_scale)
    return _toeplitz(u, QB, kw).reshape(n_kv, reps * QB, kw)


def _slc_bias_t(bias_cols, seq):
    n_chunks = seq // QB
    delta = seq - 1 - np.arange(seq + QB - 1)
    strip = _toeplitz(_bias_by_distance(bias_cols, delta, delta >= 0), QB, seq)
    tile = jnp.flip(strip.reshape(NSA_KV_HEADS, NSA_REPS, QB, n_chunks, QB), axis=3)
    tile = jnp.transpose(tile, (0, 3, 4, 1, 2)).reshape(NSA_KV_HEADS, n_chunks, QB, NSA_REPS * QB)
    return jnp.concatenate([jnp.full_like(tile[:, :1], NEG), tile], axis=1)


def _rope_tables(seq):
    half = MLA_ROPE // 2
    inv = jnp.power(ROPE_THETA, -jnp.arange(half, dtype=F32) / half)
    ang = jnp.arange(seq, dtype=F32)[:, None] * inv[None, :]
    cos, sin = jnp.cos(ang), jnp.sin(ang)
    zeros = lambda w: jnp.zeros((seq, w), F32)
    tail = MLA_PAD - MLA_QK_DIM
    rope_c = jnp.concatenate([jnp.ones((seq, MLA_NOPE), F32), cos, cos, zeros(tail)], axis=1)
    rope_m = jnp.concatenate([zeros(MLA_NOPE), -sin, zeros(half), zeros(tail)], axis=1)
    rope_p = jnp.concatenate([zeros(MLA_NOPE), zeros(half), sin, zeros(tail)], axis=1)
    return rope_c, rope_m, rope_p


def _pad_cols(w, width):
    return jnp.pad(w, ((0, 0), (0, width - w.shape[1])))


def _nsa_column_order():
    g_cols = NSA_KV_HEADS * HEAD_DIM
    q_cols = NSA_HEADS * HEAD_DIM
    order = []
    for g in range(NSA_KV_HEADS):
        cols = list(range(g * _NQ, (g + 1) * _NQ))
        for part in range(6):
            start = q_cols + part * g_cols + g * HEAD_DIM
            cols += list(range(start, start + HEAD_DIM))
        gate0 = q_cols + 6 * g_cols + g * NSA_REPS * 3
        cols += list(range(gate0, gate0 + NSA_REPS * 3))
        order.append(cols)
    return order


def _nsa_cmp_columns():
    q_cols = NSA_HEADS * HEAD_DIM
    g_cols = NSA_KV_HEADS * HEAD_DIM
    cols = []
    for g in range(NSA_KV_HEADS):
        for part in range(2):
            start = q_cols + part * g_cols + g * HEAD_DIM
            cols += list(range(start, start + HEAD_DIM))
    return cols


def _nsa_mixer(z_nsa3, z_cmp, rel_bias, q_norm, k_norm, cmp_pos, cmp_w1, cmp_w2, batch, seq):
    rows16 = seq // NSA_CMP_STRIDE
    zc = z_cmp.reshape(batch, rows16, NSA_CMP_STRIDE, NSA_KV_HEADS, 2, HEAD_DIM)
    zc = jnp.transpose(zc, (0, 3, 4, 1, 2, 5)).reshape(batch, NSA_KV_HEADS, 2, rows16, NSA_CMP_STRIDE * HEAD_DIM)
    n_cmp = (seq - NSA_CMP_LEN) // NSA_CMP_STRIDE + 1
    ci = np.arange(NSA_N_CMP_PAD)[:, None] * NSA_CMP_STRIDE
    sj = np.arange(NSA_N_SLC)[None, :] * NSA_SLC_BLOCK
    overlap = ((ci < sj + NSA_SLC_BLOCK) & (ci + NSA_CMP_LEN > sj) & (np.arange(NSA_N_CMP_PAD)[:, None] < n_cmp))
    ovt = jnp.asarray(overlap.T.astype(np.float32))
    nsa_cols = rel_bias[:, NSA_BIAS_COL0:NSA_BIAS_COL0 + NSA_HEADS]
    bslc = _slc_bias_t(nsa_cols, seq)
    bwin = jnp.transpose(_band_bias(nsa_cols, NSA_WINDOW, NSA_WIN_PREV, 1, NSA_KV_HEADS, NSA_REPS), (0, 2, 1))
    return _nsa(z_nsa3, zc, cmp_w1.astype(BF16), cmp_w2.astype(BF16), cmp_w2[1].T.astype(BF16),
                cmp_pos.reshape(2, 1, NSA_CMP_LEN * HEAD_DIM), q_norm.reshape(HEAD_DIM, 1),
                k_norm.reshape(3, 1, HEAD_DIM), ovt, bslc, bwin, batch, seq)


def kernel(x, rel_bias, ffn1_norm, ffn1_w_gate, ffn1_w_up, ffn1_w_down, mix_norm, ffn2_norm, ffn2_w_gate,
           ffn2_w_up, ffn2_w_down, ab_w_in, mla_q_a_norm, mla_w_q_b, mla_kv_a_norm, mla_w_kv_b, mla_q_norm,
           mla_k_norm, dil_q_norm, dil_k_norm, ab_w_out, cd_w_in, swa_q_norm, swa_k_norm, swa_sinks,
           nsa_q_norm, nsa_k_norm, nsa_cmp_pos, nsa_cmp_w1, nsa_cmp_w2, cd_w_out):
    batch, seq, _ = x.shape
    n = batch * seq
    assert seq % (16 * QB) == 0 and n % TM == 0 and seq % TM == 0
    bf = lambda a: a.astype(BF16)
    xf = x.reshape(n, D_MODEL)

    w_in = ab_w_in[0]
    mla_cols = MLA_Q_RANK + MLA_KV_RANK
    w_krope = jnp.pad(w_in[:, mla_cols:mla_cols + MLA_ROPE], ((0, 0), (MLA_NOPE, MLA_PAD - MLA_QK_DIM)))
    w_mla = jnp.concatenate([w_in[:, :mla_cols], w_krope], axis=1)
    xf, z_mla, z_dil = _ffn_proj(xf, ffn1_norm[0], bf(ffn1_w_gate[0]), bf(ffn1_w_up[0]), bf(ffn1_w_down[0]),
                                 mix_norm[0], [bf(w_mla), bf(w_in[:, mla_cols + MLA_ROPE:])])

    wq = _pad_cols(mla_w_q_b[0].reshape(MLA_Q_RANK * MLA_HEADS, MLA_QK_DIM), MLA_PAD)
    wq = wq.reshape(MLA_Q_RANK, MLA_HEADS * MLA_PAD)
    wkv = mla_w_kv_b[0].reshape(MLA_KV_RANK, MLA_HEADS, MLA_NOPE + MLA_V)
    wk = _pad_cols(wkv[:, :, :MLA_NOPE].reshape(MLA_KV_RANK * MLA_HEADS, MLA_NOPE), MLA_PAD)
    wk = wk.reshape(MLA_KV_RANK, MLA_HEADS * MLA_PAD)
    wv = wkv[:, :, MLA_NOPE:].reshape(MLA_KV_RANK, MLA_HEADS * MLA_V)
    rope_c, rope_m, rope_p = _rope_tables(seq)
    q_mla, k_mla, vt_mla = _mla_prep(
        z_mla, mla_q_a_norm[0].reshape(1, -1), mla_kv_a_norm[0].reshape(1, -1), bf(wq), bf(wk), bf(wv),
        _pad_cols(mla_q_norm[0].reshape(1, -1), MLA_PAD), _pad_cols(mla_k_norm[0].reshape(1, -1), MLA_PAD),
        rope_c, rope_m, rope_p, seq)
    o_a = _mla_attn(q_mla.reshape(batch, seq, -1), k_mla.reshape(batch, seq, -1),
                    vt_mla.reshape(batch, seq // MLA_TQ, MLA_HEADS * MLA_V, MLA_TQ),
                    batch, seq).reshape(n, MLA_HEADS * MLA_V)

    z_dil3 = z_dil.reshape(batch, seq, -1)
    gq, gk = dil_q_norm[0].reshape(HEAD_DIM, 1), dil_k_norm[0].reshape(1, HEAD_DIM)
    dil_outs, dil_lses = [], []
    for grp, (window, dilation) in enumerate(DIL_PATTERNS):
        bm = _band_bias(rel_bias[:, grp * DIL_HPG:(grp + 1) * DIL_HPG], window // dilation + 1, 1, dilation,
                        DIL_HPG, 1)
        o, lse = _dilated_group(z_dil3, batch, seq, grp, dilation, gq, gk, bm)
        dil_outs.append(o)
        dil_lses.append(lse)
    w_out = ab_w_out[0]
    xf = _out_ab_ffn(xf, o_a, dil_outs, dil_lses, bf(w_out[:MLA_HEADS * MLA_V]), bf(w_out[MLA_HEADS * MLA_V:]),
                     ffn2_norm[0], bf(ffn2_w_gate[0]), bf(ffn2_w_up[0]), bf(ffn2_w_down[0]))

    w_in = cd_w_in[0]
    swa_cols = (SWA_HEADS + 2 * SWA_KV_HEADS) * HEAD_DIM
    w_nsa_src = w_in[:, swa_cols:]
    w_nsa = jnp.concatenate([_pad_cols(w_nsa_src[:, np.asarray(cols)], NSA_GROUP_COLS)
                             for cols in _nsa_column_order()], axis=1)
    w_cmp = w_nsa_src[:, np.asarray(_nsa_cmp_columns())]
    xf, z_swa, z_nsa, z_cmp = _ffn_proj(xf, ffn1_norm[1], bf(ffn1_w_gate[1]), bf(ffn1_w_up[1]), bf(ffn1_w_down[1]),
                                        mix_norm[1], [bf(w_in[:, :swa_cols]), bf(w_nsa), bf(w_cmp)])

    swa_reps = SWA_HEADS // SWA_KV_HEADS
    swa_prev = -(-(SWA_WINDOW - 1) // QB)
    bm_swa = _band_bias(rel_bias[:, :SWA_HEADS], SWA_WINDOW, swa_prev, 1, SWA_KV_HEADS, swa_reps)
    sinks = jnp.broadcast_to((swa_sinks[0].astype(F32) * LOG2E).reshape(SWA_KV_HEADS, 1, swa_reps, 1),
                             (SWA_KV_HEADS, 1, swa_reps, QB)).reshape(SWA_KV_HEADS, 1, swa_reps * QB)
    o_c = _swa(z_swa.reshape(batch, seq, -1), batch, seq, swa_q_norm[0].reshape(HEAD_DIM, 1),
               swa_k_norm[0].reshape(1, HEAD_DIM), bm_swa, sinks)

    o_d = _nsa_mixer(z_nsa.reshape(batch, seq, NSA_KV_HEADS * NSA_GROUP_COLS), z_cmp, rel_bias, nsa_q_norm[0],
                     nsa_k_norm[0], nsa_cmp_pos[0], nsa_cmp_w1[0], nsa_cmp_w2[0], batch, seq)

    w_out = cd_w_out[0]
    xf = _out_cd_ffn(xf, o_c.reshape(n, -1), o_d.reshape(n, -1), bf(w_out[:SWA_HEADS * HEAD_DIM]),
                     bf(w_out[SWA_HEADS * HEAD_DIM:]), ffn2_norm[1], bf(ffn2_w_gate[1]), bf(ffn2_w_up[1]),
                     bf(ffn2_w_down[1]))
    return xf.reshape(batch, seq, D_MODEL)
```

```python
import functools
import math

import numpy as np
import jax
import jax.numpy as jnp
from jax import lax
from jax.experimental import pallas as pl
from jax.experimental.pallas import tpu as pltpu

F32 = jnp.float32
BF16 = jnp.bfloat16

EPS = 1e-6
NEG = -1e30
TINY = 1e-30
LOG2E = math.log2(math.e)
LN2 = math.log(2.0)
D_MODEL = 1024
D_FF = 2816
NUM_BUCKETS = 32
MAX_DISTANCE = 2048
HEAD_DIM = 64
QB = 128

MLA_HEADS = 8
MLA_Q_RANK = 256
MLA_KV_RANK = 128
MLA_NOPE = 64
MLA_ROPE = 32
MLA_V = 64
MLA_QK_DIM = MLA_NOPE + MLA_ROPE
MLA_PAD = 128
ROPE_THETA = 10000.0

DIL_PATTERNS = ((128, 1), (512, 4), (2048, 16))
DIL_HPG = 4
SWA_HEADS = 8
SWA_KV_HEADS = 2
SWA_WINDOW = 128
NSA_HEADS = 8
NSA_KV_HEADS = 2
NSA_CMP_LEN = 32
NSA_CMP_STRIDE = 16
NSA_CMP_HIDDEN = 128
NSA_SLC_BLOCK = 64
NSA_TOP_N = 16
NSA_WINDOW = 512
NSA_FORCE = 1e6
NSA_BIAS_COL0 = 8
NSA_GROUP_COLS = 768

VMEM_LIMIT = 56 * 1024 * 1024
TM = 512

_NT = (((1,), (1,)), ((), ()))
_HI = lax.Precision.HIGHEST


def _cparams(n_axes):
    return pltpu.CompilerParams(dimension_semantics=("arbitrary",) * n_axes,
                                vmem_limit_bytes=VMEM_LIMIT)


def _resident(shape):
    nd = len(shape)
    return pl.BlockSpec(shape, lambda *_: (0,) * nd, pipeline_mode=pl.Buffered(1))


def _rms(x):
    return x * lax.rsqrt(jnp.mean(x * x, axis=-1, keepdims=True) + EPS)


def _dot(a, b):
    return jnp.dot(a, b, preferred_element_type=F32)


def _dot_nt(a, b, precision=None):
    return lax.dot_general(a, b, _NT, preferred_element_type=F32, precision=precision)


FF_CHUNK = 256


N_CHUNK = 512


def _swiglu_half_step(x, g_ref, wg_ref, wu_ref, wd_ref):
    hb = (_rms(x) * g_ref[...]).astype(BF16)
    acc = jnp.zeros(x.shape, F32)
    for c in range(D_FF // FF_CHUNK):
        sl = slice(c * FF_CHUNK, (c + 1) * FF_CHUNK)
        gate = _dot(hb, wg_ref[:, sl])
        up = _dot(hb, wu_ref[:, sl])
        act = (gate * jax.nn.sigmoid(gate) * up).astype(BF16)
        acc = acc + _dot(act, wd_ref[sl, :])
    return x + 0.5 * acc


def _ffn_proj_kernel(n_out, x_ref, g_ref, wg_ref, wu_ref, wd_ref, gm_ref, *refs):
    w_refs, x_out_ref, z_refs = refs[:n_out], refs[n_out], refs[n_out + 1:]
    x = _swiglu_half_step(x_ref[...], g_ref, wg_ref, wu_ref, wd_ref)
    x_out_ref[...] = x
    hb = (_rms(x) * gm_ref[...]).astype(BF16)
    for w_ref, z_ref in zip(w_refs, z_refs):
        width = w_ref.shape[1]
        for c0 in range(0, width, N_CHUNK):
            sl = slice(c0, min(c0 + N_CHUNK, width))
            z_ref[:, sl] = _dot(hb, w_ref[:, sl]).astype(z_ref.dtype)


def _tile(width):
    return pl.BlockSpec((TM, width), lambda i: (i, 0))


def _ffn_proj(x, g, wg, wu, wd, g_mix, ws, z_dtypes=None):
    n = x.shape[0]
    z_dtypes = z_dtypes or [F32] * len(ws)
    ffn_w = [_resident((D_MODEL, D_FF)), _resident((D_MODEL, D_FF)), _resident((D_FF, D_MODEL))]
    return pl.pallas_call(
        functools.partial(_ffn_proj_kernel, len(ws)),
        grid=(n // TM,),
        in_specs=[_tile(D_MODEL), _resident((1, D_MODEL))] + ffn_w + [_resident((1, D_MODEL))]
        + [_resident(w.shape) for w in ws],
        out_specs=[_tile(D_MODEL)] + [_tile(w.shape[1]) for w in ws],
        out_shape=[jax.ShapeDtypeStruct((n, D_MODEL), F32)]
        + [jax.ShapeDtypeStruct((n, w.shape[1]), dt) for w, dt in zip(ws, z_dtypes)],
        compiler_params=_cparams(1),
        name="ffn_proj",
    )(x, g.reshape(1, D_MODEL), wg, wu, wd, g_mix.reshape(1, D_MODEL), *ws)


def _out_ab_ffn_kernel(x_ref, oa_ref, o0_ref, o1_ref, o2_ref, l0_ref, l1_ref, l2_ref, wa_ref, wb_ref,
                       g_ref, wg_ref, wu_ref, wd_ref, out_ref):
    l0, l1, l2 = l0_ref[...], l1_ref[...], l2_ref[...]
    m = jnp.maximum(jnp.maximum(l0, l1), l2)
    e0, e1, e2 = jnp.exp(l0 - m), jnp.exp(l1 - m), jnp.exp(l2 - m)
    ob = (e0 * o0_ref[...] + e1 * o1_ref[...] + e2 * o2_ref[...]) / (e0 + e1 + e2)
    x = x_ref[...] + _dot(oa_ref[...], wa_ref[...]) + _dot(ob.astype(BF16), wb_ref[...])
    out_ref[...] = _swiglu_half_step(x, g_ref, wg_ref, wu_ref, wd_ref)


def _out_ab_ffn(x, oa, dil_outs, dil_lses, wa, wb, g, wg, wu, wd):
    n = x.shape[0]
    dw = DIL_HPG * HEAD_DIM
    ffn_w = [_resident((D_MODEL, D_FF)), _resident((D_MODEL, D_FF)), _resident((D_FF, D_MODEL))]
    return pl.pallas_call(
        _out_ab_ffn_kernel,
        grid=(n // TM,),
        in_specs=[_tile(D_MODEL), _tile(oa.shape[1])] + [_tile(dw)] * 6 + [_resident(wa.shape), _resident(wb.shape),
                                                                        _resident((1, D_MODEL))] + ffn_w,
        out_specs=_tile(D_MODEL),
        out_shape=jax.ShapeDtypeStruct((n, D_MODEL), F32),
        compiler_params=_cparams(1),
        name="out_ab_ffn",
    )(x, oa, *dil_outs, *dil_lses, wa, wb, g.reshape(1, D_MODEL), wg, wu, wd)


def _out_cd_ffn_kernel(x_ref, oc_ref, od_ref, wc_ref, wdd_ref, g_ref, wg_ref, wu_ref, wd_ref, out_ref):
    x = x_ref[...] + _dot(oc_ref[...], wc_ref[...]) + _dot(od_ref[...], wdd_ref[...])
    out_ref[...] = _swiglu_half_step(x, g_ref, wg_ref, wu_ref, wd_ref)


def _out_cd_ffn(x, oc, od, wc, wdd, g, wg, wu, wd):
    n = x.shape[0]
    ffn_w = [_resident((D_MODEL, D_FF)), _resident((D_MODEL, D_FF)), _resident((D_FF, D_MODEL))]
    return pl.pallas_call(
        _out_cd_ffn_kernel,
        grid=(n // TM,),
        in_specs=[_tile(D_MODEL), _tile(oc.shape[1]), _tile(od.shape[1]), _resident(wc.shape), _resident(wdd.shape),
                  _resident((1, D_MODEL))] + ffn_w,
        out_specs=_tile(D_MODEL),
        out_shape=jax.ShapeDtypeStruct((n, D_MODEL), F32),
        compiler_params=_cparams(1),
        name="out_cd_ffn",
    )(x, oc, od, wc, wdd, g.reshape(1, D_MODEL), wg, wu, wd)


def _mla_prep_kernel(z_ref, qan_ref, kvan_ref, wq_ref, wk_ref, wv_ref, gq_ref, gk_ref,
                     rc_ref, rm_ref, rp_ref, q_ref, k_ref, vt_ref):
    cq = (_rms(z_ref[:, :MLA_Q_RANK]) * qan_ref[...]).astype(BF16)
    ckv = (_rms(z_ref[:, MLA_Q_RANK:MLA_Q_RANK + MLA_KV_RANK]) * kvan_ref[...]).astype(BF16)
    kr = z_ref[:, MLA_Q_RANK + MLA_KV_RANK:]
    rc, rm, rp = rc_ref[...], rm_ref[...], rp_ref[...]
    scale = MLA_QK_DIM ** -0.5 * LOG2E

    def head_norm_rope(x, gain):
        x = x * lax.rsqrt(jnp.sum(x * x, axis=-1, keepdims=True) * (1.0 / MLA_QK_DIM) + EPS) * gain
        half = MLA_ROPE // 2
        return x * rc + pltpu.roll(x, MLA_PAD - half, 1) * rm + pltpu.roll(x, half, 1) * rp

    for h in range(MLA_HEADS):
        sl = slice(h * MLA_PAD, (h + 1) * MLA_PAD)
        qh = head_norm_rope(_dot(cq, wq_ref[:, sl]), gq_ref[...])
        q_ref[:, sl] = (qh * scale).astype(BF16)
        kh = head_norm_rope(_dot(ckv, wk_ref[:, sl]) + kr, gk_ref[...])
        k_ref[:, sl] = kh.astype(BF16)
    v = _dot(ckv, wv_ref[...])
    for c in range(TM // MLA_TQ):
        vt_ref[c] = v[c * MLA_TQ:(c + 1) * MLA_TQ].T.astype(BF16)


MLA_TQ = 256
MLA_STEP = 2


def _mla_prep(z, qan, kvan, wq, wk, wv, gq, gk, rope_c, rope_m, rope_p, seq):
    n = z.shape[0]
    per_seq = seq // TM
    tile = lambda w: pl.BlockSpec((TM, w), lambda i: (i, 0))
    rope_spec = pl.BlockSpec((TM, MLA_PAD), lambda i: (i % per_seq, 0))
    hw = MLA_HEADS * MLA_PAD
    vw = MLA_HEADS * MLA_V
    return pl.pallas_call(
        _mla_prep_kernel,
        grid=(n // TM,),
        in_specs=[tile(z.shape[1]), _resident(qan.shape), _resident(kvan.shape), _resident(wq.shape),
                  _resident(wk.shape), _resident(wv.shape), _resident(gq.shape), _resident(gk.shape),
                  rope_spec, rope_spec, rope_spec],
        out_specs=[tile(hw), tile(hw), pl.BlockSpec((TM // MLA_TQ, vw, MLA_TQ), lambda i: (i, 0, 0))],
        out_shape=[jax.ShapeDtypeStruct((n, hw), BF16), jax.ShapeDtypeStruct((n, hw), BF16),
                   jax.ShapeDtypeStruct((n // MLA_TQ, vw, MLA_TQ), BF16)],
        compiler_params=_cparams(1),
        name="mla_prep",
    )(z, qan, kvan, wq, wk, wv, gq, gk, rope_c, rope_m, rope_p)


def _flash_step_t(s, m, l, acc, pv, vt):
    m_new = jnp.maximum(m, jnp.max(s, axis=0, keepdims=True))
    p = jnp.exp2(s - m_new)
    alpha = jnp.exp2(m - m_new)
    l = alpha * l + jnp.sum(p, axis=0, keepdims=True)
    acc = alpha * (acc + pv)
    return m_new, l, acc, _dot(vt, p.astype(BF16))


def _flash_init(dv, n_queries):
    return (jnp.full((1, n_queries), NEG, F32), jnp.zeros((1, n_queries), F32), jnp.zeros((dv, n_queries), F32),
            jnp.zeros((dv, n_queries), F32))


def _flash_finish(m, l, acc, pv):
    return (acc + pv) * (1.0 / l)


def _mla_attn_kernel(seq, q_ref, k_ref, vt_ref, o_ref):
    t = MLA_TQ
    key = lax.broadcasted_iota(jnp.int32, (t, t), 0)
    qry = lax.broadcasted_iota(jnp.int32, (t, t), 1)
    diag_ok = key <= qry
    heads = [(slice(hh * MLA_PAD, (hh + 1) * MLA_PAD), slice(hh * MLA_V, (hh + 1) * MLA_V)) for hh in range(2)]

    init = _flash_init(MLA_V, t)
    for i in range(seq // t):
        rows = slice(i * t, (i + 1) * t)
        qs = [q_ref[rows, ql] for ql, _ in heads]

        steps = [(j, min(j + MLA_STEP, i + 1)) for j in range(0, i + 1, MLA_STEP)]

        def scores(h, step):
            lo, hi = steps[step]
            s = _dot_nt(k_ref[lo * t:hi * t, heads[h][0]], qs[h])
            if hi == i + 1:
                last = jnp.where(diag_ok, s[(i - lo) * t:], NEG)
                s = last if i == lo else jnp.concatenate([s[:(i - lo) * t], last], axis=0)
            return s

        def values(h, step):
            lo, hi = steps[step]
            return jnp.concatenate([vt_ref[j, heads[h][1], :] for j in range(lo, hi)], axis=1)

        cur = [scores(h, 0) for h in range(2)]
        states = [init, init]
        for step in range(len(steps)):
            for h in range(2):
                nxt = scores(h, step + 1) if step + 1 < len(steps) else None
                states[h] = _flash_step_t(cur[h], *states[h], values(h, step))
                cur[h] = nxt
        outs = [_flash_finish(*state) for state in states]
        o_ref[rows, :] = jnp.concatenate(outs, axis=0).T.astype(o_ref.dtype)


def _mla_attn(q, k, vt, batch, seq):
    return pl.pallas_call(
        functools.partial(_mla_attn_kernel, seq),
        grid=(batch, MLA_HEADS // 2),
        in_specs=[pl.BlockSpec((None, seq, 2 * MLA_PAD), lambda b, h: (b, 0, h)),
                  pl.BlockSpec((None, seq, 2 * MLA_PAD), lambda b, h: (b, 0, h)),
                  pl.BlockSpec((None, seq // MLA_TQ, 2 * MLA_V, MLA_TQ), lambda b, h: (b, 0, h, 0))],
        out_specs=pl.BlockSpec((None, seq, 2 * MLA_V), lambda b, h: (b, 0, h)),
        out_shape=jax.ShapeDtypeStruct((batch, seq, MLA_HEADS * MLA_V), BF16),
        compiler_params=_cparams(2),
        name="mla_attn",
    )(q, k, vt)


def _head_norm(x, gain):
    return _rms(x) * gain


def _heads_t(qt, n_heads, gain):
    tiles = []
    for c0 in range(0, n_heads * HEAD_DIM, 2 * HEAD_DIM):
        pair = qt[:, c0:c0 + 2 * HEAD_DIM].T
        for h in range(2):
            x = pair[h * HEAD_DIM:(h + 1) * HEAD_DIM]
            tiles.append(x * lax.rsqrt(jnp.mean(x * x, axis=0, keepdims=True) + EPS) * gain)
    return tiles


def _q_gain(gq_ref):
    return jnp.broadcast_to(gq_ref[...] * (HEAD_DIM ** -0.5 * LOG2E), (HEAD_DIM, QB))


BAND_KV = 2


def _banded_kernel(seq, dilation, reps, n_prev, with_sink, with_lse, group_size, *refs):
    q_ref, k_ref, v_ref, gq_ref, gk_ref, bm_ref = refs[:6]
    refs = refs[6:]
    sink_ref = None
    if with_sink:
        sink_ref, refs = refs[0], refs[1:]
    o_ref, refs = refs[0], refs[1:]
    lse_ref = None
    if with_lse:
        lse_ref, refs = refs[0], refs[1:]
    kn_ref, vt_ref = refs
    length = seq // dilation
    n_blocks = length // QB
    pad = n_prev * QB
    kw = pad + QB
    gq, gk = _q_gain(gq_ref), gk_ref[...]
    win_row = lax.broadcasted_iota(jnp.int32, (kw, 1), 0)

    def rows(start, size):
        return pl.ds(start, size) if dilation == 1 else pl.ds(start, size, stride=dilation)

    for res in range(dilation):
        kall, vall = k_ref[rows(res, length), :], v_ref[rows(res, length), :]
        for g in range(BAND_KV):
            kn_ref[res, g, :pad, :] = jnp.zeros((pad, HEAD_DIM), BF16)
            kn_ref[res, g, pad:, :] = _head_norm(kall[:, g * HEAD_DIM:(g + 1) * HEAD_DIM], gk).astype(BF16)
        for j in range(n_prev):
            vt_ref[res, j] = jnp.zeros((BAND_KV * HEAD_DIM, QB), BF16)
        for j in range(n_blocks):
            vt_ref[res, n_prev + j] = vall[j * QB:(j + 1) * QB].T.astype(BF16)

    def attend(res, i, g, s):
        bias = bm_ref[g]
        if i < n_prev:
            bias = jnp.where(win_row >= (n_prev - i) * QB, bias, NEG)
        s = s + bias
        m = jnp.max(s, axis=0, keepdims=True)
        if with_sink:
            m = jnp.maximum(m, sink_ref[g])
        p = jnp.exp2(s - m)
        l = jnp.sum(p, axis=0, keepdims=True)
        if with_sink:
            l = l + jnp.exp2(sink_ref[g] - m)
        p = p.astype(BF16)
        vl = slice(g * HEAD_DIM, (g + 1) * HEAD_DIM)
        o = _dot(vt_ref[res, i, vl, :], p[0:QB])
        for w in range(1, n_prev + 1):
            o = o + _dot(vt_ref[res, i + w, vl, :], p[w * QB:(w + 1) * QB])
        return o * (1.0 / l), m * LN2 + jnp.log(l)

    items = [(res, i) for res in range(dilation) for i in range(n_blocks)]
    for at in range(0, len(items), group_size):
        group = items[at:at + group_size]
        toks = [rows(res + i * QB * dilation, QB) for res, i in group]
        heads = [_heads_t(q_ref[tok, :], BAND_KV * reps, gq) for tok in toks]
        qss = [[jnp.concatenate(h[g * reps:(g + 1) * reps], axis=1).astype(BF16) for g in range(BAND_KV)]
               for h in heads]
        scores = [[_dot(kn_ref[res, g, i * QB:i * QB + kw, :], qs[g]) for g in range(BAND_KV)]
                  for (res, i), qs in zip(group, qss)]
        for (res, i), tok, sc in zip(group, toks, scores):
            outs, lses = [], []
            for g in range(BAND_KV):
                o, lse = attend(res, i, g, sc[g])
                for r in range(reps):
                    outs.append(o[:, r * QB:(r + 1) * QB])
                    if with_lse:
                        lses.append(jnp.broadcast_to(lse[:, r * QB:(r + 1) * QB], (HEAD_DIM, QB)))
            o_ref[tok, :] = jnp.concatenate(outs, axis=0).T.astype(o_ref.dtype)
            if with_lse:
                lse_ref[tok, :] = jnp.concatenate(lses, axis=0).T


def _banded_scratch(seq, dilation, n_prev):
    length = seq // dilation
    return [pltpu.VMEM((dilation, BAND_KV, n_prev * QB + length, HEAD_DIM), BF16),
            pltpu.VMEM((dilation, n_prev + length // QB, BAND_KV * HEAD_DIM, QB), BF16)]


def _dilated_group(z_dil, batch, seq, group, dilation, gq, gk, bm):
    gw = DIL_HPG * HEAD_DIM
    n_groups = len(DIL_PATTERNS)
    hps = BAND_KV
    halves = DIL_HPG // hps
    bm_t = jnp.transpose(bm, (0, 2, 1))

    def spec(part):
        return pl.BlockSpec((None, seq, hps * HEAD_DIM), lambda b, h: (b, 0, (part * n_groups + group) * halves + h))

    out_spec = pl.BlockSpec((None, seq, hps * HEAD_DIM), lambda b, h: (b, 0, h))
    shape = jax.ShapeDtypeStruct((batch, seq, gw), F32)
    out, lse = pl.pallas_call(
        functools.partial(_banded_kernel, seq, dilation, 1, 1, False, True, 4),
        grid=(batch, halves),
        in_specs=[spec(0), spec(1), spec(2), _resident(gq.shape), _resident(gk.shape),
                  pl.BlockSpec((hps,) + bm_t.shape[1:], lambda b, h: (h, 0, 0))],
        out_specs=[out_spec, out_spec],
        out_shape=[shape, shape],
        scratch_shapes=_banded_scratch(seq, dilation, 1),
        compiler_params=_cparams(2),
        name="dilated_g%d" % group,
    )(z_dil, z_dil, z_dil, gq, gk, bm_t)
    return out.reshape(batch * seq, gw), lse.reshape(batch * seq, gw)


def _swa(z_swa, batch, seq, gq, gk, bm, sinks):
    qw = SWA_HEADS * HEAD_DIM
    kvw = SWA_KV_HEADS * HEAD_DIM
    reps = SWA_HEADS // SWA_KV_HEADS
    n_prev = -(-(SWA_WINDOW - 1) // QB)
    assert SWA_KV_HEADS == BAND_KV
    bm_t = jnp.transpose(bm, (0, 2, 1))
    return pl.pallas_call(
        functools.partial(_banded_kernel, seq, 1, reps, n_prev, True, False, 2),
        grid=(batch,),
        in_specs=[pl.BlockSpec((None, seq, qw), lambda b: (b, 0, 0)),
                  pl.BlockSpec((None, seq, kvw), lambda b: (b, 0, qw // kvw)),
                  pl.BlockSpec((None, seq, kvw), lambda b: (b, 0, qw // kvw + 1)),
                  _resident(gq.shape), _resident(gk.shape), _resident(bm_t.shape), _resident(sinks.shape)],
        out_specs=pl.BlockSpec((None, seq, qw), lambda b: (b, 0, 0)),
        out_shape=jax.ShapeDtypeStruct((batch, seq, qw), BF16),
        scratch_shapes=_banded_scratch(seq, 1, n_prev),
        compiler_params=_cparams(1),
        name="swa",
    )(z_swa, z_swa, z_swa, gq, gk, bm_t, sinks)


NSA_REPS = NSA_HEADS // NSA_KV_HEADS
NSA_N_SLC = 32
NSA_N_CMP_PAD = 128
NSA_WIN_PREV = -(-(NSA_WINDOW - 1) // QB)
_NQ = NSA_REPS * HEAD_DIM
_OFF_KC, _OFF_VC, _OFF_KS, _OFF_VS, _OFF_KW, _OFF_VW, _OFF_GATE = (_NQ + i * HEAD_DIM for i in range(7))


NSA_TIERS = 4
NSA_STEP = 8
NSA_QPAIR = 2


def _nsa_kernel(seq, z_ref, zc_ref, w1_ref, w2_ref, w2vt_ref, pos_ref, gq_ref, gk_ref, ovt_ref,
                bslc_ref, bwin_ref, o_ref, ks_ref, vst_ref, kw_ref, vwt_ref):
    n_chunks = seq // QB
    lanes = NSA_REPS * QB
    gq = _q_gain(gq_ref)
    win_pad = NSA_WIN_PREV * QB
    win_kw = win_pad + QB

    half = NSA_CMP_LEN * HEAD_DIM // 2

    def hidden(kv):
        x2 = zc_ref[kv].astype(BF16)
        first = _dot(x2, w1_ref[kv, :half, :])
        second = _dot(x2, w1_ref[kv, half:, :])
        pos_rows = jnp.broadcast_to(pos_ref[kv], (8, NSA_CMP_LEN * HEAD_DIM)).astype(BF16)
        pos_term = _dot(pos_rows, w1_ref[kv])[0:1]
        return jax.nn.gelu(first + pltpu.roll(second, NSA_N_CMP_PAD - 1, 0) + pos_term).astype(BF16)

    kc = _head_norm(_dot(hidden(0), w2_ref[0]), gk_ref[0]).astype(BF16)
    vct = _dot_nt(w2vt_ref[...], hidden(1)).astype(BF16)

    ks_ref[...] = _head_norm(z_ref[:, _OFF_KS:_OFF_KS + HEAD_DIM], gk_ref[1]).astype(BF16)
    kw_ref[:win_pad, :] = jnp.zeros((win_pad, HEAD_DIM), BF16)
    kw_ref[win_pad:, :] = _head_norm(z_ref[:, _OFF_KW:_OFF_KW + HEAD_DIM], gk_ref[2]).astype(BF16)
    for j in range(NSA_WIN_PREV):
        vwt_ref[j] = jnp.zeros((HEAD_DIM, QB), BF16)
    for j in range(n_chunks):
        rows = slice(j * QB, (j + 1) * QB)
        vst_ref[:, rows] = z_ref[rows, _OFF_KS:_OFF_KS + 2 * HEAD_DIM].T[HEAD_DIM:].astype(BF16)
        vwt_ref[NSA_WIN_PREV + j] = z_ref[rows, _OFF_KW:_OFF_KW + 2 * HEAD_DIM].T[HEAD_DIM:].astype(BF16)

    cmp_id = lax.broadcasted_iota(jnp.int32, (NSA_N_CMP_PAD, lanes), 0)
    cmp_end = cmp_id * NSA_CMP_STRIDE + (NSA_CMP_LEN - 1)
    cmp_real = cmp_id < NSA_N_CMP_PAD - 1
    q_in_blk = lax.broadcasted_iota(jnp.int32, (NSA_N_CMP_PAD, lanes), 1) & (QB - 1)
    blk_id = lax.broadcasted_iota(jnp.int32, (NSA_N_SLC, QB), 0)
    q_lane = lax.broadcasted_iota(jnp.int32, (NSA_N_SLC, QB), 1)
    win_row = lax.broadcasted_iota(jnp.int32, (win_kw, 1), 0)
    init = _flash_init(HEAD_DIM, lanes)
    both = range(NSA_QPAIR)

    def q_blocks(n_keys, mask_pad, pair, carry):
        step = max(d for d in range(1, NSA_STEP + 1) if n_keys % d == 0)
        step_keys = step * QB
        blocks_per_step = step_keys // NSA_SLC_BLOCK
        blk = [pair * NSA_QPAIR + b for b in both]
        r0 = [pl.multiple_of(i * QB, QB) for i in blk]
        qs = [jnp.concatenate(_heads_t(z_ref[pl.ds(r, QB), :_NQ], NSA_REPS, gq), axis=1).astype(BF16)
              for r in r0]

        sc = [_dot(kc, q) for q in qs]
        s_win = [_dot(kw_ref[pl.ds(r, win_kw), :], q) for r, q in zip(r0, qs)]

        def slc_scores(b, j):
            return _dot(ks_ref[j * step_keys:(j + 1) * step_keys, :], qs[b])

        cur = [slc_scores(b, 0) for b in both]

        o_cmp, imp = [], []
        for b in both:
            ok = (cmp_end <= q_in_blk + blk[b] * QB) & cmp_real
            s = jnp.where(ok, sc[b], NEG)
            e = jnp.where(ok, jnp.exp2(s - jnp.max(s, axis=0, keepdims=True)), 0.0)
            p = e * (1.0 / jnp.maximum(jnp.sum(e, axis=0, keepdims=True), TINY))
            o_cmp.append(_dot(vct, p.astype(BF16)))
            p_sum = p[:, 0:QB] + p[:, QB:2 * QB] + p[:, 2 * QB:3 * QB] + p[:, 3 * QB:4 * QB]
            imp.append(jnp.dot(ovt_ref[...], p_sum, precision=_HI, preferred_element_type=F32))

        o_win = []
        for b in both:
            bias = bwin_ref[...]
            if mask_pad:
                bias = jnp.where(win_row >= (NSA_WIN_PREV - blk[b]) * QB, bias, NEG)
            s = s_win[b] + bias
            pw = jnp.exp2(s - jnp.max(s, axis=0, keepdims=True))
            lw = jnp.sum(pw, axis=0, keepdims=True)
            pw = pw.astype(BF16)
            o = _dot(vwt_ref[blk[b]], pw[0:QB])
            for c in range(1, NSA_WIN_PREV + 1):
                o = o + _dot(vwt_ref[blk[b] + c], pw[c * QB:(c + 1) * QB])
            o_win.append(o * (1.0 / lw))

        drop = []
        for b in both:
            tb = lax.shift_right_logical(q_lane + blk[b] * QB, 6)
            forced = (blk_id == 0) | (blk_id == tb) | (blk_id == tb - 1)
            score = jnp.where(blk_id <= tb, imp[b] + jnp.where(forced, NSA_FORCE, 0.0), -NSA_FORCE)
            rank = jnp.zeros((NSA_N_SLC, QB), F32)
            for other in range(NSA_N_SLC):
                s_o = score[other:other + 1, :]
                beats = (s_o > score) | ((s_o == score) & (blk_id > other))
                rank = rank + jnp.where(beats, 1.0, 0.0)
            drop.append(jnp.where(rank < NSA_TOP_N, 0.0, NEG))

        states = [init for _ in both]
        for j in range(n_keys // step):
            for b in both:
                nxt = slc_scores(b, j + 1) if (j + 1) * step < n_keys else None
                bias = jnp.concatenate(
                    [bslc_ref[jnp.maximum(blk[b] - (j * step + c) + 1, 0)] for c in range(step)], axis=0)
                km = jnp.concatenate([jnp.broadcast_to(drop[b][k:k + 1], (NSA_SLC_BLOCK, QB))
                                      for k in range(j * blocks_per_step, (j + 1) * blocks_per_step)], axis=0)
                s = cur[b] + bias + jnp.concatenate([km] * NSA_REPS, axis=1)
                states[b] = _flash_step_t(s, *states[b], vst_ref[:, j * step_keys:(j + 1) * step_keys])
                cur[b] = nxt

        for b in both:
            o_slc = _flash_finish(*states[b])
            gate = jax.nn.sigmoid(z_ref[pl.ds(r0[b], QB), _OFF_GATE:_OFF_GATE + QB]).T
            outs = []
            for r in range(NSA_REPS):
                qsl = slice(r * QB, (r + 1) * QB)
                outs.append(gate[3 * r:3 * r + 1] * o_cmp[b][:, qsl] + gate[3 * r + 1:3 * r + 2] * o_slc[:, qsl]
                            + gate[3 * r + 2:3 * r + 3] * o_win[b][:, qsl])
            o_ref[pl.ds(r0[b], QB), :] = jnp.concatenate(outs, axis=0).T.astype(o_ref.dtype)
        return carry

    per_tier = n_chunks // NSA_TIERS
    for tier in range(NSA_TIERS):
        lo, hi = tier * per_tier, (tier + 1) * per_tier
        lax.fori_loop(lo // NSA_QPAIR, hi // NSA_QPAIR, functools.partial(q_blocks, hi, lo < NSA_WIN_PREV), 0)


def _nsa(z_nsa, zc, w1, w2, w2vt, pos, gq, gk, ovt, bslc, bwin, batch, seq):
    gw = NSA_GROUP_COLS
    n_chunks = seq // QB
    return pl.pallas_call(
        functools.partial(_nsa_kernel, seq),
        grid=(batch, NSA_KV_HEADS),
        in_specs=[pl.BlockSpec((None, seq, gw), lambda b, g: (b, 0, g)),
                  pl.BlockSpec((None, None) + zc.shape[2:], lambda b, g: (b, g, 0, 0, 0)),
                  _resident(w1.shape), _resident(w2.shape), _resident(w2vt.shape), _resident(pos.shape),
                  _resident(gq.shape), _resident(gk.shape), _resident(ovt.shape),
                  pl.BlockSpec((None,) + bslc.shape[1:], lambda b, g: (g, 0, 0, 0)),
                  pl.BlockSpec((None,) + bwin.shape[1:], lambda b, g: (g, 0, 0))],
        out_specs=pl.BlockSpec((None, seq, _NQ), lambda b, g: (b, 0, g)),
        out_shape=jax.ShapeDtypeStruct((batch, seq, NSA_HEADS * HEAD_DIM), BF16),
        scratch_shapes=[pltpu.VMEM((seq, HEAD_DIM), BF16),
                        pltpu.VMEM((HEAD_DIM, seq), BF16),
                        pltpu.VMEM((NSA_WIN_PREV * QB + seq, HEAD_DIM), BF16),
                        pltpu.VMEM((NSA_WIN_PREV + n_chunks, HEAD_DIM, QB), BF16)],
        compiler_params=_cparams(2),
        name="nsa",
    )(z_nsa, zc, w1, w2, w2vt, pos, gq, gk, ovt, bslc, bwin)


def _t5_bucket(dist):
    n = jnp.maximum(dist, 0)
    max_exact = NUM_BUCKETS // 2
    nf = jnp.maximum(n, 1).astype(F32)
    large = max_exact + (jnp.log(nf / max_exact) / math.log(MAX_DISTANCE / max_exact)
                         * (NUM_BUCKETS - max_exact)).astype(jnp.int32)
    large = jnp.minimum(large, NUM_BUCKETS - 1)
    return jnp.where(n < max_exact, n, large)


def _toeplitz(u, rows, cols):
    lead = u.shape[:-1]
    lu = rows + cols - 1
    assert u.shape[-1] == lu
    padded = jnp.pad(u, [(0, 0)] * len(lead) + [(0, 1)])
    flat = jnp.broadcast_to(padded[..., None, :], lead + (rows, lu + 1)).reshape(lead + (rows * (lu + 1),))
    return flat[..., :rows * lu].reshape(lead + (rows, lu))[..., rows - 1:]


def _bias_by_distance(bias_cols, delta, valid, dist_scale=1):
    bucket = _t5_bucket(jnp.asarray(np.maximum(delta, 0) * dist_scale, dtype=jnp.int32))
    return jnp.where(jnp.asarray(valid)[None, :], bias_cols.astype(F32)[bucket].T * LOG2E, NEG)


def _band_bias(bias_cols, window, n_prev, dist_scale, n_kv, reps):
    kw = (n_prev + 1) * QB
    delta = n_prev * QB + QB - 1 - np.arange(kw + QB - 1)
    u = _bias_by_distance(bias_cols, delta, (delta >= 0) & (delta < window), dist_scale)
    return _toeplitz(u, QB, kw).reshape(n_kv, reps * QB, kw)


def _slc_bias_t(bias_cols, seq):
    n_chunks = seq // QB
    delta = seq - 1 - np.arange(seq + QB - 1)
    strip = _toeplitz(_bias_by_distance(bias_cols, delta, delta >= 0), QB, seq)
    tile = jnp.flip(strip.reshape(NSA_KV_HEADS, NSA_REPS, QB, n_chunks, QB), axis=3)
    tile = jnp.transpose(tile, (0, 3, 4, 1, 2)).reshape(NSA_KV_HEADS, n_chunks, QB, NSA_REPS * QB)
    return jnp.concatenate([jnp.full_like(tile[:, :1], NEG), tile], axis=1)


def _rope_tables(seq):
    half = MLA_ROPE // 2
    inv = jnp.power(ROPE_THETA, -jnp.arange(half, dtype=F32) / half)
    ang = jnp.arange(seq, dtype=F32)[:, None] * inv[None, :]
    cos, sin = jnp.cos(ang), jnp.sin(ang)
    zeros = lambda w: jnp.zeros((seq, w), F32)
    tail = MLA_PAD - MLA_QK_DIM
    rope_c = jnp.concatenate([jnp.ones((seq, MLA_NOPE), F32), cos, cos, zeros(tail)], axis=1)
    rope_m = jnp.concatenate([zeros(MLA_NOPE), -sin, zeros(half), zeros(tail)], axis=1)
    rope_p = jnp.concatenate([zeros(MLA_NOPE), zeros(half), sin, zeros(tail)], axis=1)
    return rope_c, rope_m, rope_p


def _pad_cols(w, width):
    return jnp.pad(w, ((0, 0), (0, width - w.shape[1])))


def _nsa_column_order():
    g_cols = NSA_KV_HEADS * HEAD_DIM
    q_cols = NSA_HEADS * HEAD_DIM
    order = []
    for g in range(NSA_KV_HEADS):
        cols = list(range(g * _NQ, (g + 1) * _NQ))
        for part in range(6):
            start = q_cols + part * g_cols + g * HEAD_DIM
            cols += list(range(start, start + HEAD_DIM))
        gate0 = q_cols + 6 * g_cols + g * NSA_REPS * 3
        cols += list(range(gate0, gate0 + NSA_REPS * 3))
        order.append(cols)
    return order


def _nsa_cmp_columns():
    q_cols = NSA_HEADS * HEAD_DIM
    g_cols = NSA_KV_HEADS * HEAD_DIM
    cols = []
    for g in range(NSA_KV_HEADS):
        for part in range(2):
            start = q_cols + part * g_cols + g * HEAD_DIM
            cols += list(range(start, start + HEAD_DIM))
    return cols


def _nsa_mixer(z_nsa3, z_cmp, rel_bias, q_norm, k_norm, cmp_pos, cmp_w1, cmp_w2, batch, seq):
    rows16 = seq // NSA_CMP_STRIDE
    zc = z_cmp.reshape(batch, rows16, NSA_CMP_STRIDE, NSA_KV_HEADS, 2, HEAD_DIM)
    zc = jnp.transpose(zc, (0, 3, 4, 1, 2, 5)).reshape(batch, NSA_KV_HEADS, 2, rows16, NSA_CMP_STRIDE * HEAD_DIM)
    n_cmp = (seq - NSA_CMP_LEN) // NSA_CMP_STRIDE + 1
    ci = np.arange(NSA_N_CMP_PAD)[:, None] * NSA_CMP_STRIDE
    sj = np.arange(NSA_N_SLC)[None, :] * NSA_SLC_BLOCK
    overlap = ((ci < sj + NSA_SLC_BLOCK) & (ci + NSA_CMP_LEN > sj) & (np.arange(NSA_N_CMP_PAD)[:, None] < n_cmp))
    ovt = jnp.asarray(overlap.T.astype(np.float32))
    nsa_cols = rel_bias[:, NSA_BIAS_COL0:NSA_BIAS_COL0 + NSA_HEADS]
    bslc = _slc_bias_t(nsa_cols, seq)
    bwin = jnp.transpose(_band_bias(nsa_cols, NSA_WINDOW, NSA_WIN_PREV, 1, NSA_KV_HEADS, NSA_REPS), (0, 2, 1))
    return _nsa(z_nsa3, zc, cmp_w1.astype(BF16), cmp_w2.astype(BF16), cmp_w2[1].T.astype(BF16),
                cmp_pos.reshape(2, 1, NSA_CMP_LEN * HEAD_DIM), q_norm.reshape(HEAD_DIM, 1),
                k_norm.reshape(3, 1, HEAD_DIM), ovt, bslc, bwin, batch, seq)


def kernel(x, rel_bias, ffn1_norm, ffn1_w_gate, ffn1_w_up, ffn1_w_down, mix_norm, ffn2_norm, ffn2_w_gate,
           ffn2_w_up, ffn2_w_down, ab_w_in, mla_q_a_norm, mla_w_q_b, mla_kv_a_norm, mla_w_kv_b, mla_q_norm,
           mla_k_norm, dil_q_norm, dil_k_norm, ab_w_out, cd_w_in, swa_q_norm, swa_k_norm, swa_sinks,
           nsa_q_norm, nsa_k_norm, nsa_cmp_pos, nsa_cmp_w1, nsa_cmp_w2, cd_w_out):
    batch, seq, _ = x.shape
    n = batch * seq
    assert seq % (16 * QB) == 0 and n % TM == 0 and seq % TM == 0
    bf = lambda a: a.astype(BF16)
    xf = x.reshape(n, D_MODEL)

    w_in = ab_w_in[0]
    mla_cols = MLA_Q_RANK + MLA_KV_RANK
    w_krope = jnp.pad(w_in[:, mla_cols:mla_cols + MLA_ROPE], ((0, 0), (MLA_NOPE, MLA_PAD - MLA_QK_DIM)))
    w_mla = jnp.concatenate([w_in[:, :mla_cols], w_krope], axis=1)
    xf, z_mla, z_dil = _ffn_proj(xf, ffn1_norm[0], bf(ffn1_w_gate[0]), bf(ffn1_w_up[0]), bf(ffn1_w_down[0]),
                                 mix_norm[0], [bf(w_mla), bf(w_in[:, mla_cols + MLA_ROPE:])])

    wq = _pad_cols(mla_w_q_b[0].reshape(MLA_Q_RANK * MLA_HEADS, MLA_QK_DIM), MLA_PAD)
    wq = wq.reshape(MLA_Q_RANK, MLA_HEADS * MLA_PAD)
    wkv = mla_w_kv_b[0].reshape(MLA_KV_RANK, MLA_HEADS, MLA_NOPE + MLA_V)
    wk = _pad_cols(wkv[:, :, :MLA_NOPE].reshape(MLA_KV_RANK * MLA_HEADS, MLA_NOPE), MLA_PAD)
    wk = wk.reshape(MLA_KV_RANK, MLA_HEADS * MLA_PAD)
    wv = wkv[:, :, MLA_NOPE:].reshape(MLA_KV_RANK, MLA_HEADS * MLA_V)
    rope_c, rope_m, rope_p = _rope_tables(seq)
    q_mla, k_mla, vt_mla = _mla_prep(
        z_mla, mla_q_a_norm[0].reshape(1, -1), mla_kv_a_norm[0].reshape(1, -1), bf(wq), bf(wk), bf(wv),
        _pad_cols(mla_q_norm[0].reshape(1, -1), MLA_PAD), _pad_cols(mla_k_norm[0].reshape(1, -1), MLA_PAD),
        rope_c, rope_m, rope_p, seq)
    o_a = _mla_attn(q_mla.reshape(batch, seq, -1), k_mla.reshape(batch, seq, -1),
                    vt_mla.reshape(batch, seq // MLA_TQ, MLA_HEADS * MLA_V, MLA_TQ),
                    batch, seq).reshape(n, MLA_HEADS * MLA_V)

    z_dil3 = z_dil.reshape(batch, seq, -1)
    gq, gk = dil_q_norm[0].reshape(HEAD_DIM, 1), dil_k_norm[0].reshape(1, HEAD_DIM)
    dil_outs, dil_lses = [], []
    for grp, (window, dilation) in enumerate(DIL_PATTERNS):
        bm = _band_bias(rel_bias[:, grp * DIL_HPG:(grp + 1) * DIL_HPG], window // dilation + 1, 1, dilation,
                        DIL_HPG, 1)
        o, lse = _dilated_group(z_dil3, batch, seq, grp, dilation, gq, gk, bm)
        dil_outs.append(o)
        dil_lses.append(lse)
    w_out = ab_w_out[0]
    xf = _out_ab_ffn(xf, o_a, dil_outs, dil_lses, bf(w_out[:MLA_HEADS * MLA_V]), bf(w_out[MLA_HEADS * MLA_V:]),
                     ffn2_norm[0], bf(ffn2_w_gate[0]), bf(ffn2_w_up[0]), bf(ffn2_w_down[0]))

    w_in = cd_w_in[0]
    swa_cols = (SWA_HEADS + 2 * SWA_KV_HEADS) * HEAD_DIM
    w_nsa_src = w_in[:, swa_cols:]
    w_nsa = jnp.concatenate([_pad_cols(w_nsa_src[:, np.asarray(cols)], NSA_GROUP_COLS)
                             for cols in _nsa_column_order()], axis=1)
    w_cmp = w_nsa_src[:, np.asarray(_nsa_cmp_columns())]
    xf, z_swa, z_nsa, z_cmp = _ffn_proj(xf, ffn1_norm[1], bf(ffn1_w_gate[1]), bf(ffn1_w_up[1]), bf(ffn1_w_down[1]),
                                        mix_norm[1], [bf(w_in[:, :swa_cols]), bf(w_nsa), bf(w_cmp)],
                                        [F32, F32, BF16])

    swa_reps = SWA_HEADS // SWA_KV_HEADS
    swa_prev = -(-(SWA_WINDOW - 1) // QB)
    bm_swa = _band_bias(rel_bias[:, :SWA_HEADS], SWA_WINDOW, swa_prev, 1, SWA_KV_HEADS, swa_reps)
    sinks = jnp.broadcast_to((swa_sinks[0].astype(F32) * LOG2E).reshape(SWA_KV_HEADS, 1, swa_reps, 1),
                             (SWA_KV_HEADS, 1, swa_reps, QB)).reshape(SWA_KV_HEADS, 1, swa_reps * QB)
    o_c = _swa(z_swa.reshape(batch, seq, -1), batch, seq, swa_q_norm[0].reshape(HEAD_DIM, 1),
               swa_k_norm[0].reshape(1, HEAD_DIM), bm_swa, sinks)

    o_d = _nsa_mixer(z_nsa.reshape(batch, seq, NSA_KV_HEADS * NSA_GROUP_COLS), z_cmp, rel_bias, nsa_q_norm[0],
                     nsa_k_norm[0], nsa_cmp_pos[0], nsa_cmp_w1[0], nsa_cmp_w2[0], batch, seq)

    w_out = cd_w_out[0]
    xf = _out_cd_ffn(xf, o_c.reshape(n, -1), o_d.reshape(n, -1), bf(w_out[:SWA_HEADS * HEAD_DIM]),
                     bf(w_out[SWA_HEADS * HEAD_DIM:]), ffn2_norm[1], bf(ffn2_w_gate[1]), bf(ffn2_w_up[1]),
                     bf(ffn2_w_down[1]))
    return xf.reshape(batch, seq, D_MODEL)
```

```python
import functools
import math

import numpy as np
import jax
import jax.numpy as jnp
from jax import lax
from jax.experimental import pallas as pl
from jax.experimental.pallas import tpu as pltpu

F32 = jnp.float32
BF16 = jnp.bfloat16

EPS = 1e-6
NEG = -1e30
TINY = 1e-30
LOG2E = math.log2(math.e)
LN2 = math.log(2.0)
D_MODEL = 1024
D_FF = 2816
NUM_BUCKETS = 32
MAX_DISTANCE = 2048
HEAD_DIM = 64
QB = 128

MLA_HEADS = 8
MLA_Q_RANK = 256
MLA_KV_RANK = 128
MLA_NOPE = 64
MLA_ROPE = 32
MLA_V = 64
MLA_QK_DIM = MLA_NOPE + MLA_ROPE
MLA_PAD = 128
ROPE_THETA = 10000.0

DIL_PATTERNS = ((128, 1), (512, 4), (2048, 16))
DIL_HPG = 4
SWA_HEADS = 8
SWA_KV_HEADS = 2
SWA_WINDOW = 128
NSA_HEADS = 8
NSA_KV_HEADS = 2
NSA_CMP_LEN = 32
NSA_CMP_STRIDE = 16
NSA_CMP_HIDDEN = 128
NSA_SLC_BLOCK = 64
NSA_TOP_N = 16
NSA_WINDOW = 512
NSA_FORCE = 1e6
NSA_BIAS_COL0 = 8
NSA_GROUP_COLS = 768

VMEM_LIMIT = 56 * 1024 * 1024
TM = 512

_NT = (((1,), (1,)), ((), ()))
_HI = lax.Precision.HIGHEST


def _cparams(n_axes):
    return pltpu.CompilerParams(dimension_semantics=("arbitrary",) * n_axes,
                                vmem_limit_bytes=VMEM_LIMIT)


def _resident(shape):
    nd = len(shape)
    return pl.BlockSpec(shape, lambda *_: (0,) * nd, pipeline_mode=pl.Buffered(1))


def _rms(x):
    return x * lax.rsqrt(jnp.mean(x * x, axis=-1, keepdims=True) + EPS)


def _dot(a, b):
    return jnp.dot(a, b, preferred_element_type=F32)


def _dot_nt(a, b, precision=None):
    return lax.dot_general(a, b, _NT, preferred_element_type=F32, precision=precision)


FF_CHUNK = 256


N_CHUNK = 512


def _swiglu_half_step(x, g_ref, wg_ref, wu_ref, wd_ref):
    hb = (_rms(x) * g_ref[...]).astype(BF16)
    acc = jnp.zeros(x.shape, F32)
    for c in range(D_FF // FF_CHUNK):
        sl = slice(c * FF_CHUNK, (c + 1) * FF_CHUNK)
        gate = _dot(hb, wg_ref[:, sl])
        up = _dot(hb, wu_ref[:, sl])
        act = (gate * jax.nn.sigmoid(gate) * up).astype(BF16)
        acc = acc + _dot(act, wd_ref[sl, :])
    return x + 0.5 * acc


def _ffn_proj_kernel(n_out, x_ref, g_ref, wg_ref, wu_ref, wd_ref, gm_ref, *refs):
    w_refs, x_out_ref, z_refs = refs[:n_out], refs[n_out], refs[n_out + 1:]
    x = _swiglu_half_step(x_ref[...], g_ref, wg_ref, wu_ref, wd_ref)
    x_out_ref[...] = x
    hb = (_rms(x) * gm_ref[...]).astype(BF16)
    for w_ref, z_ref in zip(w_refs, z_refs):
        width = w_ref.shape[1]
        for c0 in range(0, width, N_CHUNK):
            sl = slice(c0, min(c0 + N_CHUNK, width))
            z_ref[:, sl] = _dot(hb, w_ref[:, sl]).astype(z_ref.dtype)


def _tile(width):
    return pl.BlockSpec((TM, width), lambda i: (i, 0))


def _ffn_proj(x, g, wg, wu, wd, g_mix, ws, z_dtypes=None):
    n = x.shape[0]
    z_dtypes = z_dtypes or [F32] * len(ws)
    ffn_w = [_resident((D_MODEL, D_FF)), _resident((D_MODEL, D_FF)), _resident((D_FF, D_MODEL))]
    return pl.pallas_call(
        functools.partial(_ffn_proj_kernel, len(ws)),
        grid=(n // TM,),
        in_specs=[_tile(D_MODEL), _resident((1, D_MODEL))] + ffn_w + [_resident((1, D_MODEL))]
        + [_resident(w.shape) for w in ws],
        out_specs=[_tile(D_MODEL)] + [_tile(w.shape[1]) for w in ws],
        out_shape=[jax.ShapeDtypeStruct((n, D_MODEL), F32)]
        + [jax.ShapeDtypeStruct((n, w.shape[1]), dt) for w, dt in zip(ws, z_dtypes)],
        compiler_params=_cparams(1),
        name="ffn_proj",
    )(x, g.reshape(1, D_MODEL), wg, wu, wd, g_mix.reshape(1, D_MODEL), *ws)


def _out_ab_ffn_kernel(x_ref, oa_ref, o0_ref, o1_ref, o2_ref, l0_ref, l1_ref, l2_ref, wa_ref, wb_ref,
                       g_ref, wg_ref, wu_ref, wd_ref, out_ref):
    l0, l1, l2 = l0_ref[...], l1_ref[...], l2_ref[...]
    m = jnp.maximum(jnp.maximum(l0, l1), l2)
    e0, e1, e2 = jnp.exp(l0 - m), jnp.exp(l1 - m), jnp.exp(l2 - m)
    ob = (e0 * o0_ref[...] + e1 * o1_ref[...] + e2 * o2_ref[...]) / (e0 + e1 + e2)
    x = x_ref[...] + _dot(oa_ref[...], wa_ref[...]) + _dot(ob.astype(BF16), wb_ref[...])
    out_ref[...] = _swiglu_half_step(x, g_ref, wg_ref, wu_ref, wd_ref)


def _out_ab_ffn(x, oa, dil_outs, dil_lses, wa, wb, g, wg, wu, wd):
    n = x.shape[0]
    dw = DIL_HPG * HEAD_DIM
    ffn_w = [_resident((D_MODEL, D_FF)), _resident((D_MODEL, D_FF)), _resident((D_FF, D_MODEL))]
    return pl.pallas_call(
        _out_ab_ffn_kernel,
        grid=(n // TM,),
        in_specs=[_tile(D_MODEL), _tile(oa.shape[1])] + [_tile(dw)] * 6 + [_resident(wa.shape), _resident(wb.shape),
                                                                        _resident((1, D_MODEL))] + ffn_w,
        out_specs=_tile(D_MODEL),
        out_shape=jax.ShapeDtypeStruct((n, D_MODEL), F32),
        compiler_params=_cparams(1),
        name="out_ab_ffn",
    )(x, oa, *dil_outs, *dil_lses, wa, wb, g.reshape(1, D_MODEL), wg, wu, wd)


def _out_cd_ffn_kernel(x_ref, oc_ref, od_ref, wc_ref, wdd_ref, g_ref, wg_ref, wu_ref, wd_ref, out_ref):
    x = x_ref[...] + _dot(oc_ref[...], wc_ref[...]) + _dot(od_ref[...], wdd_ref[...])
    out_ref[...] = _swiglu_half_step(x, g_ref, wg_ref, wu_ref, wd_ref)


def _out_cd_ffn(x, oc, od, wc, wdd, g, wg, wu, wd):
    n = x.shape[0]
    ffn_w = [_resident((D_MODEL, D_FF)), _resident((D_MODEL, D_FF)), _resident((D_FF, D_MODEL))]
    return pl.pallas_call(
        _out_cd_ffn_kernel,
        grid=(n // TM,),
        in_specs=[_tile(D_MODEL), _tile(oc.shape[1]), _tile(od.shape[1]), _resident(wc.shape), _resident(wdd.shape),
                  _resident((1, D_MODEL))] + ffn_w,
        out_specs=_tile(D_MODEL),
        out_shape=jax.ShapeDtypeStruct((n, D_MODEL), F32),
        compiler_params=_cparams(1),
        name="out_cd_ffn",
    )(x, oc, od, wc, wdd, g.reshape(1, D_MODEL), wg, wu, wd)


def _mla_prep_kernel(z_ref, qan_ref, kvan_ref, wq_ref, wk_ref, wv_ref, gq_ref, gk_ref,
                     qc_ref, qs_ref, kc_ref, ks_ref, q_ref, k_ref, vt_ref):
    cq = (_rms(z_ref[:, :MLA_Q_RANK]) * qan_ref[...]).astype(BF16)
    ckv = (_rms(z_ref[:, MLA_Q_RANK:MLA_Q_RANK + MLA_KV_RANK]) * kvan_ref[...]).astype(BF16)
    kr = z_ref[:, MLA_Q_RANK + MLA_KV_RANK:]
    real = jnp.where(lax.broadcasted_iota(jnp.int32, (1, MLA_PAD), 1) < MLA_QK_DIM, 1.0, 0.0)

    def head_norm_rope(x, gain, cos, sin):
        ssq = jnp.sum(x * x * real, axis=-1, keepdims=True)
        x = x * lax.rsqrt(ssq * (1.0 / MLA_QK_DIM) + EPS) * gain
        return x * cos + pltpu.roll(x, MLA_PAD - MLA_ROPE, 1) * sin

    for h in range(MLA_HEADS):
        sl = slice(h * MLA_PAD, (h + 1) * MLA_PAD)
        qh = head_norm_rope(_dot(cq, wq_ref[:, sl]), gq_ref[...], qc_ref[...], qs_ref[...])
        q_ref[:, sl] = qh.astype(BF16)
        kh = head_norm_rope(_dot(ckv, wk_ref[:, sl]) + kr, gk_ref[...], kc_ref[...], ks_ref[...])
        k_ref[:, sl] = kh.astype(BF16)
    v = _dot(ckv, wv_ref[...])
    for c in range(TM // MLA_TQ):
        vt_ref[c] = v[c * MLA_TQ:(c + 1) * MLA_TQ].T.astype(BF16)


MLA_TQ = 256
MLA_STEP = 2


def _mla_prep(z, qan, kvan, wq, wk, wv, gq, gk, q_cos, q_sin, k_cos, k_sin, seq):
    n = z.shape[0]
    per_seq = seq // TM
    tile = lambda w: pl.BlockSpec((TM, w), lambda i: (i, 0))
    rope_spec = pl.BlockSpec((TM, MLA_PAD), lambda i: (i % per_seq, 0))
    hw = MLA_HEADS * MLA_PAD
    vw = MLA_HEADS * MLA_V
    return pl.pallas_call(
        _mla_prep_kernel,
        grid=(n // TM,),
        in_specs=[tile(z.shape[1]), _resident(qan.shape), _resident(kvan.shape), _resident(wq.shape),
                  _resident(wk.shape), _resident(wv.shape), _resident(gq.shape), _resident(gk.shape),
                  rope_spec, rope_spec, rope_spec, rope_spec],
        out_specs=[tile(hw), tile(hw), pl.BlockSpec((TM // MLA_TQ, vw, MLA_TQ), lambda i: (i, 0, 0))],
        out_shape=[jax.ShapeDtypeStruct((n, hw), BF16), jax.ShapeDtypeStruct((n, hw), BF16),
                   jax.ShapeDtypeStruct((n // MLA_TQ, vw, MLA_TQ), BF16)],
        compiler_params=_cparams(1),
        name="mla_prep",
    )(z, qan, kvan, wq, wk, wv, gq, gk, q_cos, q_sin, k_cos, k_sin)


def _flash_step_t(s, m, l, acc, pv, vt):
    m_new = jnp.maximum(m, jnp.max(s, axis=0, keepdims=True))
    p = jnp.exp2(s - m_new)
    alpha = jnp.exp2(m - m_new)
    l = alpha * l + jnp.sum(p, axis=0, keepdims=True)
    acc = alpha * (acc + pv)
    return m_new, l, acc, _dot(vt, p.astype(BF16))


def _flash_init(dv, n_queries):
    return (jnp.full((1, n_queries), NEG, F32), jnp.zeros((1, n_queries), F32), jnp.zeros((dv, n_queries), F32),
            jnp.zeros((dv, n_queries), F32))


def _flash_finish(m, l, acc, pv):
    return (acc + pv) * (1.0 / l)


def _mla_attn_kernel(seq, q_ref, k_ref, vt_ref, o_ref):
    t = MLA_TQ
    key = lax.broadcasted_iota(jnp.int32, (t, t), 0)
    qry = lax.broadcasted_iota(jnp.int32, (t, t), 1)
    diag_ok = key <= qry
    heads = [(slice(hh * MLA_PAD, (hh + 1) * MLA_PAD), slice(hh * MLA_V, (hh + 1) * MLA_V)) for hh in range(2)]

    init = _flash_init(MLA_V, t)
    for i in range(seq // t):
        rows = slice(i * t, (i + 1) * t)
        qs = [q_ref[rows, ql] for ql, _ in heads]

        steps = [(j, min(j + MLA_STEP, i + 1)) for j in range(0, i + 1, MLA_STEP)]

        def scores(h, step):
            lo, hi = steps[step]
            s = _dot_nt(k_ref[lo * t:hi * t, heads[h][0]], qs[h])
            if hi == i + 1:
                last = jnp.where(diag_ok, s[(i - lo) * t:], NEG)
                s = last if i == lo else jnp.concatenate([s[:(i - lo) * t], last], axis=0)
            return s

        def values(h, step):
            lo, hi = steps[step]
            return jnp.concatenate([vt_ref[j, heads[h][1], :] for j in range(lo, hi)], axis=1)

        cur = [scores(h, 0) for h in range(2)]
        states = [init, init]
        for step in range(len(steps)):
            for h in range(2):
                nxt = scores(h, step + 1) if step + 1 < len(steps) else None
                states[h] = _flash_step_t(cur[h], *states[h], values(h, step))
                cur[h] = nxt
        outs = [_flash_finish(*state) for state in states]
        o_ref[rows, :] = jnp.concatenate(outs, axis=0).T.astype(o_ref.dtype)


def _mla_attn(q, k, vt, batch, seq):
    return pl.pallas_call(
        functools.partial(_mla_attn_kernel, seq),
        grid=(batch, MLA_HEADS // 2),
        in_specs=[pl.BlockSpec((None, seq, 2 * MLA_PAD), lambda b, h: (b, 0, h)),
                  pl.BlockSpec((None, seq, 2 * MLA_PAD), lambda b, h: (b, 0, h)),
                  pl.BlockSpec((None, seq // MLA_TQ, 2 * MLA_V, MLA_TQ), lambda b, h: (b, 0, h, 0))],
        out_specs=pl.BlockSpec((None, seq, 2 * MLA_V), lambda b, h: (b, 0, h)),
        out_shape=jax.ShapeDtypeStruct((batch, seq, MLA_HEADS * MLA_V), BF16),
        compiler_params=_cparams(2),
        name="mla_attn",
    )(q, k, vt)


def _head_norm(x, gain):
    return _rms(x) * gain


def _heads_t(qt, n_heads, gain):
    tiles = []
    for c0 in range(0, n_heads * HEAD_DIM, 2 * HEAD_DIM):
        pair = qt[:, c0:c0 + 2 * HEAD_DIM].T
        for h in range(2):
            x = pair[h * HEAD_DIM:(h + 1) * HEAD_DIM]
            tiles.append(x * lax.rsqrt(jnp.mean(x * x, axis=0, keepdims=True) + EPS) * gain)
    return tiles


def _q_gain(gq_ref):
    return jnp.broadcast_to(gq_ref[...] * (HEAD_DIM ** -0.5 * LOG2E), (HEAD_DIM, QB))


def _pair_ones():
    shape = (2 * HEAD_DIM, 2 * HEAD_DIM)
    same = lax.broadcasted_iota(jnp.int32, shape, 0) // HEAD_DIM == lax.broadcasted_iota(jnp.int32, shape, 1) // HEAD_DIM
    return jnp.where(same, 1.0, 0.0).astype(BF16)


def _pair_norm(x, seg, gain):
    sq = x * x
    hi = sq.astype(BF16)
    lo = (sq - hi.astype(F32)).astype(BF16)
    ssq = _dot(hi, seg) + _dot(lo, seg)
    return x * lax.rsqrt(ssq * (1.0 / HEAD_DIM) + EPS) * gain


BAND_KV = 2


def _banded_kernel(seq, dilation, reps, n_prev, with_sink, with_lse, group_size, paired, *refs):
    q_ref, k_ref, v_ref, gq_ref, gk_ref, bm_ref = refs[:6]
    refs = refs[6:]
    sink_ref = None
    if with_sink:
        sink_ref, refs = refs[0], refs[1:]
    o_ref, refs = refs[0], refs[1:]
    lse_ref = None
    if with_lse:
        lse_ref, refs = refs[0], refs[1:]
    kn_ref, vt_ref = refs
    length = seq // dilation
    n_blocks = length // QB
    pad = n_prev * QB
    kw = pad + QB
    win_row = lax.broadcasted_iota(jnp.int32, (kw, 1), 0)
    assert not paired or reps == 1
    if paired:
        gq, gk = gq_ref[...] * (HEAD_DIM ** -0.5 * LOG2E), gk_ref[...]
        seg = _pair_ones()
        lane = lax.broadcasted_iota(jnp.int32, (1, BAND_KV * HEAD_DIM), 1)
        head_lanes = [jnp.where(lane // HEAD_DIM == g, 1.0, 0.0) for g in range(BAND_KV)]
    else:
        gq, gk = _q_gain(gq_ref), gk_ref[...]

    def rows(start, size):
        return pl.ds(start, size) if dilation == 1 else pl.ds(start, size, stride=dilation)

    for res in range(dilation):
        kall, vall = k_ref[rows(res, length), :], v_ref[rows(res, length), :]
        if paired:
            kn_ref[res, :pad, :] = jnp.zeros((pad, BAND_KV * HEAD_DIM), BF16)
            kn_ref[res, pad:, :] = _pair_norm(kall, seg, gk).astype(BF16)
        else:
            for g in range(BAND_KV):
                kn_ref[res, g, :pad, :] = jnp.zeros((pad, HEAD_DIM), BF16)
                kn_ref[res, g, pad:, :] = _head_norm(kall[:, g * HEAD_DIM:(g + 1) * HEAD_DIM], gk).astype(BF16)
        for j in range(n_prev):
            vt_ref[res, j] = jnp.zeros((BAND_KV * HEAD_DIM, QB), BF16)
        for j in range(n_blocks):
            vt_ref[res, n_prev + j] = vall[j * QB:(j + 1) * QB].T.astype(BF16)

    def attend(res, i, g, s):
        bias = bm_ref[g]
        if i < n_prev:
            bias = jnp.where(win_row >= (n_prev - i) * QB, bias, NEG)
        s = s + bias
        m = jnp.max(s, axis=0, keepdims=True)
        if with_sink:
            m = jnp.maximum(m, sink_ref[g])
        p = jnp.exp2(s - m)
        l = jnp.sum(p, axis=0, keepdims=True)
        if with_sink:
            l = l + jnp.exp2(sink_ref[g] - m)
        p = p.astype(BF16)
        vl = slice(g * HEAD_DIM, (g + 1) * HEAD_DIM)
        o = _dot(vt_ref[res, i, vl, :], p[0:QB])
        for w in range(1, n_prev + 1):
            o = o + _dot(vt_ref[res, i + w, vl, :], p[w * QB:(w + 1) * QB])
        return o * (1.0 / l), m * LN2 + jnp.log(l)

    items = [(res, i) for res in range(dilation) for i in range(n_blocks)]
    for at in range(0, len(items), group_size):
        group = items[at:at + group_size]
        toks = [rows(res + i * QB * dilation, QB) for res, i in group]
        if paired:
            qns = [_pair_norm(q_ref[tok, :], seg, gq) for tok in toks]
            qss = [[(qn * head_lanes[g]).astype(BF16) for g in range(BAND_KV)] for qn in qns]
            scores = [[_dot_nt(kn_ref[res, i * QB:i * QB + kw, :], qs[g]) for g in range(BAND_KV)]
                      for (res, i), qs in zip(group, qss)]
        else:
            heads = [_heads_t(q_ref[tok, :], BAND_KV * reps, gq) for tok in toks]
            qss = [[jnp.concatenate(h[g * reps:(g + 1) * reps], axis=1).astype(BF16) for g in range(BAND_KV)]
                   for h in heads]
            scores = [[_dot(kn_ref[res, g, i * QB:i * QB + kw, :], qs[g]) for g in range(BAND_KV)]
                      for (res, i), qs in zip(group, qss)]
        for (res, i), tok, sc in zip(group, toks, scores):
            outs, lses = [], []
            for g in range(BAND_KV):
                o, lse = attend(res, i, g, sc[g])
                for r in range(reps):
                    outs.append(o[:, r * QB:(r + 1) * QB])
                    if with_lse:
                        lses.append(jnp.broadcast_to(lse[:, r * QB:(r + 1) * QB], (HEAD_DIM, QB)))
            o_ref[tok, :] = jnp.concatenate(outs, axis=0).T.astype(o_ref.dtype)
            if with_lse:
                lse_ref[tok, :] = jnp.concatenate(lses, axis=0).T


def _banded_scratch(seq, dilation, n_prev, paired):
    length = seq // dilation
    rows = n_prev * QB + length
    keys = (dilation, rows, BAND_KV * HEAD_DIM) if paired else (dilation, BAND_KV, rows, HEAD_DIM)
    return [pltpu.VMEM(keys, BF16),
            pltpu.VMEM((dilation, n_prev + length // QB, BAND_KV * HEAD_DIM, QB), BF16)]


PAIRED_MIN_DILATION = 16


def _dilated_group(z_dil, batch, seq, group, dilation, q_norm, k_norm, bm):
    gw = DIL_HPG * HEAD_DIM
    n_groups = len(DIL_PATTERNS)
    hps = BAND_KV
    halves = DIL_HPG // hps
    bm_t = jnp.transpose(bm, (0, 2, 1))
    paired = dilation >= PAIRED_MIN_DILATION
    if paired:
        gq = jnp.tile(q_norm.reshape(1, HEAD_DIM), (1, BAND_KV))
        gk = jnp.tile(k_norm.reshape(1, HEAD_DIM), (1, BAND_KV))
    else:
        gq, gk = q_norm.reshape(HEAD_DIM, 1), k_norm.reshape(1, HEAD_DIM)

    def spec(part):
        return pl.BlockSpec((None, seq, hps * HEAD_DIM), lambda b, h: (b, 0, (part * n_groups + group) * halves + h))

    out_spec = pl.BlockSpec((None, seq, hps * HEAD_DIM), lambda b, h: (b, 0, h))
    shape = jax.ShapeDtypeStruct((batch, seq, gw), F32)
    out, lse = pl.pallas_call(
        functools.partial(_banded_kernel, seq, dilation, 1, 1, False, True, 4, paired),
        grid=(batch, halves),
        in_specs=[spec(0), spec(1), spec(2), _resident(gq.shape), _resident(gk.shape),
                  pl.BlockSpec((hps,) + bm_t.shape[1:], lambda b, h: (h, 0, 0))],
        out_specs=[out_spec, out_spec],
        out_shape=[shape, shape],
        scratch_shapes=_banded_scratch(seq, dilation, 1, paired),
        compiler_params=_cparams(2),
        name="dilated_g%d" % group,
    )(z_dil, z_dil, z_dil, gq, gk, bm_t)
    return out.reshape(batch * seq, gw), lse.reshape(batch * seq, gw)


def _swa(z_swa, batch, seq, gq, gk, bm, sinks):
    qw = SWA_HEADS * HEAD_DIM
    kvw = SWA_KV_HEADS * HEAD_DIM
    reps = SWA_HEADS // SWA_KV_HEADS
    n_prev = -(-(SWA_WINDOW - 1) // QB)
    assert SWA_KV_HEADS == BAND_KV
    bm_t = jnp.transpose(bm, (0, 2, 1))
    return pl.pallas_call(
        functools.partial(_banded_kernel, seq, 1, reps, n_prev, True, False, 2, False),
        grid=(batch,),
        in_specs=[pl.BlockSpec((None, seq, qw), lambda b: (b, 0, 0)),
                  pl.BlockSpec((None, seq, kvw), lambda b: (b, 0, qw // kvw)),
                  pl.BlockSpec((None, seq, kvw), lambda b: (b, 0, qw // kvw + 1)),
                  _resident(gq.shape), _resident(gk.shape), _resident(bm_t.shape), _resident(sinks.shape)],
        out_specs=pl.BlockSpec((None, seq, qw), lambda b: (b, 0, 0)),
        out_shape=jax.ShapeDtypeStruct((batch, seq, qw), BF16),
        scratch_shapes=_banded_scratch(seq, 1, n_prev, False),
        compiler_params=_cparams(1),
        name="swa",
    )(z_swa, z_swa, z_swa, gq, gk, bm_t, sinks)


NSA_REPS = NSA_HEADS // NSA_KV_HEADS
NSA_N_SLC = 32
NSA_N_CMP_PAD = 128
NSA_WIN_PREV = -(-(NSA_WINDOW - 1) // QB)
_NQ = NSA_REPS * HEAD_DIM
_OFF_KC, _OFF_VC, _OFF_KS, _OFF_VS, _OFF_KW, _OFF_VW, _OFF_GATE = (_NQ + i * HEAD_DIM for i in range(7))


NSA_TIERS = 4
NSA_STEP = 8
NSA_QPAIR = 2


def _nsa_kernel(seq, z_ref, zc_ref, w1a_ref, w1b_ref, w2_ref, pos_ref, gq_ref, gk_ref, ovt_ref,
                bslc_ref, bwin_ref, o_ref, ks_ref, vst_ref, kw_ref, vwt_ref):
    n_chunks = seq // QB
    lanes = NSA_REPS * QB
    gq = _q_gain(gq_ref)
    win_pad = NSA_WIN_PREV * QB
    win_kw = win_pad + QB

    first = jnp.zeros((NSA_N_CMP_PAD, 2 * NSA_CMP_HIDDEN), F32)
    second = jnp.zeros((NSA_N_CMP_PAD, 2 * NSA_CMP_HIDDEN), F32)
    for l in range(NSA_CMP_STRIDE):
        tok = zc_ref[pl.ds(l, NSA_N_CMP_PAD, stride=NSA_CMP_STRIDE), :]
        first = first + _dot((tok + pos_ref[l:l + 1, :]).astype(BF16), w1a_ref[l])
        second = second + _dot((tok + pos_ref[NSA_CMP_STRIDE + l:NSA_CMP_STRIDE + l + 1, :]).astype(BF16),
                               w1b_ref[l])
    hidden = jax.nn.gelu(first + pltpu.roll(second, NSA_N_CMP_PAD - 1, 0)).astype(BF16)
    cmp_kv = _dot(hidden, w2_ref[...])
    kc = _head_norm(cmp_kv[:, :HEAD_DIM], gk_ref[0]).astype(BF16)
    vct = cmp_kv.T[HEAD_DIM:].astype(BF16)

    ks_ref[...] = _head_norm(z_ref[:, _OFF_KS:_OFF_KS + HEAD_DIM], gk_ref[1]).astype(BF16)
    kw_ref[:win_pad, :] = jnp.zeros((win_pad, HEAD_DIM), BF16)
    kw_ref[win_pad:, :] = _head_norm(z_ref[:, _OFF_KW:_OFF_KW + HEAD_DIM], gk_ref[2]).astype(BF16)
    for j in range(NSA_WIN_PREV):
        vwt_ref[j] = jnp.zeros((HEAD_DIM, QB), BF16)
    for j in range(n_chunks):
        rows = slice(j * QB, (j + 1) * QB)
        vst_ref[:, rows] = z_ref[rows, _OFF_KS:_OFF_KS + 2 * HEAD_DIM].T[HEAD_DIM:].astype(BF16)
        vwt_ref[NSA_WIN_PREV + j] = z_ref[rows, _OFF_KW:_OFF_KW + 2 * HEAD_DIM].T[HEAD_DIM:].astype(BF16)

    cmp_id = lax.broadcasted_iota(jnp.int32, (NSA_N_CMP_PAD, lanes), 0)
    cmp_end = cmp_id * NSA_CMP_STRIDE + (NSA_CMP_LEN - 1)
    cmp_real = cmp_id < NSA_N_CMP_PAD - 1
    q_in_blk = lax.broadcasted_iota(jnp.int32, (NSA_N_CMP_PAD, lanes), 1) & (QB - 1)
    blk_id = lax.broadcasted_iota(jnp.int32, (NSA_N_SLC, QB), 0)
    q_lane = lax.broadcasted_iota(jnp.int32, (NSA_N_SLC, QB), 1)
    win_row = lax.broadcasted_iota(jnp.int32, (win_kw, 1), 0)
    init = _flash_init(HEAD_DIM, lanes)
    both = range(NSA_QPAIR)

    def q_blocks(n_keys, mask_pad, pair, carry):
        step = max(d for d in range(1, NSA_STEP + 1) if n_keys % d == 0)
        step_keys = step * QB
        blocks_per_step = step_keys // NSA_SLC_BLOCK
        blk = [pair * NSA_QPAIR + b for b in both]
        r0 = [pl.multiple_of(i * QB, QB) for i in blk]
        qs = [jnp.concatenate(_heads_t(z_ref[pl.ds(r, QB), :_NQ], NSA_REPS, gq), axis=1).astype(BF16)
              for r in r0]

        sc = [_dot(kc, q) for q in qs]
        s_win = [_dot(kw_ref[pl.ds(r, win_kw), :], q) for r, q in zip(r0, qs)]

        def slc_scores(b, j):
            return _dot(ks_ref[j * step_keys:(j + 1) * step_keys, :], qs[b])

        cur = [slc_scores(b, 0) for b in both]

        o_cmp, imp = [], []
        for b in both:
            ok = (cmp_end <= q_in_blk + blk[b] * QB) & cmp_real
            s = jnp.where(ok, sc[b], NEG)
            e = jnp.where(ok, jnp.exp2(s - jnp.max(s, axis=0, keepdims=True)), 0.0)
            p = e * (1.0 / jnp.maximum(jnp.sum(e, axis=0, keepdims=True), TINY))
            o_cmp.append(_dot(vct, p.astype(BF16)))
            p_sum = p[:, 0:QB] + p[:, QB:2 * QB] + p[:, 2 * QB:3 * QB] + p[:, 3 * QB:4 * QB]
            imp.append(jnp.dot(ovt_ref[...], p_sum, precision=_HI, preferred_element_type=F32))

        o_win = []
        for b in both:
            bias = bwin_ref[...]
            if mask_pad:
                bias = jnp.where(win_row >= (NSA_WIN_PREV - blk[b]) * QB, bias, NEG)
            s = s_win[b] + bias
            pw = jnp.exp2(s - jnp.max(s, axis=0, keepdims=True))
            lw = jnp.sum(pw, axis=0, keepdims=True)
            pw = pw.astype(BF16)
            o = _dot(vwt_ref[blk[b]], pw[0:QB])
            for c in range(1, NSA_WIN_PREV + 1):
                o = o + _dot(vwt_ref[blk[b] + c], pw[c * QB:(c + 1) * QB])
            o_win.append(o * (1.0 / lw))

        drop = []
        for b in both:
            tb = lax.shift_right_logical(q_lane + blk[b] * QB, 6)
            forced = (blk_id == 0) | (blk_id == tb) | (blk_id == tb - 1)
            score = jnp.where(blk_id <= tb, imp[b] + jnp.where(forced, NSA_FORCE, 0.0), -NSA_FORCE)
            rank = jnp.zeros((NSA_N_SLC, QB), F32)
            for other in range(NSA_N_SLC):
                s_o = score[other:other + 1, :]
                beats = (s_o > score) | ((s_o == score) & (blk_id > other))
                rank = rank + jnp.where(beats, 1.0, 0.0)
            drop.append(jnp.where(rank < NSA_TOP_N, 0.0, NEG))

        states = [init for _ in both]
        for j in range(n_keys // step):
            for b in both:
                nxt = slc_scores(b, j + 1) if (j + 1) * step < n_keys else None
                bias = jnp.concatenate(
                    [bslc_ref[jnp.maximum(blk[b] - (j * step + c) + 1, 0)] for c in range(step)], axis=0)
                km = jnp.concatenate([jnp.broadcast_to(drop[b][k:k + 1], (NSA_SLC_BLOCK, QB))
                                      for k in range(j * blocks_per_step, (j + 1) * blocks_per_step)], axis=0)
                s = cur[b] + bias + jnp.concatenate([km] * NSA_REPS, axis=1)
                states[b] = _flash_step_t(s, *states[b], vst_ref[:, j * step_keys:(j + 1) * step_keys])
                cur[b] = nxt

        for b in both:
            o_slc = _flash_finish(*states[b])
            gate = jax.nn.sigmoid(z_ref[pl.ds(r0[b], QB), _OFF_GATE:_OFF_GATE + QB]).T
            outs = []
            for r in range(NSA_REPS):
                qsl = slice(r * QB, (r + 1) * QB)
                outs.append(gate[3 * r:3 * r + 1] * o_cmp[b][:, qsl] + gate[3 * r + 1:3 * r + 2] * o_slc[:, qsl]
                            + gate[3 * r + 2:3 * r + 3] * o_win[b][:, qsl])
            o_ref[pl.ds(r0[b], QB), :] = jnp.concatenate(outs, axis=0).T.astype(o_ref.dtype)
        return carry

    per_tier = n_chunks // NSA_TIERS
    for tier in range(NSA_TIERS):
        lo, hi = tier * per_tier, (tier + 1) * per_tier
        lax.fori_loop(lo // NSA_QPAIR, hi // NSA_QPAIR, functools.partial(q_blocks, hi, lo < NSA_WIN_PREV), 0)


def _nsa(z_nsa, zc, w1a, w1b, w2, pos, gq, gk, ovt, bslc, bwin, batch, seq):
    gw = NSA_GROUP_COLS
    n_chunks = seq // QB
    return pl.pallas_call(
        functools.partial(_nsa_kernel, seq),
        grid=(batch, NSA_KV_HEADS),
        in_specs=[pl.BlockSpec((None, seq, gw), lambda b, g: (b, 0, g)),
                  pl.BlockSpec((None, seq, 2 * HEAD_DIM), lambda b, g: (b, 0, g)),
                  _resident(w1a.shape), _resident(w1b.shape), _resident(w2.shape), _resident(pos.shape),
                  _resident(gq.shape), _resident(gk.shape), _resident(ovt.shape),
                  pl.BlockSpec((None,) + bslc.shape[1:], lambda b, g: (g, 0, 0, 0)),
                  pl.BlockSpec((None,) + bwin.shape[1:], lambda b, g: (g, 0, 0))],
        out_specs=pl.BlockSpec((None, seq, _NQ), lambda b, g: (b, 0, g)),
        out_shape=jax.ShapeDtypeStruct((batch, seq, NSA_HEADS * HEAD_DIM), BF16),
        scratch_shapes=[pltpu.VMEM((seq, HEAD_DIM), BF16),
                        pltpu.VMEM((HEAD_DIM, seq), BF16),
                        pltpu.VMEM((NSA_WIN_PREV * QB + seq, HEAD_DIM), BF16),
                        pltpu.VMEM((NSA_WIN_PREV + n_chunks, HEAD_DIM, QB), BF16)],
        compiler_params=_cparams(2),
        name="nsa",
    )(z_nsa, zc, w1a, w1b, w2, pos, gq, gk, ovt, bslc, bwin)


def _t5_bucket(dist):
    n = jnp.maximum(dist, 0)
    max_exact = NUM_BUCKETS // 2
    nf = jnp.maximum(n, 1).astype(F32)
    large = max_exact + (jnp.log(nf / max_exact) / math.log(MAX_DISTANCE / max_exact)
                         * (NUM_BUCKETS - max_exact)).astype(jnp.int32)
    large = jnp.minimum(large, NUM_BUCKETS - 1)
    return jnp.where(n < max_exact, n, large)


def _toeplitz(u, rows, cols):
    lead = u.shape[:-1]
    lu = rows + cols - 1
    assert u.shape[-1] == lu
    padded = jnp.pad(u, [(0, 0)] * len(lead) + [(0, 1)])
    flat = jnp.broadcast_to(padded[..., None, :], lead + (rows, lu + 1)).reshape(lead + (rows * (lu + 1),))
    return flat[..., :rows * lu].reshape(lead + (rows, lu))[..., rows - 1:]


def _bias_by_distance(bias_cols, delta, valid, dist_scale=1):
    bucket = _t5_bucket(jnp.asarray(np.maximum(delta, 0) * dist_scale, dtype=jnp.int32))
    return jnp.where(jnp.asarray(valid)[None, :], bias_cols.astype(F32)[bucket].T * LOG2E, NEG)


def _band_bias(bias_cols, window, n_prev, dist_scale, n_kv, reps):
    kw = (n_prev + 1) * QB
    delta = n_prev * QB + QB - 1 - np.arange(kw + QB - 1)
    u = _bias_by_distance(bias_cols, delta, (delta >= 0) & (delta < window), dist_scale)
    return _toeplitz(u, QB, kw).reshape(n_kv, reps * QB, kw)


def _slc_bias_t(bias_cols, seq):
    n_chunks = seq // QB
    delta = seq - 1 - np.arange(seq + QB - 1)
    strip = _toeplitz(_bias_by_distance(bias_cols, delta, delta >= 0), QB, seq)
    tile = jnp.flip(strip.reshape(NSA_KV_HEADS, NSA_REPS, QB, n_chunks, QB), axis=3)
    tile = jnp.transpose(tile, (0, 3, 4, 1, 2)).reshape(NSA_KV_HEADS, n_chunks, QB, NSA_REPS * QB)
    return jnp.concatenate([jnp.full_like(tile[:, :1], NEG), tile], axis=1)


def _rope_tables(seq):
    half = MLA_ROPE // 2
    inv = jnp.power(ROPE_THETA, -jnp.arange(half, dtype=F32) / half)
    ang = jnp.arange(seq, dtype=F32)[:, None] * inv[None, :]
    cos, sin = jnp.cos(ang), jnp.sin(ang)
    zeros = lambda w: jnp.zeros((seq, w), F32)
    tail = MLA_PAD - MLA_QK_DIM
    cos_t = jnp.concatenate([jnp.ones((seq, MLA_NOPE), F32), cos, cos, zeros(tail)], axis=1)
    sin_t = jnp.concatenate([zeros(MLA_NOPE), -sin, sin, zeros(tail)], axis=1)
    return cos_t, sin_t


def _pad_cols(w, width):
    return jnp.pad(w, ((0, 0), (0, width - w.shape[1])))


def _with_swapped_rope(a):
    half = MLA_ROPE // 2
    return jnp.concatenate([a, a[..., MLA_NOPE + half:], a[..., MLA_NOPE:MLA_NOPE + half]], axis=-1)


def _nsa_column_order():
    g_cols = NSA_KV_HEADS * HEAD_DIM
    q_cols = NSA_HEADS * HEAD_DIM
    order = []
    for g in range(NSA_KV_HEADS):
        cols = list(range(g * _NQ, (g + 1) * _NQ))
        for part in range(6):
            start = q_cols + part * g_cols + g * HEAD_DIM
            cols += list(range(start, start + HEAD_DIM))
        gate0 = q_cols + 6 * g_cols + g * NSA_REPS * 3
        cols += list(range(gate0, gate0 + NSA_REPS * 3))
        order.append(cols)
    return order


def _nsa_cmp_columns():
    q_cols = NSA_HEADS * HEAD_DIM
    g_cols = NSA_KV_HEADS * HEAD_DIM
    cols = []
    for g in range(NSA_KV_HEADS):
        for part in range(2):
            start = q_cols + part * g_cols + g * HEAD_DIM
            cols += list(range(start, start + HEAD_DIM))
    return cols


def _nsa_mixer(z_nsa3, z_cmp, rel_bias, q_norm, k_norm, cmp_pos, cmp_w1, cmp_w2, batch, seq):
    w1 = cmp_w1.reshape(2, 2, NSA_CMP_STRIDE, HEAD_DIM, NSA_CMP_HIDDEN)
    zero1 = jnp.zeros_like(w1[0])
    w1_bd = jnp.concatenate([jnp.concatenate([w1[0], zero1], axis=-1),
                             jnp.concatenate([zero1, w1[1]], axis=-1)], axis=-2)
    zero2 = jnp.zeros_like(cmp_w2[0])
    w2_bd = jnp.concatenate([jnp.concatenate([cmp_w2[0], zero2], axis=-1),
                             jnp.concatenate([zero2, cmp_w2[1]], axis=-1)], axis=-2)
    pos_kv = jnp.concatenate([cmp_pos[0], cmp_pos[1]], axis=-1)
    n_cmp = (seq - NSA_CMP_LEN) // NSA_CMP_STRIDE + 1
    ci = np.arange(NSA_N_CMP_PAD)[:, None] * NSA_CMP_STRIDE
    sj = np.arange(NSA_N_SLC)[None, :] * NSA_SLC_BLOCK
    overlap = ((ci < sj + NSA_SLC_BLOCK) & (ci + NSA_CMP_LEN > sj) & (np.arange(NSA_N_CMP_PAD)[:, None] < n_cmp))
    ovt = jnp.asarray(overlap.T.astype(np.float32))
    nsa_cols = rel_bias[:, NSA_BIAS_COL0:NSA_BIAS_COL0 + NSA_HEADS]
    bslc = _slc_bias_t(nsa_cols, seq)
    bwin = jnp.transpose(_band_bias(nsa_cols, NSA_WINDOW, NSA_WIN_PREV, 1, NSA_KV_HEADS, NSA_REPS), (0, 2, 1))
    return _nsa(z_nsa3, z_cmp.reshape(batch, seq, -1), w1_bd[0].astype(BF16), w1_bd[1].astype(BF16),
                w2_bd.astype(BF16), pos_kv, q_norm.reshape(HEAD_DIM, 1), k_norm.reshape(3, 1, HEAD_DIM),
                ovt, bslc, bwin, batch, seq)


def kernel(x, rel_bias, ffn1_norm, ffn1_w_gate, ffn1_w_up, ffn1_w_down, mix_norm, ffn2_norm, ffn2_w_gate,
           ffn2_w_up, ffn2_w_down, ab_w_in, mla_q_a_norm, mla_w_q_b, mla_kv_a_norm, mla_w_kv_b, mla_q_norm,
           mla_k_norm, dil_q_norm, dil_k_norm, ab_w_out, cd_w_in, swa_q_norm, swa_k_norm, swa_sinks,
           nsa_q_norm, nsa_k_norm, nsa_cmp_pos, nsa_cmp_w1, nsa_cmp_w2, cd_w_out):
    batch, seq, _ = x.shape
    n = batch * seq
    assert seq % (16 * QB) == 0 and n % TM == 0 and seq % TM == 0
    bf = lambda a: a.astype(BF16)
    xf = x.reshape(n, D_MODEL)

    w_in = ab_w_in[0]
    mla_cols = MLA_Q_RANK + MLA_KV_RANK
    w_krope = _with_swapped_rope(jnp.pad(w_in[:, mla_cols:mla_cols + MLA_ROPE], ((0, 0), (MLA_NOPE, 0))))
    w_mla = jnp.concatenate([w_in[:, :mla_cols], w_krope], axis=1)
    xf, z_mla, z_dil = _ffn_proj(xf, ffn1_norm[0], bf(ffn1_w_gate[0]), bf(ffn1_w_up[0]), bf(ffn1_w_down[0]),
                                 mix_norm[0], [bf(w_mla), bf(w_in[:, mla_cols + MLA_ROPE:])])

    wq = _with_swapped_rope(mla_w_q_b[0].reshape(MLA_Q_RANK, MLA_HEADS, MLA_QK_DIM))
    wq = wq.reshape(MLA_Q_RANK, MLA_HEADS * MLA_PAD)
    wkv = mla_w_kv_b[0].reshape(MLA_KV_RANK, MLA_HEADS, MLA_NOPE + MLA_V)
    wk = _pad_cols(wkv[:, :, :MLA_NOPE].reshape(MLA_KV_RANK * MLA_HEADS, MLA_NOPE), MLA_PAD)
    wk = wk.reshape(MLA_KV_RANK, MLA_HEADS * MLA_PAD)
    wv = wkv[:, :, MLA_NOPE:].reshape(MLA_KV_RANK, MLA_HEADS * MLA_V)
    cos_t, sin_t = _rope_tables(seq)
    q_scale = MLA_QK_DIM ** -0.5 * LOG2E
    q_mla, k_mla, vt_mla = _mla_prep(
        z_mla, mla_q_a_norm[0].reshape(1, -1), mla_kv_a_norm[0].reshape(1, -1), bf(wq), bf(wk), bf(wv),
        _with_swapped_rope(mla_q_norm[0].reshape(1, -1)), _with_swapped_rope(mla_k_norm[0].reshape(1, -1)),
        cos_t * q_scale, sin_t * q_scale, cos_t, sin_t, seq)
    o_a = _mla_attn(q_mla.reshape(batch, seq, -1), k_mla.reshape(batch, seq, -1),
                    vt_mla.reshape(batch, seq // MLA_TQ, MLA_HEADS * MLA_V, MLA_TQ),
                    batch, seq).reshape(n, MLA_HEADS * MLA_V)

    z_dil3 = z_dil.reshape(batch, seq, -1)
    gq, gk = dil_q_norm[0], dil_k_norm[0]
    dil_outs, dil_lses = [], []
    for grp, (window, dilation) in enumerate(DIL_PATTERNS):
        bm = _band_bias(rel_bias[:, grp * DIL_HPG:(grp + 1) * DIL_HPG], window // dilation + 1, 1, dilation,
                        DIL_HPG, 1)
        o, lse = _dilated_group(z_dil3, batch, seq, grp, dilation, gq, gk, bm)
        dil_outs.append(o)
        dil_lses.append(lse)
    w_out = ab_w_out[0]
    xf = _out_ab_ffn(xf, o_a, dil_outs, dil_lses, bf(w_out[:MLA_HEADS * MLA_V]), bf(w_out[MLA_HEADS * MLA_V:]),
                     ffn2_norm[0], bf(ffn2_w_gate[0]), bf(ffn2_w_up[0]), bf(ffn2_w_down[0]))

    w_in = cd_w_in[0]
    swa_cols = (SWA_HEADS + 2 * SWA_KV_HEADS) * HEAD_DIM
    w_nsa_src = w_in[:, swa_cols:]
    w_nsa = jnp.concatenate([_pad_cols(w_nsa_src[:, np.asarray(cols)], NSA_GROUP_COLS)
                             for cols in _nsa_column_order()], axis=1)
    w_cmp = w_nsa_src[:, np.asarray(_nsa_cmp_columns())]
    xf, z_swa, z_nsa, z_cmp = _ffn_proj(xf, ffn1_norm[1], bf(ffn1_w_gate[1]), bf(ffn1_w_up[1]), bf(ffn1_w_down[1]),
                                        mix_norm[1], [bf(w_in[:, :swa_cols]), bf(w_nsa), bf(w_cmp)])

    swa_reps = SWA_HEADS // SWA_KV_HEADS
    swa_prev = -(-(SWA_WINDOW - 1) // QB)
    bm_swa = _band_bias(rel_bias[:, :SWA_HEADS], SWA_WINDOW, swa_prev, 1, SWA_KV_HEADS, swa_reps)
    sinks = jnp.broadcast_to((swa_sinks[0].astype(F32) * LOG2E).reshape(SWA_KV_HEADS, 1, swa_reps, 1),
                             (SWA_KV_HEADS, 1, swa_reps, QB)).reshape(SWA_KV_HEADS, 1, swa_reps * QB)
    o_c = _swa(z_swa.reshape(batch, seq, -1), batch, seq, swa_q_norm[0].reshape(HEAD_DIM, 1),
               swa_k_norm[0].reshape(1, HEAD_DIM), bm_swa, sinks)

    o_d = _nsa_mixer(z_nsa.reshape(batch, seq, NSA_KV_HEADS * NSA_GROUP_COLS), z_cmp, rel_bias, nsa_q_norm[0],
                     nsa_k_norm[0], nsa_cmp_pos[0], nsa_cmp_w1[0], nsa_cmp_w2[0], batch, seq)

    w_out = cd_w_out[0]
    xf = _out_cd_ffn(xf, o_c.reshape(n, -1), o_d.reshape(n, -1), bf(w_out[:SWA_HEADS * HEAD_DIM]),
                     bf(w_out[SWA_HEADS * HEAD_DIM:]), ffn2_norm[1], bf(ffn2_w_gate[1]), bf(ffn2_w_up[1]),
                     bf(ffn2_w_down[1]))
    return xf.reshape(batch, seq, D_MODEL)
```

```python
import functools
import math

import numpy as np
import jax
import jax.numpy as jnp
from jax import lax
from jax.experimental import pallas as pl
from jax.experimental.pallas import tpu as pltpu

F32 = jnp.float32
BF16 = jnp.bfloat16

EPS = 1e-6
NEG = -1e30
TINY = 1e-30
LOG2E = math.log2(math.e)
LN2 = math.log(2.0)
D_MODEL = 1024
D_FF = 2816
NUM_BUCKETS = 32
MAX_DISTANCE = 2048
HEAD_DIM = 64
QB = 128

MLA_HEADS = 8
MLA_Q_RANK = 256
MLA_KV_RANK = 128
MLA_NOPE = 64
MLA_ROPE = 32
MLA_V = 64
MLA_QK_DIM = MLA_NOPE + MLA_ROPE
MLA_PAD = 128
ROPE_THETA = 10000.0

DIL_PATTERNS = ((128, 1), (512, 4), (2048, 16))
DIL_HPG = 4
SWA_HEADS = 8
SWA_KV_HEADS = 2
SWA_WINDOW = 128
NSA_HEADS = 8
NSA_KV_HEADS = 2
NSA_CMP_LEN = 32
NSA_CMP_STRIDE = 16
NSA_CMP_HIDDEN = 128
NSA_SLC_BLOCK = 64
NSA_TOP_N = 16
NSA_WINDOW = 512
NSA_FORCE = 1e6
NSA_BIAS_COL0 = 8
NSA_GROUP_COLS = 640

VMEM_LIMIT = 56 * 1024 * 1024
TM = 512

_NT = (((1,), (1,)), ((), ()))
_HI = lax.Precision.HIGHEST


def _cparams(n_axes):
    return pltpu.CompilerParams(dimension_semantics=("arbitrary",) * n_axes,
                                vmem_limit_bytes=VMEM_LIMIT)


def _resident(shape):
    nd = len(shape)
    return pl.BlockSpec(shape, lambda *_: (0,) * nd, pipeline_mode=pl.Buffered(1))


def _layer_block(w, layer):
    nd = w.ndim - 1
    return pl.BlockSpec((None,) + w.shape[1:], lambda *_: (layer,) + (0,) * nd, pipeline_mode=pl.Buffered(1))


def _rms(x):
    return x * lax.rsqrt(jnp.mean(x * x, axis=-1, keepdims=True) + EPS)


def _dot(a, b):
    return jnp.dot(a, b, preferred_element_type=F32)


def _dot_nt(a, b, precision=None):
    return lax.dot_general(a, b, _NT, preferred_element_type=F32, precision=precision)


FF_CHUNK = 256


N_CHUNK = 512


def _swiglu_half_step(x, g_ref, wg_ref, wu_ref, wd_ref):
    hb = (_rms(x) * g_ref[...]).astype(BF16)
    acc = jnp.zeros(x.shape, F32)
    for c in range(D_FF // FF_CHUNK):
        sl = slice(c * FF_CHUNK, (c + 1) * FF_CHUNK)
        gate = _dot(hb, wg_ref[:, sl])
        up = _dot(hb, wu_ref[:, sl])
        act = (gate * jax.nn.sigmoid(gate) * up).astype(BF16)
        acc = acc + _dot(act, wd_ref[sl, :])
    return x + 0.5 * acc


def _ffn_proj_kernel(n_out, x_ref, g_ref, wg_ref, wu_ref, wd_ref, gm_ref, *refs):
    w_refs, x_out_ref, z_refs = refs[:n_out], refs[n_out], refs[n_out + 1:]
    x = _swiglu_half_step(x_ref[...], g_ref, wg_ref, wu_ref, wd_ref)
    x_out_ref[...] = x
    hb = (_rms(x) * gm_ref[...]).astype(BF16)
    for w_ref, z_ref in zip(w_refs, z_refs):
        width = w_ref.shape[1]
        for c0 in range(0, width, N_CHUNK):
            sl = slice(c0, min(c0 + N_CHUNK, width))
            z_ref[:, sl] = _dot(hb, w_ref[:, sl]).astype(z_ref.dtype)


def _tile(width):
    return pl.BlockSpec((TM, width), lambda i: (i, 0))


def _ffn_proj(x, g, ffn, layer, g_mix, ws, z_dtypes=None):
    n = x.shape[0]
    z_dtypes = z_dtypes or [F32] * len(ws)
    ffn_w = [_layer_block(w, layer) for w in ffn]
    return pl.pallas_call(
        functools.partial(_ffn_proj_kernel, len(ws)),
        grid=(n // TM,),
        in_specs=[_tile(D_MODEL), _resident((1, D_MODEL))] + ffn_w + [_resident((1, D_MODEL))]
        + [_resident(w.shape) for w in ws],
        out_specs=[_tile(D_MODEL)] + [_tile(w.shape[1]) for w in ws],
        out_shape=[jax.ShapeDtypeStruct((n, D_MODEL), F32)]
        + [jax.ShapeDtypeStruct((n, w.shape[1]), dt) for w, dt in zip(ws, z_dtypes)],
        compiler_params=_cparams(1),
        name="ffn_proj",
    )(x, g.reshape(1, D_MODEL), *ffn, g_mix.reshape(1, D_MODEL), *ws)


def _out_ab_ffn_kernel(x_ref, oa_ref, o0_ref, o1_ref, o2_ref, l0_ref, l1_ref, l2_ref, wa_ref, wb_ref,
                       g_ref, wg_ref, wu_ref, wd_ref, out_ref):
    l0, l1, l2 = l0_ref[...], l1_ref[...], l2_ref[...]
    m = jnp.maximum(jnp.maximum(l0, l1), l2)
    e0, e1, e2 = jnp.exp(l0 - m), jnp.exp(l1 - m), jnp.exp(l2 - m)
    ob = (e0 * o0_ref[...] + e1 * o1_ref[...] + e2 * o2_ref[...]) / (e0 + e1 + e2)
    x = x_ref[...] + _dot(oa_ref[...], wa_ref[...]) + _dot(ob.astype(BF16), wb_ref[...])
    out_ref[...] = _swiglu_half_step(x, g_ref, wg_ref, wu_ref, wd_ref)


def _out_ab_ffn(x, oa, dil_outs, dil_lses, wa, wb, g, ffn, layer):
    n = x.shape[0]
    dw = DIL_HPG * HEAD_DIM
    ffn_w = [_layer_block(w, layer) for w in ffn]
    return pl.pallas_call(
        _out_ab_ffn_kernel,
        grid=(n // TM,),
        in_specs=[_tile(D_MODEL), _tile(oa.shape[1])] + [_tile(dw)] * 6 + [_resident(wa.shape), _resident(wb.shape),
                                                                        _resident((1, D_MODEL))] + ffn_w,
        out_specs=_tile(D_MODEL),
        out_shape=jax.ShapeDtypeStruct((n, D_MODEL), F32),
        compiler_params=_cparams(1),
        name="out_ab_ffn",
    )(x, oa, *dil_outs, *dil_lses, wa, wb, g.reshape(1, D_MODEL), *ffn)


def _out_cd_ffn_kernel(x_ref, oc_ref, od_ref, wc_ref, wdd_ref, g_ref, wg_ref, wu_ref, wd_ref, out_ref):
    x = x_ref[...] + _dot(oc_ref[...], wc_ref[...]) + _dot(od_ref[...], wdd_ref[...])
    out_ref[...] = _swiglu_half_step(x, g_ref, wg_ref, wu_ref, wd_ref)


def _out_cd_ffn(x, oc, od, wc, wdd, g, ffn, layer):
    n = x.shape[0]
    ffn_w = [_layer_block(w, layer) for w in ffn]
    return pl.pallas_call(
        _out_cd_ffn_kernel,
        grid=(n // TM,),
        in_specs=[_tile(D_MODEL), _tile(oc.shape[1]), _tile(od.shape[1]), _resident(wc.shape), _resident(wdd.shape),
                  _resident((1, D_MODEL))] + ffn_w,
        out_specs=_tile(D_MODEL),
        out_shape=jax.ShapeDtypeStruct((n, D_MODEL), F32),
        compiler_params=_cparams(1),
        name="out_cd_ffn",
    )(x, oc, od, wc, wdd, g.reshape(1, D_MODEL), *ffn)


def _mla_prep_kernel(z_ref, qan_ref, kvan_ref, wq_ref, wk_ref, wv_ref, gq_ref, gk_ref,
                     qc_ref, qs_ref, kc_ref, ks_ref, q_ref, k_ref, vt_ref):
    cq = (_rms(z_ref[:, :MLA_Q_RANK]) * qan_ref[...]).astype(BF16)
    ckv = (_rms(z_ref[:, MLA_Q_RANK:MLA_Q_RANK + MLA_KV_RANK]) * kvan_ref[...]).astype(BF16)
    kr = z_ref[:, MLA_Q_RANK + MLA_KV_RANK:]
    real = jnp.where(lax.broadcasted_iota(jnp.int32, (1, MLA_PAD), 1) < MLA_QK_DIM, 1.0, 0.0)

    def head_norm_rope(x, gain, cos, sin):
        ssq = jnp.sum(x * x * real, axis=-1, keepdims=True)
        x = x * lax.rsqrt(ssq * (1.0 / MLA_QK_DIM) + EPS) * gain
        return x * cos + pltpu.roll(x, MLA_PAD - MLA_ROPE, 1) * sin

    for h in range(MLA_HEADS):
        sl = slice(h * MLA_PAD, (h + 1) * MLA_PAD)
        qh = head_norm_rope(_dot(cq, wq_ref[:, sl]), gq_ref[...], qc_ref[...], qs_ref[...])
        q_ref[:, sl] = qh.astype(BF16)
        kh = head_norm_rope(_dot(ckv, wk_ref[:, sl]) + kr, gk_ref[...], kc_ref[...], ks_ref[...])
        k_ref[:, sl] = kh.astype(BF16)
    v = _dot(ckv, wv_ref[...])
    for c in range(TM // MLA_TQ):
        vt_ref[c] = v[c * MLA_TQ:(c + 1) * MLA_TQ].T.astype(BF16)


MLA_TQ = 256
MLA_STEP = 2


def _mla_prep(z, qan, kvan, wq, wk, wv, gq, gk, q_cos, q_sin, k_cos, k_sin, seq):
    n = z.shape[0]
    per_seq = seq // TM
    tile = lambda w: pl.BlockSpec((TM, w), lambda i: (i, 0))
    rope_spec = pl.BlockSpec((TM, MLA_PAD), lambda i: (i % per_seq, 0))
    hw = MLA_HEADS * MLA_PAD
    vw = MLA_HEADS * MLA_V
    return pl.pallas_call(
        _mla_prep_kernel,
        grid=(n // TM,),
        in_specs=[tile(z.shape[1]), _resident(qan.shape), _resident(kvan.shape), _resident(wq.shape),
                  _resident(wk.shape), _resident(wv.shape), _resident(gq.shape), _resident(gk.shape),
                  rope_spec, rope_spec, rope_spec, rope_spec],
        out_specs=[tile(hw), tile(hw), pl.BlockSpec((TM // MLA_TQ, vw, MLA_TQ), lambda i: (i, 0, 0))],
        out_shape=[jax.ShapeDtypeStruct((n, hw), BF16), jax.ShapeDtypeStruct((n, hw), BF16),
                   jax.ShapeDtypeStruct((n // MLA_TQ, vw, MLA_TQ), BF16)],
        compiler_params=_cparams(1),
        name="mla_prep",
    )(z, qan, kvan, wq, wk, wv, gq, gk, q_cos, q_sin, k_cos, k_sin)


def _flash_step_t(s, m, l, acc, pv, vt):
    m_new = jnp.maximum(m, jnp.max(s, axis=0, keepdims=True))
    p = jnp.exp2(s - m_new)
    alpha = jnp.exp2(m - m_new)
    l = alpha * l + jnp.sum(p, axis=0, keepdims=True)
    acc = alpha * (acc + pv)
    return m_new, l, acc, _dot(vt, p.astype(BF16))


def _flash_init(dv, n_queries):
    return (jnp.full((1, n_queries), NEG, F32), jnp.zeros((1, n_queries), F32), jnp.zeros((dv, n_queries), F32),
            jnp.zeros((dv, n_queries), F32))


def _flash_finish(m, l, acc, pv):
    return (acc + pv) * (1.0 / l)


def _mla_attn_kernel(seq, q_ref, k_ref, vt_ref, o_ref):
    t = MLA_TQ
    key = lax.broadcasted_iota(jnp.int32, (t, t), 0)
    qry = lax.broadcasted_iota(jnp.int32, (t, t), 1)
    diag_ok = key <= qry
    heads = [(slice(hh * MLA_PAD, (hh + 1) * MLA_PAD), slice(hh * MLA_V, (hh + 1) * MLA_V)) for hh in range(2)]

    init = _flash_init(MLA_V, t)
    for i in range(seq // t):
        rows = slice(i * t, (i + 1) * t)
        qs = [q_ref[rows, ql] for ql, _ in heads]

        steps = [(j, min(j + MLA_STEP, i + 1)) for j in range(0, i + 1, MLA_STEP)]

        def scores(h, step):
            lo, hi = steps[step]
            s = _dot_nt(k_ref[lo * t:hi * t, heads[h][0]], qs[h])
            if hi == i + 1:
                last = jnp.where(diag_ok, s[(i - lo) * t:], NEG)
                s = last if i == lo else jnp.concatenate([s[:(i - lo) * t], last], axis=0)
            return s

        def values(h, step):
            lo, hi = steps[step]
            return jnp.concatenate([vt_ref[j, heads[h][1], :] for j in range(lo, hi)], axis=1)

        cur = [scores(h, 0) for h in range(2)]
        states = [init, init]
        for step in range(len(steps)):
            for h in range(2):
                nxt = scores(h, step + 1) if step + 1 < len(steps) else None
                states[h] = _flash_step_t(cur[h], *states[h], values(h, step))
                cur[h] = nxt
        outs = [_flash_finish(*state) for state in states]
        o_ref[rows, :] = jnp.concatenate(outs, axis=0).T.astype(o_ref.dtype)


def _mla_attn(q, k, vt, batch, seq):
    return pl.pallas_call(
        functools.partial(_mla_attn_kernel, seq),
        grid=(batch, MLA_HEADS // 2),
        in_specs=[pl.BlockSpec((None, seq, 2 * MLA_PAD), lambda b, h: (b, 0, h)),
                  pl.BlockSpec((None, seq, 2 * MLA_PAD), lambda b, h: (b, 0, h)),
                  pl.BlockSpec((None, seq // MLA_TQ, 2 * MLA_V, MLA_TQ), lambda b, h: (b, 0, h, 0))],
        out_specs=pl.BlockSpec((None, seq, 2 * MLA_V), lambda b, h: (b, 0, h)),
        out_shape=jax.ShapeDtypeStruct((batch, seq, MLA_HEADS * MLA_V), BF16),
        compiler_params=_cparams(2),
        name="mla_attn",
    )(q, k, vt)


def _head_norm(x, gain):
    return _rms(x) * gain


def _heads_t(qt, n_heads, gain):
    tiles = []
    for c0 in range(0, n_heads * HEAD_DIM, 2 * HEAD_DIM):
        pair = qt[:, c0:c0 + 2 * HEAD_DIM].T
        for h in range(2):
            x = pair[h * HEAD_DIM:(h + 1) * HEAD_DIM]
            tiles.append(x * lax.rsqrt(jnp.mean(x * x, axis=0, keepdims=True) + EPS) * gain)
    return tiles


def _q_gain(gq_ref):
    return jnp.broadcast_to(gq_ref[...] * (HEAD_DIM ** -0.5 * LOG2E), (HEAD_DIM, QB))


def _pair_ones():
    shape = (2 * HEAD_DIM, 2 * HEAD_DIM)
    same = lax.broadcasted_iota(jnp.int32, shape, 0) // HEAD_DIM == lax.broadcasted_iota(jnp.int32, shape, 1) // HEAD_DIM
    return jnp.where(same, 1.0, 0.0).astype(BF16)


def _pair_norm(x, seg, gain):
    sq = x * x
    hi = sq.astype(BF16)
    lo = (sq - hi.astype(F32)).astype(BF16)
    ssq = _dot(hi, seg) + _dot(lo, seg)
    return x * lax.rsqrt(ssq * (1.0 / HEAD_DIM) + EPS) * gain


BAND_KV = 2


def _banded_kernel(seq, dilation, reps, n_prev, with_sink, with_lse, group_size, paired, *refs):
    q_ref, k_ref, v_ref, gq_ref, gk_ref, bm_ref = refs[:6]
    refs = refs[6:]
    sink_ref = None
    if with_sink:
        sink_ref, refs = refs[0], refs[1:]
    o_ref, refs = refs[0], refs[1:]
    lse_ref = None
    if with_lse:
        lse_ref, refs = refs[0], refs[1:]
    kn_ref, vt_ref = refs
    length = seq // dilation
    n_blocks = length // QB
    pad = n_prev * QB
    kw = pad + QB
    win_row = lax.broadcasted_iota(jnp.int32, (kw, 1), 0)
    assert not paired or reps == 1
    if paired:
        gq, gk = gq_ref[...] * (HEAD_DIM ** -0.5 * LOG2E), gk_ref[...]
        seg = _pair_ones()
        lane = lax.broadcasted_iota(jnp.int32, (1, BAND_KV * HEAD_DIM), 1)
        head_lanes = [jnp.where(lane // HEAD_DIM == g, 1.0, 0.0) for g in range(BAND_KV)]
    else:
        gq, gk = _q_gain(gq_ref), gk_ref[...]

    def rows(start, size):
        return pl.ds(start, size) if dilation == 1 else pl.ds(start, size, stride=dilation)

    for res in range(dilation):
        kall, vall = k_ref[rows(res, length), :], v_ref[rows(res, length), :]
        if paired:
            kn_ref[res, :pad, :] = jnp.zeros((pad, BAND_KV * HEAD_DIM), BF16)
            kn_ref[res, pad:, :] = _pair_norm(kall, seg, gk).astype(BF16)
        else:
            for g in range(BAND_KV):
                kn_ref[res, g, :pad, :] = jnp.zeros((pad, HEAD_DIM), BF16)
                kn_ref[res, g, pad:, :] = _head_norm(kall[:, g * HEAD_DIM:(g + 1) * HEAD_DIM], gk).astype(BF16)
        for j in range(n_prev):
            vt_ref[res, j] = jnp.zeros((BAND_KV * HEAD_DIM, QB), BF16)
        for j in range(n_blocks):
            vt_ref[res, n_prev + j] = vall[j * QB:(j + 1) * QB].T.astype(BF16)

    def attend(res, i, g, s):
        bias = bm_ref[g]
        if i < n_prev:
            bias = jnp.where(win_row >= (n_prev - i) * QB, bias, NEG)
        s = s + bias
        m = jnp.max(s, axis=0, keepdims=True)
        if with_sink:
            m = jnp.maximum(m, sink_ref[g])
        p = jnp.exp2(s - m)
        l = jnp.sum(p, axis=0, keepdims=True)
        if with_sink:
            l = l + jnp.exp2(sink_ref[g] - m)
        p = p.astype(BF16)
        vl = slice(g * HEAD_DIM, (g + 1) * HEAD_DIM)
        o = _dot(vt_ref[res, i, vl, :], p[0:QB])
        for w in range(1, n_prev + 1):
            o = o + _dot(vt_ref[res, i + w, vl, :], p[w * QB:(w + 1) * QB])
        return o * (1.0 / l), m * LN2 + jnp.log(l)

    items = [(res, i) for res in range(dilation) for i in range(n_blocks)]
    for at in range(0, len(items), group_size):
        group = items[at:at + group_size]
        toks = [rows(res + i * QB * dilation, QB) for res, i in group]
        if paired:
            qns = [_pair_norm(q_ref[tok, :], seg, gq) for tok in toks]
            qss = [[(qn * head_lanes[g]).astype(BF16) for g in range(BAND_KV)] for qn in qns]
            scores = [[_dot_nt(kn_ref[res, i * QB:i * QB + kw, :], qs[g]) for g in range(BAND_KV)]
                      for (res, i), qs in zip(group, qss)]
        else:
            heads = [_heads_t(q_ref[tok, :], BAND_KV * reps, gq) for tok in toks]
            qss = [[jnp.concatenate(h[g * reps:(g + 1) * reps], axis=1).astype(BF16) for g in range(BAND_KV)]
                   for h in heads]
            scores = [[_dot(kn_ref[res, g, i * QB:i * QB + kw, :], qs[g]) for g in range(BAND_KV)]
                      for (res, i), qs in zip(group, qss)]
        for (res, i), tok, sc in zip(group, toks, scores):
            outs, lses = [], []
            for g in range(BAND_KV):
                o, lse = attend(res, i, g, sc[g])
                for r in range(reps):
                    outs.append(o[:, r * QB:(r + 1) * QB])
                    if with_lse:
                        lses.append(jnp.broadcast_to(lse[:, r * QB:(r + 1) * QB], (HEAD_DIM, QB)))
            o_ref[tok, :] = jnp.concatenate(outs, axis=0).T.astype(o_ref.dtype)
            if with_lse:
                lse_ref[tok, :] = jnp.concatenate(lses, axis=0).T


def _banded_scratch(seq, dilation, n_prev, paired):
    length = seq // dilation
    rows = n_prev * QB + length
    keys = (dilation, rows, BAND_KV * HEAD_DIM) if paired else (dilation, BAND_KV, rows, HEAD_DIM)
    return [pltpu.VMEM(keys, BF16),
            pltpu.VMEM((dilation, n_prev + length // QB, BAND_KV * HEAD_DIM, QB), BF16)]


PAIRED_MIN_DILATION = 16


def _dilated_group(z_dil, batch, seq, group, dilation, q_norm, k_norm, bm):
    gw = DIL_HPG * HEAD_DIM
    n_groups = len(DIL_PATTERNS)
    hps = BAND_KV
    halves = DIL_HPG // hps
    bm_t = jnp.transpose(bm, (0, 2, 1))
    paired = dilation >= PAIRED_MIN_DILATION
    if paired:
        gq = jnp.tile(q_norm.reshape(1, HEAD_DIM), (1, BAND_KV))
        gk = jnp.tile(k_norm.reshape(1, HEAD_DIM), (1, BAND_KV))
    else:
        gq, gk = q_norm.reshape(HEAD_DIM, 1), k_norm.reshape(1, HEAD_DIM)

    def spec(part):
        return pl.BlockSpec((None, seq, hps * HEAD_DIM), lambda b, h: (b, 0, (part * n_groups + group) * halves + h))

    out_spec = pl.BlockSpec((None, seq, hps * HEAD_DIM), lambda b, h: (b, 0, h))
    shape = jax.ShapeDtypeStruct((batch, seq, gw), F32)
    out, lse = pl.pallas_call(
        functools.partial(_banded_kernel, seq, dilation, 1, 1, False, True, 8, paired),
        grid=(batch, halves),
        in_specs=[spec(0), spec(1), spec(2), _resident(gq.shape), _resident(gk.shape),
                  pl.BlockSpec((hps,) + bm_t.shape[1:], lambda b, h: (h, 0, 0))],
        out_specs=[out_spec, out_spec],
        out_shape=[shape, shape],
        scratch_shapes=_banded_scratch(seq, dilation, 1, paired),
        compiler_params=_cparams(2),
        name="dilated_g%d" % group,
    )(z_dil, z_dil, z_dil, gq, gk, bm_t)
    return out.reshape(batch * seq, gw), lse.reshape(batch * seq, gw)


def _swa(z_swa, batch, seq, gq, gk, bm, sinks):
    qw = SWA_HEADS * HEAD_DIM
    kvw = SWA_KV_HEADS * HEAD_DIM
    reps = SWA_HEADS // SWA_KV_HEADS
    n_prev = -(-(SWA_WINDOW - 1) // QB)
    assert SWA_KV_HEADS == BAND_KV
    bm_t = jnp.transpose(bm, (0, 2, 1))
    return pl.pallas_call(
        functools.partial(_banded_kernel, seq, 1, reps, n_prev, True, False, 4, False),
        grid=(batch,),
        in_specs=[pl.BlockSpec((None, seq, qw), lambda b: (b, 0, 0)),
                  pl.BlockSpec((None, seq, kvw), lambda b: (b, 0, qw // kvw)),
                  pl.BlockSpec((None, seq, kvw), lambda b: (b, 0, qw // kvw + 1)),
                  _resident(gq.shape), _resident(gk.shape), _resident(bm_t.shape), _resident(sinks.shape)],
        out_specs=pl.BlockSpec((None, seq, qw), lambda b: (b, 0, 0)),
        out_shape=jax.ShapeDtypeStruct((batch, seq, qw), BF16),
        scratch_shapes=_banded_scratch(seq, 1, n_prev, False),
        compiler_params=_cparams(1),
        name="swa",
    )(z_swa, z_swa, z_swa, gq, gk, bm_t, sinks)


NSA_REPS = NSA_HEADS // NSA_KV_HEADS
NSA_N_SLC = 32
NSA_N_CMP_PAD = 128
NSA_WIN_PREV = -(-(NSA_WINDOW - 1) // QB)
_NQ = NSA_REPS * HEAD_DIM
_OFF_KS, _OFF_VS, _OFF_KW, _OFF_VW, _OFF_GATE = (_NQ + i * HEAD_DIM for i in range(5))


NSA_TIERS = 4
NSA_STEP = 8
NSA_QPAIR = 2


def _nsa_kernel(seq, z_ref, zc_ref, w1a_ref, w1b_ref, w2_ref, pos_ref, gq_ref, gk_ref, ovt_ref,
                bslc_ref, bwin_ref, o_ref, ks_ref, vst_ref, kw_ref, vwt_ref):
    n_chunks = seq // QB
    lanes = NSA_REPS * QB
    gq = _q_gain(gq_ref)
    win_pad = NSA_WIN_PREV * QB
    win_kw = win_pad + QB

    first = jnp.zeros((NSA_N_CMP_PAD, 2 * NSA_CMP_HIDDEN), F32)
    second = jnp.zeros((NSA_N_CMP_PAD, 2 * NSA_CMP_HIDDEN), F32)
    for l in range(NSA_CMP_STRIDE):
        tok = zc_ref[pl.ds(l, NSA_N_CMP_PAD, stride=NSA_CMP_STRIDE), :]
        first = first + _dot((tok + pos_ref[l:l + 1, :]).astype(BF16), w1a_ref[l])
        second = second + _dot((tok + pos_ref[NSA_CMP_STRIDE + l:NSA_CMP_STRIDE + l + 1, :]).astype(BF16),
                               w1b_ref[l])
    hidden = jax.nn.gelu(first + pltpu.roll(second, NSA_N_CMP_PAD - 1, 0)).astype(BF16)
    cmp_kv = _dot(hidden, w2_ref[...])
    kc = _head_norm(cmp_kv[:, :HEAD_DIM], gk_ref[0]).astype(BF16)
    vct = cmp_kv.T[HEAD_DIM:].astype(BF16)

    ks_ref[...] = _head_norm(z_ref[:, _OFF_KS:_OFF_KS + HEAD_DIM], gk_ref[1]).astype(BF16)
    kw_ref[:win_pad, :] = jnp.zeros((win_pad, HEAD_DIM), BF16)
    kw_ref[win_pad:, :] = _head_norm(z_ref[:, _OFF_KW:_OFF_KW + HEAD_DIM], gk_ref[2]).astype(BF16)
    for j in range(NSA_WIN_PREV):
        vwt_ref[j] = jnp.zeros((HEAD_DIM, QB), BF16)
    for j in range(n_chunks):
        rows = slice(j * QB, (j + 1) * QB)
        vst_ref[:, rows] = z_ref[rows, _OFF_KS:_OFF_KS + 2 * HEAD_DIM].T[HEAD_DIM:].astype(BF16)
        vwt_ref[NSA_WIN_PREV + j] = z_ref[rows, _OFF_KW:_OFF_KW + 2 * HEAD_DIM].T[HEAD_DIM:].astype(BF16)

    cmp_id = lax.broadcasted_iota(jnp.int32, (NSA_N_CMP_PAD, lanes), 0)
    cmp_end = cmp_id * NSA_CMP_STRIDE + (NSA_CMP_LEN - 1)
    cmp_real = cmp_id < NSA_N_CMP_PAD - 1
    q_in_blk = lax.broadcasted_iota(jnp.int32, (NSA_N_CMP_PAD, lanes), 1) & (QB - 1)
    blk_id = lax.broadcasted_iota(jnp.int32, (NSA_N_SLC, QB), 0)
    q_lane = lax.broadcasted_iota(jnp.int32, (NSA_N_SLC, QB), 1)
    win_row = lax.broadcasted_iota(jnp.int32, (win_kw, 1), 0)
    init = _flash_init(HEAD_DIM, lanes)
    both = range(NSA_QPAIR)

    def q_blocks(n_keys, mask_pad, pair, carry):
        step = [max(d for d in range(1, NSA_STEP + 1) if n % d == 0) for n in n_keys]
        step_keys = [s * QB for s in step]
        blocks_per_step = [s // NSA_SLC_BLOCK for s in step_keys]
        blk = [pair, n_chunks - 1 - pair]
        r0 = [pl.multiple_of(i * QB, QB) for i in blk]
        qs = [jnp.concatenate(_heads_t(z_ref[pl.ds(r, QB), :_NQ], NSA_REPS, gq), axis=1).astype(BF16)
              for r in r0]

        sc = [_dot(kc, q) for q in qs]
        s_win = [_dot(kw_ref[pl.ds(r, win_kw), :], q) for r, q in zip(r0, qs)]

        def slc_scores(b, j):
            return _dot(ks_ref[j * step_keys[b]:(j + 1) * step_keys[b], :], qs[b])

        cur = [slc_scores(b, 0) for b in both]

        o_cmp, imp = [], []
        for b in both:
            ok = (cmp_end <= q_in_blk + blk[b] * QB) & cmp_real
            s = jnp.where(ok, sc[b], NEG)
            e = jnp.where(ok, jnp.exp2(s - jnp.max(s, axis=0, keepdims=True)), 0.0)
            p = e * (1.0 / jnp.maximum(jnp.sum(e, axis=0, keepdims=True), TINY))
            o_cmp.append(_dot(vct, p.astype(BF16)))
            p_sum = p[:, 0:QB] + p[:, QB:2 * QB] + p[:, 2 * QB:3 * QB] + p[:, 3 * QB:4 * QB]
            imp.append(jnp.dot(ovt_ref[...], p_sum, precision=_HI, preferred_element_type=F32))

        o_win = []
        for b in both:
            bias = bwin_ref[...]
            if mask_pad[b]:
                bias = jnp.where(win_row >= (NSA_WIN_PREV - blk[b]) * QB, bias, NEG)
            s = s_win[b] + bias
            pw = jnp.exp2(s - jnp.max(s, axis=0, keepdims=True))
            lw = jnp.sum(pw, axis=0, keepdims=True)
            pw = pw.astype(BF16)
            o = _dot(vwt_ref[blk[b]], pw[0:QB])
            for c in range(1, NSA_WIN_PREV + 1):
                o = o + _dot(vwt_ref[blk[b] + c], pw[c * QB:(c + 1) * QB])
            o_win.append(o * (1.0 / lw))

        drop = []
        for b in both:
            tb = lax.shift_right_logical(q_lane + blk[b] * QB, 6)
            forced = (blk_id == 0) | (blk_id == tb) | (blk_id == tb - 1)
            score = jnp.where(blk_id <= tb, imp[b] + jnp.where(forced, NSA_FORCE, 0.0), -NSA_FORCE)
            rank = jnp.zeros((NSA_N_SLC, QB), F32)
            for other in range(NSA_N_SLC):
                s_o = score[other:other + 1, :]
                beats = (s_o > score) | ((s_o == score) & (blk_id > other))
                rank = rank + jnp.where(beats, 1.0, 0.0)
            drop.append(jnp.where(rank < NSA_TOP_N, 0.0, NEG))

        states = [init for _ in both]
        n_steps = [n // s for n, s in zip(n_keys, step)]
        for j in range(max(n_steps)):
            for b in both:
                if j >= n_steps[b]:
                    continue
                nxt = slc_scores(b, j + 1) if j + 1 < n_steps[b] else None
                bias = jnp.concatenate(
                    [bslc_ref[jnp.maximum(blk[b] - (j * step[b] + c) + 1, 0)] for c in range(step[b])], axis=0)
                km = jnp.concatenate([jnp.broadcast_to(drop[b][k:k + 1], (NSA_SLC_BLOCK, QB))
                                      for k in range(j * blocks_per_step[b], (j + 1) * blocks_per_step[b])], axis=0)
                s = cur[b] + bias + jnp.concatenate([km] * NSA_REPS, axis=1)
                states[b] = _flash_step_t(s, *states[b], vst_ref[:, j * step_keys[b]:(j + 1) * step_keys[b]])
                cur[b] = nxt

        for b in both:
            o_slc = _flash_finish(*states[b])
            gate = jax.nn.sigmoid(z_ref[pl.ds(r0[b], QB), _OFF_GATE:_OFF_GATE + QB]).T
            outs = []
            for r in range(NSA_REPS):
                qsl = slice(r * QB, (r + 1) * QB)
                outs.append(gate[3 * r:3 * r + 1] * o_cmp[b][:, qsl] + gate[3 * r + 1:3 * r + 2] * o_slc[:, qsl]
                            + gate[3 * r + 2:3 * r + 3] * o_win[b][:, qsl])
            o_ref[pl.ds(r0[b], QB), :] = jnp.concatenate(outs, axis=0).T.astype(o_ref.dtype)
        return carry

    pairs_per_tier = n_chunks // 2 // NSA_TIERS
    for tier in range(NSA_TIERS):
        lo, hi = tier * pairs_per_tier, (tier + 1) * pairs_per_tier
        lax.fori_loop(lo, hi, functools.partial(q_blocks, [hi, n_chunks - lo], [lo < NSA_WIN_PREV, False]), 0)


def _nsa(z_nsa, zc, w1a, w1b, w2, pos, gq, gk, ovt, bslc, bwin, batch, seq):
    gw = NSA_GROUP_COLS
    n_chunks = seq // QB
    return pl.pallas_call(
        functools.partial(_nsa_kernel, seq),
        grid=(batch, NSA_KV_HEADS),
        in_specs=[pl.BlockSpec((None, seq, gw), lambda b, g: (b, 0, g)),
                  pl.BlockSpec((None, seq, 2 * HEAD_DIM), lambda b, g: (b, 0, g)),
                  _resident(w1a.shape), _resident(w1b.shape), _resident(w2.shape), _resident(pos.shape),
                  _resident(gq.shape), _resident(gk.shape), _resident(ovt.shape),
                  pl.BlockSpec((None,) + bslc.shape[1:], lambda b, g: (g, 0, 0, 0)),
                  pl.BlockSpec((None,) + bwin.shape[1:], lambda b, g: (g, 0, 0))],
        out_specs=pl.BlockSpec((None, seq, _NQ), lambda b, g: (b, 0, g)),
        out_shape=jax.ShapeDtypeStruct((batch, seq, NSA_HEADS * HEAD_DIM), BF16),
        scratch_shapes=[pltpu.VMEM((seq, HEAD_DIM), BF16),
                        pltpu.VMEM((HEAD_DIM, seq), BF16),
                        pltpu.VMEM((NSA_WIN_PREV * QB + seq, HEAD_DIM), BF16),
                        pltpu.VMEM((NSA_WIN_PREV + n_chunks, HEAD_DIM, QB), BF16)],
        compiler_params=_cparams(2),
        name="nsa",
    )(z_nsa, zc, w1a, w1b, w2, pos, gq, gk, ovt, bslc, bwin)


def _t5_bucket(dist):
    n = jnp.maximum(dist, 0)
    max_exact = NUM_BUCKETS // 2
    nf = jnp.maximum(n, 1).astype(F32)
    large = max_exact + (jnp.log(nf / max_exact) / math.log(MAX_DISTANCE / max_exact)
                         * (NUM_BUCKETS - max_exact)).astype(jnp.int32)
    large = jnp.minimum(large, NUM_BUCKETS - 1)
    return jnp.where(n < max_exact, n, large)


def _toeplitz(u, rows, cols):
    lead = u.shape[:-1]
    lu = rows + cols - 1
    assert u.shape[-1] == lu
    padded = jnp.pad(u, [(0, 0)] * len(lead) + [(0, 1)])
    flat = jnp.broadcast_to(padded[..., None, :], lead + (rows, lu + 1)).reshape(lead + (rows * (lu + 1),))
    return flat[..., :rows * lu].reshape(lead + (rows, lu))[..., rows - 1:]


def _bias_by_distance(bias_cols, delta, valid, dist_scale=1):
    bucket = _t5_bucket(jnp.asarray(np.maximum(delta, 0) * dist_scale, dtype=jnp.int32))
    return jnp.where(jnp.asarray(valid)[None, :], bias_cols.astype(F32)[bucket].T * LOG2E, NEG)


def _band_bias(bias_cols, window, n_prev, dist_scale, n_kv, reps):
    kw = (n_prev + 1) * QB
    delta = n_prev * QB + QB - 1 - np.arange(kw + QB - 1)
    u = _bias_by_distance(bias_cols, delta, (delta >= 0) & (delta < window), dist_scale)
    return _toeplitz(u, QB, kw).reshape(n_kv, reps * QB, kw)


def _slc_bias_t(bias_cols, seq):
    n_chunks = seq // QB
    delta = seq - 1 - np.arange(seq + QB - 1)
    strip = _toeplitz(_bias_by_distance(bias_cols, delta, delta >= 0), QB, seq)
    tile = jnp.flip(strip.reshape(NSA_KV_HEADS, NSA_REPS, QB, n_chunks, QB), axis=3)
    tile = jnp.transpose(tile, (0, 3, 4, 1, 2)).reshape(NSA_KV_HEADS, n_chunks, QB, NSA_REPS * QB)
    return jnp.concatenate([jnp.full_like(tile[:, :1], NEG), tile], axis=1)


def _rope_tables(seq):
    half = MLA_ROPE // 2
    inv = jnp.power(ROPE_THETA, -jnp.arange(half, dtype=F32) / half)
    ang = jnp.arange(seq, dtype=F32)[:, None] * inv[None, :]
    cos, sin = jnp.cos(ang), jnp.sin(ang)
    zeros = lambda w: jnp.zeros((seq, w), F32)
    tail = MLA_PAD - MLA_QK_DIM
    cos_t = jnp.concatenate([jnp.ones((seq, MLA_NOPE), F32), cos, cos, zeros(tail)], axis=1)
    sin_t = jnp.concatenate([zeros(MLA_NOPE), -sin, sin, zeros(tail)], axis=1)
    return cos_t, sin_t


def _pad_cols(w, width):
    return jnp.pad(w, ((0, 0), (0, width - w.shape[1])))


def _with_swapped_rope(a):
    half = MLA_ROPE // 2
    return jnp.concatenate([a, a[..., MLA_NOPE + half:], a[..., MLA_NOPE:MLA_NOPE + half]], axis=-1)


def _nsa_column_order():
    g_cols = NSA_KV_HEADS * HEAD_DIM
    q_cols = NSA_HEADS * HEAD_DIM
    order = []
    for g in range(NSA_KV_HEADS):
        cols = list(range(g * _NQ, (g + 1) * _NQ))
        for part in range(2, 6):
            start = q_cols + part * g_cols + g * HEAD_DIM
            cols += list(range(start, start + HEAD_DIM))
        gate0 = q_cols + 6 * g_cols + g * NSA_REPS * 3
        cols += list(range(gate0, gate0 + NSA_REPS * 3))
        order.append(cols)
    return order


def _nsa_cmp_columns():
    q_cols = NSA_HEADS * HEAD_DIM
    g_cols = NSA_KV_HEADS * HEAD_DIM
    cols = []
    for g in range(NSA_KV_HEADS):
        for part in range(2):
            start = q_cols + part * g_cols + g * HEAD_DIM
            cols += list(range(start, start + HEAD_DIM))
    return cols


def _nsa_mixer(z_nsa3, z_cmp, rel_bias, q_norm, k_norm, cmp_pos, cmp_w1, cmp_w2, batch, seq):
    w1 = cmp_w1.reshape(2, 2, NSA_CMP_STRIDE, HEAD_DIM, NSA_CMP_HIDDEN)
    zero1 = jnp.zeros_like(w1[0])
    w1_bd = jnp.concatenate([jnp.concatenate([w1[0], zero1], axis=-1),
                             jnp.concatenate([zero1, w1[1]], axis=-1)], axis=-2)
    zero2 = jnp.zeros_like(cmp_w2[0])
    w2_bd = jnp.concatenate([jnp.concatenate([cmp_w2[0], zero2], axis=-1),
                             jnp.concatenate([zero2, cmp_w2[1]], axis=-1)], axis=-2)
    pos_kv = jnp.concatenate([cmp_pos[0], cmp_pos[1]], axis=-1)
    n_cmp = (seq - NSA_CMP_LEN) // NSA_CMP_STRIDE + 1
    ci = np.arange(NSA_N_CMP_PAD)[:, None] * NSA_CMP_STRIDE
    sj = np.arange(NSA_N_SLC)[None, :] * NSA_SLC_BLOCK
    overlap = ((ci < sj + NSA_SLC_BLOCK) & (ci + NSA_CMP_LEN > sj) & (np.arange(NSA_N_CMP_PAD)[:, None] < n_cmp))
    ovt = jnp.asarray(overlap.T.astype(np.float32))
    nsa_cols = rel_bias[:, NSA_BIAS_COL0:NSA_BIAS_COL0 + NSA_HEADS]
    bslc = _slc_bias_t(nsa_cols, seq)
    bwin = jnp.transpose(_band_bias(nsa_cols, NSA_WINDOW, NSA_WIN_PREV, 1, NSA_KV_HEADS, NSA_REPS), (0, 2, 1))
    return _nsa(z_nsa3, z_cmp.reshape(batch, seq, -1), w1_bd[0].astype(BF16), w1_bd[1].astype(BF16),
                w2_bd.astype(BF16), pos_kv, q_norm.reshape(HEAD_DIM, 1), k_norm.reshape(3, 1, HEAD_DIM),
                ovt, bslc, bwin, batch, seq)


def kernel(x, rel_bias, ffn1_norm, ffn1_w_gate, ffn1_w_up, ffn1_w_down, mix_norm, ffn2_norm, ffn2_w_gate,
           ffn2_w_up, ffn2_w_down, ab_w_in, mla_q_a_norm, mla_w_q_b, mla_kv_a_norm, mla_w_kv_b, mla_q_norm,
           mla_k_norm, dil_q_norm, dil_k_norm, ab_w_out, cd_w_in, swa_q_norm, swa_k_norm, swa_sinks,
           nsa_q_norm, nsa_k_norm, nsa_cmp_pos, nsa_cmp_w1, nsa_cmp_w2, cd_w_out):
    batch, seq, _ = x.shape
    n = batch * seq
    assert seq % (16 * QB) == 0 and n % TM == 0 and seq % TM == 0
    bf = lambda a: a.astype(BF16)
    xf = x.reshape(n, D_MODEL)
    ffn1 = (bf(ffn1_w_gate), bf(ffn1_w_up), bf(ffn1_w_down))
    ffn2 = (bf(ffn2_w_gate), bf(ffn2_w_up), bf(ffn2_w_down))

    w_in = ab_w_in[0]
    mla_cols = MLA_Q_RANK + MLA_KV_RANK
    w_krope = _with_swapped_rope(jnp.pad(w_in[:, mla_cols:mla_cols + MLA_ROPE], ((0, 0), (MLA_NOPE, 0))))
    w_mla = jnp.concatenate([w_in[:, :mla_cols], w_krope], axis=1)
    xf, z_mla, z_dil = _ffn_proj(xf, ffn1_norm[0], ffn1, 0, mix_norm[0],
                                 [bf(w_mla), bf(w_in[:, mla_cols + MLA_ROPE:])])

    wq = _with_swapped_rope(mla_w_q_b[0].reshape(MLA_Q_RANK, MLA_HEADS, MLA_QK_DIM))
    wq = wq.reshape(MLA_Q_RANK, MLA_HEADS * MLA_PAD)
    wkv = mla_w_kv_b[0].reshape(MLA_KV_RANK, MLA_HEADS, MLA_NOPE + MLA_V)
    wk = _pad_cols(wkv[:, :, :MLA_NOPE].reshape(MLA_KV_RANK * MLA_HEADS, MLA_NOPE), MLA_PAD)
    wk = wk.reshape(MLA_KV_RANK, MLA_HEADS * MLA_PAD)
    wv = wkv[:, :, MLA_NOPE:].reshape(MLA_KV_RANK, MLA_HEADS * MLA_V)
    cos_t, sin_t = _rope_tables(seq)
    q_scale = MLA_QK_DIM ** -0.5 * LOG2E
    q_mla, k_mla, vt_mla = _mla_prep(
        z_mla, mla_q_a_norm[0].reshape(1, -1), mla_kv_a_norm[0].reshape(1, -1), bf(wq), bf(wk), bf(wv),
        _with_swapped_rope(mla_q_norm[0].reshape(1, -1)), _with_swapped_rope(mla_k_norm[0].reshape(1, -1)),
        cos_t * q_scale, sin_t * q_scale, cos_t, sin_t, seq)
    o_a = _mla_attn(q_mla.reshape(batch, seq, -1), k_mla.reshape(batch, seq, -1),
                    vt_mla.reshape(batch, seq // MLA_TQ, MLA_HEADS * MLA_V, MLA_TQ),
                    batch, seq).reshape(n, MLA_HEADS * MLA_V)

    z_dil3 = z_dil.reshape(batch, seq, -1)
    gq, gk = dil_q_norm[0], dil_k_norm[0]
    dil_outs, dil_lses = [], []
    for grp, (window, dilation) in enumerate(DIL_PATTERNS):
        bm = _band_bias(rel_bias[:, grp * DIL_HPG:(grp + 1) * DIL_HPG], window // dilation + 1, 1, dilation,
                        DIL_HPG, 1)
        o, lse = _dilated_group(z_dil3, batch, seq, grp, dilation, gq, gk, bm)
        dil_outs.append(o)
        dil_lses.append(lse)
    w_out = ab_w_out[0]
    xf = _out_ab_ffn(xf, o_a, dil_outs, dil_lses, bf(w_out[:MLA_HEADS * MLA_V]), bf(w_out[MLA_HEADS * MLA_V:]),
                     ffn2_norm[0], ffn2, 0)

    w_in = cd_w_in[0]
    swa_cols = (SWA_HEADS + 2 * SWA_KV_HEADS) * HEAD_DIM
    w_nsa_src = w_in[:, swa_cols:]
    w_nsa = jnp.concatenate([_pad_cols(w_nsa_src[:, np.asarray(cols)], NSA_GROUP_COLS)
                             for cols in _nsa_column_order()], axis=1)
    w_cmp = w_nsa_src[:, np.asarray(_nsa_cmp_columns())]
    xf, z_swa, z_nsa, z_cmp = _ffn_proj(xf, ffn1_norm[1], ffn1, 1, mix_norm[1],
                                        [bf(w_in[:, :swa_cols]), bf(w_nsa), bf(w_cmp)])

    swa_reps = SWA_HEADS // SWA_KV_HEADS
    swa_prev = -(-(SWA_WINDOW - 1) // QB)
    bm_swa = _band_bias(rel_bias[:, :SWA_HEADS], SWA_WINDOW, swa_prev, 1, SWA_KV_HEADS, swa_reps)
    sinks = jnp.broadcast_to((swa_sinks[0].astype(F32) * LOG2E).reshape(SWA_KV_HEADS, 1, swa_reps, 1),
                             (SWA_KV_HEADS, 1, swa_reps, QB)).reshape(SWA_KV_HEADS, 1, swa_reps * QB)
    o_c = _swa(z_swa.reshape(batch, seq, -1), batch, seq, swa_q_norm[0].reshape(HEAD_DIM, 1),
               swa_k_norm[0].reshape(1, HEAD_DIM), bm_swa, sinks)

    o_d = _nsa_mixer(z_nsa.reshape(batch, seq, NSA_KV_HEADS * NSA_GROUP_COLS), z_cmp, rel_bias, nsa_q_norm[0],
                     nsa_k_norm[0], nsa_cmp_pos[0], nsa_cmp_w1[0], nsa_cmp_w2[0], batch, seq)

    w_out = cd_w_out[0]
    xf = _out_cd_ffn(xf, o_c.reshape(n, -1), o_d.reshape(n, -1), bf(w_out[:SWA_HEADS * HEAD_DIM]),
                     bf(w_out[SWA_HEADS * HEAD_DIM:]), ffn2_norm[1], ffn2, 1)
    return xf.reshape(batch, seq, D_MODEL)
```

```python
import functools
import math

import numpy as np
import jax
import jax.numpy as jnp
from jax import lax
from jax.experimental import pallas as pl
from jax.experimental.pallas import tpu as pltpu

F32 = jnp.float32
BF16 = jnp.bfloat16

EPS = 1e-6
NEG = -1e30
TINY = 1e-30
LOG2E = math.log2(math.e)
LN2 = math.log(2.0)
D_MODEL = 1024
D_FF = 2816
NUM_BUCKETS = 32
MAX_DISTANCE = 2048
HEAD_DIM = 64
QB = 128

MLA_HEADS = 8
MLA_Q_RANK = 256
MLA_KV_RANK = 128
MLA_NOPE = 64
MLA_ROPE = 32
MLA_V = 64
MLA_QK_DIM = MLA_NOPE + MLA_ROPE
MLA_PAD = 128
ROPE_THETA = 10000.0

DIL_PATTERNS = ((128, 1), (512, 4), (2048, 16))
DIL_HPG = 4
SWA_HEADS = 8
SWA_KV_HEADS = 2
SWA_WINDOW = 128
NSA_HEADS = 8
NSA_KV_HEADS = 2
NSA_CMP_LEN = 32
NSA_CMP_STRIDE = 16
NSA_CMP_HIDDEN = 128
NSA_SLC_BLOCK = 64
NSA_TOP_N = 16
NSA_WINDOW = 512
NSA_FORCE = 1e6
NSA_BIAS_COL0 = 8
NSA_GROUP_COLS = 640

VMEM_LIMIT = 56 * 1024 * 1024
TM = 512

_NT = (((1,), (1,)), ((), ()))
_HI = lax.Precision.HIGHEST


def _cparams(n_axes):
    return pltpu.CompilerParams(dimension_semantics=("arbitrary",) * n_axes,
                                vmem_limit_bytes=VMEM_LIMIT)


def _resident(shape):
    nd = len(shape)
    return pl.BlockSpec(shape, lambda *_: (0,) * nd, pipeline_mode=pl.Buffered(1))


def _layer_block(w, layer):
    nd = w.ndim - 1
    return pl.BlockSpec((None,) + w.shape[1:], lambda *_: (layer,) + (0,) * nd, pipeline_mode=pl.Buffered(1))


def _rms(x):
    return x * lax.rsqrt(jnp.mean(x * x, axis=-1, keepdims=True) + EPS)


def _dot(a, b):
    return jnp.dot(a, b, preferred_element_type=F32)


def _dot_nt(a, b, precision=None):
    return lax.dot_general(a, b, _NT, preferred_element_type=F32, precision=precision)


FF_CHUNK = 256


N_CHUNK = 512


def _swiglu_half_step(x, g_ref, wg_ref, wu_ref, wd_ref):
    hb = (_rms(x) * g_ref[...]).astype(BF16)
    acc = jnp.zeros(x.shape, F32)
    for c in range(D_FF // FF_CHUNK):
        sl = slice(c * FF_CHUNK, (c + 1) * FF_CHUNK)
        gate = _dot(hb, wg_ref[:, sl])
        up = _dot(hb, wu_ref[:, sl])
        act = (gate * jax.nn.sigmoid(gate) * up).astype(BF16)
        acc = acc + _dot(act, wd_ref[sl, :])
    return x + 0.5 * acc


def _ffn_proj_kernel(n_out, x_ref, g_ref, wg_ref, wu_ref, wd_ref, gm_ref, *refs):
    w_refs, x_out_ref, z_refs = refs[:n_out], refs[n_out], refs[n_out + 1:]
    x = _swiglu_half_step(x_ref[...], g_ref, wg_ref, wu_ref, wd_ref)
    x_out_ref[...] = x
    hb = (_rms(x) * gm_ref[...]).astype(BF16)
    for w_ref, z_ref in zip(w_refs, z_refs):
        width = w_ref.shape[1]
        for c0 in range(0, width, N_CHUNK):
            sl = slice(c0, min(c0 + N_CHUNK, width))
            z_ref[:, sl] = _dot(hb, w_ref[:, sl]).astype(z_ref.dtype)


def _tile(width):
    return pl.BlockSpec((TM, width), lambda i: (i, 0))


def _ffn_proj(x, g, ffn, layer, g_mix, ws, z_dtypes=None):
    n = x.shape[0]
    z_dtypes = z_dtypes or [F32] * len(ws)
    ffn_w = [_layer_block(w, layer) for w in ffn]
    return pl.pallas_call(
        functools.partial(_ffn_proj_kernel, len(ws)),
        grid=(n // TM,),
        in_specs=[_tile(D_MODEL), _resident((1, D_MODEL))] + ffn_w + [_resident((1, D_MODEL))]
        + [_resident(w.shape) for w in ws],
        out_specs=[_tile(D_MODEL)] + [_tile(w.shape[1]) for w in ws],
        out_shape=[jax.ShapeDtypeStruct((n, D_MODEL), F32)]
        + [jax.ShapeDtypeStruct((n, w.shape[1]), dt) for w, dt in zip(ws, z_dtypes)],
        compiler_params=_cparams(1),
        name="ffn_proj",
    )(x, g.reshape(1, D_MODEL), *ffn, g_mix.reshape(1, D_MODEL), *ws)


def _out_ab_ffn_kernel(x_ref, oa_ref, o0_ref, o1_ref, o2_ref, l0_ref, l1_ref, l2_ref, wa_ref, wb_ref,
                       g_ref, wg_ref, wu_ref, wd_ref, out_ref):
    l0, l1, l2 = l0_ref[...], l1_ref[...], l2_ref[...]
    m = jnp.maximum(jnp.maximum(l0, l1), l2)
    e0, e1, e2 = jnp.exp(l0 - m), jnp.exp(l1 - m), jnp.exp(l2 - m)
    ob = (e0 * o0_ref[...] + e1 * o1_ref[...] + e2 * o2_ref[...]) / (e0 + e1 + e2)
    x = x_ref[...] + _dot(oa_ref[...], wa_ref[...]) + _dot(ob.astype(BF16), wb_ref[...])
    out_ref[...] = _swiglu_half_step(x, g_ref, wg_ref, wu_ref, wd_ref)


def _out_ab_ffn(x, oa, dil_outs, dil_lses, wa, wb, g, ffn, layer):
    n = x.shape[0]
    dw = DIL_HPG * HEAD_DIM
    ffn_w = [_layer_block(w, layer) for w in ffn]
    return pl.pallas_call(
        _out_ab_ffn_kernel,
        grid=(n // TM,),
        in_specs=[_tile(D_MODEL), _tile(oa.shape[1])] + [_tile(dw)] * 6 + [_resident(wa.shape), _resident(wb.shape),
                                                                        _resident((1, D_MODEL))] + ffn_w,
        out_specs=_tile(D_MODEL),
        out_shape=jax.ShapeDtypeStruct((n, D_MODEL), F32),
        compiler_params=_cparams(1),
        name="out_ab_ffn",
    )(x, oa, *dil_outs, *dil_lses, wa, wb, g.reshape(1, D_MODEL), *ffn)


def _out_cd_ffn_kernel(x_ref, oc_ref, od_ref, wc_ref, wdd_ref, g_ref, wg_ref, wu_ref, wd_ref, out_ref):
    x = x_ref[...] + _dot(oc_ref[...], wc_ref[...]) + _dot(od_ref[...], wdd_ref[...])
    out_ref[...] = _swiglu_half_step(x, g_ref, wg_ref, wu_ref, wd_ref)


def _out_cd_ffn(x, oc, od, wc, wdd, g, ffn, layer):
    n = x.shape[0]
    ffn_w = [_layer_block(w, layer) for w in ffn]
    return pl.pallas_call(
        _out_cd_ffn_kernel,
        grid=(n // TM,),
        in_specs=[_tile(D_MODEL), _tile(oc.shape[1]), _tile(od.shape[1]), _resident(wc.shape), _resident(wdd.shape),
                  _resident((1, D_MODEL))] + ffn_w,
        out_specs=_tile(D_MODEL),
        out_shape=jax.ShapeDtypeStruct((n, D_MODEL), F32),
        compiler_params=_cparams(1),
        name="out_cd_ffn",
    )(x, oc, od, wc, wdd, g.reshape(1, D_MODEL), *ffn)


def _mla_prep_kernel(z_ref, qan_ref, kvan_ref, wq_ref, wk_ref, wv_ref, gq_ref, gk_ref,
                     qc_ref, qs_ref, kc_ref, ks_ref, q_ref, k_ref, vt_ref):
    cq = (_rms(z_ref[:, :MLA_Q_RANK]) * qan_ref[...]).astype(BF16)
    ckv = (_rms(z_ref[:, MLA_Q_RANK:MLA_Q_RANK + MLA_KV_RANK]) * kvan_ref[...]).astype(BF16)
    kr = z_ref[:, MLA_Q_RANK + MLA_KV_RANK:]
    real = jnp.where(lax.broadcasted_iota(jnp.int32, (1, MLA_PAD), 1) < MLA_QK_DIM, 1.0, 0.0)

    def head_norm_rope(x, gain, cos, sin):
        ssq = jnp.sum(x * x * real, axis=-1, keepdims=True)
        x = x * lax.rsqrt(ssq * (1.0 / MLA_QK_DIM) + EPS) * gain
        return x * cos + pltpu.roll(x, MLA_PAD - MLA_ROPE, 1) * sin

    for h in range(MLA_HEADS):
        sl = slice(h * MLA_PAD, (h + 1) * MLA_PAD)
        qh = head_norm_rope(_dot(cq, wq_ref[:, sl]), gq_ref[...], qc_ref[...], qs_ref[...])
        q_ref[:, sl] = qh.astype(BF16)
        kh = head_norm_rope(_dot(ckv, wk_ref[:, sl]) + kr, gk_ref[...], kc_ref[...], ks_ref[...])
        k_ref[:, sl] = kh.astype(BF16)
    v = _dot(ckv, wv_ref[...])
    for c in range(TM // MLA_TQ):
        vt_ref[c] = v[c * MLA_TQ:(c + 1) * MLA_TQ].T.astype(BF16)


MLA_TQ = 256
MLA_STEP = 2


def _mla_prep(z, qan, kvan, wq, wk, wv, gq, gk, q_cos, q_sin, k_cos, k_sin, seq):
    n = z.shape[0]
    per_seq = seq // TM
    tile = lambda w: pl.BlockSpec((TM, w), lambda i: (i, 0))
    rope_spec = pl.BlockSpec((TM, MLA_PAD), lambda i: (i % per_seq, 0))
    hw = MLA_HEADS * MLA_PAD
    vw = MLA_HEADS * MLA_V
    return pl.pallas_call(
        _mla_prep_kernel,
        grid=(n // TM,),
        in_specs=[tile(z.shape[1]), _resident(qan.shape), _resident(kvan.shape), _resident(wq.shape),
                  _resident(wk.shape), _resident(wv.shape), _resident(gq.shape), _resident(gk.shape),
                  rope_spec, rope_spec, rope_spec, rope_spec],
        out_specs=[tile(hw), tile(hw), pl.BlockSpec((TM // MLA_TQ, vw, MLA_TQ), lambda i: (i, 0, 0))],
        out_shape=[jax.ShapeDtypeStruct((n, hw), BF16), jax.ShapeDtypeStruct((n, hw), BF16),
                   jax.ShapeDtypeStruct((n // MLA_TQ, vw, MLA_TQ), BF16)],
        compiler_params=_cparams(1),
        name="mla_prep",
    )(z, qan, kvan, wq, wk, wv, gq, gk, q_cos, q_sin, k_cos, k_sin)


def _flash_step_t(s, m, l, acc, pv, vt):
    m_new = jnp.maximum(m, jnp.max(s, axis=0, keepdims=True))
    p = jnp.exp2(s - m_new)
    alpha = jnp.exp2(m - m_new)
    l = alpha * l + jnp.sum(p, axis=0, keepdims=True)
    acc = alpha * (acc + pv)
    return m_new, l, acc, _dot(vt, p.astype(BF16))


def _flash_init(dv, n_queries):
    return (jnp.full((1, n_queries), NEG, F32), jnp.zeros((1, n_queries), F32), jnp.zeros((dv, n_queries), F32),
            jnp.zeros((dv, n_queries), F32))


def _flash_finish(m, l, acc, pv):
    return (acc + pv) * (1.0 / l)


def _mla_attn_kernel(seq, q_ref, k_ref, vt_ref, o_ref):
    t = MLA_TQ
    key = lax.broadcasted_iota(jnp.int32, (t, t), 0)
    qry = lax.broadcasted_iota(jnp.int32, (t, t), 1)
    diag_ok = key <= qry
    heads = [(slice(hh * MLA_PAD, (hh + 1) * MLA_PAD), slice(hh * MLA_V, (hh + 1) * MLA_V)) for hh in range(2)]

    init = _flash_init(MLA_V, t)
    n_blocks = seq // t
    for first in range(n_blocks // 2):
        blocks = [first, n_blocks - 1 - first]
        chains = [(i, h) for i in blocks for h in range(2)]
        qs = {(i, h): q_ref[i * t:(i + 1) * t, heads[h][0]] for i, h in chains}
        steps = {i: [(j, min(j + MLA_STEP, i + 1)) for j in range(0, i + 1, MLA_STEP)] for i in blocks}

        def scores(chain, step):
            i, h = chain
            lo, hi = steps[i][step]
            s = _dot_nt(k_ref[lo * t:hi * t, heads[h][0]], qs[chain])
            if hi == i + 1:
                last = jnp.where(diag_ok, s[(i - lo) * t:], NEG)
                s = last if i == lo else jnp.concatenate([s[:(i - lo) * t], last], axis=0)
            return s

        def values(chain, step):
            i, h = chain
            lo, hi = steps[i][step]
            return jnp.concatenate([vt_ref[j, heads[h][1], :] for j in range(lo, hi)], axis=1)

        cur = {c: scores(c, 0) for c in chains}
        states = {c: init for c in chains}
        for step in range(max(len(s) for s in steps.values())):
            for c in chains:
                n_steps = len(steps[c[0]])
                if step >= n_steps:
                    continue
                nxt = scores(c, step + 1) if step + 1 < n_steps else None
                states[c] = _flash_step_t(cur[c], *states[c], values(c, step))
                cur[c] = nxt
        for i in blocks:
            outs = [_flash_finish(*states[(i, h)]) for h in range(2)]
            o_ref[i * t:(i + 1) * t, :] = jnp.concatenate(outs, axis=0).T.astype(o_ref.dtype)


def _mla_attn(q, k, vt, batch, seq):
    return pl.pallas_call(
        functools.partial(_mla_attn_kernel, seq),
        grid=(batch, MLA_HEADS // 2),
        in_specs=[pl.BlockSpec((None, seq, 2 * MLA_PAD), lambda b, h: (b, 0, h)),
                  pl.BlockSpec((None, seq, 2 * MLA_PAD), lambda b, h: (b, 0, h)),
                  pl.BlockSpec((None, seq // MLA_TQ, 2 * MLA_V, MLA_TQ), lambda b, h: (b, 0, h, 0))],
        out_specs=pl.BlockSpec((None, seq, 2 * MLA_V), lambda b, h: (b, 0, h)),
        out_shape=jax.ShapeDtypeStruct((batch, seq, MLA_HEADS * MLA_V), BF16),
        compiler_params=_cparams(2),
        name="mla_attn",
    )(q, k, vt)


def _head_norm(x, gain):
    return _rms(x) * gain


def _heads_t(qt, n_heads, gain):
    tiles = []
    for c0 in range(0, n_heads * HEAD_DIM, 2 * HEAD_DIM):
        pair = qt[:, c0:c0 + 2 * HEAD_DIM].T
        for h in range(2):
            x = pair[h * HEAD_DIM:(h + 1) * HEAD_DIM]
            tiles.append(x * lax.rsqrt(jnp.mean(x * x, axis=0, keepdims=True) + EPS) * gain)
    return tiles


def _q_gain(gq_ref):
    return jnp.broadcast_to(gq_ref[...] * (HEAD_DIM ** -0.5 * LOG2E), (HEAD_DIM, QB))


def _pair_ones():
    shape = (2 * HEAD_DIM, 2 * HEAD_DIM)
    same = lax.broadcasted_iota(jnp.int32, shape, 0) // HEAD_DIM == lax.broadcasted_iota(jnp.int32, shape, 1) // HEAD_DIM
    return jnp.where(same, 1.0, 0.0).astype(BF16)


def _pair_norm(x, seg, gain):
    sq = x * x
    hi = sq.astype(BF16)
    lo = (sq - hi.astype(F32)).astype(BF16)
    ssq = _dot(hi, seg) + _dot(lo, seg)
    return x * lax.rsqrt(ssq * (1.0 / HEAD_DIM) + EPS) * gain


BAND_KV = 2


def _banded_kernel(seq, dilation, reps, n_prev, with_sink, with_lse, group_size, paired, *refs):
    q_ref, k_ref, v_ref, gq_ref, gk_ref, bm_ref = refs[:6]
    refs = refs[6:]
    sink_ref = None
    if with_sink:
        sink_ref, refs = refs[0], refs[1:]
    o_ref, refs = refs[0], refs[1:]
    lse_ref = None
    if with_lse:
        lse_ref, refs = refs[0], refs[1:]
    kn_ref, vt_ref = refs
    length = seq // dilation
    n_blocks = length // QB
    pad = n_prev * QB
    kw = pad + QB
    win_row = lax.broadcasted_iota(jnp.int32, (kw, 1), 0)
    assert not paired or reps == 1
    if paired:
        gq, gk = gq_ref[...] * (HEAD_DIM ** -0.5 * LOG2E), gk_ref[...]
        seg = _pair_ones()
        lane = lax.broadcasted_iota(jnp.int32, (1, BAND_KV * HEAD_DIM), 1)
        head_lanes = [jnp.where(lane // HEAD_DIM == g, 1.0, 0.0) for g in range(BAND_KV)]
    else:
        gq, gk = _q_gain(gq_ref), gk_ref[...]

    def rows(start, size):
        return pl.ds(start, size) if dilation == 1 else pl.ds(start, size, stride=dilation)

    for res in range(dilation):
        kall, vall = k_ref[rows(res, length), :], v_ref[rows(res, length), :]
        if paired:
            kn_ref[res, :pad, :] = jnp.zeros((pad, BAND_KV * HEAD_DIM), BF16)
            kn_ref[res, pad:, :] = _pair_norm(kall, seg, gk).astype(BF16)
        else:
            for g in range(BAND_KV):
                kn_ref[res, g, :pad, :] = jnp.zeros((pad, HEAD_DIM), BF16)
                kn_ref[res, g, pad:, :] = _head_norm(kall[:, g * HEAD_DIM:(g + 1) * HEAD_DIM], gk).astype(BF16)
        for j in range(n_prev):
            vt_ref[res, j] = jnp.zeros((BAND_KV * HEAD_DIM, QB), BF16)
        for j in range(n_blocks):
            vt_ref[res, n_prev + j] = vall[j * QB:(j + 1) * QB].T.astype(BF16)

    def attend(res, i, g, s):
        bias = bm_ref[g]
        if i < n_prev:
            bias = jnp.where(win_row >= (n_prev - i) * QB, bias, NEG)
        s = s + bias
        m = jnp.max(s, axis=0, keepdims=True)
        if with_sink:
            m = jnp.maximum(m, sink_ref[g])
        p = jnp.exp2(s - m)
        l = jnp.sum(p, axis=0, keepdims=True)
        if with_sink:
            l = l + jnp.exp2(sink_ref[g] - m)
        p = p.astype(BF16)
        vl = slice(g * HEAD_DIM, (g + 1) * HEAD_DIM)
        o = _dot(vt_ref[res, i, vl, :], p[0:QB])
        for w in range(1, n_prev + 1):
            o = o + _dot(vt_ref[res, i + w, vl, :], p[w * QB:(w + 1) * QB])
        return o * (1.0 / l), m * LN2 + jnp.log(l)

    items = [(res, i) for res in range(dilation) for i in range(n_blocks)]
    for at in range(0, len(items), group_size):
        group = items[at:at + group_size]
        toks = [rows(res + i * QB * dilation, QB) for res, i in group]
        if paired:
            qns = [_pair_norm(q_ref[tok, :], seg, gq) for tok in toks]
            qss = [[(qn * head_lanes[g]).astype(BF16) for g in range(BAND_KV)] for qn in qns]
            scores = [[_dot_nt(kn_ref[res, i * QB:i * QB + kw, :], qs[g]) for g in range(BAND_KV)]
                      for (res, i), qs in zip(group, qss)]
        else:
            heads = [_heads_t(q_ref[tok, :], BAND_KV * reps, gq) for tok in toks]
            qss = [[jnp.concatenate(h[g * reps:(g + 1) * reps], axis=1).astype(BF16) for g in range(BAND_KV)]
                   for h in heads]
            scores = [[_dot(kn_ref[res, g, i * QB:i * QB + kw, :], qs[g]) for g in range(BAND_KV)]
                      for (res, i), qs in zip(group, qss)]
        for (res, i), tok, sc in zip(group, toks, scores):
            outs, lses = [], []
            for g in range(BAND_KV):
                o, lse = attend(res, i, g, sc[g])
                for r in range(reps):
                    outs.append(o[:, r * QB:(r + 1) * QB])
                    if with_lse:
                        lses.append(jnp.broadcast_to(lse[:, r * QB:(r + 1) * QB], (HEAD_DIM, QB)))
            o_ref[tok, :] = jnp.concatenate(outs, axis=0).T.astype(o_ref.dtype)
            if with_lse:
                lse_ref[tok, :] = jnp.concatenate(lses, axis=0).T


def _banded_scratch(seq, dilation, n_prev, paired):
    length = seq // dilation
    rows = n_prev * QB + length
    keys = (dilation, rows, BAND_KV * HEAD_DIM) if paired else (dilation, BAND_KV, rows, HEAD_DIM)
    return [pltpu.VMEM(keys, BF16),
            pltpu.VMEM((dilation, n_prev + length // QB, BAND_KV * HEAD_DIM, QB), BF16)]


PAIRED_MIN_DILATION = 16


def _dilated_group(z_dil, batch, seq, group, dilation, q_norm, k_norm, bm):
    gw = DIL_HPG * HEAD_DIM
    n_groups = len(DIL_PATTERNS)
    hps = BAND_KV
    halves = DIL_HPG // hps
    bm_t = jnp.transpose(bm, (0, 2, 1))
    paired = dilation >= PAIRED_MIN_DILATION
    if paired:
        gq = jnp.tile(q_norm.reshape(1, HEAD_DIM), (1, BAND_KV))
        gk = jnp.tile(k_norm.reshape(1, HEAD_DIM), (1, BAND_KV))
    else:
        gq, gk = q_norm.reshape(HEAD_DIM, 1), k_norm.reshape(1, HEAD_DIM)

    def spec(part):
        return pl.BlockSpec((None, seq, hps * HEAD_DIM), lambda b, h: (b, 0, (part * n_groups + group) * halves + h))

    out_spec = pl.BlockSpec((None, seq, hps * HEAD_DIM), lambda b, h: (b, 0, h))
    shape = jax.ShapeDtypeStruct((batch, seq, gw), F32)
    out, lse = pl.pallas_call(
        functools.partial(_banded_kernel, seq, dilation, 1, 1, False, True, 8, paired),
        grid=(batch, halves),
        in_specs=[spec(0), spec(1), spec(2), _resident(gq.shape), _resident(gk.shape),
                  pl.BlockSpec((hps,) + bm_t.shape[1:], lambda b, h: (h, 0, 0))],
        out_specs=[out_spec, out_spec],
        out_shape=[shape, shape],
        scratch_shapes=_banded_scratch(seq, dilation, 1, paired),
        compiler_params=_cparams(2),
        name="dilated_g%d" % group,
    )(z_dil, z_dil, z_dil, gq, gk, bm_t)
    return out.reshape(batch * seq, gw), lse.reshape(batch * seq, gw)


def _swa(z_swa, batch, seq, gq, gk, bm, sinks):
    qw = SWA_HEADS * HEAD_DIM
    kvw = SWA_KV_HEADS * HEAD_DIM
    reps = SWA_HEADS // SWA_KV_HEADS
    n_prev = -(-(SWA_WINDOW - 1) // QB)
    assert SWA_KV_HEADS == BAND_KV
    bm_t = jnp.transpose(bm, (0, 2, 1))
    return pl.pallas_call(
        functools.partial(_banded_kernel, seq, 1, reps, n_prev, True, False, 4, False),
        grid=(batch,),
        in_specs=[pl.BlockSpec((None, seq, qw), lambda b: (b, 0, 0)),
                  pl.BlockSpec((None, seq, kvw), lambda b: (b, 0, qw // kvw)),
                  pl.BlockSpec((None, seq, kvw), lambda b: (b, 0, qw // kvw + 1)),
                  _resident(gq.shape), _resident(gk.shape), _resident(bm_t.shape), _resident(sinks.shape)],
        out_specs=pl.BlockSpec((None, seq, qw), lambda b: (b, 0, 0)),
        out_shape=jax.ShapeDtypeStruct((batch, seq, qw), BF16),
        scratch_shapes=_banded_scratch(seq, 1, n_prev, False),
        compiler_params=_cparams(1),
        name="swa",
    )(z_swa, z_swa, z_swa, gq, gk, bm_t, sinks)


NSA_REPS = NSA_HEADS // NSA_KV_HEADS
NSA_N_SLC = 32
NSA_N_CMP_PAD = 128
NSA_WIN_PREV = -(-(NSA_WINDOW - 1) // QB)
_NQ = NSA_REPS * HEAD_DIM
_OFF_KS, _OFF_VS, _OFF_KW, _OFF_VW, _OFF_GATE = (_NQ + i * HEAD_DIM for i in range(5))


NSA_TIERS = 4
NSA_STEP = 8
NSA_QPAIR = 2


def _nsa_kernel(seq, z_ref, zc_ref, w1a_ref, w1b_ref, w2_ref, pos_ref, gq_ref, gk_ref, ovt_ref,
                bslc_ref, bwin_ref, o_ref, ks_ref, vst_ref, kw_ref, vwt_ref):
    n_chunks = seq // QB
    lanes = NSA_REPS * QB
    gq = _q_gain(gq_ref)
    win_pad = NSA_WIN_PREV * QB
    win_kw = win_pad + QB

    first = jnp.zeros((NSA_N_CMP_PAD, 2 * NSA_CMP_HIDDEN), F32)
    second = jnp.zeros((NSA_N_CMP_PAD, 2 * NSA_CMP_HIDDEN), F32)
    for l in range(NSA_CMP_STRIDE):
        tok = zc_ref[pl.ds(l, NSA_N_CMP_PAD, stride=NSA_CMP_STRIDE), :]
        first = first + _dot((tok + pos_ref[l:l + 1, :]).astype(BF16), w1a_ref[l])
        second = second + _dot((tok + pos_ref[NSA_CMP_STRIDE + l:NSA_CMP_STRIDE + l + 1, :]).astype(BF16),
                               w1b_ref[l])
    hidden = jax.nn.gelu(first + pltpu.roll(second, NSA_N_CMP_PAD - 1, 0)).astype(BF16)
    cmp_kv = _dot(hidden, w2_ref[...])
    kc = _head_norm(cmp_kv[:, :HEAD_DIM], gk_ref[0]).astype(BF16)
    vct = cmp_kv.T[HEAD_DIM:].astype(BF16)

    ks_ref[...] = _head_norm(z_ref[:, _OFF_KS:_OFF_KS + HEAD_DIM], gk_ref[1]).astype(BF16)
    kw_ref[:win_pad, :] = jnp.zeros((win_pad, HEAD_DIM), BF16)
    kw_ref[win_pad:, :] = _head_norm(z_ref[:, _OFF_KW:_OFF_KW + HEAD_DIM], gk_ref[2]).astype(BF16)
    for j in range(NSA_WIN_PREV):
        vwt_ref[j] = jnp.zeros((HEAD_DIM, QB), BF16)
    for j in range(n_chunks):
        rows = slice(j * QB, (j + 1) * QB)
        vst_ref[:, rows] = z_ref[rows, _OFF_KS:_OFF_KS + 2 * HEAD_DIM].T[HEAD_DIM:].astype(BF16)
        vwt_ref[NSA_WIN_PREV + j] = z_ref[rows, _OFF_KW:_OFF_KW + 2 * HEAD_DIM].T[HEAD_DIM:].astype(BF16)

    cmp_id = lax.broadcasted_iota(jnp.int32, (NSA_N_CMP_PAD, lanes), 0)
    cmp_end = cmp_id * NSA_CMP_STRIDE + (NSA_CMP_LEN - 1)
    cmp_real = cmp_id < NSA_N_CMP_PAD - 1
    q_in_blk = lax.broadcasted_iota(jnp.int32, (NSA_N_CMP_PAD, lanes), 1) & (QB - 1)
    blk_id = lax.broadcasted_iota(jnp.int32, (NSA_N_SLC, QB), 0)
    q_lane = lax.broadcasted_iota(jnp.int32, (NSA_N_SLC, QB), 1)
    win_row = lax.broadcasted_iota(jnp.int32, (win_kw, 1), 0)
    init = _flash_init(HEAD_DIM, lanes)
    both = range(NSA_QPAIR)

    def q_blocks(n_keys, mask_pad, pair, carry):
        step = [max(d for d in range(1, NSA_STEP + 1) if n % d == 0) for n in n_keys]
        step_keys = [s * QB for s in step]
        blocks_per_step = [s // NSA_SLC_BLOCK for s in step_keys]
        blk = [pair, n_chunks - 1 - pair]
        r0 = [pl.multiple_of(i * QB, QB) for i in blk]
        qs = [jnp.concatenate(_heads_t(z_ref[pl.ds(r, QB), :_NQ], NSA_REPS, gq), axis=1).astype(BF16)
              for r in r0]

        sc = [_dot(kc, q) for q in qs]
        s_win = [_dot(kw_ref[pl.ds(r, win_kw), :], q) for r, q in zip(r0, qs)]

        def slc_scores(b, j):
            return _dot(ks_ref[j * step_keys[b]:(j + 1) * step_keys[b], :], qs[b])

        cur = [slc_scores(b, 0) for b in both]

        o_cmp, imp = [], []
        for b in both:
            ok = (cmp_end <= q_in_blk + blk[b] * QB) & cmp_real
            s = jnp.where(ok, sc[b], NEG)
            e = jnp.where(ok, jnp.exp2(s - jnp.max(s, axis=0, keepdims=True)), 0.0)
            p = e * (1.0 / jnp.maximum(jnp.sum(e, axis=0, keepdims=True), TINY))
            o_cmp.append(_dot(vct, p.astype(BF16)))
            p_sum = p[:, 0:QB] + p[:, QB:2 * QB] + p[:, 2 * QB:3 * QB] + p[:, 3 * QB:4 * QB]
            imp.append(jnp.dot(ovt_ref[...], p_sum, precision=_HI, preferred_element_type=F32))

        o_win = []
        for b in both:
            bias = bwin_ref[...]
            if mask_pad[b]:
                bias = jnp.where(win_row >= (NSA_WIN_PREV - blk[b]) * QB, bias, NEG)
            s = s_win[b] + bias
            pw = jnp.exp2(s - jnp.max(s, axis=0, keepdims=True))
            lw = jnp.sum(pw, axis=0, keepdims=True)
            pw = pw.astype(BF16)
            o = _dot(vwt_ref[blk[b]], pw[0:QB])
            for c in range(1, NSA_WIN_PREV + 1):
                o = o + _dot(vwt_ref[blk[b] + c], pw[c * QB:(c + 1) * QB])
            o_win.append(o * (1.0 / lw))

        drop = []
        for b in both:
            tb = lax.shift_right_logical(q_lane + blk[b] * QB, 6)
            forced = (blk_id == 0) | (blk_id == tb) | (blk_id == tb - 1)
            score = jnp.where(blk_id <= tb, imp[b] + jnp.where(forced, NSA_FORCE, 0.0), -NSA_FORCE)
            rank = jnp.zeros((NSA_N_SLC, QB), F32)
            for other in range(NSA_N_SLC):
                s_o = score[other:other + 1, :]
                beats = (s_o > score) | ((s_o == score) & (blk_id > other))
                rank = rank + jnp.where(beats, 1.0, 0.0)
            drop.append(jnp.where(rank < NSA_TOP_N, 0.0, NEG))

        states = [init for _ in both]
        n_steps = [n // s for n, s in zip(n_keys, step)]
        for j in range(max(n_steps)):
            for b in both:
                if j >= n_steps[b]:
                    continue
                nxt = slc_scores(b, j + 1) if j + 1 < n_steps[b] else None
                bias = jnp.concatenate(
                    [bslc_ref[jnp.maximum(blk[b] - (j * step[b] + c) + 1, 0)] for c in range(step[b])], axis=0)
                km = jnp.concatenate([jnp.broadcast_to(drop[b][k:k + 1], (NSA_SLC_BLOCK, QB))
                                      for k in range(j * blocks_per_step[b], (j + 1) * blocks_per_step[b])], axis=0)
                s = cur[b] + bias + jnp.concatenate([km] * NSA_REPS, axis=1)
                states[b] = _flash_step_t(s, *states[b], vst_ref[:, j * step_keys[b]:(j + 1) * step_keys[b]])
                cur[b] = nxt

        for b in both:
            o_slc = _flash_finish(*states[b])
            gate = jax.nn.sigmoid(z_ref[pl.ds(r0[b], QB), _OFF_GATE:_OFF_GATE + QB]).T
            outs = []
            for r in range(NSA_REPS):
                qsl = slice(r * QB, (r + 1) * QB)
                outs.append(gate[3 * r:3 * r + 1] * o_cmp[b][:, qsl] + gate[3 * r + 1:3 * r + 2] * o_slc[:, qsl]
                            + gate[3 * r + 2:3 * r + 3] * o_win[b][:, qsl])
            o_ref[pl.ds(r0[b], QB), :] = jnp.concatenate(outs, axis=0).T.astype(o_ref.dtype)
        return carry

    pairs_per_tier = n_chunks // 2 // NSA_TIERS
    for tier in range(NSA_TIERS):
        lo, hi = tier * pairs_per_tier, (tier + 1) * pairs_per_tier
        lax.fori_loop(lo, hi, functools.partial(q_blocks, [hi, n_chunks - lo], [lo < NSA_WIN_PREV, False]), 0)


def _nsa(z_nsa, zc, w1a, w1b, w2, pos, gq, gk, ovt, bslc, bwin, batch, seq):
    gw = NSA_GROUP_COLS
    n_chunks = seq // QB
    return pl.pallas_call(
        functools.partial(_nsa_kernel, seq),
        grid=(batch, NSA_KV_HEADS),
        in_specs=[pl.BlockSpec((None, seq, gw), lambda b, g: (b, 0, g)),
                  pl.BlockSpec((None, seq, 2 * HEAD_DIM), lambda b, g: (b, 0, g)),
                  _resident(w1a.shape), _resident(w1b.shape), _resident(w2.shape), _resident(pos.shape),
                  _resident(gq.shape), _resident(gk.shape), _resident(ovt.shape),
                  pl.BlockSpec((None,) + bslc.shape[1:], lambda b, g: (g, 0, 0, 0)),
                  pl.BlockSpec((None,) + bwin.shape[1:], lambda b, g: (g, 0, 0))],
        out_specs=pl.BlockSpec((None, seq, _NQ), lambda b, g: (b, 0, g)),
        out_shape=jax.ShapeDtypeStruct((batch, seq, NSA_HEADS * HEAD_DIM), BF16),
        scratch_shapes=[pltpu.VMEM((seq, HEAD_DIM), BF16),
                        pltpu.VMEM((HEAD_DIM, seq), BF16),
                        pltpu.VMEM((NSA_WIN_PREV * QB + seq, HEAD_DIM), BF16),
                        pltpu.VMEM((NSA_WIN_PREV + n_chunks, HEAD_DIM, QB), BF16)],
        compiler_params=_cparams(2),
        name="nsa",
    )(z_nsa, zc, w1a, w1b, w2, pos, gq, gk, ovt, bslc, bwin)


def _t5_bucket(dist):
    n = jnp.maximum(dist, 0)
    max_exact = NUM_BUCKETS // 2
    nf = jnp.maximum(n, 1).astype(F32)
    large = max_exact + (jnp.log(nf / max_exact) / math.log(MAX_DISTANCE / max_exact)
                         * (NUM_BUCKETS - max_exact)).astype(jnp.int32)
    large = jnp.minimum(large, NUM_BUCKETS - 1)
    return jnp.where(n < max_exact, n, large)


def _toeplitz(u, rows, cols):
    lead = u.shape[:-1]
    lu = rows + cols - 1
    assert u.shape[-1] == lu
    padded = jnp.pad(u, [(0, 0)] * len(lead) + [(0, 1)])
    flat = jnp.broadcast_to(padded[..., None, :], lead + (rows, lu + 1)).reshape(lead + (rows * (lu + 1),))
    return flat[..., :rows * lu].reshape(lead + (rows, lu))[..., rows - 1:]


def _bias_by_distance(bias_cols, delta, valid, dist_scale=1):
    bucket = _t5_bucket(jnp.asarray(np.maximum(delta, 0) * dist_scale, dtype=jnp.int32))
    return jnp.where(jnp.asarray(valid)[None, :], bias_cols.astype(F32)[bucket].T * LOG2E, NEG)


def _band_bias(bias_cols, window, n_prev, dist_scale, n_kv, reps):
    kw = (n_prev + 1) * QB
    delta = n_prev * QB + QB - 1 - np.arange(kw + QB - 1)
    u = _bias_by_distance(bias_cols, delta, (delta >= 0) & (delta < window), dist_scale)
    return _toeplitz(u, QB, kw).reshape(n_kv, reps * QB, kw)


def _slc_bias_t(bias_cols, seq):
    n_chunks = seq // QB
    delta = seq - 1 - np.arange(seq + QB - 1)
    strip = _toeplitz(_bias_by_distance(bias_cols, delta, delta >= 0), QB, seq)
    tile = jnp.flip(strip.reshape(NSA_KV_HEADS, NSA_REPS, QB, n_chunks, QB), axis=3)
    tile = jnp.transpose(tile, (0, 3, 4, 1, 2)).reshape(NSA_KV_HEADS, n_chunks, QB, NSA_REPS * QB)
    return jnp.concatenate([jnp.full_like(tile[:, :1], NEG), tile], axis=1)


def _rope_tables(seq):
    half = MLA_ROPE // 2
    inv = jnp.power(ROPE_THETA, -jnp.arange(half, dtype=F32) / half)
    ang = jnp.arange(seq, dtype=F32)[:, None] * inv[None, :]
    cos, sin = jnp.cos(ang), jnp.sin(ang)
    zeros = lambda w: jnp.zeros((seq, w), F32)
    tail = MLA_PAD - MLA_QK_DIM
    cos_t = jnp.concatenate([jnp.ones((seq, MLA_NOPE), F32), cos, cos, zeros(tail)], axis=1)
    sin_t = jnp.concatenate([zeros(MLA_NOPE), -sin, sin, zeros(tail)], axis=1)
    return cos_t, sin_t


def _pad_cols(w, width):
    return jnp.pad(w, ((0, 0), (0, width - w.shape[1])))


def _with_swapped_rope(a):
    half = MLA_ROPE // 2
    return jnp.concatenate([a, a[..., MLA_NOPE + half:], a[..., MLA_NOPE:MLA_NOPE + half]], axis=-1)


def _nsa_column_order():
    g_cols = NSA_KV_HEADS * HEAD_DIM
    q_cols = NSA_HEADS * HEAD_DIM
    order = []
    for g in range(NSA_KV_HEADS):
        cols = list(range(g * _NQ, (g + 1) * _NQ))
        for part in range(2, 6):
            start = q_cols + part * g_cols + g * HEAD_DIM
            cols += list(range(start, start + HEAD_DIM))
        gate0 = q_cols + 6 * g_cols + g * NSA_REPS * 3
        cols += list(range(gate0, gate0 + NSA_REPS * 3))
        order.append(cols)
    return order


def _nsa_cmp_columns():
    q_cols = NSA_HEADS * HEAD_DIM
    g_cols = NSA_KV_HEADS * HEAD_DIM
    cols = []
    for g in range(NSA_KV_HEADS):
        for part in range(2):
            start = q_cols + part * g_cols + g * HEAD_DIM
            cols += list(range(start, start + HEAD_DIM))
    return cols


def _nsa_mixer(z_nsa3, z_cmp, rel_bias, q_norm, k_norm, cmp_pos, cmp_w1, cmp_w2, batch, seq):
    w1 = cmp_w1.reshape(2, 2, NSA_CMP_STRIDE, HEAD_DIM, NSA_CMP_HIDDEN)
    zero1 = jnp.zeros_like(w1[0])
    w1_bd = jnp.concatenate([jnp.concatenate([w1[0], zero1], axis=-1),
                             jnp.concatenate([zero1, w1[1]], axis=-1)], axis=-2)
    zero2 = jnp.zeros_like(cmp_w2[0])
    w2_bd = jnp.concatenate([jnp.concatenate([cmp_w2[0], zero2], axis=-1),
                             jnp.concatenate([zero2, cmp_w2[1]], axis=-1)], axis=-2)
    pos_kv = jnp.concatenate([cmp_pos[0], cmp_pos[1]], axis=-1)
    n_cmp = (seq - NSA_CMP_LEN) // NSA_CMP_STRIDE + 1
    ci = np.arange(NSA_N_CMP_PAD)[:, None] * NSA_CMP_STRIDE
    sj = np.arange(NSA_N_SLC)[None, :] * NSA_SLC_BLOCK
    overlap = ((ci < sj + NSA_SLC_BLOCK) & (ci + NSA_CMP_LEN > sj) & (np.arange(NSA_N_CMP_PAD)[:, None] < n_cmp))
    ovt = jnp.asarray(overlap.T.astype(np.float32))
    nsa_cols = rel_bias[:, NSA_BIAS_COL0:NSA_BIAS_COL0 + NSA_HEADS]
    bslc = _slc_bias_t(nsa_cols, seq)
    bwin = jnp.transpose(_band_bias(nsa_cols, NSA_WINDOW, NSA_WIN_PREV, 1, NSA_KV_HEADS, NSA_REPS), (0, 2, 1))
    return _nsa(z_nsa3, z_cmp.reshape(batch, seq, -1), w1_bd[0].astype(BF16), w1_bd[1].astype(BF16),
                w2_bd.astype(BF16), pos_kv, q_norm.reshape(HEAD_DIM, 1), k_norm.reshape(3, 1, HEAD_DIM),
                ovt, bslc, bwin, batch, seq)


def kernel(x, rel_bias, ffn1_norm, ffn1_w_gate, ffn1_w_up, ffn1_w_down, mix_norm, ffn2_norm, ffn2_w_gate,
           ffn2_w_up, ffn2_w_down, ab_w_in, mla_q_a_norm, mla_w_q_b, mla_kv_a_norm, mla_w_kv_b, mla_q_norm,
           mla_k_norm, dil_q_norm, dil_k_norm, ab_w_out, cd_w_in, swa_q_norm, swa_k_norm, swa_sinks,
           nsa_q_norm, nsa_k_norm, nsa_cmp_pos, nsa_cmp_w1, nsa_cmp_w2, cd_w_out):
    batch, seq, _ = x.shape
    n = batch * seq
    assert seq % (16 * QB) == 0 and n % TM == 0 and seq % TM == 0
    bf = lambda a: a.astype(BF16)
    xf = x.reshape(n, D_MODEL)
    ffn1 = (bf(ffn1_w_gate), bf(ffn1_w_up), bf(ffn1_w_down))
    ffn2 = (bf(ffn2_w_gate), bf(ffn2_w_up), bf(ffn2_w_down))

    w_in = ab_w_in[0]
    mla_cols = MLA_Q_RANK + MLA_KV_RANK
    w_krope = _with_swapped_rope(jnp.pad(w_in[:, mla_cols:mla_cols + MLA_ROPE], ((0, 0), (MLA_NOPE, 0))))
    w_mla = jnp.concatenate([w_in[:, :mla_cols], w_krope], axis=1)
    xf, z_mla, z_dil = _ffn_proj(xf, ffn1_norm[0], ffn1, 0, mix_norm[0],
                                 [bf(w_mla), bf(w_in[:, mla_cols + MLA_ROPE:])])

    wq = _with_swapped_rope(mla_w_q_b[0].reshape(MLA_Q_RANK, MLA_HEADS, MLA_QK_DIM))
    wq = wq.reshape(MLA_Q_RANK, MLA_HEADS * MLA_PAD)
    wkv = mla_w_kv_b[0].reshape(MLA_KV_RANK, MLA_HEADS, MLA_NOPE + MLA_V)
    wk = _pad_cols(wkv[:, :, :MLA_NOPE].reshape(MLA_KV_RANK * MLA_HEADS, MLA_NOPE), MLA_PAD)
    wk = wk.reshape(MLA_KV_RANK, MLA_HEADS * MLA_PAD)
    wv = wkv[:, :, MLA_NOPE:].reshape(MLA_KV_RANK, MLA_HEADS * MLA_V)
    cos_t, sin_t = _rope_tables(seq)
    q_scale = MLA_QK_DIM ** -0.5 * LOG2E
    q_mla, k_mla, vt_mla = _mla_prep(
        z_mla, mla_q_a_norm[0].reshape(1, -1), mla_kv_a_norm[0].reshape(1, -1), bf(wq), bf(wk), bf(wv),
        _with_swapped_rope(mla_q_norm[0].reshape(1, -1)), _with_swapped_rope(mla_k_norm[0].reshape(1, -1)),
        cos_t * q_scale, sin_t * q_scale, cos_t, sin_t, seq)
    o_a = _mla_attn(q_mla.reshape(batch, seq, -1), k_mla.reshape(batch, seq, -1),
                    vt_mla.reshape(batch, seq // MLA_TQ, MLA_HEADS * MLA_V, MLA_TQ),
                    batch, seq).reshape(n, MLA_HEADS * MLA_V)

    z_dil3 = z_dil.reshape(batch, seq, -1)
    gq, gk = dil_q_norm[0], dil_k_norm[0]
    dil_outs, dil_lses = [], []
    for grp, (window, dilation) in enumerate(DIL_PATTERNS):
        bm = _band_bias(rel_bias[:, grp * DIL_HPG:(grp + 1) * DIL_HPG], window // dilation + 1, 1, dilation,
                        DIL_HPG, 1)
        o, lse = _dilated_group(z_dil3, batch, seq, grp, dilation, gq, gk, bm)
        dil_outs.append(o)
        dil_lses.append(lse)
    w_out = ab_w_out[0]
    xf = _out_ab_ffn(xf, o_a, dil_outs, dil_lses, bf(w_out[:MLA_HEADS * MLA_V]), bf(w_out[MLA_HEADS * MLA_V:]),
                     ffn2_norm[0], ffn2, 0)

    w_in = cd_w_in[0]
    swa_cols = (SWA_HEADS + 2 * SWA_KV_HEADS) * HEAD_DIM
    w_nsa_src = w_in[:, swa_cols:]
    w_nsa = jnp.concatenate([_pad_cols(w_nsa_src[:, np.asarray(cols)], NSA_GROUP_COLS)
                             for cols in _nsa_column_order()], axis=1)
    w_cmp = w_nsa_src[:, np.asarray(_nsa_cmp_columns())]
    xf, z_swa, z_nsa, z_cmp = _ffn_proj(xf, ffn1_norm[1], ffn1, 1, mix_norm[1],
                                        [bf(w_in[:, :swa_cols]), bf(w_nsa), bf(w_cmp)])

    swa_reps = SWA_HEADS // SWA_KV_HEADS
    swa_prev = -(-(SWA_WINDOW - 1) // QB)
    bm_swa = _band_bias(rel_bias[:, :SWA_HEADS], SWA_WINDOW, swa_prev, 1, SWA_KV_HEADS, swa_reps)
    sinks = jnp.broadcast_to((swa_sinks[0].astype(F32) * LOG2E).reshape(SWA_KV_HEADS, 1, swa_reps, 1),
                             (SWA_KV_HEADS, 1, swa_reps, QB)).reshape(SWA_KV_HEADS, 1, swa_reps * QB)
    o_c = _swa(z_swa.reshape(batch, seq, -1), batch, seq, swa_q_norm[0].reshape(HEAD_DIM, 1),
               swa_k_norm[0].reshape(1, HEAD_DIM), bm_swa, sinks)

    o_d = _nsa_mixer(z_nsa.reshape(batch, seq, NSA_KV_HEADS * NSA_GROUP_COLS), z_cmp, rel_bias, nsa_q_norm[0],
                     nsa_k_norm[0], nsa_cmp_pos[0], nsa_cmp_w1[0], nsa_cmp_w2[0], batch, seq)

    w_out = cd_w_out[0]
    xf = _out_cd_ffn(xf, o_c.reshape(n, -1), o_d.reshape(n, -1), bf(w_out[:SWA_HEADS * HEAD_DIM]),
                     bf(w_out[SWA_HEADS * HEAD_DIM:]), ffn2_norm[1], ffn2, 1)
    return xf.reshape(batch, seq, D_MODEL)
```

```python
import functools
import math

import numpy as np
import jax
import jax.numpy as jnp
from jax import lax
from jax.experimental import pallas as pl
from jax.experimental.pallas import tpu as pltpu

F32 = jnp.float32
BF16 = jnp.bfloat16

EPS = 1e-6
NEG = -1e30
TINY = 1e-30
LOG2E = math.log2(math.e)
LN2 = math.log(2.0)
D_MODEL = 1024
D_FF = 2816
NUM_BUCKETS = 32
MAX_DISTANCE = 2048
HEAD_DIM = 64
QB = 128

MLA_HEADS = 8
MLA_Q_RANK = 256
MLA_KV_RANK = 128
MLA_NOPE = 64
MLA_ROPE = 32
MLA_V = 64
MLA_QK_DIM = MLA_NOPE + MLA_ROPE
MLA_PAD = 128
ROPE_THETA = 10000.0

DIL_PATTERNS = ((128, 1), (512, 4), (2048, 16))
DIL_HPG = 4
SWA_HEADS = 8
SWA_KV_HEADS = 2
SWA_WINDOW = 128
NSA_HEADS = 8
NSA_KV_HEADS = 2
NSA_CMP_LEN = 32
NSA_CMP_STRIDE = 16
NSA_CMP_HIDDEN = 128
NSA_SLC_BLOCK = 64
NSA_TOP_N = 16
NSA_WINDOW = 512
NSA_FORCE = 1e6
NSA_BIAS_COL0 = 8
NSA_GROUP_COLS = 640

VMEM_LIMIT = 56 * 1024 * 1024
TM = 512

_NT = (((1,), (1,)), ((), ()))
_HI = lax.Precision.HIGHEST


def _cparams(n_axes):
    return pltpu.CompilerParams(dimension_semantics=("arbitrary",) * n_axes,
                                vmem_limit_bytes=VMEM_LIMIT)


def _resident(shape):
    nd = len(shape)
    return pl.BlockSpec(shape, lambda *_: (0,) * nd, pipeline_mode=pl.Buffered(1))


def _layer_block(w, layer):
    nd = w.ndim - 1
    return pl.BlockSpec((None,) + w.shape[1:], lambda *_: (layer,) + (0,) * nd, pipeline_mode=pl.Buffered(1))


def _rms(x):
    return x * lax.rsqrt(jnp.mean(x * x, axis=-1, keepdims=True) + EPS)


def _dot(a, b):
    return jnp.dot(a, b, preferred_element_type=F32)


def _dot_nt(a, b, precision=None):
    return lax.dot_general(a, b, _NT, preferred_element_type=F32, precision=precision)


FF_CHUNK = 256


N_CHUNK = 512


def _swiglu_half_step(x, g_ref, wg_ref, wu_ref, wd_ref):
    hb = (_rms(x) * g_ref[...]).astype(BF16)
    acc = jnp.zeros(x.shape, F32)
    for c in range(D_FF // FF_CHUNK):
        sl = slice(c * FF_CHUNK, (c + 1) * FF_CHUNK)
        gate = _dot(hb, wg_ref[:, sl])
        up = _dot(hb, wu_ref[:, sl])
        act = (gate * jax.nn.sigmoid(gate) * up).astype(BF16)
        acc = acc + _dot(act, wd_ref[sl, :])
    return x + 0.5 * acc


def _ffn_proj_kernel(n_out, x_ref, g_ref, wg_ref, wu_ref, wd_ref, gm_ref, *refs):
    w_refs, x_out_ref, z_refs = refs[:n_out], refs[n_out], refs[n_out + 1:]
    x = _swiglu_half_step(x_ref[...], g_ref, wg_ref, wu_ref, wd_ref)
    x_out_ref[...] = x
    hb = (_rms(x) * gm_ref[...]).astype(BF16)
    for w_ref, z_ref in zip(w_refs, z_refs):
        width = w_ref.shape[1]
        for c0 in range(0, width, N_CHUNK):
            sl = slice(c0, min(c0 + N_CHUNK, width))
            z_ref[:, sl] = _dot(hb, w_ref[:, sl]).astype(z_ref.dtype)


def _tile(width):
    return pl.BlockSpec((TM, width), lambda i: (i, 0))


def _ffn_proj(x, g, ffn, layer, g_mix, ws, z_dtypes=None):
    n = x.shape[0]
    z_dtypes = z_dtypes or [F32] * len(ws)
    ffn_w = [_layer_block(w, layer) for w in ffn]
    return pl.pallas_call(
        functools.partial(_ffn_proj_kernel, len(ws)),
        grid=(n // TM,),
        in_specs=[_tile(D_MODEL), _resident((1, D_MODEL))] + ffn_w + [_resident((1, D_MODEL))]
        + [_resident(w.shape) for w in ws],
        out_specs=[_tile(D_MODEL)] + [_tile(w.shape[1]) for w in ws],
        out_shape=[jax.ShapeDtypeStruct((n, D_MODEL), F32)]
        + [jax.ShapeDtypeStruct((n, w.shape[1]), dt) for w, dt in zip(ws, z_dtypes)],
        compiler_params=_cparams(1),
        name="ffn_proj",
    )(x, g.reshape(1, D_MODEL), *ffn, g_mix.reshape(1, D_MODEL), *ws)


def _out_ab_ffn_kernel(x_ref, oa_ref, o0_ref, o1_ref, o2_ref, l0_ref, l1_ref, l2_ref, wa_ref, wb_ref,
                       g_ref, wg_ref, wu_ref, wd_ref, out_ref):
    l0, l1, l2 = l0_ref[...], l1_ref[...], l2_ref[...]
    m = jnp.maximum(jnp.maximum(l0, l1), l2)
    e0, e1, e2 = jnp.exp(l0 - m), jnp.exp(l1 - m), jnp.exp(l2 - m)
    ob = (e0 * o0_ref[...] + e1 * o1_ref[...] + e2 * o2_ref[...]) / (e0 + e1 + e2)
    x = x_ref[...] + _dot(oa_ref[...], wa_ref[...]) + _dot(ob.astype(BF16), wb_ref[...])
    out_ref[...] = _swiglu_half_step(x, g_ref, wg_ref, wu_ref, wd_ref)


def _out_ab_ffn(x, oa, dil_outs, dil_lses, wa, wb, g, ffn, layer):
    n = x.shape[0]
    dw = DIL_HPG * HEAD_DIM
    ffn_w = [_layer_block(w, layer) for w in ffn]
    return pl.pallas_call(
        _out_ab_ffn_kernel,
        grid=(n // TM,),
        in_specs=[_tile(D_MODEL), _tile(oa.shape[1])] + [_tile(dw)] * 6 + [_resident(wa.shape), _resident(wb.shape),
                                                                        _resident((1, D_MODEL))] + ffn_w,
        out_specs=_tile(D_MODEL),
        out_shape=jax.ShapeDtypeStruct((n, D_MODEL), F32),
        compiler_params=_cparams(1),
        name="out_ab_ffn",
    )(x, oa, *dil_outs, *dil_lses, wa, wb, g.reshape(1, D_MODEL), *ffn)


def _out_cd_ffn_kernel(x_ref, oc_ref, od_ref, wc_ref, wdd_ref, g_ref, wg_ref, wu_ref, wd_ref, out_ref):
    x = x_ref[...] + _dot(oc_ref[...], wc_ref[...]) + _dot(od_ref[...], wdd_ref[...])
    out_ref[...] = _swiglu_half_step(x, g_ref, wg_ref, wu_ref, wd_ref)


def _out_cd_ffn(x, oc, od, wc, wdd, g, ffn, layer):
    n = x.shape[0]
    ffn_w = [_layer_block(w, layer) for w in ffn]
    return pl.pallas_call(
        _out_cd_ffn_kernel,
        grid=(n // TM,),
        in_specs=[_tile(D_MODEL), _tile(oc.shape[1]), _tile(od.shape[1]), _resident(wc.shape), _resident(wdd.shape),
                  _resident((1, D_MODEL))] + ffn_w,
        out_specs=_tile(D_MODEL),
        out_shape=jax.ShapeDtypeStruct((n, D_MODEL), F32),
        compiler_params=_cparams(1),
        name="out_cd_ffn",
    )(x, oc, od, wc, wdd, g.reshape(1, D_MODEL), *ffn)


MLA_TQ = 256
MLA_STEP = 2


def _ffn_proj_mla_kernel(x_ref, g_ref, wg_ref, wu_ref, wd_ref, gm_ref, wmla_ref, wdil_ref,
                         qan_ref, kvan_ref, wq_ref, wk_ref, wv_ref, gq_ref, gk_ref,
                         qc_ref, qs_ref, kc_ref, ks_ref,
                         x_out_ref, zdil_ref, q_ref, k_ref, vt_ref, zprev_ref):
    @pl.when(pl.program_id(0) == 0)
    def _():
        zprev_ref[...] = jnp.zeros(zprev_ref.shape, F32)

    cq = (_rms(zprev_ref[:, :MLA_Q_RANK]) * qan_ref[...]).astype(BF16)
    ckv = (_rms(zprev_ref[:, MLA_Q_RANK:MLA_Q_RANK + MLA_KV_RANK]) * kvan_ref[...]).astype(BF16)
    kr = zprev_ref[:, MLA_Q_RANK + MLA_KV_RANK:]
    real = jnp.where(lax.broadcasted_iota(jnp.int32, (1, MLA_PAD), 1) < MLA_QK_DIM, 1.0, 0.0)

    def head_norm_rope(x, gain, cos, sin):
        ssq = jnp.sum(x * x * real, axis=-1, keepdims=True)
        x = x * lax.rsqrt(ssq * (1.0 / MLA_QK_DIM) + EPS) * gain
        return x * cos + pltpu.roll(x, MLA_PAD - MLA_ROPE, 1) * sin

    def head(h):
        sl = slice(h * MLA_PAD, (h + 1) * MLA_PAD)
        qh = head_norm_rope(_dot(cq, wq_ref[:, sl]), gq_ref[...], qc_ref[...], qs_ref[...])
        q_ref[:, sl] = qh.astype(BF16)
        kh = head_norm_rope(_dot(ckv, wk_ref[:, sl]) + kr, gk_ref[...], kc_ref[...], ks_ref[...])
        k_ref[:, sl] = kh.astype(BF16)

    def values():
        v = _dot(ckv, wv_ref[...])
        for c in range(TM // MLA_TQ):
            vt_ref[c] = v[c * MLA_TQ:(c + 1) * MLA_TQ].T.astype(BF16)

    latent_work = [functools.partial(head, h) for h in range(MLA_HEADS)] + [values]

    x = x_ref[...]
    hb = (_rms(x) * g_ref[...]).astype(BF16)
    acc = jnp.zeros(x.shape, F32)
    for c in range(D_FF // FF_CHUNK):
        sl = slice(c * FF_CHUNK, (c + 1) * FF_CHUNK)
        gate = _dot(hb, wg_ref[:, sl])
        up = _dot(hb, wu_ref[:, sl])
        act = (gate * jax.nn.sigmoid(gate) * up).astype(BF16)
        acc = acc + _dot(act, wd_ref[sl, :])
        if c < len(latent_work):
            latent_work[c]()
    x = x + 0.5 * acc
    x_out_ref[...] = x
    hm = (_rms(x) * gm_ref[...]).astype(BF16)
    zprev_ref[...] = _dot(hm, wmla_ref[...])
    width = wdil_ref.shape[1]
    for c0 in range(0, width, N_CHUNK):
        sl = slice(c0, min(c0 + N_CHUNK, width))
        zdil_ref[:, sl] = _dot(hm, wdil_ref[:, sl])


def _ffn_proj_mla(x, g, ffn, layer, g_mix, w_mla, w_dil, qan, kvan, wq, wk, wv, gq, gk,
                  q_cos, q_sin, k_cos, k_sin, seq):
    n = x.shape[0]
    n_tiles = n // TM
    assert D_FF // FF_CHUNK >= MLA_HEADS + 1
    per_seq = seq // TM
    hw = MLA_HEADS * MLA_PAD
    vw = MLA_HEADS * MLA_V

    def cur(width):
        return pl.BlockSpec((TM, width), lambda i: (jnp.minimum(i, n_tiles - 1), 0))

    def prev(width):
        return pl.BlockSpec((TM, width), lambda i: (jnp.maximum(i - 1, 0), 0))

    rope_spec = pl.BlockSpec((TM, MLA_PAD), lambda i: (jnp.maximum(i - 1, 0) % per_seq, 0))
    small = [qan, kvan, wq, wk, wv, gq, gk]
    return pl.pallas_call(
        _ffn_proj_mla_kernel,
        grid=(n_tiles + 1,),
        in_specs=[cur(D_MODEL), _resident((1, D_MODEL))] + [_layer_block(w, layer) for w in ffn]
        + [_resident((1, D_MODEL)), _resident(w_mla.shape), _resident(w_dil.shape)]
        + [_resident(a.shape) for a in small] + [rope_spec] * 4,
        out_specs=[cur(D_MODEL), cur(w_dil.shape[1]), prev(hw), prev(hw),
                   pl.BlockSpec((TM // MLA_TQ, vw, MLA_TQ), lambda i: (jnp.maximum(i - 1, 0), 0, 0))],
        out_shape=[jax.ShapeDtypeStruct((n, D_MODEL), F32), jax.ShapeDtypeStruct((n, w_dil.shape[1]), F32),
                   jax.ShapeDtypeStruct((n, hw), BF16), jax.ShapeDtypeStruct((n, hw), BF16),
                   jax.ShapeDtypeStruct((n // MLA_TQ, vw, MLA_TQ), BF16)],
        scratch_shapes=[pltpu.VMEM((TM, w_mla.shape[1]), F32)],
        compiler_params=_cparams(1),
        name="ffn_proj_mla",
    )(x, g.reshape(1, D_MODEL), *ffn, g_mix.reshape(1, D_MODEL), w_mla, w_dil, *small,
      q_cos, q_sin, k_cos, k_sin)


def _flash_step_t(s, m, l, acc, pv, vt):
    m_new = jnp.maximum(m, jnp.max(s, axis=0, keepdims=True))
    p = jnp.exp2(s - m_new)
    alpha = jnp.exp2(m - m_new)
    l = alpha * l + jnp.sum(p, axis=0, keepdims=True)
    acc = alpha * (acc + pv)
    return m_new, l, acc, _dot(vt, p.astype(BF16))


def _flash_init(dv, n_queries):
    return (jnp.full((1, n_queries), NEG, F32), jnp.zeros((1, n_queries), F32), jnp.zeros((dv, n_queries), F32),
            jnp.zeros((dv, n_queries), F32))


def _flash_finish(m, l, acc, pv):
    return (acc + pv) * (1.0 / l)


def _mla_attn_kernel(seq, q_ref, k_ref, vt_ref, o_ref):
    t = MLA_TQ
    key = lax.broadcasted_iota(jnp.int32, (t, t), 0)
    qry = lax.broadcasted_iota(jnp.int32, (t, t), 1)
    diag_ok = key <= qry
    heads = [(slice(hh * MLA_PAD, (hh + 1) * MLA_PAD), slice(hh * MLA_V, (hh + 1) * MLA_V)) for hh in range(2)]

    init = _flash_init(MLA_V, t)
    n_blocks = seq // t
    for first in range(n_blocks // 2):
        blocks = [first, n_blocks - 1 - first]
        chains = [(i, h) for i in blocks for h in range(2)]
        qs = {(i, h): q_ref[i * t:(i + 1) * t, heads[h][0]] for i, h in chains}
        steps = {i: [(j, min(j + MLA_STEP, i + 1)) for j in range(0, i + 1, MLA_STEP)] for i in blocks}

        def scores(chain, step):
            i, h = chain
            lo, hi = steps[i][step]
            s = _dot_nt(k_ref[lo * t:hi * t, heads[h][0]], qs[chain])
            if hi == i + 1:
                last = jnp.where(diag_ok, s[(i - lo) * t:], NEG)
                s = last if i == lo else jnp.concatenate([s[:(i - lo) * t], last], axis=0)
            return s

        def values(chain, step):
            i, h = chain
            lo, hi = steps[i][step]
            return jnp.concatenate([vt_ref[j, heads[h][1], :] for j in range(lo, hi)], axis=1)

        cur = {c: scores(c, 0) for c in chains}
        states = {c: init for c in chains}
        for step in range(max(len(s) for s in steps.values())):
            for c in chains:
                n_steps = len(steps[c[0]])
                if step >= n_steps:
                    continue
                nxt = scores(c, step + 1) if step + 1 < n_steps else None
                states[c] = _flash_step_t(cur[c], *states[c], values(c, step))
                cur[c] = nxt
        for i in blocks:
            outs = [_flash_finish(*states[(i, h)]) for h in range(2)]
            o_ref[i * t:(i + 1) * t, :] = jnp.concatenate(outs, axis=0).T.astype(o_ref.dtype)


def _mla_attn(q, k, vt, batch, seq):
    return pl.pallas_call(
        functools.partial(_mla_attn_kernel, seq),
        grid=(batch, MLA_HEADS // 2),
        in_specs=[pl.BlockSpec((None, seq, 2 * MLA_PAD), lambda b, h: (b, 0, h)),
                  pl.BlockSpec((None, seq, 2 * MLA_PAD), lambda b, h: (b, 0, h)),
                  pl.BlockSpec((None, seq // MLA_TQ, 2 * MLA_V, MLA_TQ), lambda b, h: (b, 0, h, 0))],
        out_specs=pl.BlockSpec((None, seq, 2 * MLA_V), lambda b, h: (b, 0, h)),
        out_shape=jax.ShapeDtypeStruct((batch, seq, MLA_HEADS * MLA_V), BF16),
        compiler_params=_cparams(2),
        name="mla_attn",
    )(q, k, vt)


def _head_norm(x, gain):
    return _rms(x) * gain


def _heads_t(qt, n_heads, gain):
    tiles = []
    for c0 in range(0, n_heads * HEAD_DIM, 2 * HEAD_DIM):
        pair = qt[:, c0:c0 + 2 * HEAD_DIM].T
        for h in range(2):
            x = pair[h * HEAD_DIM:(h + 1) * HEAD_DIM]
            tiles.append(x * lax.rsqrt(jnp.mean(x * x, axis=0, keepdims=True) + EPS) * gain)
    return tiles


def _q_gain(gq_ref):
    return jnp.broadcast_to(gq_ref[...] * (HEAD_DIM ** -0.5 * LOG2E), (HEAD_DIM, QB))


def _pair_ones():
    shape = (2 * HEAD_DIM, 2 * HEAD_DIM)
    same = lax.broadcasted_iota(jnp.int32, shape, 0) // HEAD_DIM == lax.broadcasted_iota(jnp.int32, shape, 1) // HEAD_DIM
    return jnp.where(same, 1.0, 0.0).astype(BF16)


def _pair_norm(x, seg, gain):
    sq = x * x
    hi = sq.astype(BF16)
    lo = (sq - hi.astype(F32)).astype(BF16)
    ssq = _dot(hi, seg) + _dot(lo, seg)
    return x * lax.rsqrt(ssq * (1.0 / HEAD_DIM) + EPS) * gain


BAND_KV = 2


def _banded_kernel(seq, dilation, reps, n_prev, with_sink, with_lse, group_size, paired, *refs):
    q_ref, k_ref, v_ref, gq_ref, gk_ref, bm_ref = refs[:6]
    refs = refs[6:]
    sink_ref = None
    if with_sink:
        sink_ref, refs = refs[0], refs[1:]
    o_ref, refs = refs[0], refs[1:]
    lse_ref = None
    if with_lse:
        lse_ref, refs = refs[0], refs[1:]
    kn_ref, vt_ref = refs
    length = seq // dilation
    n_blocks = length // QB
    pad = n_prev * QB
    kw = pad + QB
    win_row = lax.broadcasted_iota(jnp.int32, (kw, 1), 0)
    assert not paired or reps == 1
    if paired:
        gq, gk = gq_ref[...] * (HEAD_DIM ** -0.5 * LOG2E), gk_ref[...]
        seg = _pair_ones()
        lane = lax.broadcasted_iota(jnp.int32, (1, BAND_KV * HEAD_DIM), 1)
        head_lanes = [jnp.where(lane // HEAD_DIM == g, 1.0, 0.0) for g in range(BAND_KV)]
    else:
        gq, gk = _q_gain(gq_ref), gk_ref[...]

    def rows(start, size):
        return pl.ds(start, size) if dilation == 1 else pl.ds(start, size, stride=dilation)

    for res in range(dilation):
        kall, vall = k_ref[rows(res, length), :], v_ref[rows(res, length), :]
        if paired:
            kn_ref[res, :pad, :] = jnp.zeros((pad, BAND_KV * HEAD_DIM), BF16)
            kn_ref[res, pad:, :] = _pair_norm(kall, seg, gk).astype(BF16)
        else:
            for g in range(BAND_KV):
                kn_ref[res, g, :pad, :] = jnp.zeros((pad, HEAD_DIM), BF16)
                kn_ref[res, g, pad:, :] = _head_norm(kall[:, g * HEAD_DIM:(g + 1) * HEAD_DIM], gk).astype(BF16)
        for j in range(n_prev):
            vt_ref[res, j] = jnp.zeros((BAND_KV * HEAD_DIM, QB), BF16)
        for j in range(n_blocks):
            vt_ref[res, n_prev + j] = vall[j * QB:(j + 1) * QB].T.astype(BF16)

    def attend(res, i, g, s):
        bias = bm_ref[g]
        if i < n_prev:
            bias = jnp.where(win_row >= (n_prev - i) * QB, bias, NEG)
        s = s + bias
        m = jnp.max(s, axis=0, keepdims=True)
        if with_sink:
            m = jnp.maximum(m, sink_ref[g])
        p = jnp.exp2(s - m)
        l = jnp.sum(p, axis=0, keepdims=True)
        if with_sink:
            l = l + jnp.exp2(sink_ref[g] - m)
        p = p.astype(BF16)
        vl = slice(g * HEAD_DIM, (g + 1) * HEAD_DIM)
        o = _dot(vt_ref[res, i, vl, :], p[0:QB])
        for w in range(1, n_prev + 1):
            o = o + _dot(vt_ref[res, i + w, vl, :], p[w * QB:(w + 1) * QB])
        return o * (1.0 / l), m * LN2 + jnp.log(l)

    items = [(res, i) for res in range(dilation) for i in range(n_blocks)]
    for at in range(0, len(items), group_size):
        group = items[at:at + group_size]
        toks = [rows(res + i * QB * dilation, QB) for res, i in group]
        if paired:
            qns = [_pair_norm(q_ref[tok, :], seg, gq) for tok in toks]
            qss = [[(qn * head_lanes[g]).astype(BF16) for g in range(BAND_KV)] for qn in qns]
            scores = [[_dot_nt(kn_ref[res, i * QB:i * QB + kw, :], qs[g]) for g in range(BAND_KV)]
                      for (res, i), qs in zip(group, qss)]
        else:
            heads = [_heads_t(q_ref[tok, :], BAND_KV * reps, gq) for tok in toks]
            qss = [[jnp.concatenate(h[g * reps:(g + 1) * reps], axis=1).astype(BF16) for g in range(BAND_KV)]
                   for h in heads]
            scores = [[_dot(kn_ref[res, g, i * QB:i * QB + kw, :], qs[g]) for g in range(BAND_KV)]
                      for (res, i), qs in zip(group, qss)]
        for (res, i), tok, sc in zip(group, toks, scores):
            outs, lses = [], []
            for g in range(BAND_KV):
                o, lse = attend(res, i, g, sc[g])
                for r in range(reps):
                    outs.append(o[:, r * QB:(r + 1) * QB])
                    if with_lse:
                        lses.append(jnp.broadcast_to(lse[:, r * QB:(r + 1) * QB], (HEAD_DIM, QB)))
            o_ref[tok, :] = jnp.concatenate(outs, axis=0).T.astype(o_ref.dtype)
            if with_lse:
                lse_ref[tok, :] = jnp.concatenate(lses, axis=0).T


def _banded_scratch(seq, dilation, n_prev, paired):
    length = seq // dilation
    rows = n_prev * QB + length
    keys = (dilation, rows, BAND_KV * HEAD_DIM) if paired else (dilation, BAND_KV, rows, HEAD_DIM)
    return [pltpu.VMEM(keys, BF16),
            pltpu.VMEM((dilation, n_prev + length // QB, BAND_KV * HEAD_DIM, QB), BF16)]


PAIRED_MIN_DILATION = 16


def _dilated_group(z_dil, batch, seq, group, dilation, q_norm, k_norm, bm):
    gw = DIL_HPG * HEAD_DIM
    n_groups = len(DIL_PATTERNS)
    hps = BAND_KV
    halves = DIL_HPG // hps
    bm_t = jnp.transpose(bm, (0, 2, 1))
    paired = dilation >= PAIRED_MIN_DILATION
    if paired:
        gq = jnp.tile(q_norm.reshape(1, HEAD_DIM), (1, BAND_KV))
        gk = jnp.tile(k_norm.reshape(1, HEAD_DIM), (1, BAND_KV))
    else:
        gq, gk = q_norm.reshape(HEAD_DIM, 1), k_norm.reshape(1, HEAD_DIM)

    def spec(part):
        return pl.BlockSpec((None, seq, hps * HEAD_DIM), lambda b, h: (b, 0, (part * n_groups + group) * halves + h))

    out_spec = pl.BlockSpec((None, seq, hps * HEAD_DIM), lambda b, h: (b, 0, h))
    shape = jax.ShapeDtypeStruct((batch, seq, gw), F32)
    out, lse = pl.pallas_call(
        functools.partial(_banded_kernel, seq, dilation, 1, 1, False, True, 8, paired),
        grid=(batch, halves),
        in_specs=[spec(0), spec(1), spec(2), _resident(gq.shape), _resident(gk.shape),
                  pl.BlockSpec((hps,) + bm_t.shape[1:], lambda b, h: (h, 0, 0))],
        out_specs=[out_spec, out_spec],
        out_shape=[shape, shape],
        scratch_shapes=_banded_scratch(seq, dilation, 1, paired),
        compiler_params=_cparams(2),
        name="dilated_g%d" % group,
    )(z_dil, z_dil, z_dil, gq, gk, bm_t)
    return out.reshape(batch * seq, gw), lse.reshape(batch * seq, gw)


def _swa(z_swa, batch, seq, gq, gk, bm, sinks):
    qw = SWA_HEADS * HEAD_DIM
    kvw = SWA_KV_HEADS * HEAD_DIM
    reps = SWA_HEADS // SWA_KV_HEADS
    n_prev = -(-(SWA_WINDOW - 1) // QB)
    assert SWA_KV_HEADS == BAND_KV
    bm_t = jnp.transpose(bm, (0, 2, 1))
    return pl.pallas_call(
        functools.partial(_banded_kernel, seq, 1, reps, n_prev, True, False, 4, False),
        grid=(batch,),
        in_specs=[pl.BlockSpec((None, seq, qw), lambda b: (b, 0, 0)),
                  pl.BlockSpec((None, seq, kvw), lambda b: (b, 0, qw // kvw)),
                  pl.BlockSpec((None, seq, kvw), lambda b: (b, 0, qw // kvw + 1)),
                  _resident(gq.shape), _resident(gk.shape), _resident(bm_t.shape), _resident(sinks.shape)],
        out_specs=pl.BlockSpec((None, seq, qw), lambda b: (b, 0, 0)),
        out_shape=jax.ShapeDtypeStruct((batch, seq, qw), BF16),
        scratch_shapes=_banded_scratch(seq, 1, n_prev, False),
        compiler_params=_cparams(1),
        name="swa",
    )(z_swa, z_swa, z_swa, gq, gk, bm_t, sinks)


NSA_REPS = NSA_HEADS // NSA_KV_HEADS
NSA_N_SLC = 32
NSA_N_CMP_PAD = 128
NSA_WIN_PREV = -(-(NSA_WINDOW - 1) // QB)
_NQ = NSA_REPS * HEAD_DIM
_OFF_KS, _OFF_VS, _OFF_KW, _OFF_VW, _OFF_GATE = (_NQ + i * HEAD_DIM for i in range(5))


NSA_TIERS = 4
NSA_STEP = 16
NSA_QPAIR = 2


def _nsa_kernel(seq, z_ref, zc_ref, w1a_ref, w1b_ref, w2_ref, pos_ref, gq_ref, gk_ref, ovt_ref,
                bslc_ref, bwin_ref, o_ref, ks_ref, vst_ref, kw_ref, vwt_ref):
    n_chunks = seq // QB
    lanes = NSA_REPS * QB
    gq = _q_gain(gq_ref)
    win_pad = NSA_WIN_PREV * QB
    win_kw = win_pad + QB

    first = jnp.zeros((NSA_N_CMP_PAD, 2 * NSA_CMP_HIDDEN), F32)
    second = jnp.zeros((NSA_N_CMP_PAD, 2 * NSA_CMP_HIDDEN), F32)
    for l in range(NSA_CMP_STRIDE):
        tok = zc_ref[pl.ds(l, NSA_N_CMP_PAD, stride=NSA_CMP_STRIDE), :]
        first = first + _dot((tok + pos_ref[l:l + 1, :]).astype(BF16), w1a_ref[l])
        second = second + _dot((tok + pos_ref[NSA_CMP_STRIDE + l:NSA_CMP_STRIDE + l + 1, :]).astype(BF16),
                               w1b_ref[l])
    hidden = jax.nn.gelu(first + pltpu.roll(second, NSA_N_CMP_PAD - 1, 0)).astype(BF16)
    cmp_kv = _dot(hidden, w2_ref[...])
    kc = _head_norm(cmp_kv[:, :HEAD_DIM], gk_ref[0]).astype(BF16)
    vct = cmp_kv.T[HEAD_DIM:].astype(BF16)

    ks_ref[...] = _head_norm(z_ref[:, _OFF_KS:_OFF_KS + HEAD_DIM], gk_ref[1]).astype(BF16)
    kw_ref[:win_pad, :] = jnp.zeros((win_pad, HEAD_DIM), BF16)
    kw_ref[win_pad:, :] = _head_norm(z_ref[:, _OFF_KW:_OFF_KW + HEAD_DIM], gk_ref[2]).astype(BF16)
    for j in range(NSA_WIN_PREV):
        vwt_ref[j] = jnp.zeros((HEAD_DIM, QB), BF16)
    for j in range(n_chunks):
        rows = slice(j * QB, (j + 1) * QB)
        vst_ref[:, rows] = z_ref[rows, _OFF_KS:_OFF_KS + 2 * HEAD_DIM].T[HEAD_DIM:].astype(BF16)
        vwt_ref[NSA_WIN_PREV + j] = z_ref[rows, _OFF_KW:_OFF_KW + 2 * HEAD_DIM].T[HEAD_DIM:].astype(BF16)

    cmp_id = lax.broadcasted_iota(jnp.int32, (NSA_N_CMP_PAD, lanes), 0)
    cmp_end = cmp_id * NSA_CMP_STRIDE + (NSA_CMP_LEN - 1)
    cmp_real = cmp_id < NSA_N_CMP_PAD - 1
    q_in_blk = lax.broadcasted_iota(jnp.int32, (NSA_N_CMP_PAD, lanes), 1) & (QB - 1)
    blk_id = lax.broadcasted_iota(jnp.int32, (NSA_N_SLC, QB), 0)
    q_lane = lax.broadcasted_iota(jnp.int32, (NSA_N_SLC, QB), 1)
    win_row = lax.broadcasted_iota(jnp.int32, (win_kw, 1), 0)
    init = _flash_init(HEAD_DIM, lanes)
    both = range(NSA_QPAIR)

    def q_blocks(n_keys, mask_pad, pair, carry):
        step = [max(d for d in range(1, NSA_STEP + 1) if n % d == 0) for n in n_keys]
        step_keys = [s * QB for s in step]
        blocks_per_step = [s // NSA_SLC_BLOCK for s in step_keys]
        blk = [pair, n_chunks - 1 - pair]
        r0 = [pl.multiple_of(i * QB, QB) for i in blk]
        qs = [jnp.concatenate(_heads_t(z_ref[pl.ds(r, QB), :_NQ], NSA_REPS, gq), axis=1).astype(BF16)
              for r in r0]

        sc = [_dot(kc, q) for q in qs]
        s_win = [_dot(kw_ref[pl.ds(r, win_kw), :], q) for r, q in zip(r0, qs)]

        def slc_scores(b, j):
            return _dot(ks_ref[j * step_keys[b]:(j + 1) * step_keys[b], :], qs[b])

        cur = [slc_scores(b, 0) for b in both]

        o_cmp, imp = [], []
        for b in both:
            ok = (cmp_end <= q_in_blk + blk[b] * QB) & cmp_real
            s = jnp.where(ok, sc[b], NEG)
            e = jnp.where(ok, jnp.exp2(s - jnp.max(s, axis=0, keepdims=True)), 0.0)
            p = e * (1.0 / jnp.maximum(jnp.sum(e, axis=0, keepdims=True), TINY))
            o_cmp.append(_dot(vct, p.astype(BF16)))
            p_sum = p[:, 0:QB] + p[:, QB:2 * QB] + p[:, 2 * QB:3 * QB] + p[:, 3 * QB:4 * QB]
            imp.append(jnp.dot(ovt_ref[...], p_sum, precision=_HI, preferred_element_type=F32))

        o_win = []
        for b in both:
            bias = bwin_ref[...]
            if mask_pad[b]:
                bias = jnp.where(win_row >= (NSA_WIN_PREV - blk[b]) * QB, bias, NEG)
            s = s_win[b] + bias
            pw = jnp.exp2(s - jnp.max(s, axis=0, keepdims=True))
            lw = jnp.sum(pw, axis=0, keepdims=True)
            pw = pw.astype(BF16)
            o = _dot(vwt_ref[blk[b]], pw[0:QB])
            for c in range(1, NSA_WIN_PREV + 1):
                o = o + _dot(vwt_ref[blk[b] + c], pw[c * QB:(c + 1) * QB])
            o_win.append(o * (1.0 / lw))

        drop = []
        for b in both:
            tb = lax.shift_right_logical(q_lane + blk[b] * QB, 6)
            forced = (blk_id == 0) | (blk_id == tb) | (blk_id == tb - 1)
            score = jnp.where(blk_id <= tb, imp[b] + jnp.where(forced, NSA_FORCE, 0.0), -NSA_FORCE)
            rank = jnp.zeros((NSA_N_SLC, QB), F32)
            for other in range(NSA_N_SLC):
                s_o = score[other:other + 1, :]
                beats = (s_o > score) | ((s_o == score) & (blk_id > other))
                rank = rank + jnp.where(beats, 1.0, 0.0)
            drop.append(jnp.where(rank < NSA_TOP_N, 0.0, NEG))

        states = [init for _ in both]
        n_steps = [n // s for n, s in zip(n_keys, step)]
        for j in range(max(n_steps)):
            for b in both:
                if j >= n_steps[b]:
                    continue
                nxt = slc_scores(b, j + 1) if j + 1 < n_steps[b] else None
                bias = jnp.concatenate(
                    [bslc_ref[jnp.maximum(blk[b] - (j * step[b] + c) + 1, 0)] for c in range(step[b])], axis=0)
                km = jnp.concatenate([jnp.broadcast_to(drop[b][k:k + 1], (NSA_SLC_BLOCK, QB))
                                      for k in range(j * blocks_per_step[b], (j + 1) * blocks_per_step[b])], axis=0)
                s = cur[b] + bias + jnp.concatenate([km] * NSA_REPS, axis=1)
                states[b] = _flash_step_t(s, *states[b], vst_ref[:, j * step_keys[b]:(j + 1) * step_keys[b]])
                cur[b] = nxt

        for b in both:
            o_slc = _flash_finish(*states[b])
            gate = jax.nn.sigmoid(z_ref[pl.ds(r0[b], QB), _OFF_GATE:_OFF_GATE + QB]).T
            outs = []
            for r in range(NSA_REPS):
                qsl = slice(r * QB, (r + 1) * QB)
                outs.append(gate[3 * r:3 * r + 1] * o_cmp[b][:, qsl] + gate[3 * r + 1:3 * r + 2] * o_slc[:, qsl]
                            + gate[3 * r + 2:3 * r + 3] * o_win[b][:, qsl])
            o_ref[pl.ds(r0[b], QB), :] = jnp.concatenate(outs, axis=0).T.astype(o_ref.dtype)
        return carry

    pairs_per_tier = n_chunks // 2 // NSA_TIERS
    for tier in range(NSA_TIERS):
        lo, hi = tier * pairs_per_tier, (tier + 1) * pairs_per_tier
        lax.fori_loop(lo, hi, functools.partial(q_blocks, [hi, n_chunks - lo], [lo < NSA_WIN_PREV, False]), 0)


def _nsa(z_nsa, zc, w1a, w1b, w2, pos, gq, gk, ovt, bslc, bwin, batch, seq):
    gw = NSA_GROUP_COLS
    n_chunks = seq // QB
    return pl.pallas_call(
        functools.partial(_nsa_kernel, seq),
        grid=(batch, NSA_KV_HEADS),
        in_specs=[pl.BlockSpec((None, seq, gw), lambda b, g: (b, 0, g)),
                  pl.BlockSpec((None, seq, 2 * HEAD_DIM), lambda b, g: (b, 0, g)),
                  _resident(w1a.shape), _resident(w1b.shape), _resident(w2.shape), _resident(pos.shape),
                  _resident(gq.shape), _resident(gk.shape), _resident(ovt.shape),
                  pl.BlockSpec((None,) + bslc.shape[1:], lambda b, g: (g, 0, 0, 0)),
                  pl.BlockSpec((None,) + bwin.shape[1:], lambda b, g: (g, 0, 0))],
        out_specs=pl.BlockSpec((None, seq, _NQ), lambda b, g: (b, 0, g)),
        out_shape=jax.ShapeDtypeStruct((batch, seq, NSA_HEADS * HEAD_DIM), BF16),
        scratch_shapes=[pltpu.VMEM((seq, HEAD_DIM), BF16),
                        pltpu.VMEM((HEAD_DIM, seq), BF16),
                        pltpu.VMEM((NSA_WIN_PREV * QB + seq, HEAD_DIM), BF16),
                        pltpu.VMEM((NSA_WIN_PREV + n_chunks, HEAD_DIM, QB), BF16)],
        compiler_params=_cparams(2),
        name="nsa",
    )(z_nsa, zc, w1a, w1b, w2, pos, gq, gk, ovt, bslc, bwin)


def _t5_bucket(dist):
    n = jnp.maximum(dist, 0)
    max_exact = NUM_BUCKETS // 2
    nf = jnp.maximum(n, 1).astype(F32)
    large = max_exact + (jnp.log(nf / max_exact) / math.log(MAX_DISTANCE / max_exact)
                         * (NUM_BUCKETS - max_exact)).astype(jnp.int32)
    large = jnp.minimum(large, NUM_BUCKETS - 1)
    return jnp.where(n < max_exact, n, large)


def _toeplitz(u, rows, cols):
    lead = u.shape[:-1]
    lu = rows + cols - 1
    assert u.shape[-1] == lu
    padded = jnp.pad(u, [(0, 0)] * len(lead) + [(0, 1)])
    flat = jnp.broadcast_to(padded[..., None, :], lead + (rows, lu + 1)).reshape(lead + (rows * (lu + 1),))
    return flat[..., :rows * lu].reshape(lead + (rows, lu))[..., rows - 1:]


def _bias_by_distance(bias_cols, delta, valid, dist_scale=1):
    bucket = _t5_bucket(jnp.asarray(np.maximum(delta, 0) * dist_scale, dtype=jnp.int32))
    return jnp.where(jnp.asarray(valid)[None, :], bias_cols.astype(F32)[bucket].T * LOG2E, NEG)


def _band_bias(bias_cols, window, n_prev, dist_scale, n_kv, reps):
    kw = (n_prev + 1) * QB
    delta = n_prev * QB + QB - 1 - np.arange(kw + QB - 1)
    u = _bias_by_distance(bias_cols, delta, (delta >= 0) & (delta < window), dist_scale)
    return _toeplitz(u, QB, kw).reshape(n_kv, reps * QB, kw)


def _slc_bias_t(bias_cols, seq):
    n_chunks = seq // QB
    delta = seq - 1 - np.arange(seq + QB - 1)
    strip = _toeplitz(_bias_by_distance(bias_cols, delta, delta >= 0), QB, seq)
    tile = jnp.flip(strip.reshape(NSA_KV_HEADS, NSA_REPS, QB, n_chunks, QB), axis=3)
    tile = jnp.transpose(tile, (0, 3, 4, 1, 2)).reshape(NSA_KV_HEADS, n_chunks, QB, NSA_REPS * QB)
    return jnp.concatenate([jnp.full_like(tile[:, :1], NEG), tile], axis=1)


def _rope_tables(seq):
    half = MLA_ROPE // 2
    inv = jnp.power(ROPE_THETA, -jnp.arange(half, dtype=F32) / half)
    ang = jnp.arange(seq, dtype=F32)[:, None] * inv[None, :]
    cos, sin = jnp.cos(ang), jnp.sin(ang)
    zeros = lambda w: jnp.zeros((seq, w), F32)
    tail = MLA_PAD - MLA_QK_DIM
    cos_t = jnp.concatenate([jnp.ones((seq, MLA_NOPE), F32), cos, cos, zeros(tail)], axis=1)
    sin_t = jnp.concatenate([zeros(MLA_NOPE), -sin, sin, zeros(tail)], axis=1)
    return cos_t, sin_t


def _pad_cols(w, width):
    return jnp.pad(w, ((0, 0), (0, width - w.shape[1])))


def _with_swapped_rope(a):
    half = MLA_ROPE // 2
    return jnp.concatenate([a, a[..., MLA_NOPE + half:], a[..., MLA_NOPE:MLA_NOPE + half]], axis=-1)


def _nsa_column_order():
    g_cols = NSA_KV_HEADS * HEAD_DIM
    q_cols = NSA_HEADS * HEAD_DIM
    order = []
    for g in range(NSA_KV_HEADS):
        cols = list(range(g * _NQ, (g + 1) * _NQ))
        for part in range(2, 6):
            start = q_cols + part * g_cols + g * HEAD_DIM
            cols += list(range(start, start + HEAD_DIM))
        gate0 = q_cols + 6 * g_cols + g * NSA_REPS * 3
        cols += list(range(gate0, gate0 + NSA_REPS * 3))
        order.append(cols)
    return order


def _nsa_cmp_columns():
    q_cols = NSA_HEADS * HEAD_DIM
    g_cols = NSA_KV_HEADS * HEAD_DIM
    cols = []
    for g in range(NSA_KV_HEADS):
        for part in range(2):
            start = q_cols + part * g_cols + g * HEAD_DIM
            cols += list(range(start, start + HEAD_DIM))
    return cols


def _nsa_mixer(z_nsa3, z_cmp, rel_bias, q_norm, k_norm, cmp_pos, cmp_w1, cmp_w2, batch, seq):
    w1 = cmp_w1.reshape(2, 2, NSA_CMP_STRIDE, HEAD_DIM, NSA_CMP_HIDDEN)
    zero1 = jnp.zeros_like(w1[0])
    w1_bd = jnp.concatenate([jnp.concatenate([w1[0], zero1], axis=-1),
                             jnp.concatenate([zero1, w1[1]], axis=-1)], axis=-2)
    zero2 = jnp.zeros_like(cmp_w2[0])
    w2_bd = jnp.concatenate([jnp.concatenate([cmp_w2[0], zero2], axis=-1),
                             jnp.concatenate([zero2, cmp_w2[1]], axis=-1)], axis=-2)
    pos_kv = jnp.concatenate([cmp_pos[0], cmp_pos[1]], axis=-1)
    n_cmp = (seq - NSA_CMP_LEN) // NSA_CMP_STRIDE + 1
    ci = np.arange(NSA_N_CMP_PAD)[:, None] * NSA_CMP_STRIDE
    sj = np.arange(NSA_N_SLC)[None, :] * NSA_SLC_BLOCK
    overlap = ((ci < sj + NSA_SLC_BLOCK) & (ci + NSA_CMP_LEN > sj) & (np.arange(NSA_N_CMP_PAD)[:, None] < n_cmp))
    ovt = jnp.asarray(overlap.T.astype(np.float32))
    nsa_cols = rel_bias[:, NSA_BIAS_COL0:NSA_BIAS_COL0 + NSA_HEADS]
    bslc = _slc_bias_t(nsa_cols, seq)
    bwin = jnp.transpose(_band_bias(nsa_cols, NSA_WINDOW, NSA_WIN_PREV, 1, NSA_KV_HEADS, NSA_REPS), (0, 2, 1))
    return _nsa(z_nsa3, z_cmp.reshape(batch, seq, -1), w1_bd[0].astype(BF16), w1_bd[1].astype(BF16),
                w2_bd.astype(BF16), pos_kv, q_norm.reshape(HEAD_DIM, 1), k_norm.reshape(3, 1, HEAD_DIM),
                ovt, bslc, bwin, batch, seq)


def kernel(x, rel_bias, ffn1_norm, ffn1_w_gate, ffn1_w_up, ffn1_w_down, mix_norm, ffn2_norm, ffn2_w_gate,
           ffn2_w_up, ffn2_w_down, ab_w_in, mla_q_a_norm, mla_w_q_b, mla_kv_a_norm, mla_w_kv_b, mla_q_norm,
           mla_k_norm, dil_q_norm, dil_k_norm, ab_w_out, cd_w_in, swa_q_norm, swa_k_norm, swa_sinks,
           nsa_q_norm, nsa_k_norm, nsa_cmp_pos, nsa_cmp_w1, nsa_cmp_w2, cd_w_out):
    batch, seq, _ = x.shape
    n = batch * seq
    assert seq % (16 * QB) == 0 and n % TM == 0 and seq % TM == 0
    bf = lambda a: a.astype(BF16)
    xf = x.reshape(n, D_MODEL)
    ffn1 = (bf(ffn1_w_gate), bf(ffn1_w_up), bf(ffn1_w_down))
    ffn2 = (bf(ffn2_w_gate), bf(ffn2_w_up), bf(ffn2_w_down))

    w_in = ab_w_in[0]
    mla_cols = MLA_Q_RANK + MLA_KV_RANK
    w_krope = _with_swapped_rope(jnp.pad(w_in[:, mla_cols:mla_cols + MLA_ROPE], ((0, 0), (MLA_NOPE, 0))))
    w_mla = jnp.concatenate([w_in[:, :mla_cols], w_krope], axis=1)
    wq = _with_swapped_rope(mla_w_q_b[0].reshape(MLA_Q_RANK, MLA_HEADS, MLA_QK_DIM))
    wq = wq.reshape(MLA_Q_RANK, MLA_HEADS * MLA_PAD)
    wkv = mla_w_kv_b[0].reshape(MLA_KV_RANK, MLA_HEADS, MLA_NOPE + MLA_V)
    wk = _pad_cols(wkv[:, :, :MLA_NOPE].reshape(MLA_KV_RANK * MLA_HEADS, MLA_NOPE), MLA_PAD)
    wk = wk.reshape(MLA_KV_RANK, MLA_HEADS * MLA_PAD)
    wv = wkv[:, :, MLA_NOPE:].reshape(MLA_KV_RANK, MLA_HEADS * MLA_V)
    cos_t, sin_t = _rope_tables(seq)
    q_scale = MLA_QK_DIM ** -0.5 * LOG2E
    xf, z_dil, q_mla, k_mla, vt_mla = _ffn_proj_mla(
        xf, ffn1_norm[0], ffn1, 0, mix_norm[0], bf(w_mla), bf(w_in[:, mla_cols + MLA_ROPE:]),
        mla_q_a_norm[0].reshape(1, -1), mla_kv_a_norm[0].reshape(1, -1), bf(wq), bf(wk), bf(wv),
        _with_swapped_rope(mla_q_norm[0].reshape(1, -1)), _with_swapped_rope(mla_k_norm[0].reshape(1, -1)),
        cos_t * q_scale, sin_t * q_scale, cos_t, sin_t, seq)
    o_a = _mla_attn(q_mla.reshape(batch, seq, -1), k_mla.reshape(batch, seq, -1),
                    vt_mla.reshape(batch, seq // MLA_TQ, MLA_HEADS * MLA_V, MLA_TQ),
                    batch, seq).reshape(n, MLA_HEADS * MLA_V)

    z_dil3 = z_dil.reshape(batch, seq, -1)
    gq, gk = dil_q_norm[0], dil_k_norm[0]
    dil_outs, dil_lses = [], []
    for grp, (window, dilation) in enumerate(DIL_PATTERNS):
        bm = _band_bias(rel_bias[:, grp * DIL_HPG:(grp + 1) * DIL_HPG], window // dilation + 1, 1, dilation,
                        DIL_HPG, 1)
        o, lse = _dilated_group(z_dil3, batch, seq, grp, dilation, gq, gk, bm)
        dil_outs.append(o)
        dil_lses.append(lse)
    w_out = ab_w_out[0]
    xf = _out_ab_ffn(xf, o_a, dil_outs, dil_lses, bf(w_out[:MLA_HEADS * MLA_V]), bf(w_out[MLA_HEADS * MLA_V:]),
                     ffn2_norm[0], ffn2, 0)

    w_in = cd_w_in[0]
    swa_cols = (SWA_HEADS + 2 * SWA_KV_HEADS) * HEAD_DIM
    w_nsa_src = w_in[:, swa_cols:]
    w_nsa = jnp.concatenate([_pad_cols(w_nsa_src[:, np.asarray(cols)], NSA_GROUP_COLS)
                             for cols in _nsa_column_order()], axis=1)
    w_cmp = w_nsa_src[:, np.asarray(_nsa_cmp_columns())]
    xf, z_swa, z_nsa, z_cmp = _ffn_proj(xf, ffn1_norm[1], ffn1, 1, mix_norm[1],
                                        [bf(w_in[:, :swa_cols]), bf(w_nsa), bf(w_cmp)])

    swa_reps = SWA_HEADS // SWA_KV_HEADS
    swa_prev = -(-(SWA_WINDOW - 1) // QB)
    bm_swa = _band_bias(rel_bias[:, :SWA_HEADS], SWA_WINDOW, swa_prev, 1, SWA_KV_HEADS, swa_reps)
    sinks = jnp.broadcast_to((swa_sinks[0].astype(F32) * LOG2E).reshape(SWA_KV_HEADS, 1, swa_reps, 1),
                             (SWA_KV_HEADS, 1, swa_reps, QB)).reshape(SWA_KV_HEADS, 1, swa_reps * QB)
    o_c = _swa(z_swa.reshape(batch, seq, -1), batch, seq, swa_q_norm[0].reshape(HEAD_DIM, 1),
               swa_k_norm[0].reshape(1, HEAD_DIM), bm_swa, sinks)

    o_d = _nsa_mixer(z_nsa.reshape(batch, seq, NSA_KV_HEADS * NSA_GROUP_COLS), z_cmp, rel_bias, nsa_q_norm[0],
                     nsa_k_norm[0], nsa_cmp_pos[0], nsa_cmp_w1[0], nsa_cmp_w2[0], batch, seq)

    w_out = cd_w_out[0]
    xf = _out_cd_ffn(xf, o_c.reshape(n, -1), o_d.reshape(n, -1), bf(w_out[:SWA_HEADS * HEAD_DIM]),
                     bf(w_out[SWA_HEADS * HEAD_DIM:]), ffn2_norm[1], ffn2, 1)
    return xf.reshape(batch, seq, D_MODEL)
```

```python
import functools
import math

import numpy as np
import jax
import jax.numpy as jnp
from jax import lax
from jax.experimental import pallas as pl
from jax.experimental.pallas import tpu as pltpu

F32 = jnp.float32
BF16 = jnp.bfloat16

EPS = 1e-6
NEG = -1e30
TINY = 1e-30
LOG2E = math.log2(math.e)
LN2 = math.log(2.0)
D_MODEL = 1024
D_FF = 2816
NUM_BUCKETS = 32
MAX_DISTANCE = 2048
HEAD_DIM = 64
QB = 128

MLA_HEADS = 8
MLA_Q_RANK = 256
MLA_KV_RANK = 128
MLA_NOPE = 64
MLA_ROPE = 32
MLA_V = 64
MLA_QK_DIM = MLA_NOPE + MLA_ROPE
MLA_PAD = 128
ROPE_THETA = 10000.0

DIL_PATTERNS = ((128, 1), (512, 4), (2048, 16))
DIL_HPG = 4
SWA_HEADS = 8
SWA_KV_HEADS = 2
SWA_WINDOW = 128
NSA_HEADS = 8
NSA_KV_HEADS = 2
NSA_CMP_LEN = 32
NSA_CMP_STRIDE = 16
NSA_CMP_HIDDEN = 128
NSA_SLC_BLOCK = 64
NSA_TOP_N = 16
NSA_WINDOW = 512
NSA_FORCE = 1e6
NSA_BIAS_COL0 = 8
NSA_GROUP_COLS = 640

VMEM_LIMIT = 56 * 1024 * 1024
TM = 512

_NT = (((1,), (1,)), ((), ()))
_HI = lax.Precision.HIGHEST


def _cparams(n_axes):
    return pltpu.CompilerParams(dimension_semantics=("arbitrary",) * n_axes,
                                vmem_limit_bytes=VMEM_LIMIT)


def _resident(shape):
    nd = len(shape)
    return pl.BlockSpec(shape, lambda *_: (0,) * nd, pipeline_mode=pl.Buffered(1))


def _layer_block(w, layer):
    nd = w.ndim - 1
    return pl.BlockSpec((None,) + w.shape[1:], lambda *_: (layer,) + (0,) * nd, pipeline_mode=pl.Buffered(1))


def _rms(x):
    return x * lax.rsqrt(jnp.mean(x * x, axis=-1, keepdims=True) + EPS)


def _dot(a, b):
    return jnp.dot(a, b, preferred_element_type=F32)


def _dot_nt(a, b, precision=None):
    return lax.dot_general(a, b, _NT, preferred_element_type=F32, precision=precision)


FF_CHUNK = 256


N_CHUNK = 512


def _swiglu_half_step(x, g_ref, wg_ref, wu_ref, wd_ref):
    hb = (_rms(x) * g_ref[...]).astype(BF16)
    acc = jnp.zeros(x.shape, F32)
    for c in range(D_FF // FF_CHUNK):
        sl = slice(c * FF_CHUNK, (c + 1) * FF_CHUNK)
        gate = _dot(hb, wg_ref[:, sl])
        up = _dot(hb, wu_ref[:, sl])
        act = (gate * jax.nn.sigmoid(gate) * up).astype(BF16)
        acc = acc + _dot(act, wd_ref[sl, :])
    return x + 0.5 * acc


def _ffn_proj_kernel(n_out, x_ref, g_ref, wg_ref, wu_ref, wd_ref, gm_ref, *refs):
    w_refs, x_out_ref, z_refs = refs[:n_out], refs[n_out], refs[n_out + 1:]
    x = _swiglu_half_step(x_ref[...], g_ref, wg_ref, wu_ref, wd_ref)
    x_out_ref[...] = x
    hb = (_rms(x) * gm_ref[...]).astype(BF16)
    for w_ref, z_ref in zip(w_refs, z_refs):
        width = w_ref.shape[1]
        for c0 in range(0, width, N_CHUNK):
            sl = slice(c0, min(c0 + N_CHUNK, width))
            z_ref[:, sl] = _dot(hb, w_ref[:, sl]).astype(z_ref.dtype)


def _tile(width):
    return pl.BlockSpec((TM, width), lambda i: (i, 0))


def _ffn_proj(x, g, ffn, layer, g_mix, ws, z_dtypes=None):
    n = x.shape[0]
    z_dtypes = z_dtypes or [F32] * len(ws)
    ffn_w = [_layer_block(w, layer) for w in ffn]
    return pl.pallas_call(
        functools.partial(_ffn_proj_kernel, len(ws)),
        grid=(n // TM,),
        in_specs=[_tile(D_MODEL), _resident((1, D_MODEL))] + ffn_w + [_resident((1, D_MODEL))]
        + [_resident(w.shape) for w in ws],
        out_specs=[_tile(D_MODEL)] + [_tile(w.shape[1]) for w in ws],
        out_shape=[jax.ShapeDtypeStruct((n, D_MODEL), F32)]
        + [jax.ShapeDtypeStruct((n, w.shape[1]), dt) for w, dt in zip(ws, z_dtypes)],
        compiler_params=_cparams(1),
        name="ffn_proj",
    )(x, g.reshape(1, D_MODEL), *ffn, g_mix.reshape(1, D_MODEL), *ws)


def _out_ab_ffn_kernel(x_ref, oa_ref, o0_ref, o1_ref, o2_ref, l0_ref, l1_ref, l2_ref, wa_ref, wb_ref,
                       g_ref, wg_ref, wu_ref, wd_ref, out_ref):
    l0, l1, l2 = l0_ref[...], l1_ref[...], l2_ref[...]
    m = jnp.maximum(jnp.maximum(l0, l1), l2)
    e0, e1, e2 = jnp.exp(l0 - m), jnp.exp(l1 - m), jnp.exp(l2 - m)
    ob = (e0 * o0_ref[...] + e1 * o1_ref[...] + e2 * o2_ref[...]) / (e0 + e1 + e2)
    x = x_ref[...] + _dot(oa_ref[...], wa_ref[...]) + _dot(ob.astype(BF16), wb_ref[...])
    out_ref[...] = _swiglu_half_step(x, g_ref, wg_ref, wu_ref, wd_ref)


def _out_ab_ffn(x, oa, dil_outs, dil_lses, wa, wb, g, ffn, layer):
    n = x.shape[0]
    dw = DIL_HPG * HEAD_DIM
    ffn_w = [_layer_block(w, layer) for w in ffn]
    return pl.pallas_call(
        _out_ab_ffn_kernel,
        grid=(n // TM,),
        in_specs=[_tile(D_MODEL), _tile(oa.shape[1])] + [_tile(dw)] * 6 + [_resident(wa.shape), _resident(wb.shape),
                                                                        _resident((1, D_MODEL))] + ffn_w,
        out_specs=_tile(D_MODEL),
        out_shape=jax.ShapeDtypeStruct((n, D_MODEL), F32),
        compiler_params=_cparams(1),
        name="out_ab_ffn",
    )(x, oa, *dil_outs, *dil_lses, wa, wb, g.reshape(1, D_MODEL), *ffn)


def _out_cd_ffn_kernel(x_ref, oc_ref, od_ref, wc_ref, wdd_ref, g_ref, wg_ref, wu_ref, wd_ref, out_ref):
    x = x_ref[...] + _dot(oc_ref[...], wc_ref[...]) + _dot(od_ref[...], wdd_ref[...])
    out_ref[...] = _swiglu_half_step(x, g_ref, wg_ref, wu_ref, wd_ref)


def _out_cd_ffn(x, oc, od, wc, wdd, g, ffn, layer):
    n = x.shape[0]
    ffn_w = [_layer_block(w, layer) for w in ffn]
    return pl.pallas_call(
        _out_cd_ffn_kernel,
        grid=(n // TM,),
        in_specs=[_tile(D_MODEL), _tile(oc.shape[1]), _tile(od.shape[1]), _resident(wc.shape), _resident(wdd.shape),
                  _resident((1, D_MODEL))] + ffn_w,
        out_specs=_tile(D_MODEL),
        out_shape=jax.ShapeDtypeStruct((n, D_MODEL), F32),
        compiler_params=_cparams(1),
        name="out_cd_ffn",
    )(x, oc, od, wc, wdd, g.reshape(1, D_MODEL), *ffn)


MLA_TQ = 256
MLA_STEP = 2


def _ffn_proj_mla_kernel(x_ref, g_ref, wg_ref, wu_ref, wd_ref, gm_ref, wmla_ref, wdil_ref,
                         qan_ref, kvan_ref, wq_ref, wk_ref, wv_ref, gq_ref, gk_ref,
                         qc_ref, qs_ref, kc_ref, ks_ref,
                         x_out_ref, zdil_ref, q_ref, k_ref, vt_ref, zprev_ref):
    @pl.when(pl.program_id(0) == 0)
    def _():
        zprev_ref[...] = jnp.zeros(zprev_ref.shape, F32)

    cq = (_rms(zprev_ref[:, :MLA_Q_RANK]) * qan_ref[...]).astype(BF16)
    ckv = (_rms(zprev_ref[:, MLA_Q_RANK:MLA_Q_RANK + MLA_KV_RANK]) * kvan_ref[...]).astype(BF16)
    kr = zprev_ref[:, MLA_Q_RANK + MLA_KV_RANK:]
    real = jnp.where(lax.broadcasted_iota(jnp.int32, (1, MLA_PAD), 1) < MLA_QK_DIM, 1.0, 0.0)

    def head_norm_rope(x, gain, cos, sin):
        ssq = jnp.sum(x * x * real, axis=-1, keepdims=True)
        x = x * lax.rsqrt(ssq * (1.0 / MLA_QK_DIM) + EPS) * gain
        return x * cos + pltpu.roll(x, MLA_PAD - MLA_ROPE, 1) * sin

    def head(h):
        sl = slice(h * MLA_PAD, (h + 1) * MLA_PAD)
        qh = head_norm_rope(_dot(cq, wq_ref[:, sl]), gq_ref[...], qc_ref[...], qs_ref[...])
        q_ref[:, sl] = qh.astype(BF16)
        kh = head_norm_rope(_dot(ckv, wk_ref[:, sl]) + kr, gk_ref[...], kc_ref[...], ks_ref[...])
        k_ref[:, sl] = kh.astype(BF16)

    def values():
        v = _dot(ckv, wv_ref[...])
        for c in range(TM // MLA_TQ):
            vt_ref[c] = v[c * MLA_TQ:(c + 1) * MLA_TQ].T.astype(BF16)

    latent_work = [functools.partial(head, h) for h in range(MLA_HEADS)] + [values]

    x = x_ref[...]
    hb = (_rms(x) * g_ref[...]).astype(BF16)
    acc = jnp.zeros(x.shape, F32)
    for c in range(D_FF // FF_CHUNK):
        sl = slice(c * FF_CHUNK, (c + 1) * FF_CHUNK)
        gate = _dot(hb, wg_ref[:, sl])
        up = _dot(hb, wu_ref[:, sl])
        act = (gate * jax.nn.sigmoid(gate) * up).astype(BF16)
        acc = acc + _dot(act, wd_ref[sl, :])
        if c < len(latent_work):
            latent_work[c]()
    x = x + 0.5 * acc
    x_out_ref[...] = x
    hm = (_rms(x) * gm_ref[...]).astype(BF16)
    zprev_ref[...] = _dot(hm, wmla_ref[...])
    width = wdil_ref.shape[1]
    for c0 in range(0, width, N_CHUNK):
        sl = slice(c0, min(c0 + N_CHUNK, width))
        zdil_ref[:, sl] = _dot(hm, wdil_ref[:, sl])


def _ffn_proj_mla(x, g, ffn, layer, g_mix, w_mla, w_dil, qan, kvan, wq, wk, wv, gq, gk,
                  q_cos, q_sin, k_cos, k_sin, seq):
    n = x.shape[0]
    n_tiles = n // TM
    assert D_FF // FF_CHUNK >= MLA_HEADS + 1
    per_seq = seq // TM
    hw = MLA_HEADS * MLA_PAD
    vw = MLA_HEADS * MLA_V

    def cur(width):
        return pl.BlockSpec((TM, width), lambda i: (jnp.minimum(i, n_tiles - 1), 0))

    def prev(width):
        return pl.BlockSpec((TM, width), lambda i: (jnp.maximum(i - 1, 0), 0))

    rope_spec = pl.BlockSpec((TM, MLA_PAD), lambda i: (jnp.maximum(i - 1, 0) % per_seq, 0))
    small = [qan, kvan, wq, wk, wv, gq, gk]
    return pl.pallas_call(
        _ffn_proj_mla_kernel,
        grid=(n_tiles + 1,),
        in_specs=[cur(D_MODEL), _resident((1, D_MODEL))] + [_layer_block(w, layer) for w in ffn]
        + [_resident((1, D_MODEL)), _resident(w_mla.shape), _resident(w_dil.shape)]
        + [_resident(a.shape) for a in small] + [rope_spec] * 4,
        out_specs=[cur(D_MODEL), cur(w_dil.shape[1]), prev(hw), prev(hw),
                   pl.BlockSpec((TM // MLA_TQ, vw, MLA_TQ), lambda i: (jnp.maximum(i - 1, 0), 0, 0))],
        out_shape=[jax.ShapeDtypeStruct((n, D_MODEL), F32), jax.ShapeDtypeStruct((n, w_dil.shape[1]), F32),
                   jax.ShapeDtypeStruct((n, hw), BF16), jax.ShapeDtypeStruct((n, hw), BF16),
                   jax.ShapeDtypeStruct((n // MLA_TQ, vw, MLA_TQ), BF16)],
        scratch_shapes=[pltpu.VMEM((TM, w_mla.shape[1]), F32)],
        compiler_params=_cparams(1),
        name="ffn_proj_mla",
    )(x, g.reshape(1, D_MODEL), *ffn, g_mix.reshape(1, D_MODEL), w_mla, w_dil, *small,
      q_cos, q_sin, k_cos, k_sin)


def _flash_step_t(s, m, l, acc, pv, vt):
    m_new = jnp.maximum(m, jnp.max(s, axis=0, keepdims=True))
    p = jnp.exp2(s - m_new)
    alpha = jnp.exp2(m - m_new)
    l = alpha * l + jnp.sum(p, axis=0, keepdims=True)
    acc = alpha * (acc + pv)
    return m_new, l, acc, _dot(vt, p.astype(BF16))


def _flash_init(dv, n_queries):
    return (jnp.full((1, n_queries), NEG, F32), jnp.zeros((1, n_queries), F32), jnp.zeros((dv, n_queries), F32),
            jnp.zeros((dv, n_queries), F32))


def _flash_finish(m, l, acc, pv):
    return (acc + pv) * (1.0 / l)


def _mla_attn_kernel(seq, q_ref, k_ref, vt_ref, o_ref):
    t = MLA_TQ
    key = lax.broadcasted_iota(jnp.int32, (t, t), 0)
    qry = lax.broadcasted_iota(jnp.int32, (t, t), 1)
    diag_ok = key <= qry
    heads = [(slice(hh * MLA_PAD, (hh + 1) * MLA_PAD), slice(hh * MLA_V, (hh + 1) * MLA_V)) for hh in range(2)]

    init = _flash_init(MLA_V, t)
    n_blocks = seq // t
    for first in range(n_blocks // 2):
        blocks = [first, n_blocks - 1 - first]
        chains = [(i, h) for i in blocks for h in range(2)]
        qs = {(i, h): q_ref[i * t:(i + 1) * t, heads[h][0]] for i, h in chains}
        steps = {i: [(j, min(j + MLA_STEP, i + 1)) for j in range(0, i + 1, MLA_STEP)] for i in blocks}

        def scores(chain, step):
            i, h = chain
            lo, hi = steps[i][step]
            s = _dot_nt(k_ref[lo * t:hi * t, heads[h][0]], qs[chain])
            if hi == i + 1:
                last = jnp.where(diag_ok, s[(i - lo) * t:], NEG)
                s = last if i == lo else jnp.concatenate([s[:(i - lo) * t], last], axis=0)
            return s

        def values(chain, step):
            i, h = chain
            lo, hi = steps[i][step]
            return jnp.concatenate([vt_ref[j, heads[h][1], :] for j in range(lo, hi)], axis=1)

        cur = {c: scores(c, 0) for c in chains}
        states = {c: init for c in chains}
        for step in range(max(len(s) for s in steps.values())):
            for c in chains:
                n_steps = len(steps[c[0]])
                if step >= n_steps:
                    continue
                nxt = scores(c, step + 1) if step + 1 < n_steps else None
                states[c] = _flash_step_t(cur[c], *states[c], values(c, step))
                cur[c] = nxt
        for i in blocks:
            outs = [_flash_finish(*states[(i, h)]) for h in range(2)]
            o_ref[i * t:(i + 1) * t, :] = jnp.concatenate(outs, axis=0).T.astype(o_ref.dtype)


def _mla_attn(q, k, vt, batch, seq):
    return pl.pallas_call(
        functools.partial(_mla_attn_kernel, seq),
        grid=(batch, MLA_HEADS // 2),
        in_specs=[pl.BlockSpec((None, seq, 2 * MLA_PAD), lambda b, h: (b, 0, h)),
                  pl.BlockSpec((None, seq, 2 * MLA_PAD), lambda b, h: (b, 0, h)),
                  pl.BlockSpec((None, seq // MLA_TQ, 2 * MLA_V, MLA_TQ), lambda b, h: (b, 0, h, 0))],
        out_specs=pl.BlockSpec((None, seq, 2 * MLA_V), lambda b, h: (b, 0, h)),
        out_shape=jax.ShapeDtypeStruct((batch, seq, MLA_HEADS * MLA_V), BF16),
        compiler_params=_cparams(2),
        name="mla_attn",
    )(q, k, vt)


def _head_norm(x, gain):
    return _rms(x) * gain


def _heads_t(qt, n_heads, gain):
    tiles = []
    for c0 in range(0, n_heads * HEAD_DIM, 2 * HEAD_DIM):
        pair = qt[:, c0:c0 + 2 * HEAD_DIM].T
        for h in range(2):
            x = pair[h * HEAD_DIM:(h + 1) * HEAD_DIM]
            tiles.append(x * lax.rsqrt(jnp.mean(x * x, axis=0, keepdims=True) + EPS) * gain)
    return tiles


def _q_gain(gq_ref):
    return jnp.broadcast_to(gq_ref[...] * (HEAD_DIM ** -0.5 * LOG2E), (HEAD_DIM, QB))


def _pair_ones():
    shape = (2 * HEAD_DIM, 2 * HEAD_DIM)
    same = lax.broadcasted_iota(jnp.int32, shape, 0) // HEAD_DIM == lax.broadcasted_iota(jnp.int32, shape, 1) // HEAD_DIM
    return jnp.where(same, 1.0, 0.0).astype(BF16)


def _pair_norm(x, seg, gain):
    sq = x * x
    hi = sq.astype(BF16)
    lo = (sq - hi.astype(F32)).astype(BF16)
    ssq = _dot(hi, seg) + _dot(lo, seg)
    return x * lax.rsqrt(ssq * (1.0 / HEAD_DIM) + EPS) * gain


BAND_KV = 2


def _banded_kernel(seq, dilation, reps, n_prev, with_sink, with_lse, group_size, paired, *refs):
    q_ref, k_ref, v_ref, gq_ref, gk_ref, bm_ref = refs[:6]
    refs = refs[6:]
    sink_ref = None
    if with_sink:
        sink_ref, refs = refs[0], refs[1:]
    o_ref, refs = refs[0], refs[1:]
    lse_ref = None
    if with_lse:
        lse_ref, refs = refs[0], refs[1:]
    kn_ref, vt_ref = refs
    length = seq // dilation
    n_blocks = length // QB
    pad = n_prev * QB
    kw = pad + QB
    win_row = lax.broadcasted_iota(jnp.int32, (kw, 1), 0)
    assert not paired or reps == 1
    if paired:
        gq, gk = gq_ref[...] * (HEAD_DIM ** -0.5 * LOG2E), gk_ref[...]
        seg = _pair_ones()
        lane = lax.broadcasted_iota(jnp.int32, (1, BAND_KV * HEAD_DIM), 1)
        head_lanes = [jnp.where(lane // HEAD_DIM == g, 1.0, 0.0) for g in range(BAND_KV)]
    else:
        gq, gk = _q_gain(gq_ref), gk_ref[...]

    def rows(start, size):
        return pl.ds(start, size) if dilation == 1 else pl.ds(start, size, stride=dilation)

    for res in range(dilation):
        kall, vall = k_ref[rows(res, length), :], v_ref[rows(res, length), :]
        if paired:
            kn_ref[res, :pad, :] = jnp.zeros((pad, BAND_KV * HEAD_DIM), BF16)
            kn_ref[res, pad:, :] = _pair_norm(kall, seg, gk).astype(BF16)
        else:
            for g in range(BAND_KV):
                kn_ref[res, g, :pad, :] = jnp.zeros((pad, HEAD_DIM), BF16)
                kn_ref[res, g, pad:, :] = _head_norm(kall[:, g * HEAD_DIM:(g + 1) * HEAD_DIM], gk).astype(BF16)
        for j in range(n_prev):
            vt_ref[res, j] = jnp.zeros((BAND_KV * HEAD_DIM, QB), BF16)
        for j in range(n_blocks):
            vt_ref[res, n_prev + j] = vall[j * QB:(j + 1) * QB].T.astype(BF16)

    def attend(res, i, g, s):
        bias = bm_ref[g]
        if i < n_prev:
            bias = jnp.where(win_row >= (n_prev - i) * QB, bias, NEG)
        s = s + bias
        m = jnp.max(s, axis=0, keepdims=True)
        if with_sink:
            m = jnp.maximum(m, sink_ref[g])
        p = jnp.exp2(s - m)
        l = jnp.sum(p, axis=0, keepdims=True)
        if with_sink:
            l = l + jnp.exp2(sink_ref[g] - m)
        p = p.astype(BF16)
        vl = slice(g * HEAD_DIM, (g + 1) * HEAD_DIM)
        o = _dot(vt_ref[res, i, vl, :], p[0:QB])
        for w in range(1, n_prev + 1):
            o = o + _dot(vt_ref[res, i + w, vl, :], p[w * QB:(w + 1) * QB])
        return o * (1.0 / l), m * LN2 + jnp.log(l)

    items = [(res, i) for res in range(dilation) for i in range(n_blocks)]
    for at in range(0, len(items), group_size):
        group = items[at:at + group_size]
        toks = [rows(res + i * QB * dilation, QB) for res, i in group]
        if paired:
            qns = [_pair_norm(q_ref[tok, :], seg, gq) for tok in toks]
            qss = [[(qn * head_lanes[g]).astype(BF16) for g in range(BAND_KV)] for qn in qns]
            scores = [[_dot_nt(kn_ref[res, i * QB:i * QB + kw, :], qs[g]) for g in range(BAND_KV)]
                      for (res, i), qs in zip(group, qss)]
        else:
            heads = [_heads_t(q_ref[tok, :], BAND_KV * reps, gq) for tok in toks]
            qss = [[jnp.concatenate(h[g * reps:(g + 1) * reps], axis=1).astype(BF16) for g in range(BAND_KV)]
                   for h in heads]
            scores = [[_dot(kn_ref[res, g, i * QB:i * QB + kw, :], qs[g]) for g in range(BAND_KV)]
                      for (res, i), qs in zip(group, qss)]
        for (res, i), tok, sc in zip(group, toks, scores):
            outs, lses = [], []
            for g in range(BAND_KV):
                o, lse = attend(res, i, g, sc[g])
                for r in range(reps):
                    outs.append(o[:, r * QB:(r + 1) * QB])
                    if with_lse:
                        lses.append(jnp.broadcast_to(lse[:, r * QB:(r + 1) * QB], (HEAD_DIM, QB)))
            o_ref[tok, :] = jnp.concatenate(outs, axis=0).T.astype(o_ref.dtype)
            if with_lse:
                lse_ref[tok, :] = jnp.concatenate(lses, axis=0).T


def _banded_scratch(seq, dilation, n_prev, paired):
    length = seq // dilation
    rows = n_prev * QB + length
    keys = (dilation, rows, BAND_KV * HEAD_DIM) if paired else (dilation, BAND_KV, rows, HEAD_DIM)
    return [pltpu.VMEM(keys, BF16),
            pltpu.VMEM((dilation, n_prev + length // QB, BAND_KV * HEAD_DIM, QB), BF16)]


PAIRED_MIN_DILATION = 16


def _dilated_group(z_dil, batch, seq, group, dilation, q_norm, k_norm, bm):
    gw = DIL_HPG * HEAD_DIM
    n_groups = len(DIL_PATTERNS)
    hps = BAND_KV
    halves = DIL_HPG // hps
    bm_t = jnp.transpose(bm, (0, 2, 1))
    paired = dilation >= PAIRED_MIN_DILATION
    if paired:
        gq = jnp.tile(q_norm.reshape(1, HEAD_DIM), (1, BAND_KV))
        gk = jnp.tile(k_norm.reshape(1, HEAD_DIM), (1, BAND_KV))
    else:
        gq, gk = q_norm.reshape(HEAD_DIM, 1), k_norm.reshape(1, HEAD_DIM)

    def spec(part):
        return pl.BlockSpec((None, seq, hps * HEAD_DIM), lambda b, h: (b, 0, (part * n_groups + group) * halves + h))

    out_spec = pl.BlockSpec((None, seq, hps * HEAD_DIM), lambda b, h: (b, 0, h))
    shape = jax.ShapeDtypeStruct((batch, seq, gw), F32)
    out, lse = pl.pallas_call(
        functools.partial(_banded_kernel, seq, dilation, 1, 1, False, True, 8, paired),
        grid=(batch, halves),
        in_specs=[spec(0), spec(1), spec(2), _resident(gq.shape), _resident(gk.shape),
                  pl.BlockSpec((hps,) + bm_t.shape[1:], lambda b, h: (h, 0, 0))],
        out_specs=[out_spec, out_spec],
        out_shape=[shape, shape],
        scratch_shapes=_banded_scratch(seq, dilation, 1, paired),
        compiler_params=_cparams(2),
        name="dilated_g%d" % group,
    )(z_dil, z_dil, z_dil, gq, gk, bm_t)
    return out.reshape(batch * seq, gw), lse.reshape(batch * seq, gw)


def _swa(z_swa, batch, seq, gq, gk, bm, sinks):
    qw = SWA_HEADS * HEAD_DIM
    kvw = SWA_KV_HEADS * HEAD_DIM
    reps = SWA_HEADS // SWA_KV_HEADS
    n_prev = -(-(SWA_WINDOW - 1) // QB)
    assert SWA_KV_HEADS == BAND_KV
    bm_t = jnp.transpose(bm, (0, 2, 1))
    return pl.pallas_call(
        functools.partial(_banded_kernel, seq, 1, reps, n_prev, True, False, 4, False),
        grid=(batch,),
        in_specs=[pl.BlockSpec((None, seq, qw), lambda b: (b, 0, 0)),
                  pl.BlockSpec((None, seq, kvw), lambda b: (b, 0, qw // kvw)),
                  pl.BlockSpec((None, seq, kvw), lambda b: (b, 0, qw // kvw + 1)),
                  _resident(gq.shape), _resident(gk.shape), _resident(bm_t.shape), _resident(sinks.shape)],
        out_specs=pl.BlockSpec((None, seq, qw), lambda b: (b, 0, 0)),
        out_shape=jax.ShapeDtypeStruct((batch, seq, qw), BF16),
        scratch_shapes=_banded_scratch(seq, 1, n_prev, False),
        compiler_params=_cparams(1),
        name="swa",
    )(z_swa, z_swa, z_swa, gq, gk, bm_t, sinks)


NSA_REPS = NSA_HEADS // NSA_KV_HEADS
NSA_N_SLC = 32
NSA_N_CMP_PAD = 128
NSA_WIN_PREV = -(-(NSA_WINDOW - 1) // QB)
_NQ = NSA_REPS * HEAD_DIM
_OFF_KS, _OFF_VS, _OFF_KW, _OFF_VW, _OFF_GATE = (_NQ + i * HEAD_DIM for i in range(5))


NSA_STEP = 16
NSA_GROUP = 4


def _nsa_kernel(seq, z_ref, zc_ref, w1a_ref, w1b_ref, w2_ref, pos_ref, gq_ref, gk_ref, ovt_ref,
                bslc_ref, bwin_ref, o_ref, ks_ref, vst_ref, kw_ref, vwt_ref):
    n_chunks = seq // QB
    lanes = NSA_REPS * QB
    gq = _q_gain(gq_ref)
    win_pad = NSA_WIN_PREV * QB
    win_kw = win_pad + QB

    first = jnp.zeros((NSA_N_CMP_PAD, 2 * NSA_CMP_HIDDEN), F32)
    second = jnp.zeros((NSA_N_CMP_PAD, 2 * NSA_CMP_HIDDEN), F32)
    for l in range(NSA_CMP_STRIDE):
        tok = zc_ref[pl.ds(l, NSA_N_CMP_PAD, stride=NSA_CMP_STRIDE), :]
        first = first + _dot((tok + pos_ref[l:l + 1, :]).astype(BF16), w1a_ref[l])
        second = second + _dot((tok + pos_ref[NSA_CMP_STRIDE + l:NSA_CMP_STRIDE + l + 1, :]).astype(BF16),
                               w1b_ref[l])
    hidden = jax.nn.gelu(first + pltpu.roll(second, NSA_N_CMP_PAD - 1, 0)).astype(BF16)
    cmp_kv = _dot(hidden, w2_ref[...])
    kc = _head_norm(cmp_kv[:, :HEAD_DIM], gk_ref[0]).astype(BF16)
    vct = cmp_kv.T[HEAD_DIM:].astype(BF16)

    ks_ref[...] = _head_norm(z_ref[:, _OFF_KS:_OFF_KS + HEAD_DIM], gk_ref[1]).astype(BF16)
    kw_ref[:win_pad, :] = jnp.zeros((win_pad, HEAD_DIM), BF16)
    kw_ref[win_pad:, :] = _head_norm(z_ref[:, _OFF_KW:_OFF_KW + HEAD_DIM], gk_ref[2]).astype(BF16)
    for j in range(NSA_WIN_PREV):
        vwt_ref[j] = jnp.zeros((HEAD_DIM, QB), BF16)
    for j in range(n_chunks):
        rows = slice(j * QB, (j + 1) * QB)
        vst_ref[:, rows] = z_ref[rows, _OFF_KS:_OFF_KS + 2 * HEAD_DIM].T[HEAD_DIM:].astype(BF16)
        vwt_ref[NSA_WIN_PREV + j] = z_ref[rows, _OFF_KW:_OFF_KW + 2 * HEAD_DIM].T[HEAD_DIM:].astype(BF16)

    cmp_id = lax.broadcasted_iota(jnp.int32, (NSA_N_CMP_PAD, lanes), 0)
    cmp_end = cmp_id * NSA_CMP_STRIDE + (NSA_CMP_LEN - 1)
    cmp_real = cmp_id < NSA_N_CMP_PAD - 1
    q_in_blk = lax.broadcasted_iota(jnp.int32, (NSA_N_CMP_PAD, lanes), 1) & (QB - 1)
    blk_id = lax.broadcasted_iota(jnp.int32, (NSA_N_SLC, QB), 0)
    q_lane = lax.broadcasted_iota(jnp.int32, (NSA_N_SLC, QB), 1)
    win_row = lax.broadcasted_iota(jnp.int32, (win_kw, 1), 0)
    init = _flash_init(HEAD_DIM, lanes)

    def q_blocks(blk):
        both = range(len(blk))
        n_keys = [i + 1 for i in blk]
        mask_pad = [i < NSA_WIN_PREV for i in blk]
        step = [max(d for d in range(1, NSA_STEP + 1) if n % d == 0) for n in n_keys]
        step_keys = [s * QB for s in step]
        blocks_per_step = [s // NSA_SLC_BLOCK for s in step_keys]
        r0 = [i * QB for i in blk]
        qs = [jnp.concatenate(_heads_t(z_ref[pl.ds(r, QB), :_NQ], NSA_REPS, gq), axis=1).astype(BF16)
              for r in r0]

        sc = [_dot(kc, q) for q in qs]
        s_win = [_dot(kw_ref[pl.ds(r, win_kw), :], q) for r, q in zip(r0, qs)]

        def slc_scores(b, j):
            return _dot(ks_ref[j * step_keys[b]:(j + 1) * step_keys[b], :], qs[b])

        cur = [slc_scores(b, 0) for b in both]

        o_cmp, imp = [], []
        for b in both:
            ok = (cmp_end <= q_in_blk + blk[b] * QB) & cmp_real
            s = jnp.where(ok, sc[b], NEG)
            e = jnp.where(ok, jnp.exp2(s - jnp.max(s, axis=0, keepdims=True)), 0.0)
            p = e * (1.0 / jnp.maximum(jnp.sum(e, axis=0, keepdims=True), TINY))
            o_cmp.append(_dot(vct, p.astype(BF16)))
            p_sum = p[:, 0:QB] + p[:, QB:2 * QB] + p[:, 2 * QB:3 * QB] + p[:, 3 * QB:4 * QB]
            imp.append(jnp.dot(ovt_ref[...], p_sum, precision=_HI, preferred_element_type=F32))

        o_win = []
        for b in both:
            bias = bwin_ref[...]
            if mask_pad[b]:
                bias = jnp.where(win_row >= (NSA_WIN_PREV - blk[b]) * QB, bias, NEG)
            s = s_win[b] + bias
            pw = jnp.exp2(s - jnp.max(s, axis=0, keepdims=True))
            lw = jnp.sum(pw, axis=0, keepdims=True)
            pw = pw.astype(BF16)
            o = _dot(vwt_ref[blk[b]], pw[0:QB])
            for c in range(1, NSA_WIN_PREV + 1):
                o = o + _dot(vwt_ref[blk[b] + c], pw[c * QB:(c + 1) * QB])
            o_win.append(o * (1.0 / lw))

        drop = []
        for b in both:
            tb = lax.shift_right_logical(q_lane + blk[b] * QB, 6)
            forced = (blk_id == 0) | (blk_id == tb) | (blk_id == tb - 1)
            score = jnp.where(blk_id <= tb, imp[b] + jnp.where(forced, NSA_FORCE, 0.0), -NSA_FORCE)
            rank = jnp.zeros((NSA_N_SLC, QB), F32)
            for other in range(NSA_N_SLC):
                s_o = score[other:other + 1, :]
                beats = (s_o > score) | ((s_o == score) & (blk_id > other))
                rank = rank + jnp.where(beats, 1.0, 0.0)
            drop.append(jnp.where(rank < NSA_TOP_N, 0.0, NEG))

        states = [init for _ in both]
        n_steps = [n // s for n, s in zip(n_keys, step)]
        for j in range(max(n_steps)):
            for b in both:
                if j >= n_steps[b]:
                    continue
                nxt = slc_scores(b, j + 1) if j + 1 < n_steps[b] else None
                bias = jnp.concatenate(
                    [bslc_ref[blk[b] - (j * step[b] + c) + 1] for c in range(step[b])], axis=0)
                km = jnp.concatenate([jnp.broadcast_to(drop[b][k:k + 1], (NSA_SLC_BLOCK, QB))
                                      for k in range(j * blocks_per_step[b], (j + 1) * blocks_per_step[b])], axis=0)
                s = cur[b] + bias + jnp.concatenate([km] * NSA_REPS, axis=1)
                states[b] = _flash_step_t(s, *states[b], vst_ref[:, j * step_keys[b]:(j + 1) * step_keys[b]])
                cur[b] = nxt

        for b in both:
            o_slc = _flash_finish(*states[b])
            gate = jax.nn.sigmoid(z_ref[pl.ds(r0[b], QB), _OFF_GATE:_OFF_GATE + QB]).T
            outs = []
            for r in range(NSA_REPS):
                qsl = slice(r * QB, (r + 1) * QB)
                outs.append(gate[3 * r:3 * r + 1] * o_cmp[b][:, qsl] + gate[3 * r + 1:3 * r + 2] * o_slc[:, qsl]
                            + gate[3 * r + 2:3 * r + 3] * o_win[b][:, qsl])
            o_ref[pl.ds(r0[b], QB), :] = jnp.concatenate(outs, axis=0).T.astype(o_ref.dtype)
    for first in range(0, n_chunks // 2, NSA_GROUP // 2):
        q_blocks([i for p in range(first, first + NSA_GROUP // 2) for i in (p, n_chunks - 1 - p)])


def _nsa(z_nsa, zc, w1a, w1b, w2, pos, gq, gk, ovt, bslc, bwin, batch, seq):
    gw = NSA_GROUP_COLS
    n_chunks = seq // QB
    return pl.pallas_call(
        functools.partial(_nsa_kernel, seq),
        grid=(batch, NSA_KV_HEADS),
        in_specs=[pl.BlockSpec((None, seq, gw), lambda b, g: (b, 0, g)),
                  pl.BlockSpec((None, seq, 2 * HEAD_DIM), lambda b, g: (b, 0, g)),
                  _resident(w1a.shape), _resident(w1b.shape), _resident(w2.shape), _resident(pos.shape),
                  _resident(gq.shape), _resident(gk.shape), _resident(ovt.shape),
                  pl.BlockSpec((None,) + bslc.shape[1:], lambda b, g: (g, 0, 0, 0)),
                  pl.BlockSpec((None,) + bwin.shape[1:], lambda b, g: (g, 0, 0))],
        out_specs=pl.BlockSpec((None, seq, _NQ), lambda b, g: (b, 0, g)),
        out_shape=jax.ShapeDtypeStruct((batch, seq, NSA_HEADS * HEAD_DIM), BF16),
        scratch_shapes=[pltpu.VMEM((seq, HEAD_DIM), BF16),
                        pltpu.VMEM((HEAD_DIM, seq), BF16),
                        pltpu.VMEM((NSA_WIN_PREV * QB + seq, HEAD_DIM), BF16),
                        pltpu.VMEM((NSA_WIN_PREV + n_chunks, HEAD_DIM, QB), BF16)],
        compiler_params=_cparams(2),
        name="nsa",
    )(z_nsa, zc, w1a, w1b, w2, pos, gq, gk, ovt, bslc, bwin)


def _t5_bucket(dist):
    n = jnp.maximum(dist, 0)
    max_exact = NUM_BUCKETS // 2
    nf = jnp.maximum(n, 1).astype(F32)
    large = max_exact + (jnp.log(nf / max_exact) / math.log(MAX_DISTANCE / max_exact)
                         * (NUM_BUCKETS - max_exact)).astype(jnp.int32)
    large = jnp.minimum(large, NUM_BUCKETS - 1)
    return jnp.where(n < max_exact, n, large)


def _toeplitz(u, rows, cols):
    lead = u.shape[:-1]
    lu = rows + cols - 1
    assert u.shape[-1] == lu
    padded = jnp.pad(u, [(0, 0)] * len(lead) + [(0, 1)])
    flat = jnp.broadcast_to(padded[..., None, :], lead + (rows, lu + 1)).reshape(lead + (rows * (lu + 1),))
    return flat[..., :rows * lu].reshape(lead + (rows, lu))[..., rows - 1:]


def _bias_by_distance(bias_cols, delta, valid, dist_scale=1):
    bucket = _t5_bucket(jnp.asarray(np.maximum(delta, 0) * dist_scale, dtype=jnp.int32))
    return jnp.where(jnp.asarray(valid)[None, :], bias_cols.astype(F32)[bucket].T * LOG2E, NEG)


def _band_bias(bias_cols, window, n_prev, dist_scale, n_kv, reps):
    kw = (n_prev + 1) * QB
    delta = n_prev * QB + QB - 1 - np.arange(kw + QB - 1)
    u = _bias_by_distance(bias_cols, delta, (delta >= 0) & (delta < window), dist_scale)
    return _toeplitz(u, QB, kw).reshape(n_kv, reps * QB, kw)


def _slc_bias_t(bias_cols, seq):
    n_chunks = seq // QB
    delta = seq - 1 - np.arange(seq + QB - 1)
    strip = _toeplitz(_bias_by_distance(bias_cols, delta, delta >= 0), QB, seq)
    tile = jnp.flip(strip.reshape(NSA_KV_HEADS, NSA_REPS, QB, n_chunks, QB), axis=3)
    tile = jnp.transpose(tile, (0, 3, 4, 1, 2)).reshape(NSA_KV_HEADS, n_chunks, QB, NSA_REPS * QB)
    return jnp.concatenate([jnp.full_like(tile[:, :1], NEG), tile], axis=1)


def _rope_tables(seq):
    half = MLA_ROPE // 2
    inv = jnp.power(ROPE_THETA, -jnp.arange(half, dtype=F32) / half)
    ang = jnp.arange(seq, dtype=F32)[:, None] * inv[None, :]
    cos, sin = jnp.cos(ang), jnp.sin(ang)
    zeros = lambda w: jnp.zeros((seq, w), F32)
    tail = MLA_PAD - MLA_QK_DIM
    cos_t = jnp.concatenate([jnp.ones((seq, MLA_NOPE), F32), cos, cos, zeros(tail)], axis=1)
    sin_t = jnp.concatenate([zeros(MLA_NOPE), -sin, sin, zeros(tail)], axis=1)
    return cos_t, sin_t


def _pad_cols(w, width):
    return jnp.pad(w, ((0, 0), (0, width - w.shape[1])))


def _with_swapped_rope(a):
    half = MLA_ROPE // 2
    return jnp.concatenate([a, a[..., MLA_NOPE + half:], a[..., MLA_NOPE:MLA_NOPE + half]], axis=-1)


def _nsa_column_order():
    g_cols = NSA_KV_HEADS * HEAD_DIM
    q_cols = NSA_HEADS * HEAD_DIM
    order = []
    for g in range(NSA_KV_HEADS):
        cols = list(range(g * _NQ, (g + 1) * _NQ))
        for part in range(2, 6):
            start = q_cols + part * g_cols + g * HEAD_DIM
            cols += list(range(start, start + HEAD_DIM))
        gate0 = q_cols + 6 * g_cols + g * NSA_REPS * 3
        cols += list(range(gate0, gate0 + NSA_REPS * 3))
        order.append(cols)
    return order


def _nsa_cmp_columns():
    q_cols = NSA_HEADS * HEAD_DIM
    g_cols = NSA_KV_HEADS * HEAD_DIM
    cols = []
    for g in range(NSA_KV_HEADS):
        for part in range(2):
            start = q_cols + part * g_cols + g * HEAD_DIM
            cols += list(range(start, start + HEAD_DIM))
    return cols


def _nsa_mixer(z_nsa3, z_cmp, rel_bias, q_norm, k_norm, cmp_pos, cmp_w1, cmp_w2, batch, seq):
    w1 = cmp_w1.reshape(2, 2, NSA_CMP_STRIDE, HEAD_DIM, NSA_CMP_HIDDEN)
    zero1 = jnp.zeros_like(w1[0])
    w1_bd = jnp.concatenate([jnp.concatenate([w1[0], zero1], axis=-1),
                             jnp.concatenate([zero1, w1[1]], axis=-1)], axis=-2)
    zero2 = jnp.zeros_like(cmp_w2[0])
    w2_bd = jnp.concatenate([jnp.concatenate([cmp_w2[0], zero2], axis=-1),
                             jnp.concatenate([zero2, cmp_w2[1]], axis=-1)], axis=-2)
    pos_kv = jnp.concatenate([cmp_pos[0], cmp_pos[1]], axis=-1)
    n_cmp = (seq - NSA_CMP_LEN) // NSA_CMP_STRIDE + 1
    ci = np.arange(NSA_N_CMP_PAD)[:, None] * NSA_CMP_STRIDE
    sj = np.arange(NSA_N_SLC)[None, :] * NSA_SLC_BLOCK
    overlap = ((ci < sj + NSA_SLC_BLOCK) & (ci + NSA_CMP_LEN > sj) & (np.arange(NSA_N_CMP_PAD)[:, None] < n_cmp))
    ovt = jnp.asarray(overlap.T.astype(np.float32))
    nsa_cols = rel_bias[:, NSA_BIAS_COL0:NSA_BIAS_COL0 + NSA_HEADS]
    bslc = _slc_bias_t(nsa_cols, seq)
    bwin = jnp.transpose(_band_bias(nsa_cols, NSA_WINDOW, NSA_WIN_PREV, 1, NSA_KV_HEADS, NSA_REPS), (0, 2, 1))
    return _nsa(z_nsa3, z_cmp.reshape(batch, seq, -1), w1_bd[0].astype(BF16), w1_bd[1].astype(BF16),
                w2_bd.astype(BF16), pos_kv, q_norm.reshape(HEAD_DIM, 1), k_norm.reshape(3, 1, HEAD_DIM),
                ovt, bslc, bwin, batch, seq)


def kernel(x, rel_bias, ffn1_norm, ffn1_w_gate, ffn1_w_up, ffn1_w_down, mix_norm, ffn2_norm, ffn2_w_gate,
           ffn2_w_up, ffn2_w_down, ab_w_in, mla_q_a_norm, mla_w_q_b, mla_kv_a_norm, mla_w_kv_b, mla_q_norm,
           mla_k_norm, dil_q_norm, dil_k_norm, ab_w_out, cd_w_in, swa_q_norm, swa_k_norm, swa_sinks,
           nsa_q_norm, nsa_k_norm, nsa_cmp_pos, nsa_cmp_w1, nsa_cmp_w2, cd_w_out):
    batch, seq, _ = x.shape
    n = batch * seq
    assert seq % (16 * QB) == 0 and n % TM == 0 and seq % TM == 0
    bf = lambda a: a.astype(BF16)
    xf = x.reshape(n, D_MODEL)
    ffn1 = (bf(ffn1_w_gate), bf(ffn1_w_up), bf(ffn1_w_down))
    ffn2 = (bf(ffn2_w_gate), bf(ffn2_w_up), bf(ffn2_w_down))

    w_in = ab_w_in[0]
    mla_cols = MLA_Q_RANK + MLA_KV_RANK
    w_krope = _with_swapped_rope(jnp.pad(w_in[:, mla_cols:mla_cols + MLA_ROPE], ((0, 0), (MLA_NOPE, 0))))
    w_mla = jnp.concatenate([w_in[:, :mla_cols], w_krope], axis=1)
    wq = _with_swapped_rope(mla_w_q_b[0].reshape(MLA_Q_RANK, MLA_HEADS, MLA_QK_DIM))
    wq = wq.reshape(MLA_Q_RANK, MLA_HEADS * MLA_PAD)
    wkv = mla_w_kv_b[0].reshape(MLA_KV_RANK, MLA_HEADS, MLA_NOPE + MLA_V)
    wk = _pad_cols(wkv[:, :, :MLA_NOPE].reshape(MLA_KV_RANK * MLA_HEADS, MLA_NOPE), MLA_PAD)
    wk = wk.reshape(MLA_KV_RANK, MLA_HEADS * MLA_PAD)
    wv = wkv[:, :, MLA_NOPE:].reshape(MLA_KV_RANK, MLA_HEADS * MLA_V)
    cos_t, sin_t = _rope_tables(seq)
    q_scale = MLA_QK_DIM ** -0.5 * LOG2E
    xf, z_dil, q_mla, k_mla, vt_mla = _ffn_proj_mla(
        xf, ffn1_norm[0], ffn1, 0, mix_norm[0], bf(w_mla), bf(w_in[:, mla_cols + MLA_ROPE:]),
        mla_q_a_norm[0].reshape(1, -1), mla_kv_a_norm[0].reshape(1, -1), bf(wq), bf(wk), bf(wv),
        _with_swapped_rope(mla_q_norm[0].reshape(1, -1)), _with_swapped_rope(mla_k_norm[0].reshape(1, -1)),
        cos_t * q_scale, sin_t * q_scale, cos_t, sin_t, seq)
    o_a = _mla_attn(q_mla.reshape(batch, seq, -1), k_mla.reshape(batch, seq, -1),
                    vt_mla.reshape(batch, seq // MLA_TQ, MLA_HEADS * MLA_V, MLA_TQ),
                    batch, seq).reshape(n, MLA_HEADS * MLA_V)

    z_dil3 = z_dil.reshape(batch, seq, -1)
    gq, gk = dil_q_norm[0], dil_k_norm[0]
    dil_outs, dil_lses = [], []
    for grp, (window, dilation) in enumerate(DIL_PATTERNS):
        bm = _band_bias(rel_bias[:, grp * DIL_HPG:(grp + 1) * DIL_HPG], window // dilation + 1, 1, dilation,
                        DIL_HPG, 1)
        o, lse = _dilated_group(z_dil3, batch, seq, grp, dilation, gq, gk, bm)
        dil_outs.append(o)
        dil_lses.append(lse)
    w_out = ab_w_out[0]
    xf = _out_ab_ffn(xf, o_a, dil_outs, dil_lses, bf(w_out[:MLA_HEADS * MLA_V]), bf(w_out[MLA_HEADS * MLA_V:]),
                     ffn2_norm[0], ffn2, 0)

    w_in = cd_w_in[0]
    swa_cols = (SWA_HEADS + 2 * SWA_KV_HEADS) * HEAD_DIM
    w_nsa_src = w_in[:, swa_cols:]
    w_nsa = jnp.concatenate([_pad_cols(w_nsa_src[:, np.asarray(cols)], NSA_GROUP_COLS)
                             for cols in _nsa_column_order()], axis=1)
    w_cmp = w_nsa_src[:, np.asarray(_nsa_cmp_columns())]
    xf, z_swa, z_nsa, z_cmp = _ffn_proj(xf, ffn1_norm[1], ffn1, 1, mix_norm[1],
                                        [bf(w_in[:, :swa_cols]), bf(w_nsa), bf(w_cmp)])

    swa_reps = SWA_HEADS // SWA_KV_HEADS
    swa_prev = -(-(SWA_WINDOW - 1) // QB)
    bm_swa = _band_bias(rel_bias[:, :SWA_HEADS], SWA_WINDOW, swa_prev, 1, SWA_KV_HEADS, swa_reps)
    sinks = jnp.broadcast_to((swa_sinks[0].astype(F32) * LOG2E).reshape(SWA_KV_HEADS, 1, swa_reps, 1),
                             (SWA_KV_HEADS, 1, swa_reps, QB)).reshape(SWA_KV_HEADS, 1, swa_reps * QB)
    o_c = _swa(z_swa.reshape(batch, seq, -1), batch, seq, swa_q_norm[0].reshape(HEAD_DIM, 1),
               swa_k_norm[0].reshape(1, HEAD_DIM), bm_swa, sinks)

    o_d = _nsa_mixer(z_nsa.reshape(batch, seq, NSA_KV_HEADS * NSA_GROUP_COLS), z_cmp, rel_bias, nsa_q_norm[0],
                     nsa_k_norm[0], nsa_cmp_pos[0], nsa_cmp_w1[0], nsa_cmp_w2[0], batch, seq)

    w_out = cd_w_out[0]
    xf = _out_cd_ffn(xf, o_c.reshape(n, -1), o_d.reshape(n, -1), bf(w_out[:SWA_HEADS * HEAD_DIM]),
                     bf(w_out[SWA_HEADS * HEAD_DIM:]), ffn2_norm[1], ffn2, 1)
    return xf.reshape(batch, seq, D_MODEL)
```

```python
import functools
import math

import numpy as np
import jax
import jax.numpy as jnp
from jax import lax
from jax.experimental import pallas as pl
from jax.experimental.pallas import tpu as pltpu

F32 = jnp.float32
BF16 = jnp.bfloat16

EPS = 1e-6
NEG = -1e30
TINY = 1e-30
LOG2E = math.log2(math.e)
LN2 = math.log(2.0)
D_MODEL = 1024
D_FF = 2816
NUM_BUCKETS = 32
MAX_DISTANCE = 2048
HEAD_DIM = 64
QB = 128

MLA_HEADS = 8
MLA_Q_RANK = 256
MLA_KV_RANK = 128
MLA_NOPE = 64
MLA_ROPE = 32
MLA_V = 64
MLA_QK_DIM = MLA_NOPE + MLA_ROPE
MLA_PAD = 128
ROPE_THETA = 10000.0

DIL_PATTERNS = ((128, 1), (512, 4), (2048, 16))
DIL_HPG = 4
SWA_HEADS = 8
SWA_KV_HEADS = 2
SWA_WINDOW = 128
NSA_HEADS = 8
NSA_KV_HEADS = 2
NSA_CMP_LEN = 32
NSA_CMP_STRIDE = 16
NSA_CMP_HIDDEN = 128
NSA_SLC_BLOCK = 64
NSA_TOP_N = 16
NSA_WINDOW = 512
NSA_FORCE = 1e6
NSA_BIAS_COL0 = 8
NSA_GROUP_COLS = 640

VMEM_LIMIT = 56 * 1024 * 1024
TM = 512

_NT = (((1,), (1,)), ((), ()))
_HI = lax.Precision.HIGHEST


def _cparams(n_axes):
    return pltpu.CompilerParams(dimension_semantics=("arbitrary",) * n_axes,
                                vmem_limit_bytes=VMEM_LIMIT)


def _resident(shape):
    nd = len(shape)
    return pl.BlockSpec(shape, lambda *_: (0,) * nd, pipeline_mode=pl.Buffered(1))


def _layer_block(w, layer):
    nd = w.ndim - 1
    return pl.BlockSpec((None,) + w.shape[1:], lambda *_: (layer,) + (0,) * nd, pipeline_mode=pl.Buffered(1))


def _rms(x):
    return x * lax.rsqrt(jnp.mean(x * x, axis=-1, keepdims=True) + EPS)


def _dot(a, b):
    return jnp.dot(a, b, preferred_element_type=F32)


def _dot_nt(a, b, precision=None):
    return lax.dot_general(a, b, _NT, preferred_element_type=F32, precision=precision)


FF_CHUNK = 256


N_CHUNK = 512


def _swiglu_half_step(x, g_ref, wg_ref, wu_ref, wd_ref):
    hb = (_rms(x) * g_ref[...]).astype(BF16)
    acc = jnp.zeros(x.shape, F32)
    for c in range(D_FF // FF_CHUNK):
        sl = slice(c * FF_CHUNK, (c + 1) * FF_CHUNK)
        gate = _dot(hb, wg_ref[:, sl])
        up = _dot(hb, wu_ref[:, sl])
        act = (gate * jax.nn.sigmoid(gate) * up).astype(BF16)
        acc = acc + _dot(act, wd_ref[sl, :])
    return x + 0.5 * acc


def _ffn_proj_kernel(n_out, x_ref, g_ref, wg_ref, wu_ref, wd_ref, gm_ref, *refs):
    w_refs, x_out_ref, z_refs = refs[:n_out], refs[n_out], refs[n_out + 1:]
    x = _swiglu_half_step(x_ref[...], g_ref, wg_ref, wu_ref, wd_ref)
    x_out_ref[...] = x
    hb = (_rms(x) * gm_ref[...]).astype(BF16)
    for w_ref, z_ref in zip(w_refs, z_refs):
        width = w_ref.shape[1]
        for c0 in range(0, width, N_CHUNK):
            sl = slice(c0, min(c0 + N_CHUNK, width))
            z_ref[:, sl] = _dot(hb, w_ref[:, sl]).astype(z_ref.dtype)


def _tile(width):
    return pl.BlockSpec((TM, width), lambda i: (i, 0))


def _ffn_proj(x, g, ffn, layer, g_mix, ws, z_dtypes=None):
    n = x.shape[0]
    z_dtypes = z_dtypes or [F32] * len(ws)
    ffn_w = [_layer_block(w, layer) for w in ffn]
    return pl.pallas_call(
        functools.partial(_ffn_proj_kernel, len(ws)),
        grid=(n // TM,),
        in_specs=[_tile(D_MODEL), _resident((1, D_MODEL))] + ffn_w + [_resident((1, D_MODEL))]
        + [_resident(w.shape) for w in ws],
        out_specs=[_tile(D_MODEL)] + [_tile(w.shape[1]) for w in ws],
        out_shape=[jax.ShapeDtypeStruct((n, D_MODEL), F32)]
        + [jax.ShapeDtypeStruct((n, w.shape[1]), dt) for w, dt in zip(ws, z_dtypes)],
        compiler_params=_cparams(1),
        name="ffn_proj",
    )(x, g.reshape(1, D_MODEL), *ffn, g_mix.reshape(1, D_MODEL), *ws)


def _out_ab_ffn_kernel(x_ref, oa_ref, o0_ref, o1_ref, o2_ref, l0_ref, l1_ref, l2_ref, wa_ref, wb_ref,
                       g_ref, wg_ref, wu_ref, wd_ref, out_ref):
    l0, l1, l2 = l0_ref[...], l1_ref[...], l2_ref[...]
    m = jnp.maximum(jnp.maximum(l0, l1), l2)
    e0, e1, e2 = jnp.exp(l0 - m), jnp.exp(l1 - m), jnp.exp(l2 - m)
    ob = (e0 * o0_ref[...] + e1 * o1_ref[...] + e2 * o2_ref[...]) / (e0 + e1 + e2)
    x = x_ref[...] + _dot(oa_ref[...], wa_ref[...]) + _dot(ob.astype(BF16), wb_ref[...])
    out_ref[...] = _swiglu_half_step(x, g_ref, wg_ref, wu_ref, wd_ref)


def _out_ab_ffn(x, oa, dil_outs, dil_lses, wa, wb, g, ffn, layer):
    n = x.shape[0]
    dw = DIL_HPG * HEAD_DIM
    ffn_w = [_layer_block(w, layer) for w in ffn]
    return pl.pallas_call(
        _out_ab_ffn_kernel,
        grid=(n // TM,),
        in_specs=[_tile(D_MODEL), _tile(oa.shape[1])] + [_tile(dw)] * 6 + [_resident(wa.shape), _resident(wb.shape),
                                                                        _resident((1, D_MODEL))] + ffn_w,
        out_specs=_tile(D_MODEL),
        out_shape=jax.ShapeDtypeStruct((n, D_MODEL), F32),
        compiler_params=_cparams(1),
        name="out_ab_ffn",
    )(x, oa, *dil_outs, *dil_lses, wa, wb, g.reshape(1, D_MODEL), *ffn)


def _out_cd_ffn_kernel(x_ref, oc_ref, od_ref, wc_ref, wdd_ref, g_ref, wg_ref, wu_ref, wd_ref, out_ref):
    x = x_ref[...] + _dot(oc_ref[...], wc_ref[...]) + _dot(od_ref[...], wdd_ref[...])
    out_ref[...] = _swiglu_half_step(x, g_ref, wg_ref, wu_ref, wd_ref)


def _out_cd_ffn(x, oc, od, wc, wdd, g, ffn, layer):
    n = x.shape[0]
    ffn_w = [_layer_block(w, layer) for w in ffn]
    return pl.pallas_call(
        _out_cd_ffn_kernel,
        grid=(n // TM,),
        in_specs=[_tile(D_MODEL), _tile(oc.shape[1]), _tile(od.shape[1]), _resident(wc.shape), _resident(wdd.shape),
                  _resident((1, D_MODEL))] + ffn_w,
        out_specs=_tile(D_MODEL),
        out_shape=jax.ShapeDtypeStruct((n, D_MODEL), F32),
        compiler_params=_cparams(1),
        name="out_cd_ffn",
    )(x, oc, od, wc, wdd, g.reshape(1, D_MODEL), *ffn)


MLA_TQ = 256
MLA_STEP = 4


def _ffn_proj_mla_kernel(x_ref, g_ref, wg_ref, wu_ref, wd_ref, gm_ref, wmla_ref, wdil_ref,
                         qan_ref, kvan_ref, wq_ref, wk_ref, wv_ref, gq_ref, gk_ref,
                         qc_ref, qs_ref, kc_ref, ks_ref,
                         x_out_ref, zdil_ref, q_ref, k_ref, vt_ref, zprev_ref):
    @pl.when(pl.program_id(0) == 0)
    def _():
        zprev_ref[...] = jnp.zeros(zprev_ref.shape, F32)

    cq = (_rms(zprev_ref[:, :MLA_Q_RANK]) * qan_ref[...]).astype(BF16)
    ckv = (_rms(zprev_ref[:, MLA_Q_RANK:MLA_Q_RANK + MLA_KV_RANK]) * kvan_ref[...]).astype(BF16)
    kr = zprev_ref[:, MLA_Q_RANK + MLA_KV_RANK:]
    real = jnp.where(lax.broadcasted_iota(jnp.int32, (1, MLA_PAD), 1) < MLA_QK_DIM, 1.0, 0.0)

    def head_norm_rope(x, gain, cos, sin):
        ssq = jnp.sum(x * x * real, axis=-1, keepdims=True)
        x = x * lax.rsqrt(ssq * (1.0 / MLA_QK_DIM) + EPS) * gain
        return x * cos + pltpu.roll(x, MLA_PAD - MLA_ROPE, 1) * sin

    def head(h):
        sl = slice(h * MLA_PAD, (h + 1) * MLA_PAD)
        qh = head_norm_rope(_dot(cq, wq_ref[:, sl]), gq_ref[...], qc_ref[...], qs_ref[...])
        q_ref[:, sl] = qh.astype(BF16)
        kh = head_norm_rope(_dot(ckv, wk_ref[:, sl]) + kr, gk_ref[...], kc_ref[...], ks_ref[...])
        k_ref[:, sl] = kh.astype(BF16)

    def values():
        v = _dot(ckv, wv_ref[...])
        for c in range(TM // MLA_TQ):
            vt_ref[c] = v[c * MLA_TQ:(c + 1) * MLA_TQ].T.astype(BF16)

    latent_work = [functools.partial(head, h) for h in range(MLA_HEADS)] + [values]

    x = x_ref[...]
    hb = (_rms(x) * g_ref[...]).astype(BF16)
    acc = jnp.zeros(x.shape, F32)
    for c in range(D_FF // FF_CHUNK):
        sl = slice(c * FF_CHUNK, (c + 1) * FF_CHUNK)
        gate = _dot(hb, wg_ref[:, sl])
        up = _dot(hb, wu_ref[:, sl])
        act = (gate * jax.nn.sigmoid(gate) * up).astype(BF16)
        acc = acc + _dot(act, wd_ref[sl, :])
        if c < len(latent_work):
            latent_work[c]()
    x = x + 0.5 * acc
    x_out_ref[...] = x
    hm = (_rms(x) * gm_ref[...]).astype(BF16)
    zprev_ref[...] = _dot(hm, wmla_ref[...])
    width = wdil_ref.shape[1]
    for c0 in range(0, width, N_CHUNK):
        sl = slice(c0, min(c0 + N_CHUNK, width))
        zdil_ref[:, sl] = _dot(hm, wdil_ref[:, sl])


def _ffn_proj_mla(x, g, ffn, layer, g_mix, w_mla, w_dil, qan, kvan, wq, wk, wv, gq, gk,
                  q_cos, q_sin, k_cos, k_sin, seq):
    n = x.shape[0]
    n_tiles = n // TM
    assert D_FF // FF_CHUNK >= MLA_HEADS + 1
    per_seq = seq // TM
    hw = MLA_HEADS * MLA_PAD
    vw = MLA_HEADS * MLA_V

    def cur(width):
        return pl.BlockSpec((TM, width), lambda i: (jnp.minimum(i, n_tiles - 1), 0))

    def prev(width):
        return pl.BlockSpec((TM, width), lambda i: (jnp.maximum(i - 1, 0), 0))

    rope_spec = pl.BlockSpec((TM, MLA_PAD), lambda i: (jnp.maximum(i - 1, 0) % per_seq, 0))
    small = [qan, kvan, wq, wk, wv, gq, gk]
    return pl.pallas_call(
        _ffn_proj_mla_kernel,
        grid=(n_tiles + 1,),
        in_specs=[cur(D_MODEL), _resident((1, D_MODEL))] + [_layer_block(w, layer) for w in ffn]
        + [_resident((1, D_MODEL)), _resident(w_mla.shape), _resident(w_dil.shape)]
        + [_resident(a.shape) for a in small] + [rope_spec] * 4,
        out_specs=[cur(D_MODEL), cur(w_dil.shape[1]), prev(hw), prev(hw),
                   pl.BlockSpec((TM // MLA_TQ, vw, MLA_TQ), lambda i: (jnp.maximum(i - 1, 0), 0, 0))],
        out_shape=[jax.ShapeDtypeStruct((n, D_MODEL), F32), jax.ShapeDtypeStruct((n, w_dil.shape[1]), F32),
                   jax.ShapeDtypeStruct((n, hw), BF16), jax.ShapeDtypeStruct((n, hw), BF16),
                   jax.ShapeDtypeStruct((n // MLA_TQ, vw, MLA_TQ), BF16)],
        scratch_shapes=[pltpu.VMEM((TM, w_mla.shape[1]), F32)],
        compiler_params=_cparams(1),
        name="ffn_proj_mla",
    )(x, g.reshape(1, D_MODEL), *ffn, g_mix.reshape(1, D_MODEL), w_mla, w_dil, *small,
      q_cos, q_sin, k_cos, k_sin)


def _flash_step_t(s, m, l, acc, pv, vt):
    m_new = jnp.maximum(m, jnp.max(s, axis=0, keepdims=True))
    p = jnp.exp2(s - m_new)
    alpha = jnp.exp2(m - m_new)
    l = alpha * l + jnp.sum(p, axis=0, keepdims=True)
    acc = alpha * (acc + pv)
    return m_new, l, acc, _dot(vt, p.astype(BF16))


def _flash_init(dv, n_queries):
    return (jnp.full((1, n_queries), NEG, F32), jnp.zeros((1, n_queries), F32), jnp.zeros((dv, n_queries), F32),
            jnp.zeros((dv, n_queries), F32))


def _flash_finish(m, l, acc, pv):
    return (acc + pv) * (1.0 / l)


def _mla_attn_kernel(seq, q_ref, k_ref, vt_ref, o_ref):
    t = MLA_TQ
    key = lax.broadcasted_iota(jnp.int32, (t, t), 0)
    qry = lax.broadcasted_iota(jnp.int32, (t, t), 1)
    diag_ok = key <= qry
    heads = [(slice(hh * MLA_PAD, (hh + 1) * MLA_PAD), slice(hh * MLA_V, (hh + 1) * MLA_V)) for hh in range(2)]

    init = _flash_init(MLA_V, t)
    n_blocks = seq // t
    for first in range(n_blocks // 2):
        blocks = [first, n_blocks - 1 - first]
        chains = [(i, h) for i in blocks for h in range(2)]
        qs = {(i, h): q_ref[i * t:(i + 1) * t, heads[h][0]] for i, h in chains}
        steps = {i: [(j, min(j + MLA_STEP, i + 1)) for j in range(0, i + 1, MLA_STEP)] for i in blocks}

        def scores(chain, step):
            i, h = chain
            lo, hi = steps[i][step]
            s = _dot_nt(k_ref[lo * t:hi * t, heads[h][0]], qs[chain])
            if hi == i + 1:
                last = jnp.where(diag_ok, s[(i - lo) * t:], NEG)
                s = last if i == lo else jnp.concatenate([s[:(i - lo) * t], last], axis=0)
            return s

        def values(chain, step):
            i, h = chain
            lo, hi = steps[i][step]
            return jnp.concatenate([vt_ref[j, heads[h][1], :] for j in range(lo, hi)], axis=1)

        cur = {c: scores(c, 0) for c in chains}
        states = {c: init for c in chains}
        for step in range(max(len(s) for s in steps.values())):
            for c in chains:
                n_steps = len(steps[c[0]])
                if step >= n_steps:
                    continue
                nxt = scores(c, step + 1) if step + 1 < n_steps else None
                states[c] = _flash_step_t(cur[c], *states[c], values(c, step))
                cur[c] = nxt
        for i in blocks:
            outs = [_flash_finish(*states[(i, h)]) for h in range(2)]
            o_ref[i * t:(i + 1) * t, :] = jnp.concatenate(outs, axis=0).T.astype(o_ref.dtype)


def _mla_attn(q, k, vt, batch, seq):
    return pl.pallas_call(
        functools.partial(_mla_attn_kernel, seq),
        grid=(batch, MLA_HEADS // 2),
        in_specs=[pl.BlockSpec((None, seq, 2 * MLA_PAD), lambda b, h: (b, 0, h)),
                  pl.BlockSpec((None, seq, 2 * MLA_PAD), lambda b, h: (b, 0, h)),
                  pl.BlockSpec((None, seq // MLA_TQ, 2 * MLA_V, MLA_TQ), lambda b, h: (b, 0, h, 0))],
        out_specs=pl.BlockSpec((None, seq, 2 * MLA_V), lambda b, h: (b, 0, h)),
        out_shape=jax.ShapeDtypeStruct((batch, seq, MLA_HEADS * MLA_V), BF16),
        compiler_params=_cparams(2),
        name="mla_attn",
    )(q, k, vt)


def _head_norm(x, gain):
    return _rms(x) * gain


def _heads_t(qt, n_heads, gain):
    tiles = []
    for c0 in range(0, n_heads * HEAD_DIM, 2 * HEAD_DIM):
        pair = qt[:, c0:c0 + 2 * HEAD_DIM].T
        for h in range(2):
            x = pair[h * HEAD_DIM:(h + 1) * HEAD_DIM]
            tiles.append(x * lax.rsqrt(jnp.mean(x * x, axis=0, keepdims=True) + EPS) * gain)
    return tiles


def _q_gain(gq_ref):
    return jnp.broadcast_to(gq_ref[...] * (HEAD_DIM ** -0.5 * LOG2E), (HEAD_DIM, QB))


def _pair_ones():
    shape = (2 * HEAD_DIM, 2 * HEAD_DIM)
    same = lax.broadcasted_iota(jnp.int32, shape, 0) // HEAD_DIM == lax.broadcasted_iota(jnp.int32, shape, 1) // HEAD_DIM
    return jnp.where(same, 1.0, 0.0).astype(BF16)


def _pair_norm(x, seg, gain):
    sq = x * x
    hi = sq.astype(BF16)
    lo = (sq - hi.astype(F32)).astype(BF16)
    ssq = _dot(hi, seg) + _dot(lo, seg)
    return x * lax.rsqrt(ssq * (1.0 / HEAD_DIM) + EPS) * gain


BAND_KV = 2


def _banded_kernel(seq, dilation, reps, n_prev, with_sink, with_lse, group_size, paired, *refs):
    q_ref, k_ref, v_ref, gq_ref, gk_ref, bm_ref = refs[:6]
    refs = refs[6:]
    sink_ref = None
    if with_sink:
        sink_ref, refs = refs[0], refs[1:]
    o_ref, refs = refs[0], refs[1:]
    lse_ref = None
    if with_lse:
        lse_ref, refs = refs[0], refs[1:]
    kn_ref, vt_ref = refs
    length = seq // dilation
    n_blocks = length // QB
    pad = n_prev * QB
    kw = pad + QB
    win_row = lax.broadcasted_iota(jnp.int32, (kw, 1), 0)
    assert not paired or reps == 1
    if paired:
        gq, gk = gq_ref[...] * (HEAD_DIM ** -0.5 * LOG2E), gk_ref[...]
        seg = _pair_ones()
        lane = lax.broadcasted_iota(jnp.int32, (1, BAND_KV * HEAD_DIM), 1)
        head_lanes = [jnp.where(lane // HEAD_DIM == g, 1.0, 0.0) for g in range(BAND_KV)]
    else:
        gq, gk = _q_gain(gq_ref), gk_ref[...]

    def rows(start, size):
        return pl.ds(start, size) if dilation == 1 else pl.ds(start, size, stride=dilation)

    for res in range(dilation):
        kall, vall = k_ref[rows(res, length), :], v_ref[rows(res, length), :]
        if paired:
            kn_ref[res, :pad, :] = jnp.zeros((pad, BAND_KV * HEAD_DIM), BF16)
            kn_ref[res, pad:, :] = _pair_norm(kall, seg, gk).astype(BF16)
        else:
            for g in range(BAND_KV):
                kn_ref[res, g, :pad, :] = jnp.zeros((pad, HEAD_DIM), BF16)
                kn_ref[res, g, pad:, :] = _head_norm(kall[:, g * HEAD_DIM:(g + 1) * HEAD_DIM], gk).astype(BF16)
        for j in range(n_prev):
            vt_ref[res, j] = jnp.zeros((BAND_KV * HEAD_DIM, QB), BF16)
        for j in range(n_blocks):
            vt_ref[res, n_prev + j] = vall[j * QB:(j + 1) * QB].T.astype(BF16)

    def attend(res, i, g, s):
        bias = bm_ref[g]
        if i < n_prev:
            bias = jnp.where(win_row >= (n_prev - i) * QB, bias, NEG)
        s = s + bias
        m = jnp.max(s, axis=0, keepdims=True)
        if with_sink:
            m = jnp.maximum(m, sink_ref[g])
        p = jnp.exp2(s - m)
        l = jnp.sum(p, axis=0, keepdims=True)
        if with_sink:
            l = l + jnp.exp2(sink_ref[g] - m)
        p = p.astype(BF16)
        vl = slice(g * HEAD_DIM, (g + 1) * HEAD_DIM)
        o = _dot(vt_ref[res, i, vl, :], p[0:QB])
        for w in range(1, n_prev + 1):
            o = o + _dot(vt_ref[res, i + w, vl, :], p[w * QB:(w + 1) * QB])
        return o * (1.0 / l), m * LN2 + jnp.log(l)

    items = [(res, i) for res in range(dilation) for i in range(n_blocks)]
    for at in range(0, len(items), group_size):
        group = items[at:at + group_size]
        toks = [rows(res + i * QB * dilation, QB) for res, i in group]
        if paired:
            qns = [_pair_norm(q_ref[tok, :], seg, gq) for tok in toks]
            qss = [[(qn * head_lanes[g]).astype(BF16) for g in range(BAND_KV)] for qn in qns]
            scores = [[_dot_nt(kn_ref[res, i * QB:i * QB + kw, :], qs[g]) for g in range(BAND_KV)]
                      for (res, i), qs in zip(group, qss)]
        else:
            heads = [_heads_t(q_ref[tok, :], BAND_KV * reps, gq) for tok in toks]
            qss = [[jnp.concatenate(h[g * reps:(g + 1) * reps], axis=1).astype(BF16) for g in range(BAND_KV)]
                   for h in heads]
            scores = [[_dot(kn_ref[res, g, i * QB:i * QB + kw, :], qs[g]) for g in range(BAND_KV)]
                      for (res, i), qs in zip(group, qss)]
        for (res, i), tok, sc in zip(group, toks, scores):
            outs, lses = [], []
            for g in range(BAND_KV):
                o, lse = attend(res, i, g, sc[g])
                for r in range(reps):
                    outs.append(o[:, r * QB:(r + 1) * QB])
                    if with_lse:
                        lses.append(jnp.broadcast_to(lse[:, r * QB:(r + 1) * QB], (HEAD_DIM, QB)))
            o_ref[tok, :] = jnp.concatenate(outs, axis=0).T.astype(o_ref.dtype)
            if with_lse:
                lse_ref[tok, :] = jnp.concatenate(lses, axis=0).T


def _banded_scratch(seq, dilation, n_prev, paired):
    length = seq // dilation
    rows = n_prev * QB + length
    keys = (dilation, rows, BAND_KV * HEAD_DIM) if paired else (dilation, BAND_KV, rows, HEAD_DIM)
    return [pltpu.VMEM(keys, BF16),
            pltpu.VMEM((dilation, n_prev + length // QB, BAND_KV * HEAD_DIM, QB), BF16)]


PAIRED_MIN_DILATION = 16


def _dilated_group(z_dil, batch, seq, group, dilation, q_norm, k_norm, bm):
    gw = DIL_HPG * HEAD_DIM
    n_groups = len(DIL_PATTERNS)
    hps = BAND_KV
    halves = DIL_HPG // hps
    bm_t = jnp.transpose(bm, (0, 2, 1))
    paired = dilation >= PAIRED_MIN_DILATION
    if paired:
        gq = jnp.tile(q_norm.reshape(1, HEAD_DIM), (1, BAND_KV))
        gk = jnp.tile(k_norm.reshape(1, HEAD_DIM), (1, BAND_KV))
    else:
        gq, gk = q_norm.reshape(HEAD_DIM, 1), k_norm.reshape(1, HEAD_DIM)

    def spec(part):
        return pl.BlockSpec((None, seq, hps * HEAD_DIM), lambda b, h: (b, 0, (part * n_groups + group) * halves + h))

    out_spec = pl.BlockSpec((None, seq, hps * HEAD_DIM), lambda b, h: (b, 0, h))
    shape = jax.ShapeDtypeStruct((batch, seq, gw), F32)
    out, lse = pl.pallas_call(
        functools.partial(_banded_kernel, seq, dilation, 1, 1, False, True, 16, paired),
        grid=(batch, halves),
        in_specs=[spec(0), spec(1), spec(2), _resident(gq.shape), _resident(gk.shape),
                  pl.BlockSpec((hps,) + bm_t.shape[1:], lambda b, h: (h, 0, 0))],
        out_specs=[out_spec, out_spec],
        out_shape=[shape, shape],
        scratch_shapes=_banded_scratch(seq, dilation, 1, paired),
        compiler_params=_cparams(2),
        name="dilated_g%d" % group,
    )(z_dil, z_dil, z_dil, gq, gk, bm_t)
    return out.reshape(batch * seq, gw), lse.reshape(batch * seq, gw)


def _swa(z_swa, batch, seq, gq, gk, bm, sinks):
    qw = SWA_HEADS * HEAD_DIM
    kvw = SWA_KV_HEADS * HEAD_DIM
    reps = SWA_HEADS // SWA_KV_HEADS
    n_prev = -(-(SWA_WINDOW - 1) // QB)
    assert SWA_KV_HEADS == BAND_KV
    bm_t = jnp.transpose(bm, (0, 2, 1))
    return pl.pallas_call(
        functools.partial(_banded_kernel, seq, 1, reps, n_prev, True, False, 4, False),
        grid=(batch,),
        in_specs=[pl.BlockSpec((None, seq, qw), lambda b: (b, 0, 0)),
                  pl.BlockSpec((None, seq, kvw), lambda b: (b, 0, qw // kvw)),
                  pl.BlockSpec((None, seq, kvw), lambda b: (b, 0, qw // kvw + 1)),
                  _resident(gq.shape), _resident(gk.shape), _resident(bm_t.shape), _resident(sinks.shape)],
        out_specs=pl.BlockSpec((None, seq, qw), lambda b: (b, 0, 0)),
        out_shape=jax.ShapeDtypeStruct((batch, seq, qw), BF16),
        scratch_shapes=_banded_scratch(seq, 1, n_prev, False),
        compiler_params=_cparams(1),
        name="swa",
    )(z_swa, z_swa, z_swa, gq, gk, bm_t, sinks)


NSA_REPS = NSA_HEADS // NSA_KV_HEADS
NSA_N_SLC = 32
NSA_N_CMP_PAD = 128
NSA_WIN_PREV = -(-(NSA_WINDOW - 1) // QB)
_NQ = NSA_REPS * HEAD_DIM
_OFF_KS, _OFF_VS, _OFF_KW, _OFF_VW, _OFF_GATE = (_NQ + i * HEAD_DIM for i in range(5))


NSA_STEP = 16
NSA_GROUP = 4


def _nsa_kernel(seq, z_ref, zc_ref, w1a_ref, w1b_ref, w2_ref, pos_ref, gq_ref, gk_ref, ovt_ref,
                bslc_ref, bwin_ref, o_ref, ks_ref, vst_ref, kw_ref, vwt_ref):
    n_chunks = seq // QB
    lanes = NSA_REPS * QB
    gq = _q_gain(gq_ref)
    win_pad = NSA_WIN_PREV * QB
    win_kw = win_pad + QB

    first = jnp.zeros((NSA_N_CMP_PAD, 2 * NSA_CMP_HIDDEN), F32)
    second = jnp.zeros((NSA_N_CMP_PAD, 2 * NSA_CMP_HIDDEN), F32)
    for l in range(NSA_CMP_STRIDE):
        tok = zc_ref[pl.ds(l, NSA_N_CMP_PAD, stride=NSA_CMP_STRIDE), :]
        first = first + _dot((tok + pos_ref[l:l + 1, :]).astype(BF16), w1a_ref[l])
        second = second + _dot((tok + pos_ref[NSA_CMP_STRIDE + l:NSA_CMP_STRIDE + l + 1, :]).astype(BF16),
                               w1b_ref[l])
    hidden = jax.nn.gelu(first + pltpu.roll(second, NSA_N_CMP_PAD - 1, 0)).astype(BF16)
    cmp_kv = _dot(hidden, w2_ref[...])
    kc = _head_norm(cmp_kv[:, :HEAD_DIM], gk_ref[0]).astype(BF16)
    vct = cmp_kv.T[HEAD_DIM:].astype(BF16)

    ks_ref[...] = _head_norm(z_ref[:, _OFF_KS:_OFF_KS + HEAD_DIM], gk_ref[1]).astype(BF16)
    kw_ref[:win_pad, :] = jnp.zeros((win_pad, HEAD_DIM), BF16)
    kw_ref[win_pad:, :] = _head_norm(z_ref[:, _OFF_KW:_OFF_KW + HEAD_DIM], gk_ref[2]).astype(BF16)
    for j in range(NSA_WIN_PREV):
        vwt_ref[j] = jnp.zeros((HEAD_DIM, QB), BF16)
    for j in range(n_chunks):
        rows = slice(j * QB, (j + 1) * QB)
        vst_ref[:, rows] = z_ref[rows, _OFF_KS:_OFF_KS + 2 * HEAD_DIM].T[HEAD_DIM:].astype(BF16)
        vwt_ref[NSA_WIN_PREV + j] = z_ref[rows, _OFF_KW:_OFF_KW + 2 * HEAD_DIM].T[HEAD_DIM:].astype(BF16)

    cmp_id = lax.broadcasted_iota(jnp.int32, (NSA_N_CMP_PAD, lanes), 0)
    cmp_end = cmp_id * NSA_CMP_STRIDE + (NSA_CMP_LEN - 1)
    cmp_real = cmp_id < NSA_N_CMP_PAD - 1
    q_in_blk = lax.broadcasted_iota(jnp.int32, (NSA_N_CMP_PAD, lanes), 1) & (QB - 1)
    blk_id = lax.broadcasted_iota(jnp.int32, (NSA_N_SLC, QB), 0)
    q_lane = lax.broadcasted_iota(jnp.int32, (NSA_N_SLC, QB), 1)
    win_row = lax.broadcasted_iota(jnp.int32, (win_kw, 1), 0)
    init = _flash_init(HEAD_DIM, lanes)

    def q_blocks(blk):
        both = range(len(blk))
        n_keys = [i + 1 for i in blk]
        mask_pad = [i < NSA_WIN_PREV for i in blk]
        step = [max(d for d in range(1, NSA_STEP + 1) if n % d == 0) for n in n_keys]
        step_keys = [s * QB for s in step]
        blocks_per_step = [s // NSA_SLC_BLOCK for s in step_keys]
        r0 = [i * QB for i in blk]
        qs = [jnp.concatenate(_heads_t(z_ref[pl.ds(r, QB), :_NQ], NSA_REPS, gq), axis=1).astype(BF16)
              for r in r0]

        sc = [_dot(kc, q) for q in qs]
        s_win = [_dot(kw_ref[pl.ds(r, win_kw), :], q) for r, q in zip(r0, qs)]

        def slc_scores(b, j):
            return _dot(ks_ref[j * step_keys[b]:(j + 1) * step_keys[b], :], qs[b])

        o_cmp, imp = [], []
        for b in both:
            ok = (cmp_end <= q_in_blk + blk[b] * QB) & cmp_real
            s = jnp.where(ok, sc[b], NEG)
            e = jnp.where(ok, jnp.exp2(s - jnp.max(s, axis=0, keepdims=True)), 0.0)
            p = e * (1.0 / jnp.maximum(jnp.sum(e, axis=0, keepdims=True), TINY))
            o_cmp.append(_dot(vct, p.astype(BF16)))
            p_sum = p[:, 0:QB] + p[:, QB:2 * QB] + p[:, 2 * QB:3 * QB] + p[:, 3 * QB:4 * QB]
            imp.append(jnp.dot(ovt_ref[...], p_sum, precision=_HI, preferred_element_type=F32))

        o_win = []
        for b in both:
            bias = bwin_ref[...]
            if mask_pad[b]:
                bias = jnp.where(win_row >= (NSA_WIN_PREV - blk[b]) * QB, bias, NEG)
            s = s_win[b] + bias
            pw = jnp.exp2(s - jnp.max(s, axis=0, keepdims=True))
            lw = jnp.sum(pw, axis=0, keepdims=True)
            pw = pw.astype(BF16)
            o = _dot(vwt_ref[blk[b]], pw[0:QB])
            for c in range(1, NSA_WIN_PREV + 1):
                o = o + _dot(vwt_ref[blk[b] + c], pw[c * QB:(c + 1) * QB])
            o_win.append(o * (1.0 / lw))

        drop = []
        for b in both:
            tb = lax.shift_right_logical(q_lane + blk[b] * QB, 6)
            forced = (blk_id == 0) | (blk_id == tb) | (blk_id == tb - 1)
            score = jnp.where(blk_id <= tb, imp[b] + jnp.where(forced, NSA_FORCE, 0.0), -NSA_FORCE)
            rank = jnp.zeros((NSA_N_SLC, QB), F32)
            for other in range(NSA_N_SLC):
                s_o = score[other:other + 1, :]
                beats = (s_o > score) | ((s_o == score) & (blk_id > other))
                rank = rank + jnp.where(beats, 1.0, 0.0)
            drop.append(jnp.where(rank < NSA_TOP_N, 0.0, NEG))

        states = [init for _ in both]
        n_steps = [n // s for n, s in zip(n_keys, step)]
        work = [(b, j) for j in range(max(n_steps)) for b in both if j < n_steps[b]]
        nxt = slc_scores(*work[0])
        for at, (b, j) in enumerate(work):
            cur = nxt
            nxt = slc_scores(*work[at + 1]) if at + 1 < len(work) else None
            bias = jnp.concatenate(
                [bslc_ref[blk[b] - (j * step[b] + c) + 1] for c in range(step[b])], axis=0)
            km = jnp.concatenate([jnp.broadcast_to(drop[b][k:k + 1], (NSA_SLC_BLOCK, QB))
                                  for k in range(j * blocks_per_step[b], (j + 1) * blocks_per_step[b])], axis=0)
            s = cur + bias + jnp.concatenate([km] * NSA_REPS, axis=1)
            states[b] = _flash_step_t(s, *states[b], vst_ref[:, j * step_keys[b]:(j + 1) * step_keys[b]])

        for b in both:
            o_slc = _flash_finish(*states[b])
            gate = jax.nn.sigmoid(z_ref[pl.ds(r0[b], QB), _OFF_GATE:_OFF_GATE + QB]).T
            outs = []
            for r in range(NSA_REPS):
                qsl = slice(r * QB, (r + 1) * QB)
                outs.append(gate[3 * r:3 * r + 1] * o_cmp[b][:, qsl] + gate[3 * r + 1:3 * r + 2] * o_slc[:, qsl]
                            + gate[3 * r + 2:3 * r + 3] * o_win[b][:, qsl])
            o_ref[pl.ds(r0[b], QB), :] = jnp.concatenate(outs, axis=0).T.astype(o_ref.dtype)
    for first in range(0, n_chunks // 2, NSA_GROUP // 2):
        q_blocks([i for p in range(first, first + NSA_GROUP // 2) for i in (p, n_chunks - 1 - p)])


def _nsa(z_nsa, zc, w1a, w1b, w2, pos, gq, gk, ovt, bslc, bwin, batch, seq):
    gw = NSA_GROUP_COLS
    n_chunks = seq // QB
    return pl.pallas_call(
        functools.partial(_nsa_kernel, seq),
        grid=(batch, NSA_KV_HEADS),
        in_specs=[pl.BlockSpec((None, seq, gw), lambda b, g: (b, 0, g)),
                  pl.BlockSpec((None, seq, 2 * HEAD_DIM), lambda b, g: (b, 0, g)),
                  _resident(w1a.shape), _resident(w1b.shape), _resident(w2.shape), _resident(pos.shape),
                  _resident(gq.shape), _resident(gk.shape), _resident(ovt.shape),
                  pl.BlockSpec((None,) + bslc.shape[1:], lambda b, g: (g, 0, 0, 0)),
                  pl.BlockSpec((None,) + bwin.shape[1:], lambda b, g: (g, 0, 0))],
        out_specs=pl.BlockSpec((None, seq, _NQ), lambda b, g: (b, 0, g)),
        out_shape=jax.ShapeDtypeStruct((batch, seq, NSA_HEADS * HEAD_DIM), BF16),
        scratch_shapes=[pltpu.VMEM((seq, HEAD_DIM), BF16),
                        pltpu.VMEM((HEAD_DIM, seq), BF16),
                        pltpu.VMEM((NSA_WIN_PREV * QB + seq, HEAD_DIM), BF16),
                        pltpu.VMEM((NSA_WIN_PREV + n_chunks, HEAD_DIM, QB), BF16)],
        compiler_params=_cparams(2),
        name="nsa",
    )(z_nsa, zc, w1a, w1b, w2, pos, gq, gk, ovt, bslc, bwin)


def _t5_bucket(dist):
    n = jnp.maximum(dist, 0)
    max_exact = NUM_BUCKETS // 2
    nf = jnp.maximum(n, 1).astype(F32)
    large = max_exact + (jnp.log(nf / max_exact) / math.log(MAX_DISTANCE / max_exact)
                         * (NUM_BUCKETS - max_exact)).astype(jnp.int32)
    large = jnp.minimum(large, NUM_BUCKETS - 1)
    return jnp.where(n < max_exact, n, large)


def _toeplitz(u, rows, cols):
    lead = u.shape[:-1]
    lu = rows + cols - 1
    assert u.shape[-1] == lu
    padded = jnp.pad(u, [(0, 0)] * len(lead) + [(0, 1)])
    flat = jnp.broadcast_to(padded[..., None, :], lead + (rows, lu + 1)).reshape(lead + (rows * (lu + 1),))
    return flat[..., :rows * lu].reshape(lead + (rows, lu))[..., rows - 1:]


def _bias_by_distance(bias_cols, delta, valid, dist_scale=1):
    bucket = _t5_bucket(jnp.asarray(np.maximum(delta, 0) * dist_scale, dtype=jnp.int32))
    return jnp.where(jnp.asarray(valid)[None, :], bias_cols.astype(F32)[bucket].T * LOG2E, NEG)


def _band_bias(bias_cols, window, n_prev, dist_scale, n_kv, reps):
    kw = (n_prev + 1) * QB
    delta = n_prev * QB + QB - 1 - np.arange(kw + QB - 1)
    u = _bias_by_distance(bias_cols, delta, (delta >= 0) & (delta < window), dist_scale)
    return _toeplitz(u, QB, kw).reshape(n_kv, reps * QB, kw)


def _slc_bias_t(bias_cols, seq):
    n_chunks = seq // QB
    delta = seq - 1 - np.arange(seq + QB - 1)
    strip = _toeplitz(_bias_by_distance(bias_cols, delta, delta >= 0), QB, seq)
    tile = jnp.flip(strip.reshape(NSA_KV_HEADS, NSA_REPS, QB, n_chunks, QB), axis=3)
    tile = jnp.transpose(tile, (0, 3, 4, 1, 2)).reshape(NSA_KV_HEADS, n_chunks, QB, NSA_REPS * QB)
    return jnp.concatenate([jnp.full_like(tile[:, :1], NEG), tile], axis=1)


def _rope_tables(seq):
    half = MLA_ROPE // 2
    inv = jnp.power(ROPE_THETA, -jnp.arange(half, dtype=F32) / half)
    ang = jnp.arange(seq, dtype=F32)[:, None] * inv[None, :]
    cos, sin = jnp.cos(ang), jnp.sin(ang)
    zeros = lambda w: jnp.zeros((seq, w), F32)
    tail = MLA_PAD - MLA_QK_DIM
    cos_t = jnp.concatenate([jnp.ones((seq, MLA_NOPE), F32), cos, cos, zeros(tail)], axis=1)
    sin_t = jnp.concatenate([zeros(MLA_NOPE), -sin, sin, zeros(tail)], axis=1)
    return cos_t, sin_t


def _pad_cols(w, width):
    return jnp.pad(w, ((0, 0), (0, width - w.shape[1])))


def _with_swapped_rope(a):
    half = MLA_ROPE // 2
    return jnp.concatenate([a, a[..., MLA_NOPE + half:], a[..., MLA_NOPE:MLA_NOPE + half]], axis=-1)


def _nsa_column_order():
    g_cols = NSA_KV_HEADS * HEAD_DIM
    q_cols = NSA_HEADS * HEAD_DIM
    order = []
    for g in range(NSA_KV_HEADS):
        cols = list(range(g * _NQ, (g + 1) * _NQ))
        for part in range(2, 6):
            start = q_cols + part * g_cols + g * HEAD_DIM
            cols += list(range(start, start + HEAD_DIM))
        gate0 = q_cols + 6 * g_cols + g * NSA_REPS * 3
        cols += list(range(gate0, gate0 + NSA_REPS * 3))
        order.append(cols)
    return order


def _nsa_cmp_columns():
    q_cols = NSA_HEADS * HEAD_DIM
    g_cols = NSA_KV_HEADS * HEAD_DIM
    cols = []
    for g in range(NSA_KV_HEADS):
        for part in range(2):
            start = q_cols + part * g_cols + g * HEAD_DIM
            cols += list(range(start, start + HEAD_DIM))
    return cols


def _nsa_mixer(z_nsa3, z_cmp, rel_bias, q_norm, k_norm, cmp_pos, cmp_w1, cmp_w2, batch, seq):
    w1 = cmp_w1.reshape(2, 2, NSA_CMP_STRIDE, HEAD_DIM, NSA_CMP_HIDDEN)
    zero1 = jnp.zeros_like(w1[0])
    w1_bd = jnp.concatenate([jnp.concatenate([w1[0], zero1], axis=-1),
                             jnp.concatenate([zero1, w1[1]], axis=-1)], axis=-2)
    zero2 = jnp.zeros_like(cmp_w2[0])
    w2_bd = jnp.concatenate([jnp.concatenate([cmp_w2[0], zero2], axis=-1),
                             jnp.concatenate([zero2, cmp_w2[1]], axis=-1)], axis=-2)
    pos_kv = jnp.concatenate([cmp_pos[0], cmp_pos[1]], axis=-1)
    n_cmp = (seq - NSA_CMP_LEN) // NSA_CMP_STRIDE + 1
    ci = np.arange(NSA_N_CMP_PAD)[:, None] * NSA_CMP_STRIDE
    sj = np.arange(NSA_N_SLC)[None, :] * NSA_SLC_BLOCK
    overlap = ((ci < sj + NSA_SLC_BLOCK) & (ci + NSA_CMP_LEN > sj) & (np.arange(NSA_N_CMP_PAD)[:, None] < n_cmp))
    ovt = jnp.asarray(overlap.T.astype(np.float32))
    nsa_cols = rel_bias[:, NSA_BIAS_COL0:NSA_BIAS_COL0 + NSA_HEADS]
    bslc = _slc_bias_t(nsa_cols, seq)
    bwin = jnp.transpose(_band_bias(nsa_cols, NSA_WINDOW, NSA_WIN_PREV, 1, NSA_KV_HEADS, NSA_REPS), (0, 2, 1))
    return _nsa(z_nsa3, z_cmp.reshape(batch, seq, -1), w1_bd[0].astype(BF16), w1_bd[1].astype(BF16),
                w2_bd.astype(BF16), pos_kv, q_norm.reshape(HEAD_DIM, 1), k_norm.reshape(3, 1, HEAD_DIM),
                ovt, bslc, bwin, batch, seq)


def kernel(x, rel_bias, ffn1_norm, ffn1_w_gate, ffn1_w_up, ffn1_w_down, mix_norm, ffn2_norm, ffn2_w_gate,
           ffn2_w_up, ffn2_w_down, ab_w_in, mla_q_a_norm, mla_w_q_b, mla_kv_a_norm, mla_w_kv_b, mla_q_norm,
           mla_k_norm, dil_q_norm, dil_k_norm, ab_w_out, cd_w_in, swa_q_norm, swa_k_norm, swa_sinks,
           nsa_q_norm, nsa_k_norm, nsa_cmp_pos, nsa_cmp_w1, nsa_cmp_w2, cd_w_out):
    batch, seq, _ = x.shape
    n = batch * seq
    assert seq % (16 * QB) == 0 and n % TM == 0 and seq % TM == 0
    bf = lambda a: a.astype(BF16)
    xf = x.reshape(n, D_MODEL)
    ffn1 = (bf(ffn1_w_gate), bf(ffn1_w_up), bf(ffn1_w_down))
    ffn2 = (bf(ffn2_w_gate), bf(ffn2_w_up), bf(ffn2_w_down))

    w_in = ab_w_in[0]
    mla_cols = MLA_Q_RANK + MLA_KV_RANK
    w_krope = _with_swapped_rope(jnp.pad(w_in[:, mla_cols:mla_cols + MLA_ROPE], ((0, 0), (MLA_NOPE, 0))))
    w_mla = jnp.concatenate([w_in[:, :mla_cols], w_krope], axis=1)
    wq = _with_swapped_rope(mla_w_q_b[0].reshape(MLA_Q_RANK, MLA_HEADS, MLA_QK_DIM))
    wq = wq.reshape(MLA_Q_RANK, MLA_HEADS * MLA_PAD)
    wkv = mla_w_kv_b[0].reshape(MLA_KV_RANK, MLA_HEADS, MLA_NOPE + MLA_V)
    wk = _pad_cols(wkv[:, :, :MLA_NOPE].reshape(MLA_KV_RANK * MLA_HEADS, MLA_NOPE), MLA_PAD)
    wk = wk.reshape(MLA_KV_RANK, MLA_HEADS * MLA_PAD)
    wv = wkv[:, :, MLA_NOPE:].reshape(MLA_KV_RANK, MLA_HEADS * MLA_V)
    cos_t, sin_t = _rope_tables(seq)
    q_scale = MLA_QK_DIM ** -0.5 * LOG2E
    xf, z_dil, q_mla, k_mla, vt_mla = _ffn_proj_mla(
        xf, ffn1_norm[0], ffn1, 0, mix_norm[0], bf(w_mla), bf(w_in[:, mla_cols + MLA_ROPE:]),
        mla_q_a_norm[0].reshape(1, -1), mla_kv_a_norm[0].reshape(1, -1), bf(wq), bf(wk), bf(wv),
        _with_swapped_rope(mla_q_norm[0].reshape(1, -1)), _with_swapped_rope(mla_k_norm[0].reshape(1, -1)),
        cos_t * q_scale, sin_t * q_scale, cos_t, sin_t, seq)
    o_a = _mla_attn(q_mla.reshape(batch, seq, -1), k_mla.reshape(batch, seq, -1),
                    vt_mla.reshape(batch, seq // MLA_TQ, MLA_HEADS * MLA_V, MLA_TQ),
                    batch, seq).reshape(n, MLA_HEADS * MLA_V)

    z_dil3 = z_dil.reshape(batch, seq, -1)
    gq, gk = dil_q_norm[0], dil_k_norm[0]
    dil_outs, dil_lses = [], []
    for grp, (window, dilation) in enumerate(DIL_PATTERNS):
        bm = _band_bias(rel_bias[:, grp * DIL_HPG:(grp + 1) * DIL_HPG], window // dilation + 1, 1, dilation,
                        DIL_HPG, 1)
        o, lse = _dilated_group(z_dil3, batch, seq, grp, dilation, gq, gk, bm)
        dil_outs.append(o)
        dil_lses.append(lse)
    w_out = ab_w_out[0]
    xf = _out_ab_ffn(xf, o_a, dil_outs, dil_lses, bf(w_out[:MLA_HEADS * MLA_V]), bf(w_out[MLA_HEADS * MLA_V:]),
                     ffn2_norm[0], ffn2, 0)

    w_in = cd_w_in[0]
    swa_cols = (SWA_HEADS + 2 * SWA_KV_HEADS) * HEAD_DIM
    w_nsa_src = w_in[:, swa_cols:]
    w_nsa = jnp.concatenate([_pad_cols(w_nsa_src[:, np.asarray(cols)], NSA_GROUP_COLS)
                             for cols in _nsa_column_order()], axis=1)
    w_cmp = w_nsa_src[:, np.asarray(_nsa_cmp_columns())]
    xf, z_swa, z_nsa, z_cmp = _ffn_proj(xf, ffn1_norm[1], ffn1, 1, mix_norm[1],
                                        [bf(w_in[:, :swa_cols]), bf(w_nsa), bf(w_cmp)])

    swa_reps = SWA_HEADS // SWA_KV_HEADS
    swa_prev = -(-(SWA_WINDOW - 1) // QB)
    bm_swa = _band_bias(rel_bias[:, :SWA_HEADS], SWA_WINDOW, swa_prev, 1, SWA_KV_HEADS, swa_reps)
    sinks = jnp.broadcast_to((swa_sinks[0].astype(F32) * LOG2E).reshape(SWA_KV_HEADS, 1, swa_reps, 1),
                             (SWA_KV_HEADS, 1, swa_reps, QB)).reshape(SWA_KV_HEADS, 1, swa_reps * QB)
    o_c = _swa(z_swa.reshape(batch, seq, -1), batch, seq, swa_q_norm[0].reshape(HEAD_DIM, 1),
               swa_k_norm[0].reshape(1, HEAD_DIM), bm_swa, sinks)

    o_d = _nsa_mixer(z_nsa.reshape(batch, seq, NSA_KV_HEADS * NSA_GROUP_COLS), z_cmp, rel_bias, nsa_q_norm[0],
                     nsa_k_norm[0], nsa_cmp_pos[0], nsa_cmp_w1[0], nsa_cmp_w2[0], batch, seq)

    w_out = cd_w_out[0]
    xf = _out_cd_ffn(xf, o_c.reshape(n, -1), o_d.reshape(n, -1), bf(w_out[:SWA_HEADS * HEAD_DIM]),
                     bf(w_out[SWA_HEADS * HEAD_DIM:]), ffn2_norm[1], ffn2, 1)
    return xf.reshape(batch, seq, D_MODEL)
```

```python
import functools
import math

import numpy as np
import jax
import jax.numpy as jnp
from jax import lax
from jax.experimental import pallas as pl
from jax.experimental.pallas import tpu as pltpu

F32 = jnp.float32
BF16 = jnp.bfloat16

EPS = 1e-6
NEG = -1e30
TINY = 1e-30
LOG2E = math.log2(math.e)
LN2 = math.log(2.0)
D_MODEL = 1024
D_FF = 2816
NUM_BUCKETS = 32
MAX_DISTANCE = 2048
HEAD_DIM = 64
QB = 128

MLA_HEADS = 8
MLA_Q_RANK = 256
MLA_KV_RANK = 128
MLA_NOPE = 64
MLA_ROPE = 32
MLA_V = 64
MLA_QK_DIM = MLA_NOPE + MLA_ROPE
MLA_PAD = 128
ROPE_THETA = 10000.0

DIL_PATTERNS = ((128, 1), (512, 4), (2048, 16))
DIL_HPG = 4
SWA_HEADS = 8
SWA_KV_HEADS = 2
SWA_WINDOW = 128
NSA_HEADS = 8
NSA_KV_HEADS = 2
NSA_CMP_LEN = 32
NSA_CMP_STRIDE = 16
NSA_CMP_HIDDEN = 128
NSA_SLC_BLOCK = 64
NSA_TOP_N = 16
NSA_WINDOW = 512
NSA_FORCE = 1e6
NSA_BIAS_COL0 = 8
NSA_GROUP_COLS = 640

VMEM_LIMIT = 56 * 1024 * 1024
TM = 512

_NT = (((1,), (1,)), ((), ()))
_HI = lax.Precision.HIGHEST


def _cparams(n_axes):
    return pltpu.CompilerParams(dimension_semantics=("arbitrary",) * n_axes,
                                vmem_limit_bytes=VMEM_LIMIT)


def _resident(shape):
    nd = len(shape)
    return pl.BlockSpec(shape, lambda *_: (0,) * nd, pipeline_mode=pl.Buffered(1))


def _layer_block(w, layer):
    nd = w.ndim - 1
    return pl.BlockSpec((None,) + w.shape[1:], lambda *_: (layer,) + (0,) * nd, pipeline_mode=pl.Buffered(1))


def _rms(x):
    return x * lax.rsqrt(jnp.mean(x * x, axis=-1, keepdims=True) + EPS)


def _dot(a, b):
    return jnp.dot(a, b, preferred_element_type=F32)


def _dot_nt(a, b, precision=None):
    return lax.dot_general(a, b, _NT, preferred_element_type=F32, precision=precision)


FF_CHUNK = 256


N_CHUNK = 512


def _swiglu_half_step(x, g_ref, wg_ref, wu_ref, wd_ref):
    hb = (_rms(x) * g_ref[...]).astype(BF16)
    acc = jnp.zeros(x.shape, F32)
    for c in range(D_FF // FF_CHUNK):
        sl = slice(c * FF_CHUNK, (c + 1) * FF_CHUNK)
        gate = _dot(hb, wg_ref[:, sl])
        up = _dot(hb, wu_ref[:, sl])
        act = (gate * jax.nn.sigmoid(gate) * up).astype(BF16)
        acc = acc + _dot(act, wd_ref[sl, :])
    return x + 0.5 * acc


def _ffn_proj_kernel(n_out, x_ref, g_ref, wg_ref, wu_ref, wd_ref, gm_ref, *refs):
    w_refs, x_out_ref, z_refs = refs[:n_out], refs[n_out], refs[n_out + 1:]
    x = _swiglu_half_step(x_ref[...], g_ref, wg_ref, wu_ref, wd_ref)
    x_out_ref[...] = x
    hb = (_rms(x) * gm_ref[...]).astype(BF16)
    for w_ref, z_ref in zip(w_refs, z_refs):
        width = w_ref.shape[1]
        for c0 in range(0, width, N_CHUNK):
            sl = slice(c0, min(c0 + N_CHUNK, width))
            z_ref[:, sl] = _dot(hb, w_ref[:, sl]).astype(z_ref.dtype)


def _tile(width):
    return pl.BlockSpec((TM, width), lambda i: (i, 0))


def _ffn_proj(x, g, ffn, layer, g_mix, ws, z_dtypes=None):
    n = x.shape[0]
    z_dtypes = z_dtypes or [F32] * len(ws)
    ffn_w = [_layer_block(w, layer) for w in ffn]
    return pl.pallas_call(
        functools.partial(_ffn_proj_kernel, len(ws)),
        grid=(n // TM,),
        in_specs=[_tile(D_MODEL), _resident((1, D_MODEL))] + ffn_w + [_resident((1, D_MODEL))]
        + [_resident(w.shape) for w in ws],
        out_specs=[_tile(D_MODEL)] + [_tile(w.shape[1]) for w in ws],
        out_shape=[jax.ShapeDtypeStruct((n, D_MODEL), F32)]
        + [jax.ShapeDtypeStruct((n, w.shape[1]), dt) for w, dt in zip(ws, z_dtypes)],
        compiler_params=_cparams(1),
        name="ffn_proj",
    )(x, g.reshape(1, D_MODEL), *ffn, g_mix.reshape(1, D_MODEL), *ws)


def _out_ab_ffn_kernel(x_ref, oa_ref, o0_ref, o1_ref, o2_ref, l0_ref, l1_ref, l2_ref, wa_ref, wb_ref,
                       g_ref, wg_ref, wu_ref, wd_ref, out_ref):
    l0, l1, l2 = l0_ref[...], l1_ref[...], l2_ref[...]
    m = jnp.maximum(jnp.maximum(l0, l1), l2)
    e0, e1, e2 = jnp.exp(l0 - m), jnp.exp(l1 - m), jnp.exp(l2 - m)
    ob = (e0 * o0_ref[...] + e1 * o1_ref[...] + e2 * o2_ref[...]) / (e0 + e1 + e2)
    x = x_ref[...] + _dot(oa_ref[...], wa_ref[...]) + _dot(ob.astype(BF16), wb_ref[...])
    out_ref[...] = _swiglu_half_step(x, g_ref, wg_ref, wu_ref, wd_ref)


def _out_ab_ffn(x, oa, dil_outs, dil_lses, wa, wb, g, ffn, layer):
    n = x.shape[0]
    dw = DIL_HPG * HEAD_DIM
    ffn_w = [_layer_block(w, layer) for w in ffn]
    return pl.pallas_call(
        _out_ab_ffn_kernel,
        grid=(n // TM,),
        in_specs=[_tile(D_MODEL), _tile(oa.shape[1])] + [_tile(dw)] * 6 + [_resident(wa.shape), _resident(wb.shape),
                                                                        _resident((1, D_MODEL))] + ffn_w,
        out_specs=_tile(D_MODEL),
        out_shape=jax.ShapeDtypeStruct((n, D_MODEL), F32),
        compiler_params=_cparams(1),
        name="out_ab_ffn",
    )(x, oa, *dil_outs, *dil_lses, wa, wb, g.reshape(1, D_MODEL), *ffn)


def _out_cd_ffn_kernel(x_ref, oc_ref, od_ref, wc_ref, wdd_ref, g_ref, wg_ref, wu_ref, wd_ref, out_ref):
    x = x_ref[...] + _dot(oc_ref[...], wc_ref[...]) + _dot(od_ref[...], wdd_ref[...])
    out_ref[...] = _swiglu_half_step(x, g_ref, wg_ref, wu_ref, wd_ref)


def _out_cd_ffn(x, oc, od, wc, wdd, g, ffn, layer):
    n = x.shape[0]
    ffn_w = [_layer_block(w, layer) for w in ffn]
    return pl.pallas_call(
        _out_cd_ffn_kernel,
        grid=(n // TM,),
        in_specs=[_tile(D_MODEL), _tile(oc.shape[1]), _tile(od.shape[1]), _resident(wc.shape), _resident(wdd.shape),
                  _resident((1, D_MODEL))] + ffn_w,
        out_specs=_tile(D_MODEL),
        out_shape=jax.ShapeDtypeStruct((n, D_MODEL), F32),
        compiler_params=_cparams(1),
        name="out_cd_ffn",
    )(x, oc, od, wc, wdd, g.reshape(1, D_MODEL), *ffn)


MLA_TQ = 256
MLA_STEP = 2


def _ffn_proj_mla_kernel(x_ref, g_ref, wg_ref, wu_ref, wd_ref, gm_ref, wmla_ref, wdil_ref,
                         qan_ref, kvan_ref, wq_ref, wk_ref, wv_ref, gq_ref, gk_ref,
                         qc_ref, qs_ref, kc_ref, ks_ref,
                         x_out_ref, zdil_ref, q_ref, k_ref, vt_ref, zprev_ref):
    @pl.when(pl.program_id(0) == 0)
    def _():
        zprev_ref[...] = jnp.zeros(zprev_ref.shape, F32)

    cq = (_rms(zprev_ref[:, :MLA_Q_RANK]) * qan_ref[...]).astype(BF16)
    ckv = (_rms(zprev_ref[:, MLA_Q_RANK:MLA_Q_RANK + MLA_KV_RANK]) * kvan_ref[...]).astype(BF16)
    kr = zprev_ref[:, MLA_Q_RANK + MLA_KV_RANK:]
    real = jnp.where(lax.broadcasted_iota(jnp.int32, (1, MLA_PAD), 1) < MLA_QK_DIM, 1.0, 0.0)

    def head_norm_rope(x, gain, cos, sin):
        ssq = jnp.sum(x * x * real, axis=-1, keepdims=True)
        x = x * lax.rsqrt(ssq * (1.0 / MLA_QK_DIM) + EPS) * gain
        return x * cos + pltpu.roll(x, MLA_PAD - MLA_ROPE, 1) * sin

    def head(h):
        sl = slice(h * MLA_PAD, (h + 1) * MLA_PAD)
        qh = head_norm_rope(_dot(cq, wq_ref[:, sl]), gq_ref[...], qc_ref[...], qs_ref[...])
        q_ref[:, sl] = qh.astype(BF16)
        kh = head_norm_rope(_dot(ckv, wk_ref[:, sl]) + kr, gk_ref[...], kc_ref[...], ks_ref[...])
        k_ref[:, sl] = kh.astype(BF16)

    def values():
        v = _dot(ckv, wv_ref[...])
        for c in range(TM // MLA_TQ):
            vt_ref[c] = v[c * MLA_TQ:(c + 1) * MLA_TQ].T.astype(BF16)

    latent_work = [functools.partial(head, h) for h in range(MLA_HEADS)] + [values]

    x = x_ref[...]
    hb = (_rms(x) * g_ref[...]).astype(BF16)
    acc = jnp.zeros(x.shape, F32)
    for c in range(D_FF // FF_CHUNK):
        sl = slice(c * FF_CHUNK, (c + 1) * FF_CHUNK)
        gate = _dot(hb, wg_ref[:, sl])
        up = _dot(hb, wu_ref[:, sl])
        act = (gate * jax.nn.sigmoid(gate) * up).astype(BF16)
        acc = acc + _dot(act, wd_ref[sl, :])
        if c < len(latent_work):
            latent_work[c]()
    x = x + 0.5 * acc
    x_out_ref[...] = x
    hm = (_rms(x) * gm_ref[...]).astype(BF16)
    zprev_ref[...] = _dot(hm, wmla_ref[...])
    width = wdil_ref.shape[1]
    for c0 in range(0, width, N_CHUNK):
        sl = slice(c0, min(c0 + N_CHUNK, width))
        zdil_ref[:, sl] = _dot(hm, wdil_ref[:, sl])


def _ffn_proj_mla(x, g, ffn, layer, g_mix, w_mla, w_dil, qan, kvan, wq, wk, wv, gq, gk,
                  q_cos, q_sin, k_cos, k_sin, seq):
    n = x.shape[0]
    n_tiles = n // TM
    assert D_FF // FF_CHUNK >= MLA_HEADS + 1
    per_seq = seq // TM
    hw = MLA_HEADS * MLA_PAD
    vw = MLA_HEADS * MLA_V

    def cur(width):
        return pl.BlockSpec((TM, width), lambda i: (jnp.minimum(i, n_tiles - 1), 0))

    def prev(width):
        return pl.BlockSpec((TM, width), lambda i: (jnp.maximum(i - 1, 0), 0))

    rope_spec = pl.BlockSpec((TM, MLA_PAD), lambda i: (jnp.maximum(i - 1, 0) % per_seq, 0))
    small = [qan, kvan, wq, wk, wv, gq, gk]
    return pl.pallas_call(
        _ffn_proj_mla_kernel,
        grid=(n_tiles + 1,),
        in_specs=[cur(D_MODEL), _resident((1, D_MODEL))] + [_layer_block(w, layer) for w in ffn]
        + [_resident((1, D_MODEL)), _resident(w_mla.shape), _resident(w_dil.shape)]
        + [_resident(a.shape) for a in small] + [rope_spec] * 4,
        out_specs=[cur(D_MODEL), cur(w_dil.shape[1]), prev(hw), prev(hw),
                   pl.BlockSpec((TM // MLA_TQ, vw, MLA_TQ), lambda i: (jnp.maximum(i - 1, 0), 0, 0))],
        out_shape=[jax.ShapeDtypeStruct((n, D_MODEL), F32), jax.ShapeDtypeStruct((n, w_dil.shape[1]), F32),
                   jax.ShapeDtypeStruct((n, hw), BF16), jax.ShapeDtypeStruct((n, hw), BF16),
                   jax.ShapeDtypeStruct((n // MLA_TQ, vw, MLA_TQ), BF16)],
        scratch_shapes=[pltpu.VMEM((TM, w_mla.shape[1]), F32)],
        compiler_params=_cparams(1),
        name="ffn_proj_mla",
    )(x, g.reshape(1, D_MODEL), *ffn, g_mix.reshape(1, D_MODEL), w_mla, w_dil, *small,
      q_cos, q_sin, k_cos, k_sin)


def _flash_step_t(s, m, l, acc, pv, vt):
    m_new = jnp.maximum(m, jnp.max(s, axis=0, keepdims=True))
    p = jnp.exp2(s - m_new)
    alpha = jnp.exp2(m - m_new)
    l = alpha * l + jnp.sum(p, axis=0, keepdims=True)
    acc = alpha * (acc + pv)
    return m_new, l, acc, _dot(vt, p.astype(BF16))


def _flash_init(dv, n_queries):
    return (jnp.full((1, n_queries), NEG, F32), jnp.zeros((1, n_queries), F32), jnp.zeros((dv, n_queries), F32),
            jnp.zeros((dv, n_queries), F32))


def _flash_finish(m, l, acc, pv):
    return (acc + pv) * (1.0 / l)


def _mla_attn_kernel(seq, q_ref, k_ref, vt_ref, o_ref):
    t = MLA_TQ
    key = lax.broadcasted_iota(jnp.int32, (t, t), 0)
    qry = lax.broadcasted_iota(jnp.int32, (t, t), 1)
    diag_ok = key <= qry
    heads = [(slice(hh * MLA_PAD, (hh + 1) * MLA_PAD), slice(hh * MLA_V, (hh + 1) * MLA_V)) for hh in range(2)]

    init = _flash_init(MLA_V, t)
    n_blocks = seq // t
    for first in range(n_blocks // 2):
        blocks = [first, n_blocks - 1 - first]
        chains = [(i, h) for i in blocks for h in range(2)]
        qs = {(i, h): q_ref[i * t:(i + 1) * t, heads[h][0]] for i, h in chains}
        steps = {i: [(j, min(j + MLA_STEP, i + 1)) for j in range(0, i + 1, MLA_STEP)] for i in blocks}

        def scores(chain, step):
            i, h = chain
            lo, hi = steps[i][step]
            s = _dot_nt(k_ref[lo * t:hi * t, heads[h][0]], qs[chain])
            if hi == i + 1:
                last = jnp.where(diag_ok, s[(i - lo) * t:], NEG)
                s = last if i == lo else jnp.concatenate([s[:(i - lo) * t], last], axis=0)
            return s

        def values(chain, step):
            i, h = chain
            lo, hi = steps[i][step]
            return jnp.concatenate([vt_ref[j, heads[h][1], :] for j in range(lo, hi)], axis=1)

        cur = {c: scores(c, 0) for c in chains}
        states = {c: init for c in chains}
        for step in range(max(len(s) for s in steps.values())):
            for c in chains:
                n_steps = len(steps[c[0]])
                if step >= n_steps:
                    continue
                nxt = scores(c, step + 1) if step + 1 < n_steps else None
                states[c] = _flash_step_t(cur[c], *states[c], values(c, step))
                cur[c] = nxt
        for i in blocks:
            outs = [_flash_finish(*states[(i, h)]) for h in range(2)]
            o_ref[i * t:(i + 1) * t, :] = jnp.concatenate(outs, axis=0).T.astype(o_ref.dtype)


def _mla_attn(q, k, vt, batch, seq):
    return pl.pallas_call(
        functools.partial(_mla_attn_kernel, seq),
        grid=(batch, MLA_HEADS // 2),
        in_specs=[pl.BlockSpec((None, seq, 2 * MLA_PAD), lambda b, h: (b, 0, h)),
                  pl.BlockSpec((None, seq, 2 * MLA_PAD), lambda b, h: (b, 0, h)),
                  pl.BlockSpec((None, seq // MLA_TQ, 2 * MLA_V, MLA_TQ), lambda b, h: (b, 0, h, 0))],
        out_specs=pl.BlockSpec((None, seq, 2 * MLA_V), lambda b, h: (b, 0, h)),
        out_shape=jax.ShapeDtypeStruct((batch, seq, MLA_HEADS * MLA_V), BF16),
        compiler_params=_cparams(2),
        name="mla_attn",
    )(q, k, vt)


def _head_norm(x, gain):
    return _rms(x) * gain


def _heads_t(qt, n_heads, gain):
    tiles = []
    for c0 in range(0, n_heads * HEAD_DIM, 2 * HEAD_DIM):
        pair = qt[:, c0:c0 + 2 * HEAD_DIM].T
        for h in range(2):
            x = pair[h * HEAD_DIM:(h + 1) * HEAD_DIM]
            tiles.append(x * lax.rsqrt(jnp.mean(x * x, axis=0, keepdims=True) + EPS) * gain)
    return tiles


def _q_gain(gq_ref):
    return jnp.broadcast_to(gq_ref[...] * (HEAD_DIM ** -0.5 * LOG2E), (HEAD_DIM, QB))


def _pair_ones():
    shape = (2 * HEAD_DIM, 2 * HEAD_DIM)
    same = lax.broadcasted_iota(jnp.int32, shape, 0) // HEAD_DIM == lax.broadcasted_iota(jnp.int32, shape, 1) // HEAD_DIM
    return jnp.where(same, 1.0, 0.0).astype(BF16)


def _pair_norm(x, seg, gain):
    sq = x * x
    hi = sq.astype(BF16)
    lo = (sq - hi.astype(F32)).astype(BF16)
    ssq = _dot(hi, seg) + _dot(lo, seg)
    return x * lax.rsqrt(ssq * (1.0 / HEAD_DIM) + EPS) * gain


BAND_KV = 2


def _banded_kernel(seq, dilation, reps, n_prev, with_sink, with_lse, group_size, paired, *refs):
    q_ref, k_ref, v_ref, gq_ref, gk_ref, bm_ref = refs[:6]
    refs = refs[6:]
    sink_ref = None
    if with_sink:
        sink_ref, refs = refs[0], refs[1:]
    o_ref, refs = refs[0], refs[1:]
    lse_ref = None
    if with_lse:
        lse_ref, refs = refs[0], refs[1:]
    kn_ref, vt_ref = refs
    length = seq // dilation
    n_blocks = length // QB
    pad = n_prev * QB
    kw = pad + QB
    win_row = lax.broadcasted_iota(jnp.int32, (kw, 1), 0)
    assert not paired or reps == 1
    if paired:
        gq, gk = gq_ref[...] * (HEAD_DIM ** -0.5 * LOG2E), gk_ref[...]
        seg = _pair_ones()
        lane = lax.broadcasted_iota(jnp.int32, (1, BAND_KV * HEAD_DIM), 1)
        head_lanes = [jnp.where(lane // HEAD_DIM == g, 1.0, 0.0) for g in range(BAND_KV)]
    else:
        gq, gk = _q_gain(gq_ref), gk_ref[...]

    def rows(start, size):
        return pl.ds(start, size) if dilation == 1 else pl.ds(start, size, stride=dilation)

    for res in range(dilation):
        kall, vall = k_ref[rows(res, length), :], v_ref[rows(res, length), :]
        if paired:
            kn_ref[res, :pad, :] = jnp.zeros((pad, BAND_KV * HEAD_DIM), BF16)
            kn_ref[res, pad:, :] = _pair_norm(kall, seg, gk).astype(BF16)
        else:
            for g in range(BAND_KV):
                kn_ref[res, g, :pad, :] = jnp.zeros((pad, HEAD_DIM), BF16)
                kn_ref[res, g, pad:, :] = _head_norm(kall[:, g * HEAD_DIM:(g + 1) * HEAD_DIM], gk).astype(BF16)
        for j in range(n_prev):
            vt_ref[res, j] = jnp.zeros((BAND_KV * HEAD_DIM, QB), BF16)
        for j in range(n_blocks):
            vt_ref[res, n_prev + j] = vall[j * QB:(j + 1) * QB].T.astype(BF16)

    def attend(res, i, g, s):
        bias = bm_ref[g]
        if i < n_prev:
            bias = jnp.where(win_row >= (n_prev - i) * QB, bias, NEG)
        s = s + bias
        m = jnp.max(s, axis=0, keepdims=True)
        if with_sink:
            m = jnp.maximum(m, sink_ref[g])
        p = jnp.exp2(s - m)
        l = jnp.sum(p, axis=0, keepdims=True)
        if with_sink:
            l = l + jnp.exp2(sink_ref[g] - m)
        p = p.astype(BF16)
        vl = slice(g * HEAD_DIM, (g + 1) * HEAD_DIM)
        o = _dot(vt_ref[res, i, vl, :], p[0:QB])
        for w in range(1, n_prev + 1):
            o = o + _dot(vt_ref[res, i + w, vl, :], p[w * QB:(w + 1) * QB])
        return o * (1.0 / l), m * LN2 + jnp.log(l)

    items = [(res, i) for res in range(dilation) for i in range(n_blocks)]
    for at in range(0, len(items), group_size):
        group = items[at:at + group_size]
        toks = [rows(res + i * QB * dilation, QB) for res, i in group]
        if paired:
            qns = [_pair_norm(q_ref[tok, :], seg, gq) for tok in toks]
            qss = [[(qn * head_lanes[g]).astype(BF16) for g in range(BAND_KV)] for qn in qns]
            scores = [[_dot_nt(kn_ref[res, i * QB:i * QB + kw, :], qs[g]) for g in range(BAND_KV)]
                      for (res, i), qs in zip(group, qss)]
        else:
            heads = [_heads_t(q_ref[tok, :], BAND_KV * reps, gq) for tok in toks]
            qss = [[jnp.concatenate(h[g * reps:(g + 1) * reps], axis=1).astype(BF16) for g in range(BAND_KV)]
                   for h in heads]
            scores = [[_dot(kn_ref[res, g, i * QB:i * QB + kw, :], qs[g]) for g in range(BAND_KV)]
                      for (res, i), qs in zip(group, qss)]
        for (res, i), tok, sc in zip(group, toks, scores):
            outs, lses = [], []
            for g in range(BAND_KV):
                o, lse = attend(res, i, g, sc[g])
                for r in range(reps):
                    outs.append(o[:, r * QB:(r + 1) * QB])
                    if with_lse:
                        lses.append(jnp.broadcast_to(lse[:, r * QB:(r + 1) * QB], (HEAD_DIM, QB)))
            o_ref[tok, :] = jnp.concatenate(outs, axis=0).T.astype(o_ref.dtype)
            if with_lse:
                lse_ref[tok, :] = jnp.concatenate(lses, axis=0).T


def _banded_scratch(seq, dilation, n_prev, paired):
    length = seq // dilation
    rows = n_prev * QB + length
    keys = (dilation, rows, BAND_KV * HEAD_DIM) if paired else (dilation, BAND_KV, rows, HEAD_DIM)
    return [pltpu.VMEM(keys, BF16),
            pltpu.VMEM((dilation, n_prev + length // QB, BAND_KV * HEAD_DIM, QB), BF16)]


PAIRED_MIN_DILATION = 16


def _dilated_group(z_dil, batch, seq, group, dilation, q_norm, k_norm, bm):
    gw = DIL_HPG * HEAD_DIM
    n_groups = len(DIL_PATTERNS)
    hps = BAND_KV
    halves = DIL_HPG // hps
    bm_t = jnp.transpose(bm, (0, 2, 1))
    paired = dilation >= PAIRED_MIN_DILATION
    if paired:
        gq = jnp.tile(q_norm.reshape(1, HEAD_DIM), (1, BAND_KV))
        gk = jnp.tile(k_norm.reshape(1, HEAD_DIM), (1, BAND_KV))
    else:
        gq, gk = q_norm.reshape(HEAD_DIM, 1), k_norm.reshape(1, HEAD_DIM)

    def spec(part):
        return pl.BlockSpec((None, seq, hps * HEAD_DIM), lambda b, h: (b, 0, (part * n_groups + group) * halves + h))

    out_spec = pl.BlockSpec((None, seq, hps * HEAD_DIM), lambda b, h: (b, 0, h))
    shape = jax.ShapeDtypeStruct((batch, seq, gw), F32)
    out, lse = pl.pallas_call(
        functools.partial(_banded_kernel, seq, dilation, 1, 1, False, True, 16, paired),
        grid=(batch, halves),
        in_specs=[spec(0), spec(1), spec(2), _resident(gq.shape), _resident(gk.shape),
                  pl.BlockSpec((hps,) + bm_t.shape[1:], lambda b, h: (h, 0, 0))],
        out_specs=[out_spec, out_spec],
        out_shape=[shape, shape],
        scratch_shapes=_banded_scratch(seq, dilation, 1, paired),
        compiler_params=_cparams(2),
        name="dilated_g%d" % group,
    )(z_dil, z_dil, z_dil, gq, gk, bm_t)
    return out.reshape(batch * seq, gw), lse.reshape(batch * seq, gw)


def _swa(z_swa, batch, seq, gq, gk, bm, sinks):
    qw = SWA_HEADS * HEAD_DIM
    kvw = SWA_KV_HEADS * HEAD_DIM
    reps = SWA_HEADS // SWA_KV_HEADS
    n_prev = -(-(SWA_WINDOW - 1) // QB)
    assert SWA_KV_HEADS == BAND_KV
    bm_t = jnp.transpose(bm, (0, 2, 1))
    return pl.pallas_call(
        functools.partial(_banded_kernel, seq, 1, reps, n_prev, True, False, 4, False),
        grid=(batch,),
        in_specs=[pl.BlockSpec((None, seq, qw), lambda b: (b, 0, 0)),
                  pl.BlockSpec((None, seq, kvw), lambda b: (b, 0, qw // kvw)),
                  pl.BlockSpec((None, seq, kvw), lambda b: (b, 0, qw // kvw + 1)),
                  _resident(gq.shape), _resident(gk.shape), _resident(bm_t.shape), _resident(sinks.shape)],
        out_specs=pl.BlockSpec((None, seq, qw), lambda b: (b, 0, 0)),
        out_shape=jax.ShapeDtypeStruct((batch, seq, qw), BF16),
        scratch_shapes=_banded_scratch(seq, 1, n_prev, False),
        compiler_params=_cparams(1),
        name="swa",
    )(z_swa, z_swa, z_swa, gq, gk, bm_t, sinks)


NSA_REPS = NSA_HEADS // NSA_KV_HEADS
NSA_N_SLC = 32
NSA_N_CMP_PAD = 128
NSA_WIN_PREV = -(-(NSA_WINDOW - 1) // QB)
_NQ = NSA_REPS * HEAD_DIM
_OFF_KS, _OFF_VS, _OFF_KW, _OFF_VW, _OFF_GATE = (_NQ + i * HEAD_DIM for i in range(5))


NSA_STEP = 16
NSA_GROUP = 8


def _nsa_kernel(seq, z_ref, zc_ref, w1a_ref, w1b_ref, w2_ref, pos_ref, gq_ref, gk_ref, ovt_ref,
                bslc_ref, bwin_ref, o_ref, ks_ref, vst_ref, kw_ref, vwt_ref):
    n_chunks = seq // QB
    lanes = NSA_REPS * QB
    gq = _q_gain(gq_ref)
    win_pad = NSA_WIN_PREV * QB
    win_kw = win_pad + QB

    first = jnp.zeros((NSA_N_CMP_PAD, 2 * NSA_CMP_HIDDEN), F32)
    second = jnp.zeros((NSA_N_CMP_PAD, 2 * NSA_CMP_HIDDEN), F32)
    for l in range(NSA_CMP_STRIDE):
        tok = zc_ref[pl.ds(l, NSA_N_CMP_PAD, stride=NSA_CMP_STRIDE), :]
        first = first + _dot((tok + pos_ref[l:l + 1, :]).astype(BF16), w1a_ref[l])
        second = second + _dot((tok + pos_ref[NSA_CMP_STRIDE + l:NSA_CMP_STRIDE + l + 1, :]).astype(BF16),
                               w1b_ref[l])
    hidden = jax.nn.gelu(first + pltpu.roll(second, NSA_N_CMP_PAD - 1, 0)).astype(BF16)
    cmp_kv = _dot(hidden, w2_ref[...])
    kc = _head_norm(cmp_kv[:, :HEAD_DIM], gk_ref[0]).astype(BF16)
    vct = cmp_kv.T[HEAD_DIM:].astype(BF16)

    ks_ref[...] = _head_norm(z_ref[:, _OFF_KS:_OFF_KS + HEAD_DIM], gk_ref[1]).astype(BF16)
    kw_ref[:win_pad, :] = jnp.zeros((win_pad, HEAD_DIM), BF16)
    kw_ref[win_pad:, :] = _head_norm(z_ref[:, _OFF_KW:_OFF_KW + HEAD_DIM], gk_ref[2]).astype(BF16)
    for j in range(NSA_WIN_PREV):
        vwt_ref[j] = jnp.zeros((HEAD_DIM, QB), BF16)
    for j in range(n_chunks):
        rows = slice(j * QB, (j + 1) * QB)
        vst_ref[:, rows] = z_ref[rows, _OFF_KS:_OFF_KS + 2 * HEAD_DIM].T[HEAD_DIM:].astype(BF16)
        vwt_ref[NSA_WIN_PREV + j] = z_ref[rows, _OFF_KW:_OFF_KW + 2 * HEAD_DIM].T[HEAD_DIM:].astype(BF16)

    cmp_id = lax.broadcasted_iota(jnp.int32, (NSA_N_CMP_PAD, lanes), 0)
    cmp_end = cmp_id * NSA_CMP_STRIDE + (NSA_CMP_LEN - 1)
    cmp_real = cmp_id < NSA_N_CMP_PAD - 1
    q_in_blk = lax.broadcasted_iota(jnp.int32, (NSA_N_CMP_PAD, lanes), 1) & (QB - 1)
    blk_id = lax.broadcasted_iota(jnp.int32, (NSA_N_SLC, QB), 0)
    q_lane = lax.broadcasted_iota(jnp.int32, (NSA_N_SLC, QB), 1)
    win_row = lax.broadcasted_iota(jnp.int32, (win_kw, 1), 0)
    init = _flash_init(HEAD_DIM, lanes)

    def q_blocks(blk):
        both = range(len(blk))
        n_keys = [i + 1 for i in blk]
        mask_pad = [i < NSA_WIN_PREV for i in blk]
        step = [max(d for d in range(1, NSA_STEP + 1) if n % d == 0) for n in n_keys]
        step_keys = [s * QB for s in step]
        blocks_per_step = [s // NSA_SLC_BLOCK for s in step_keys]
        r0 = [i * QB for i in blk]
        qs = [jnp.concatenate(_heads_t(z_ref[pl.ds(r, QB), :_NQ], NSA_REPS, gq), axis=1).astype(BF16)
              for r in r0]

        sc = [_dot(kc, q) for q in qs]
        s_win = [_dot(kw_ref[pl.ds(r, win_kw), :], q) for r, q in zip(r0, qs)]

        def slc_scores(b, j):
            return _dot(ks_ref[j * step_keys[b]:(j + 1) * step_keys[b], :], qs[b])

        o_cmp, imp = [], []
        for b in both:
            ok = (cmp_end <= q_in_blk + blk[b] * QB) & cmp_real
            s = jnp.where(ok, sc[b], NEG)
            e = jnp.where(ok, jnp.exp2(s - jnp.max(s, axis=0, keepdims=True)), 0.0)
            p = e * (1.0 / jnp.maximum(jnp.sum(e, axis=0, keepdims=True), TINY))
            o_cmp.append(_dot(vct, p.astype(BF16)))
            p_sum = p[:, 0:QB] + p[:, QB:2 * QB] + p[:, 2 * QB:3 * QB] + p[:, 3 * QB:4 * QB]
            imp.append(jnp.dot(ovt_ref[...], p_sum, precision=_HI, preferred_element_type=F32))

        o_win = []
        for b in both:
            bias = bwin_ref[...]
            if mask_pad[b]:
                bias = jnp.where(win_row >= (NSA_WIN_PREV - blk[b]) * QB, bias, NEG)
            s = s_win[b] + bias
            pw = jnp.exp2(s - jnp.max(s, axis=0, keepdims=True))
            lw = jnp.sum(pw, axis=0, keepdims=True)
            pw = pw.astype(BF16)
            o = _dot(vwt_ref[blk[b]], pw[0:QB])
            for c in range(1, NSA_WIN_PREV + 1):
                o = o + _dot(vwt_ref[blk[b] + c], pw[c * QB:(c + 1) * QB])
            o_win.append(o * (1.0 / lw))

        drop = []
        for b in both:
            tb = lax.shift_right_logical(q_lane + blk[b] * QB, 6)
            forced = (blk_id == 0) | (blk_id == tb) | (blk_id == tb - 1)
            score = jnp.where(blk_id <= tb, imp[b] + jnp.where(forced, NSA_FORCE, 0.0), -NSA_FORCE)
            rank = jnp.zeros((NSA_N_SLC, QB), F32)
            for other in range(NSA_N_SLC):
                s_o = score[other:other + 1, :]
                beats = (s_o > score) | ((s_o == score) & (blk_id > other))
                rank = rank + jnp.where(beats, 1.0, 0.0)
            drop.append(jnp.where(rank < NSA_TOP_N, 0.0, NEG))

        states = [init for _ in both]
        n_steps = [n // s for n, s in zip(n_keys, step)]
        work = [(b, j) for j in range(max(n_steps)) for b in both if j < n_steps[b]]
        nxt = slc_scores(*work[0])
        for at, (b, j) in enumerate(work):
            cur = nxt
            nxt = slc_scores(*work[at + 1]) if at + 1 < len(work) else None
            bias = jnp.concatenate(
                [bslc_ref[blk[b] - (j * step[b] + c) + 1] for c in range(step[b])], axis=0)
            km = jnp.concatenate([jnp.broadcast_to(drop[b][k:k + 1], (NSA_SLC_BLOCK, QB))
                                  for k in range(j * blocks_per_step[b], (j + 1) * blocks_per_step[b])], axis=0)
            s = cur + bias + jnp.concatenate([km] * NSA_REPS, axis=1)
            states[b] = _flash_step_t(s, *states[b], vst_ref[:, j * step_keys[b]:(j + 1) * step_keys[b]])

        for b in both:
            o_slc = _flash_finish(*states[b])
            gate = jax.nn.sigmoid(z_ref[pl.ds(r0[b], QB), _OFF_GATE:_OFF_GATE + QB]).T
            outs = []
            for r in range(NSA_REPS):
                qsl = slice(r * QB, (r + 1) * QB)
                outs.append(gate[3 * r:3 * r + 1] * o_cmp[b][:, qsl] + gate[3 * r + 1:3 * r + 2] * o_slc[:, qsl]
                            + gate[3 * r + 2:3 * r + 3] * o_win[b][:, qsl])
            o_ref[pl.ds(r0[b], QB), :] = jnp.concatenate(outs, axis=0).T.astype(o_ref.dtype)
    for first in range(0, n_chunks // 2, NSA_GROUP // 2):
        q_blocks([i for p in range(first, first + NSA_GROUP // 2) for i in (p, n_chunks - 1 - p)])


def _nsa(z_nsa, zc, w1a, w1b, w2, pos, gq, gk, ovt, bslc, bwin, batch, seq):
    gw = NSA_GROUP_COLS
    n_chunks = seq // QB
    return pl.pallas_call(
        functools.partial(_nsa_kernel, seq),
        grid=(batch, NSA_KV_HEADS),
        in_specs=[pl.BlockSpec((None, seq, gw), lambda b, g: (b, 0, g)),
                  pl.BlockSpec((None, seq, 2 * HEAD_DIM), lambda b, g: (b, 0, g)),
                  _resident(w1a.shape), _resident(w1b.shape), _resident(w2.shape), _resident(pos.shape),
                  _resident(gq.shape), _resident(gk.shape), _resident(ovt.shape),
                  pl.BlockSpec((None,) + bslc.shape[1:], lambda b, g: (g, 0, 0, 0)),
                  pl.BlockSpec((None,) + bwin.shape[1:], lambda b, g: (g, 0, 0))],
        out_specs=pl.BlockSpec((None, seq, _NQ), lambda b, g: (b, 0, g)),
        out_shape=jax.ShapeDtypeStruct((batch, seq, NSA_HEADS * HEAD_DIM), BF16),
        scratch_shapes=[pltpu.VMEM((seq, HEAD_DIM), BF16),
                        pltpu.VMEM((HEAD_DIM, seq), BF16),
                        pltpu.VMEM((NSA_WIN_PREV * QB + seq, HEAD_DIM), BF16),
                        pltpu.VMEM((NSA_WIN_PREV + n_chunks, HEAD_DIM, QB), BF16)],
        compiler_params=_cparams(2),
        name="nsa",
    )(z_nsa, zc, w1a, w1b, w2, pos, gq, gk, ovt, bslc, bwin)


def _t5_bucket(dist):
    n = jnp.maximum(dist, 0)
    max_exact = NUM_BUCKETS // 2
    nf = jnp.maximum(n, 1).astype(F32)
    large = max_exact + (jnp.log(nf / max_exact) / math.log(MAX_DISTANCE / max_exact)
                         * (NUM_BUCKETS - max_exact)).astype(jnp.int32)
    large = jnp.minimum(large, NUM_BUCKETS - 1)
    return jnp.where(n < max_exact, n, large)


def _toeplitz(u, rows, cols):
    lead = u.shape[:-1]
    lu = rows + cols - 1
    assert u.shape[-1] == lu
    padded = jnp.pad(u, [(0, 0)] * len(lead) + [(0, 1)])
    flat = jnp.broadcast_to(padded[..., None, :], lead + (rows, lu + 1)).reshape(lead + (rows * (lu + 1),))
    return flat[..., :rows * lu].reshape(lead + (rows, lu))[..., rows - 1:]


def _bias_by_distance(bias_cols, delta, valid, dist_scale=1):
    bucket = _t5_bucket(jnp.asarray(np.maximum(delta, 0) * dist_scale, dtype=jnp.int32))
    return jnp.where(jnp.asarray(valid)[None, :], bias_cols.astype(F32)[bucket].T * LOG2E, NEG)


def _band_bias(bias_cols, window, n_prev, dist_scale, n_kv, reps):
    kw = (n_prev + 1) * QB
    delta = n_prev * QB + QB - 1 - np.arange(kw + QB - 1)
    u = _bias_by_distance(bias_cols, delta, (delta >= 0) & (delta < window), dist_scale)
    return _toeplitz(u, QB, kw).reshape(n_kv, reps * QB, kw)


def _slc_bias_t(bias_cols, seq):
    n_chunks = seq // QB
    delta = seq - 1 - np.arange(seq + QB - 1)
    strip = _toeplitz(_bias_by_distance(bias_cols, delta, delta >= 0), QB, seq)
    tile = jnp.flip(strip.reshape(NSA_KV_HEADS, NSA_REPS, QB, n_chunks, QB), axis=3)
    tile = jnp.transpose(tile, (0, 3, 4, 1, 2)).reshape(NSA_KV_HEADS, n_chunks, QB, NSA_REPS * QB)
    return jnp.concatenate([jnp.full_like(tile[:, :1], NEG), tile], axis=1)


def _rope_tables(seq):
    half = MLA_ROPE // 2
    inv = jnp.power(ROPE_THETA, -jnp.arange(half, dtype=F32) / half)
    ang = jnp.arange(seq, dtype=F32)[:, None] * inv[None, :]
    cos, sin = jnp.cos(ang), jnp.sin(ang)
    zeros = lambda w: jnp.zeros((seq, w), F32)
    tail = MLA_PAD - MLA_QK_DIM
    cos_t = jnp.concatenate([jnp.ones((seq, MLA_NOPE), F32), cos, cos, zeros(tail)], axis=1)
    sin_t = jnp.concatenate([zeros(MLA_NOPE), -sin, sin, zeros(tail)], axis=1)
    return cos_t, sin_t


def _pad_cols(w, width):
    return jnp.pad(w, ((0, 0), (0, width - w.shape[1])))


def _with_swapped_rope(a):
    half = MLA_ROPE // 2
    return jnp.concatenate([a, a[..., MLA_NOPE + half:], a[..., MLA_NOPE:MLA_NOPE + half]], axis=-1)


def _nsa_column_order():
    g_cols = NSA_KV_HEADS * HEAD_DIM
    q_cols = NSA_HEADS * HEAD_DIM
    order = []
    for g in range(NSA_KV_HEADS):
        cols = list(range(g * _NQ, (g + 1) * _NQ))
        for part in range(2, 6):
            start = q_cols + part * g_cols + g * HEAD_DIM
            cols += list(range(start, start + HEAD_DIM))
        gate0 = q_cols + 6 * g_cols + g * NSA_REPS * 3
        cols += list(range(gate0, gate0 + NSA_REPS * 3))
        order.append(cols)
    return order


def _nsa_cmp_columns():
    q_cols = NSA_HEADS * HEAD_DIM
    g_cols = NSA_KV_HEADS * HEAD_DIM
    cols = []
    for g in range(NSA_KV_HEADS):
        for part in range(2):
            start = q_cols + part * g_cols + g * HEAD_DIM
            cols += list(range(start, start + HEAD_DIM))
    return cols


def _nsa_mixer(z_nsa3, z_cmp, rel_bias, q_norm, k_norm, cmp_pos, cmp_w1, cmp_w2, batch, seq):
    w1 = cmp_w1.reshape(2, 2, NSA_CMP_STRIDE, HEAD_DIM, NSA_CMP_HIDDEN)
    zero1 = jnp.zeros_like(w1[0])
    w1_bd = jnp.concatenate([jnp.concatenate([w1[0], zero1], axis=-1),
                             jnp.concatenate([zero1, w1[1]], axis=-1)], axis=-2)
    zero2 = jnp.zeros_like(cmp_w2[0])
    w2_bd = jnp.concatenate([jnp.concatenate([cmp_w2[0], zero2], axis=-1),
                             jnp.concatenate([zero2, cmp_w2[1]], axis=-1)], axis=-2)
    pos_kv = jnp.concatenate([cmp_pos[0], cmp_pos[1]], axis=-1)
    n_cmp = (seq - NSA_CMP_LEN) // NSA_CMP_STRIDE + 1
    ci = np.arange(NSA_N_CMP_PAD)[:, None] * NSA_CMP_STRIDE
    sj = np.arange(NSA_N_SLC)[None, :] * NSA_SLC_BLOCK
    overlap = ((ci < sj + NSA_SLC_BLOCK) & (ci + NSA_CMP_LEN > sj) & (np.arange(NSA_N_CMP_PAD)[:, None] < n_cmp))
    ovt = jnp.asarray(overlap.T.astype(np.float32))
    nsa_cols = rel_bias[:, NSA_BIAS_COL0:NSA_BIAS_COL0 + NSA_HEADS]
    bslc = _slc_bias_t(nsa_cols, seq)
    bwin = jnp.transpose(_band_bias(nsa_cols, NSA_WINDOW, NSA_WIN_PREV, 1, NSA_KV_HEADS, NSA_REPS), (0, 2, 1))
    return _nsa(z_nsa3, z_cmp.reshape(batch, seq, -1), w1_bd[0].astype(BF16), w1_bd[1].astype(BF16),
                w2_bd.astype(BF16), pos_kv, q_norm.reshape(HEAD_DIM, 1), k_norm.reshape(3, 1, HEAD_DIM),
                ovt, bslc, bwin, batch, seq)


def kernel(x, rel_bias, ffn1_norm, ffn1_w_gate, ffn1_w_up, ffn1_w_down, mix_norm, ffn2_norm, ffn2_w_gate,
           ffn2_w_up, ffn2_w_down, ab_w_in, mla_q_a_norm, mla_w_q_b, mla_kv_a_norm, mla_w_kv_b, mla_q_norm,
           mla_k_norm, dil_q_norm, dil_k_norm, ab_w_out, cd_w_in, swa_q_norm, swa_k_norm, swa_sinks,
           nsa_q_norm, nsa_k_norm, nsa_cmp_pos, nsa_cmp_w1, nsa_cmp_w2, cd_w_out):
    batch, seq, _ = x.shape
    n = batch * seq
    assert seq % (16 * QB) == 0 and n % TM == 0 and seq % TM == 0
    bf = lambda a: a.astype(BF16)
    xf = x.reshape(n, D_MODEL)
    ffn1 = (bf(ffn1_w_gate), bf(ffn1_w_up), bf(ffn1_w_down))
    ffn2 = (bf(ffn2_w_gate), bf(ffn2_w_up), bf(ffn2_w_down))

    w_in = ab_w_in[0]
    mla_cols = MLA_Q_RANK + MLA_KV_RANK
    w_krope = _with_swapped_rope(jnp.pad(w_in[:, mla_cols:mla_cols + MLA_ROPE], ((0, 0), (MLA_NOPE, 0))))
    w_mla = jnp.concatenate([w_in[:, :mla_cols], w_krope], axis=1)
    wq = _with_swapped_rope(mla_w_q_b[0].reshape(MLA_Q_RANK, MLA_HEADS, MLA_QK_DIM))
    wq = wq.reshape(MLA_Q_RANK, MLA_HEADS * MLA_PAD)
    wkv = mla_w_kv_b[0].reshape(MLA_KV_RANK, MLA_HEADS, MLA_NOPE + MLA_V)
    wk = _pad_cols(wkv[:, :, :MLA_NOPE].reshape(MLA_KV_RANK * MLA_HEADS, MLA_NOPE), MLA_PAD)
    wk = wk.reshape(MLA_KV_RANK, MLA_HEADS * MLA_PAD)
    wv = wkv[:, :, MLA_NOPE:].reshape(MLA_KV_RANK, MLA_HEADS * MLA_V)
    cos_t, sin_t = _rope_tables(seq)
    q_scale = MLA_QK_DIM ** -0.5 * LOG2E
    xf, z_dil, q_mla, k_mla, vt_mla = _ffn_proj_mla(
        xf, ffn1_norm[0], ffn1, 0, mix_norm[0], bf(w_mla), bf(w_in[:, mla_cols + MLA_ROPE:]),
        mla_q_a_norm[0].reshape(1, -1), mla_kv_a_norm[0].reshape(1, -1), bf(wq), bf(wk), bf(wv),
        _with_swapped_rope(mla_q_norm[0].reshape(1, -1)), _with_swapped_rope(mla_k_norm[0].reshape(1, -1)),
        cos_t * q_scale, sin_t * q_scale, cos_t, sin_t, seq)
    o_a = _mla_attn(q_mla.reshape(batch, seq, -1), k_mla.reshape(batch, seq, -1),
                    vt_mla.reshape(batch, seq // MLA_TQ, MLA_HEADS * MLA_V, MLA_TQ),
                    batch, seq).reshape(n, MLA_HEADS * MLA_V)

    z_dil3 = z_dil.reshape(batch, seq, -1)
    gq, gk = dil_q_norm[0], dil_k_norm[0]
    dil_outs, dil_lses = [], []
    for grp, (window, dilation) in enumerate(DIL_PATTERNS):
        bm = _band_bias(rel_bias[:, grp * DIL_HPG:(grp + 1) * DIL_HPG], window // dilation + 1, 1, dilation,
                        DIL_HPG, 1)
        o, lse = _dilated_group(z_dil3, batch, seq, grp, dilation, gq, gk, bm)
        dil_outs.append(o)
        dil_lses.append(lse)
    w_out = ab_w_out[0]
    xf = _out_ab_ffn(xf, o_a, dil_outs, dil_lses, bf(w_out[:MLA_HEADS * MLA_V]), bf(w_out[MLA_HEADS * MLA_V:]),
                     ffn2_norm[0], ffn2, 0)

    w_in = cd_w_in[0]
    swa_cols = (SWA_HEADS + 2 * SWA_KV_HEADS) * HEAD_DIM
    w_nsa_src = w_in[:, swa_cols:]
    w_nsa = jnp.concatenate([_pad_cols(w_nsa_src[:, np.asarray(cols)], NSA_GROUP_COLS)
                             for cols in _nsa_column_order()], axis=1)
    w_cmp = w_nsa_src[:, np.asarray(_nsa_cmp_columns())]
    xf, z_swa, z_nsa, z_cmp = _ffn_proj(xf, ffn1_norm[1], ffn1, 1, mix_norm[1],
                                        [bf(w_in[:, :swa_cols]), bf(w_nsa), bf(w_cmp)])

    swa_reps = SWA_HEADS // SWA_KV_HEADS
    swa_prev = -(-(SWA_WINDOW - 1) // QB)
    bm_swa = _band_bias(rel_bias[:, :SWA_HEADS], SWA_WINDOW, swa_prev, 1, SWA_KV_HEADS, swa_reps)
    sinks = jnp.broadcast_to((swa_sinks[0].astype(F32) * LOG2E).reshape(SWA_KV_HEADS, 1, swa_reps, 1),
                             (SWA_KV_HEADS, 1, swa_reps, QB)).reshape(SWA_KV_HEADS, 1, swa_reps * QB)
    o_c = _swa(z_swa.reshape(batch, seq, -1), batch, seq, swa_q_norm[0].reshape(HEAD_DIM, 1),
               swa_k_norm[0].reshape(1, HEAD_DIM), bm_swa, sinks)

    o_d = _nsa_mixer(z_nsa.reshape(batch, seq, NSA_KV_HEADS * NSA_GROUP_COLS), z_cmp, rel_bias, nsa_q_norm[0],
                     nsa_k_norm[0], nsa_cmp_pos[0], nsa_cmp_w1[0], nsa_cmp_w2[0], batch, seq)

    w_out = cd_w_out[0]
    xf = _out_cd_ffn(xf, o_c.reshape(n, -1), o_d.reshape(n, -1), bf(w_out[:SWA_HEADS * HEAD_DIM]),
                     bf(w_out[SWA_HEADS * HEAD_DIM:]), ffn2_norm[1], ffn2, 1)
    return xf.reshape(batch, seq, D_MODEL)
```

```python
import functools
import math

import numpy as np
import jax
import jax.numpy as jnp
from jax import lax
from jax.experimental import pallas as pl
from jax.experimental.pallas import tpu as pltpu

F32 = jnp.float32
BF16 = jnp.bfloat16

EPS = 1e-6
NEG = -1e30
TINY = 1e-30
LOG2E = math.log2(math.e)
LN2 = math.log(2.0)
D_MODEL = 1024
D_FF = 2816
NUM_BUCKETS = 32
MAX_DISTANCE = 2048
HEAD_DIM = 64
QB = 128

MLA_HEADS = 8
MLA_Q_RANK = 256
MLA_KV_RANK = 128
MLA_NOPE = 64
MLA_ROPE = 32
MLA_V = 64
MLA_QK_DIM = MLA_NOPE + MLA_ROPE
MLA_PAD = 128
ROPE_THETA = 10000.0

DIL_PATTERNS = ((128, 1), (512, 4), (2048, 16))
DIL_HPG = 4
SWA_HEADS = 8
SWA_KV_HEADS = 2
SWA_WINDOW = 128
NSA_HEADS = 8
NSA_KV_HEADS = 2
NSA_CMP_LEN = 32
NSA_CMP_STRIDE = 16
NSA_CMP_HIDDEN = 128
NSA_SLC_BLOCK = 64
NSA_TOP_N = 16
NSA_WINDOW = 512
NSA_FORCE = 1e6
NSA_BIAS_COL0 = 8
NSA_GROUP_COLS = 640

VMEM_LIMIT = 56 * 1024 * 1024
TM = 512

_NT = (((1,), (1,)), ((), ()))
_HI = lax.Precision.HIGHEST


def _cparams(n_axes):
    return pltpu.CompilerParams(dimension_semantics=("arbitrary",) * n_axes,
                                vmem_limit_bytes=VMEM_LIMIT)


def _resident(shape):
    nd = len(shape)
    return pl.BlockSpec(shape, lambda *_: (0,) * nd, pipeline_mode=pl.Buffered(1))


def _layer_block(w, layer):
    nd = w.ndim - 1
    return pl.BlockSpec((None,) + w.shape[1:], lambda *_: (layer,) + (0,) * nd, pipeline_mode=pl.Buffered(1))


def _rms(x):
    return x * lax.rsqrt(jnp.mean(x * x, axis=-1, keepdims=True) + EPS)


def _dot(a, b):
    return jnp.dot(a, b, preferred_element_type=F32)


def _dot_nt(a, b, precision=None):
    return lax.dot_general(a, b, _NT, preferred_element_type=F32, precision=precision)


FF_CHUNK = 256


N_CHUNK = 512


def _swiglu_half_step(x, g_ref, wg_ref, wu_ref, wd_ref):
    hb = (_rms(x) * g_ref[...]).astype(BF16)
    acc = jnp.zeros(x.shape, F32)
    for c in range(D_FF // FF_CHUNK):
        sl = slice(c * FF_CHUNK, (c + 1) * FF_CHUNK)
        gate = _dot(hb, wg_ref[:, sl])
        up = _dot(hb, wu_ref[:, sl])
        act = (gate * jax.nn.sigmoid(gate) * up).astype(BF16)
        acc = acc + _dot(act, wd_ref[sl, :])
    return x + 0.5 * acc


def _ffn_proj_kernel(n_out, x_ref, g_ref, wg_ref, wu_ref, wd_ref, gm_ref, *refs):
    w_refs, x_out_ref, z_refs = refs[:n_out], refs[n_out], refs[n_out + 1:]
    x = _swiglu_half_step(x_ref[...], g_ref, wg_ref, wu_ref, wd_ref)
    x_out_ref[...] = x
    hb = (_rms(x) * gm_ref[...]).astype(BF16)
    for w_ref, z_ref in zip(w_refs, z_refs):
        width = w_ref.shape[1]
        for c0 in range(0, width, N_CHUNK):
            sl = slice(c0, min(c0 + N_CHUNK, width))
            z_ref[:, sl] = _dot(hb, w_ref[:, sl]).astype(z_ref.dtype)


def _tile(width):
    return pl.BlockSpec((TM, width), lambda i: (i, 0))


def _ffn_proj(x, g, ffn, layer, g_mix, ws, z_dtypes=None):
    n = x.shape[0]
    z_dtypes = z_dtypes or [F32] * len(ws)
    ffn_w = [_layer_block(w, layer) for w in ffn]
    return pl.pallas_call(
        functools.partial(_ffn_proj_kernel, len(ws)),
        grid=(n // TM,),
        in_specs=[_tile(D_MODEL), _resident((1, D_MODEL))] + ffn_w + [_resident((1, D_MODEL))]
        + [_resident(w.shape) for w in ws],
        out_specs=[_tile(D_MODEL)] + [_tile(w.shape[1]) for w in ws],
        out_shape=[jax.ShapeDtypeStruct((n, D_MODEL), F32)]
        + [jax.ShapeDtypeStruct((n, w.shape[1]), dt) for w, dt in zip(ws, z_dtypes)],
        compiler_params=_cparams(1),
        name="ffn_proj",
    )(x, g.reshape(1, D_MODEL), *ffn, g_mix.reshape(1, D_MODEL), *ws)


def _out_ab_ffn_kernel(x_ref, oa_ref, o0_ref, o1_ref, o2_ref, l0_ref, l1_ref, l2_ref, wa_ref, wb_ref,
                       g_ref, wg_ref, wu_ref, wd_ref, out_ref):
    l0, l1, l2 = l0_ref[...], l1_ref[...], l2_ref[...]
    m = jnp.maximum(jnp.maximum(l0, l1), l2)
    e0, e1, e2 = jnp.exp(l0 - m), jnp.exp(l1 - m), jnp.exp(l2 - m)
    ob = (e0 * o0_ref[...] + e1 * o1_ref[...] + e2 * o2_ref[...]) / (e0 + e1 + e2)
    x = x_ref[...] + _dot(oa_ref[...], wa_ref[...]) + _dot(ob.astype(BF16), wb_ref[...])
    out_ref[...] = _swiglu_half_step(x, g_ref, wg_ref, wu_ref, wd_ref)


def _out_ab_ffn(x, oa, dil_outs, dil_lses, wa, wb, g, ffn, layer):
    n = x.shape[0]
    dw = DIL_HPG * HEAD_DIM
    ffn_w = [_layer_block(w, layer) for w in ffn]
    return pl.pallas_call(
        _out_ab_ffn_kernel,
        grid=(n // TM,),
        in_specs=[_tile(D_MODEL), _tile(oa.shape[1])] + [_tile(dw)] * 6 + [_resident(wa.shape), _resident(wb.shape),
                                                                        _resident((1, D_MODEL))] + ffn_w,
        out_specs=_tile(D_MODEL),
        out_shape=jax.ShapeDtypeStruct((n, D_MODEL), F32),
        compiler_params=_cparams(1),
        name="out_ab_ffn",
    )(x, oa, *dil_outs, *dil_lses, wa, wb, g.reshape(1, D_MODEL), *ffn)


def _out_cd_ffn_kernel(x_ref, oc_ref, od_ref, wc_ref, wdd_ref, g_ref, wg_ref, wu_ref, wd_ref, out_ref):
    x = x_ref[...] + _dot(oc_ref[...], wc_ref[...]) + _dot(od_ref[...], wdd_ref[...])
    out_ref[...] = _swiglu_half_step(x, g_ref, wg_ref, wu_ref, wd_ref)


def _out_cd_ffn(x, oc, od, wc, wdd, g, ffn, layer):
    n = x.shape[0]
    ffn_w = [_layer_block(w, layer) for w in ffn]
    return pl.pallas_call(
        _out_cd_ffn_kernel,
        grid=(n // TM,),
        in_specs=[_tile(D_MODEL), _tile(oc.shape[1]), _tile(od.shape[1]), _resident(wc.shape), _resident(wdd.shape),
                  _resident((1, D_MODEL))] + ffn_w,
        out_specs=_tile(D_MODEL),
        out_shape=jax.ShapeDtypeStruct((n, D_MODEL), F32),
        compiler_params=_cparams(1),
        name="out_cd_ffn",
    )(x, oc, od, wc, wdd, g.reshape(1, D_MODEL), *ffn)


MLA_TQ = 256
MLA_STEP = 2


def _ffn_proj_mla_kernel(x_ref, g_ref, wg_ref, wu_ref, wd_ref, gm_ref, wmla_ref, wdil_ref,
                         qan_ref, kvan_ref, wq_ref, wk_ref, wv_ref, gq_ref, gk_ref,
                         qc_ref, qs_ref, kc_ref, ks_ref,
                         x_out_ref, zdil_ref, q_ref, k_ref, vt_ref, zprev_ref):
    @pl.when(pl.program_id(0) == 0)
    def _():
        zprev_ref[...] = jnp.zeros(zprev_ref.shape, F32)

    cq = (_rms(zprev_ref[:, :MLA_Q_RANK]) * qan_ref[...]).astype(BF16)
    ckv = (_rms(zprev_ref[:, MLA_Q_RANK:MLA_Q_RANK + MLA_KV_RANK]) * kvan_ref[...]).astype(BF16)
    kr = zprev_ref[:, MLA_Q_RANK + MLA_KV_RANK:]
    real = jnp.where(lax.broadcasted_iota(jnp.int32, (1, MLA_PAD), 1) < MLA_QK_DIM, 1.0, 0.0)

    def head_norm_rope(x, gain, cos, sin):
        ssq = jnp.sum(x * x * real, axis=-1, keepdims=True)
        x = x * lax.rsqrt(ssq * (1.0 / MLA_QK_DIM) + EPS) * gain
        return x * cos + pltpu.roll(x, MLA_PAD - MLA_ROPE, 1) * sin

    def head(h):
        sl = slice(h * MLA_PAD, (h + 1) * MLA_PAD)
        qh = head_norm_rope(_dot(cq, wq_ref[:, sl]), gq_ref[...], qc_ref[...], qs_ref[...])
        q_ref[:, sl] = qh.astype(BF16)
        kh = head_norm_rope(_dot(ckv, wk_ref[:, sl]) + kr, gk_ref[...], kc_ref[...], ks_ref[...])
        k_ref[:, sl] = kh.astype(BF16)

    def values():
        v = _dot(ckv, wv_ref[...])
        for c in range(TM // MLA_TQ):
            vt_ref[c] = v[c * MLA_TQ:(c + 1) * MLA_TQ].T.astype(BF16)

    latent_work = [functools.partial(head, h) for h in range(MLA_HEADS)] + [values]

    x = x_ref[...]
    hb = (_rms(x) * g_ref[...]).astype(BF16)
    acc = jnp.zeros(x.shape, F32)
    for c in range(D_FF // FF_CHUNK):
        sl = slice(c * FF_CHUNK, (c + 1) * FF_CHUNK)
        gate = _dot(hb, wg_ref[:, sl])
        up = _dot(hb, wu_ref[:, sl])
        act = (gate * jax.nn.sigmoid(gate) * up).astype(BF16)
        acc = acc + _dot(act, wd_ref[sl, :])
        if c < len(latent_work):
            latent_work[c]()
    x = x + 0.5 * acc
    x_out_ref[...] = x
    hm = (_rms(x) * gm_ref[...]).astype(BF16)
    zprev_ref[...] = _dot(hm, wmla_ref[...])
    width = wdil_ref.shape[1]
    for c0 in range(0, width, N_CHUNK):
        sl = slice(c0, min(c0 + N_CHUNK, width))
        zdil_ref[:, sl] = _dot(hm, wdil_ref[:, sl])


def _ffn_proj_mla(x, g, ffn, layer, g_mix, w_mla, w_dil, qan, kvan, wq, wk, wv, gq, gk,
                  q_cos, q_sin, k_cos, k_sin, seq):
    n = x.shape[0]
    n_tiles = n // TM
    assert D_FF // FF_CHUNK >= MLA_HEADS + 1
    per_seq = seq // TM
    hw = MLA_HEADS * MLA_PAD
    vw = MLA_HEADS * MLA_V

    def cur(width):
        return pl.BlockSpec((TM, width), lambda i: (jnp.minimum(i, n_tiles - 1), 0))

    def prev(width):
        return pl.BlockSpec((TM, width), lambda i: (jnp.maximum(i - 1, 0), 0))

    rope_spec = pl.BlockSpec((TM, MLA_PAD), lambda i: (jnp.maximum(i - 1, 0) % per_seq, 0))
    small = [qan, kvan, wq, wk, wv, gq, gk]
    return pl.pallas_call(
        _ffn_proj_mla_kernel,
        grid=(n_tiles + 1,),
        in_specs=[cur(D_MODEL), _resident((1, D_MODEL))] + [_layer_block(w, layer) for w in ffn]
        + [_resident((1, D_MODEL)), _resident(w_mla.shape), _resident(w_dil.shape)]
        + [_resident(a.shape) for a in small] + [rope_spec] * 4,
        out_specs=[cur(D_MODEL), cur(w_dil.shape[1]), prev(hw), prev(hw),
                   pl.BlockSpec((TM // MLA_TQ, vw, MLA_TQ), lambda i: (jnp.maximum(i - 1, 0), 0, 0))],
        out_shape=[jax.ShapeDtypeStruct((n, D_MODEL), F32), jax.ShapeDtypeStruct((n, w_dil.shape[1]), F32),
                   jax.ShapeDtypeStruct((n, hw), BF16), jax.ShapeDtypeStruct((n, hw), BF16),
                   jax.ShapeDtypeStruct((n // MLA_TQ, vw, MLA_TQ), BF16)],
        scratch_shapes=[pltpu.VMEM((TM, w_mla.shape[1]), F32)],
        compiler_params=_cparams(1),
        name="ffn_proj_mla",
    )(x, g.reshape(1, D_MODEL), *ffn, g_mix.reshape(1, D_MODEL), w_mla, w_dil, *small,
      q_cos, q_sin, k_cos, k_sin)


def _flash_step_t(s, m, l, acc, pv, vt):
    m_new = jnp.maximum(m, jnp.max(s, axis=0, keepdims=True))
    p = jnp.exp2(s - m_new)
    alpha = jnp.exp2(m - m_new)
    l = alpha * l + jnp.sum(p, axis=0, keepdims=True)
    acc = alpha * (acc + pv)
    return m_new, l, acc, _dot(vt, p.astype(BF16))


def _flash_init(dv, n_queries):
    return (jnp.full((1, n_queries), NEG, F32), jnp.zeros((1, n_queries), F32), jnp.zeros((dv, n_queries), F32),
            jnp.zeros((dv, n_queries), F32))


def _flash_finish(m, l, acc, pv):
    return (acc + pv) * (1.0 / l)


def _mla_attn_kernel(seq, q_ref, k_ref, vt_ref, o_ref):
    t = MLA_TQ
    key = lax.broadcasted_iota(jnp.int32, (t, t), 0)
    qry = lax.broadcasted_iota(jnp.int32, (t, t), 1)
    diag_ok = key <= qry
    heads = [(slice(hh * MLA_PAD, (hh + 1) * MLA_PAD), slice(hh * MLA_V, (hh + 1) * MLA_V)) for hh in range(2)]

    init = _flash_init(MLA_V, t)
    n_blocks = seq // t
    for first in range(0, n_blocks // 2, 2):
        blocks = [i for p in (first, first + 1) for i in (p, n_blocks - 1 - p)]
        chains = [(i, h) for i in blocks for h in range(2)]
        qs = {(i, h): q_ref[i * t:(i + 1) * t, heads[h][0]] for i, h in chains}
        steps = {i: [(j, min(j + MLA_STEP, i + 1)) for j in range(0, i + 1, MLA_STEP)] for i in blocks}

        def scores(chain, step):
            i, h = chain
            lo, hi = steps[i][step]
            s = _dot_nt(k_ref[lo * t:hi * t, heads[h][0]], qs[chain])
            if hi == i + 1:
                last = jnp.where(diag_ok, s[(i - lo) * t:], NEG)
                s = last if i == lo else jnp.concatenate([s[:(i - lo) * t], last], axis=0)
            return s

        def values(chain, step):
            i, h = chain
            lo, hi = steps[i][step]
            return jnp.concatenate([vt_ref[j, heads[h][1], :] for j in range(lo, hi)], axis=1)

        cur = {c: scores(c, 0) for c in chains}
        states = {c: init for c in chains}
        for step in range(max(len(s) for s in steps.values())):
            for c in chains:
                n_steps = len(steps[c[0]])
                if step >= n_steps:
                    continue
                nxt = scores(c, step + 1) if step + 1 < n_steps else None
                states[c] = _flash_step_t(cur[c], *states[c], values(c, step))
                cur[c] = nxt
        for i in blocks:
            outs = [_flash_finish(*states[(i, h)]) for h in range(2)]
            o_ref[i * t:(i + 1) * t, :] = jnp.concatenate(outs, axis=0).T.astype(o_ref.dtype)


def _mla_attn(q, k, vt, batch, seq):
    return pl.pallas_call(
        functools.partial(_mla_attn_kernel, seq),
        grid=(batch, MLA_HEADS // 2),
        in_specs=[pl.BlockSpec((None, seq, 2 * MLA_PAD), lambda b, h: (b, 0, h)),
                  pl.BlockSpec((None, seq, 2 * MLA_PAD), lambda b, h: (b, 0, h)),
                  pl.BlockSpec((None, seq // MLA_TQ, 2 * MLA_V, MLA_TQ), lambda b, h: (b, 0, h, 0))],
        out_specs=pl.BlockSpec((None, seq, 2 * MLA_V), lambda b, h: (b, 0, h)),
        out_shape=jax.ShapeDtypeStruct((batch, seq, MLA_HEADS * MLA_V), BF16),
        compiler_params=_cparams(2),
        name="mla_attn",
    )(q, k, vt)


def _head_norm(x, gain):
    return _rms(x) * gain


def _heads_t(qt, n_heads, gain):
    tiles = []
    for c0 in range(0, n_heads * HEAD_DIM, 2 * HEAD_DIM):
        pair = qt[:, c0:c0 + 2 * HEAD_DIM].T
        for h in range(2):
            x = pair[h * HEAD_DIM:(h + 1) * HEAD_DIM]
            tiles.append(x * lax.rsqrt(jnp.mean(x * x, axis=0, keepdims=True) + EPS) * gain)
    return tiles


def _q_gain(gq_ref):
    return jnp.broadcast_to(gq_ref[...] * (HEAD_DIM ** -0.5 * LOG2E), (HEAD_DIM, QB))


def _pair_ones():
    shape = (2 * HEAD_DIM, 2 * HEAD_DIM)
    same = lax.broadcasted_iota(jnp.int32, shape, 0) // HEAD_DIM == lax.broadcasted_iota(jnp.int32, shape, 1) // HEAD_DIM
    return jnp.where(same, 1.0, 0.0).astype(BF16)


def _pair_norm(x, seg, gain):
    sq = x * x
    hi = sq.astype(BF16)
    lo = (sq - hi.astype(F32)).astype(BF16)
    ssq = _dot(hi, seg) + _dot(lo, seg)
    return x * lax.rsqrt(ssq * (1.0 / HEAD_DIM) + EPS) * gain


BAND_KV = 2


def _banded_kernel(seq, dilation, reps, n_prev, with_sink, with_lse, group_size, paired, *refs):
    q_ref, k_ref, v_ref, gq_ref, gk_ref, bm_ref = refs[:6]
    refs = refs[6:]
    sink_ref = None
    if with_sink:
        sink_ref, refs = refs[0], refs[1:]
    o_ref, refs = refs[0], refs[1:]
    lse_ref = None
    if with_lse:
        lse_ref, refs = refs[0], refs[1:]
    kn_ref, vt_ref = refs
    length = seq // dilation
    n_blocks = length // QB
    pad = n_prev * QB
    kw = pad + QB
    win_row = lax.broadcasted_iota(jnp.int32, (kw, 1), 0)
    assert not paired or reps == 1
    if paired:
        gq, gk = gq_ref[...] * (HEAD_DIM ** -0.5 * LOG2E), gk_ref[...]
        seg = _pair_ones()
        lane = lax.broadcasted_iota(jnp.int32, (1, BAND_KV * HEAD_DIM), 1)
        head_lanes = [jnp.where(lane // HEAD_DIM == g, 1.0, 0.0) for g in range(BAND_KV)]
    else:
        gq, gk = _q_gain(gq_ref), gk_ref[...]

    def rows(start, size):
        return pl.ds(start, size) if dilation == 1 else pl.ds(start, size, stride=dilation)

    for res in range(dilation):
        kall, vall = k_ref[rows(res, length), :], v_ref[rows(res, length), :]
        if paired:
            kn_ref[res, :pad, :] = jnp.zeros((pad, BAND_KV * HEAD_DIM), BF16)
            kn_ref[res, pad:, :] = _pair_norm(kall, seg, gk).astype(BF16)
        else:
            for g in range(BAND_KV):
                kn_ref[res, g, :pad, :] = jnp.zeros((pad, HEAD_DIM), BF16)
                kn_ref[res, g, pad:, :] = _head_norm(kall[:, g * HEAD_DIM:(g + 1) * HEAD_DIM], gk).astype(BF16)
        for j in range(n_prev):
            vt_ref[res, j] = jnp.zeros((BAND_KV * HEAD_DIM, QB), BF16)
        for j in range(n_blocks):
            vt_ref[res, n_prev + j] = vall[j * QB:(j + 1) * QB].T.astype(BF16)

    def attend(res, i, g, s):
        bias = bm_ref[g]
        if i < n_prev:
            bias = jnp.where(win_row >= (n_prev - i) * QB, bias, NEG)
        s = s + bias
        m = jnp.max(s, axis=0, keepdims=True)
        if with_sink:
            m = jnp.maximum(m, sink_ref[g])
        p = jnp.exp2(s - m)
        l = jnp.sum(p, axis=0, keepdims=True)
        if with_sink:
            l = l + jnp.exp2(sink_ref[g] - m)
        p = p.astype(BF16)
        vl = slice(g * HEAD_DIM, (g + 1) * HEAD_DIM)
        o = _dot(vt_ref[res, i, vl, :], p[0:QB])
        for w in range(1, n_prev + 1):
            o = o + _dot(vt_ref[res, i + w, vl, :], p[w * QB:(w + 1) * QB])
        return o * (1.0 / l), m * LN2 + jnp.log(l)

    items = [(res, i) for res in range(dilation) for i in range(n_blocks)]
    for at in range(0, len(items), group_size):
        group = items[at:at + group_size]
        toks = [rows(res + i * QB * dilation, QB) for res, i in group]
        if paired:
            qns = [_pair_norm(q_ref[tok, :], seg, gq) for tok in toks]
            qss = [[(qn * head_lanes[g]).astype(BF16) for g in range(BAND_KV)] for qn in qns]
            scores = [[_dot_nt(kn_ref[res, i * QB:i * QB + kw, :], qs[g]) for g in range(BAND_KV)]
                      for (res, i), qs in zip(group, qss)]
        else:
            heads = [_heads_t(q_ref[tok, :], BAND_KV * reps, gq) for tok in toks]
            qss = [[jnp.concatenate(h[g * reps:(g + 1) * reps], axis=1).astype(BF16) for g in range(BAND_KV)]
                   for h in heads]
            scores = [[_dot(kn_ref[res, g, i * QB:i * QB + kw, :], qs[g]) for g in range(BAND_KV)]
                      for (res, i), qs in zip(group, qss)]
        for (res, i), tok, sc in zip(group, toks, scores):
            outs, lses = [], []
            for g in range(BAND_KV):
                o, lse = attend(res, i, g, sc[g])
                for r in range(reps):
                    outs.append(o[:, r * QB:(r + 1) * QB])
                    if with_lse:
                        lses.append(jnp.broadcast_to(lse[:, r * QB:(r + 1) * QB], (HEAD_DIM, QB)))
            o_ref[tok, :] = jnp.concatenate(outs, axis=0).T.astype(o_ref.dtype)
            if with_lse:
                lse_ref[tok, :] = jnp.concatenate(lses, axis=0).T


def _banded_scratch(seq, dilation, n_prev, paired):
    length = seq // dilation
    rows = n_prev * QB + length
    keys = (dilation, rows, BAND_KV * HEAD_DIM) if paired else (dilation, BAND_KV, rows, HEAD_DIM)
    return [pltpu.VMEM(keys, BF16),
            pltpu.VMEM((dilation, n_prev + length // QB, BAND_KV * HEAD_DIM, QB), BF16)]


PAIRED_MIN_DILATION = 16


def _dilated_group(z_dil, batch, seq, group, dilation, q_norm, k_norm, bm):
    gw = DIL_HPG * HEAD_DIM
    n_groups = len(DIL_PATTERNS)
    hps = BAND_KV
    halves = DIL_HPG // hps
    bm_t = jnp.transpose(bm, (0, 2, 1))
    paired = dilation >= PAIRED_MIN_DILATION
    if paired:
        gq = jnp.tile(q_norm.reshape(1, HEAD_DIM), (1, BAND_KV))
        gk = jnp.tile(k_norm.reshape(1, HEAD_DIM), (1, BAND_KV))
    else:
        gq, gk = q_norm.reshape(HEAD_DIM, 1), k_norm.reshape(1, HEAD_DIM)

    def spec(part):
        return pl.BlockSpec((None, seq, hps * HEAD_DIM), lambda b, h: (b, 0, (part * n_groups + group) * halves + h))

    out_spec = pl.BlockSpec((None, seq, hps * HEAD_DIM), lambda b, h: (b, 0, h))
    shape = jax.ShapeDtypeStruct((batch, seq, gw), F32)
    out, lse = pl.pallas_call(
        functools.partial(_banded_kernel, seq, dilation, 1, 1, False, True, 16, paired),
        grid=(batch, halves),
        in_specs=[spec(0), spec(1), spec(2), _resident(gq.shape), _resident(gk.shape),
                  pl.BlockSpec((hps,) + bm_t.shape[1:], lambda b, h: (h, 0, 0))],
        out_specs=[out_spec, out_spec],
        out_shape=[shape, shape],
        scratch_shapes=_banded_scratch(seq, dilation, 1, paired),
        compiler_params=_cparams(2),
        name="dilated_g%d" % group,
    )(z_dil, z_dil, z_dil, gq, gk, bm_t)
    return out.reshape(batch * seq, gw), lse.reshape(batch * seq, gw)


def _swa(z_swa, batch, seq, gq, gk, bm, sinks):
    qw = SWA_HEADS * HEAD_DIM
    kvw = SWA_KV_HEADS * HEAD_DIM
    reps = SWA_HEADS // SWA_KV_HEADS
    n_prev = -(-(SWA_WINDOW - 1) // QB)
    assert SWA_KV_HEADS == BAND_KV
    bm_t = jnp.transpose(bm, (0, 2, 1))
    return pl.pallas_call(
        functools.partial(_banded_kernel, seq, 1, reps, n_prev, True, False, 4, False),
        grid=(batch,),
        in_specs=[pl.BlockSpec((None, seq, qw), lambda b: (b, 0, 0)),
                  pl.BlockSpec((None, seq, kvw), lambda b: (b, 0, qw // kvw)),
                  pl.BlockSpec((None, seq, kvw), lambda b: (b, 0, qw // kvw + 1)),
                  _resident(gq.shape), _resident(gk.shape), _resident(bm_t.shape), _resident(sinks.shape)],
        out_specs=pl.BlockSpec((None, seq, qw), lambda b: (b, 0, 0)),
        out_shape=jax.ShapeDtypeStruct((batch, seq, qw), BF16),
        scratch_shapes=_banded_scratch(seq, 1, n_prev, False),
        compiler_params=_cparams(1),
        name="swa",
    )(z_swa, z_swa, z_swa, gq, gk, bm_t, sinks)


NSA_REPS = NSA_HEADS // NSA_KV_HEADS
NSA_N_SLC = 32
NSA_N_CMP_PAD = 128
NSA_WIN_PREV = -(-(NSA_WINDOW - 1) // QB)
_NQ = NSA_REPS * HEAD_DIM
_OFF_KS, _OFF_VS, _OFF_KW, _OFF_VW, _OFF_GATE = (_NQ + i * HEAD_DIM for i in range(5))


NSA_STEP = 16
NSA_GROUP = 8


def _nsa_kernel(seq, z_ref, zc_ref, w1a_ref, w1b_ref, w2_ref, pos_ref, gq_ref, gk_ref, ovt_ref,
                bslc_ref, bwin_ref, o_ref, ks_ref, vst_ref, kw_ref, vwt_ref):
    n_chunks = seq // QB
    lanes = NSA_REPS * QB
    gq = _q_gain(gq_ref)
    win_pad = NSA_WIN_PREV * QB
    win_kw = win_pad + QB

    first = jnp.zeros((NSA_N_CMP_PAD, 2 * NSA_CMP_HIDDEN), F32)
    second = jnp.zeros((NSA_N_CMP_PAD, 2 * NSA_CMP_HIDDEN), F32)
    for l in range(NSA_CMP_STRIDE):
        tok = zc_ref[pl.ds(l, NSA_N_CMP_PAD, stride=NSA_CMP_STRIDE), :]
        first = first + _dot((tok + pos_ref[l:l + 1, :]).astype(BF16), w1a_ref[l])
        second = second + _dot((tok + pos_ref[NSA_CMP_STRIDE + l:NSA_CMP_STRIDE + l + 1, :]).astype(BF16),
                               w1b_ref[l])
    hidden = jax.nn.gelu(first + pltpu.roll(second, NSA_N_CMP_PAD - 1, 0)).astype(BF16)
    cmp_kv = _dot(hidden, w2_ref[...])
    kc = _head_norm(cmp_kv[:, :HEAD_DIM], gk_ref[0]).astype(BF16)
    vct = cmp_kv.T[HEAD_DIM:].astype(BF16)

    ks_ref[...] = _head_norm(z_ref[:, _OFF_KS:_OFF_KS + HEAD_DIM], gk_ref[1]).astype(BF16)
    kw_ref[:win_pad, :] = jnp.zeros((win_pad, HEAD_DIM), BF16)
    kw_ref[win_pad:, :] = _head_norm(z_ref[:, _OFF_KW:_OFF_KW + HEAD_DIM], gk_ref[2]).astype(BF16)
    for j in range(NSA_WIN_PREV):
        vwt_ref[j] = jnp.zeros((HEAD_DIM, QB), BF16)
    for j in range(n_chunks):
        rows = slice(j * QB, (j + 1) * QB)
        vst_ref[:, rows] = z_ref[rows, _OFF_KS:_OFF_KS + 2 * HEAD_DIM].T[HEAD_DIM:].astype(BF16)
        vwt_ref[NSA_WIN_PREV + j] = z_ref[rows, _OFF_KW:_OFF_KW + 2 * HEAD_DIM].T[HEAD_DIM:].astype(BF16)

    cmp_id = lax.broadcasted_iota(jnp.int32, (NSA_N_CMP_PAD, lanes), 0)
    cmp_end = cmp_id * NSA_CMP_STRIDE + (NSA_CMP_LEN - 1)
    cmp_real = cmp_id < NSA_N_CMP_PAD - 1
    q_in_blk = lax.broadcasted_iota(jnp.int32, (NSA_N_CMP_PAD, lanes), 1) & (QB - 1)
    blk_id = lax.broadcasted_iota(jnp.int32, (NSA_N_SLC, QB), 0)
    q_lane = lax.broadcasted_iota(jnp.int32, (NSA_N_SLC, QB), 1)
    win_row = lax.broadcasted_iota(jnp.int32, (win_kw, 1), 0)
    init = _flash_init(HEAD_DIM, lanes)

    def q_blocks(blk):
        both = range(len(blk))
        n_keys = [i + 1 for i in blk]
        mask_pad = [i < NSA_WIN_PREV for i in blk]
        step = [max(d for d in range(1, NSA_STEP + 1) if n % d == 0) for n in n_keys]
        step_keys = [s * QB for s in step]
        blocks_per_step = [s // NSA_SLC_BLOCK for s in step_keys]
        r0 = [i * QB for i in blk]
        qs = [jnp.concatenate(_heads_t(z_ref[pl.ds(r, QB), :_NQ], NSA_REPS, gq), axis=1).astype(BF16)
              for r in r0]

        sc = [_dot(kc, q) for q in qs]
        s_win = [_dot(kw_ref[pl.ds(r, win_kw), :], q) for r, q in zip(r0, qs)]

        def slc_scores(b, j):
            return _dot(ks_ref[j * step_keys[b]:(j + 1) * step_keys[b], :], qs[b])

        o_cmp, imp = [], []
        for b in both:
            ok = (cmp_end <= q_in_blk + blk[b] * QB) & cmp_real
            s = jnp.where(ok, sc[b], NEG)
            e = jnp.where(ok, jnp.exp2(s - jnp.max(s, axis=0, keepdims=True)), 0.0)
            p = e * (1.0 / jnp.maximum(jnp.sum(e, axis=0, keepdims=True), TINY))
            o_cmp.append(_dot(vct, p.astype(BF16)))
            p_sum = p[:, 0:QB] + p[:, QB:2 * QB] + p[:, 2 * QB:3 * QB] + p[:, 3 * QB:4 * QB]
            imp.append(jnp.dot(ovt_ref[...], p_sum, precision=_HI, preferred_element_type=F32))

        o_win = []
        for b in both:
            bias = bwin_ref[...]
            if mask_pad[b]:
                bias = jnp.where(win_row >= (NSA_WIN_PREV - blk[b]) * QB, bias, NEG)
            s = s_win[b] + bias
            pw = jnp.exp2(s - jnp.max(s, axis=0, keepdims=True))
            lw = jnp.sum(pw, axis=0, keepdims=True)
            pw = pw.astype(BF16)
            o = _dot(vwt_ref[blk[b]], pw[0:QB])
            for c in range(1, NSA_WIN_PREV + 1):
                o = o + _dot(vwt_ref[blk[b] + c], pw[c * QB:(c + 1) * QB])
            o_win.append(o * (1.0 / lw))

        drop = []
        for b in both:
            tb = lax.shift_right_logical(q_lane + blk[b] * QB, 6)
            forced = (blk_id == 0) | (blk_id == tb) | (blk_id == tb - 1)
            score = jnp.where(blk_id <= tb, imp[b] + jnp.where(forced, NSA_FORCE, 0.0), -NSA_FORCE)
            rank = jnp.zeros((NSA_N_SLC, QB), F32)
            for other in range(NSA_N_SLC):
                s_o = score[other:other + 1, :]
                beats = (s_o > score) | ((s_o == score) & (blk_id > other))
                rank = rank + jnp.where(beats, 1.0, 0.0)
            drop.append(jnp.where(rank < NSA_TOP_N, 0.0, NEG))

        states = [init for _ in both]
        n_steps = [n // s for n, s in zip(n_keys, step)]
        work = [(b, j) for j in range(max(n_steps)) for b in both if j < n_steps[b]]
        nxt = slc_scores(*work[0])
        for at, (b, j) in enumerate(work):
            cur = nxt
            nxt = slc_scores(*work[at + 1]) if at + 1 < len(work) else None
            bias = jnp.concatenate(
                [bslc_ref[blk[b] - (j * step[b] + c) + 1] for c in range(step[b])], axis=0)
            km = jnp.concatenate([jnp.broadcast_to(drop[b][k:k + 1], (NSA_SLC_BLOCK, QB))
                                  for k in range(j * blocks_per_step[b], (j + 1) * blocks_per_step[b])], axis=0)
            s = cur + bias + jnp.concatenate([km] * NSA_REPS, axis=1)
            states[b] = _flash_step_t(s, *states[b], vst_ref[:, j * step_keys[b]:(j + 1) * step_keys[b]])

        for b in both:
            o_slc = _flash_finish(*states[b])
            gate = jax.nn.sigmoid(z_ref[pl.ds(r0[b], QB), _OFF_GATE:_OFF_GATE + QB]).T
            outs = []
            for r in range(NSA_REPS):
                qsl = slice(r * QB, (r + 1) * QB)
                outs.append(gate[3 * r:3 * r + 1] * o_cmp[b][:, qsl] + gate[3 * r + 1:3 * r + 2] * o_slc[:, qsl]
                            + gate[3 * r + 2:3 * r + 3] * o_win[b][:, qsl])
            o_ref[pl.ds(r0[b], QB), :] = jnp.concatenate(outs, axis=0).T.astype(o_ref.dtype)
    for first in range(0, n_chunks // 2, NSA_GROUP // 2):
        q_blocks([i for p in range(first, first + NSA_GROUP // 2) for i in (p, n_chunks - 1 - p)])


def _nsa(z_nsa, zc, w1a, w1b, w2, pos, gq, gk, ovt, bslc, bwin, batch, seq):
    gw = NSA_GROUP_COLS
    n_chunks = seq // QB
    return pl.pallas_call(
        functools.partial(_nsa_kernel, seq),
        grid=(batch, NSA_KV_HEADS),
        in_specs=[pl.BlockSpec((None, seq, gw), lambda b, g: (b, 0, g)),
                  pl.BlockSpec((None, seq, 2 * HEAD_DIM), lambda b, g: (b, 0, g)),
                  _resident(w1a.shape), _resident(w1b.shape), _resident(w2.shape), _resident(pos.shape),
                  _resident(gq.shape), _resident(gk.shape), _resident(ovt.shape),
                  pl.BlockSpec((None,) + bslc.shape[1:], lambda b, g: (g, 0, 0, 0)),
                  pl.BlockSpec((None,) + bwin.shape[1:], lambda b, g: (g, 0, 0))],
        out_specs=pl.BlockSpec((None, seq, _NQ), lambda b, g: (b, 0, g)),
        out_shape=jax.ShapeDtypeStruct((batch, seq, NSA_HEADS * HEAD_DIM), BF16),
        scratch_shapes=[pltpu.VMEM((seq, HEAD_DIM), BF16),
                        pltpu.VMEM((HEAD_DIM, seq), BF16),
                        pltpu.VMEM((NSA_WIN_PREV * QB + seq, HEAD_DIM), BF16),
                        pltpu.VMEM((NSA_WIN_PREV + n_chunks, HEAD_DIM, QB), BF16)],
        compiler_params=_cparams(2),
        name="nsa",
    )(z_nsa, zc, w1a, w1b, w2, pos, gq, gk, ovt, bslc, bwin)


def _t5_bucket(dist):
    n = jnp.maximum(dist, 0)
    max_exact = NUM_BUCKETS // 2
    nf = jnp.maximum(n, 1).astype(F32)
    large = max_exact + (jnp.log(nf / max_exact) / math.log(MAX_DISTANCE / max_exact)
                         * (NUM_BUCKETS - max_exact)).astype(jnp.int32)
    large = jnp.minimum(large, NUM_BUCKETS - 1)
    return jnp.where(n < max_exact, n, large)


def _toeplitz(u, rows, cols):
    lead = u.shape[:-1]
    lu = rows + cols - 1
    assert u.shape[-1] == lu
    padded = jnp.pad(u, [(0, 0)] * len(lead) + [(0, 1)])
    flat = jnp.broadcast_to(padded[..., None, :], lead + (rows, lu + 1)).reshape(lead + (rows * (lu + 1),))
    return flat[..., :rows * lu].reshape(lead + (rows, lu))[..., rows - 1:]


def _bias_by_distance(bias_cols, delta, valid, dist_scale=1):
    bucket = _t5_bucket(jnp.asarray(np.maximum(delta, 0) * dist_scale, dtype=jnp.int32))
    return jnp.where(jnp.asarray(valid)[None, :], bias_cols.astype(F32)[bucket].T * LOG2E, NEG)


def _band_bias(bias_cols, window, n_prev, dist_scale, n_kv, reps):
    kw = (n_prev + 1) * QB
    delta = n_prev * QB + QB - 1 - np.arange(kw + QB - 1)
    u = _bias_by_distance(bias_cols, delta, (delta >= 0) & (delta < window), dist_scale)
    return _toeplitz(u, QB, kw).reshape(n_kv, reps * QB, kw)


def _slc_bias_t(bias_cols, seq):
    n_chunks = seq // QB
    delta = seq - 1 - np.arange(seq + QB - 1)
    strip = _toeplitz(_bias_by_distance(bias_cols, delta, delta >= 0), QB, seq)
    tile = jnp.flip(strip.reshape(NSA_KV_HEADS, NSA_REPS, QB, n_chunks, QB), axis=3)
    tile = jnp.transpose(tile, (0, 3, 4, 1, 2)).reshape(NSA_KV_HEADS, n_chunks, QB, NSA_REPS * QB)
    return jnp.concatenate([jnp.full_like(tile[:, :1], NEG), tile], axis=1)


def _rope_tables(seq):
    half = MLA_ROPE // 2
    inv = jnp.power(ROPE_THETA, -jnp.arange(half, dtype=F32) / half)
    ang = jnp.arange(seq, dtype=F32)[:, None] * inv[None, :]
    cos, sin = jnp.cos(ang), jnp.sin(ang)
    zeros = lambda w: jnp.zeros((seq, w), F32)
    tail = MLA_PAD - MLA_QK_DIM
    cos_t = jnp.concatenate([jnp.ones((seq, MLA_NOPE), F32), cos, cos, zeros(tail)], axis=1)
    sin_t = jnp.concatenate([zeros(MLA_NOPE), -sin, sin, zeros(tail)], axis=1)
    return cos_t, sin_t


def _pad_cols(w, width):
    return jnp.pad(w, ((0, 0), (0, width - w.shape[1])))


def _with_swapped_rope(a):
    half = MLA_ROPE // 2
    return jnp.concatenate([a, a[..., MLA_NOPE + half:], a[..., MLA_NOPE:MLA_NOPE + half]], axis=-1)


def _nsa_column_order():
    g_cols = NSA_KV_HEADS * HEAD_DIM
    q_cols = NSA_HEADS * HEAD_DIM
    order = []
    for g in range(NSA_KV_HEADS):
        cols = list(range(g * _NQ, (g + 1) * _NQ))
        for part in range(2, 6):
            start = q_cols + part * g_cols + g * HEAD_DIM
            cols += list(range(start, start + HEAD_DIM))
        gate0 = q_cols + 6 * g_cols + g * NSA_REPS * 3
        cols += list(range(gate0, gate0 + NSA_REPS * 3))
        order.append(cols)
    return order


def _nsa_cmp_columns():
    q_cols = NSA_HEADS * HEAD_DIM
    g_cols = NSA_KV_HEADS * HEAD_DIM
    cols = []
    for g in range(NSA_KV_HEADS):
        for part in range(2):
            start = q_cols + part * g_cols + g * HEAD_DIM
            cols += list(range(start, start + HEAD_DIM))
    return cols


def _nsa_mixer(z_nsa3, z_cmp, rel_bias, q_norm, k_norm, cmp_pos, cmp_w1, cmp_w2, batch, seq):
    w1 = cmp_w1.reshape(2, 2, NSA_CMP_STRIDE, HEAD_DIM, NSA_CMP_HIDDEN)
    zero1 = jnp.zeros_like(w1[0])
    w1_bd = jnp.concatenate([jnp.concatenate([w1[0], zero1], axis=-1),
                             jnp.concatenate([zero1, w1[1]], axis=-1)], axis=-2)
    zero2 = jnp.zeros_like(cmp_w2[0])
    w2_bd = jnp.concatenate([jnp.concatenate([cmp_w2[0], zero2], axis=-1),
                             jnp.concatenate([zero2, cmp_w2[1]], axis=-1)], axis=-2)
    pos_kv = jnp.concatenate([cmp_pos[0], cmp_pos[1]], axis=-1)
    n_cmp = (seq - NSA_CMP_LEN) // NSA_CMP_STRIDE + 1
    ci = np.arange(NSA_N_CMP_PAD)[:, None] * NSA_CMP_STRIDE
    sj = np.arange(NSA_N_SLC)[None, :] * NSA_SLC_BLOCK
    overlap = ((ci < sj + NSA_SLC_BLOCK) & (ci + NSA_CMP_LEN > sj) & (np.arange(NSA_N_CMP_PAD)[:, None] < n_cmp))
    ovt = jnp.asarray(overlap.T.astype(np.float32))
    nsa_cols = rel_bias[:, NSA_BIAS_COL0:NSA_BIAS_COL0 + NSA_HEADS]
    bslc = _slc_bias_t(nsa_cols, seq)
    bwin = jnp.transpose(_band_bias(nsa_cols, NSA_WINDOW, NSA_WIN_PREV, 1, NSA_KV_HEADS, NSA_REPS), (0, 2, 1))
    return _nsa(z_nsa3, z_cmp.reshape(batch, seq, -1), w1_bd[0].astype(BF16), w1_bd[1].astype(BF16),
                w2_bd.astype(BF16), pos_kv, q_norm.reshape(HEAD_DIM, 1), k_norm.reshape(3, 1, HEAD_DIM),
                ovt, bslc, bwin, batch, seq)


def kernel(x, rel_bias, ffn1_norm, ffn1_w_gate, ffn1_w_up, ffn1_w_down, mix_norm, ffn2_norm, ffn2_w_gate,
           ffn2_w_up, ffn2_w_down, ab_w_in, mla_q_a_norm, mla_w_q_b, mla_kv_a_norm, mla_w_kv_b, mla_q_norm,
           mla_k_norm, dil_q_norm, dil_k_norm, ab_w_out, cd_w_in, swa_q_norm, swa_k_norm, swa_sinks,
           nsa_q_norm, nsa_k_norm, nsa_cmp_pos, nsa_cmp_w1, nsa_cmp_w2, cd_w_out):
    batch, seq, _ = x.shape
    n = batch * seq
    assert seq % (16 * QB) == 0 and n % TM == 0 and seq % TM == 0
    bf = lambda a: a.astype(BF16)
    xf = x.reshape(n, D_MODEL)
    ffn1 = (bf(ffn1_w_gate), bf(ffn1_w_up), bf(ffn1_w_down))
    ffn2 = (bf(ffn2_w_gate), bf(ffn2_w_up), bf(ffn2_w_down))

    w_in = ab_w_in[0]
    mla_cols = MLA_Q_RANK + MLA_KV_RANK
    w_krope = _with_swapped_rope(jnp.pad(w_in[:, mla_cols:mla_cols + MLA_ROPE], ((0, 0), (MLA_NOPE, 0))))
    w_mla = jnp.concatenate([w_in[:, :mla_cols], w_krope], axis=1)
    wq = _with_swapped_rope(mla_w_q_b[0].reshape(MLA_Q_RANK, MLA_HEADS, MLA_QK_DIM))
    wq = wq.reshape(MLA_Q_RANK, MLA_HEADS * MLA_PAD)
    wkv = mla_w_kv_b[0].reshape(MLA_KV_RANK, MLA_HEADS, MLA_NOPE + MLA_V)
    wk = _pad_cols(wkv[:, :, :MLA_NOPE].reshape(MLA_KV_RANK * MLA_HEADS, MLA_NOPE), MLA_PAD)
    wk = wk.reshape(MLA_KV_RANK, MLA_HEADS * MLA_PAD)
    wv = wkv[:, :, MLA_NOPE:].reshape(MLA_KV_RANK, MLA_HEADS * MLA_V)
    cos_t, sin_t = _rope_tables(seq)
    q_scale = MLA_QK_DIM ** -0.5 * LOG2E
    xf, z_dil, q_mla, k_mla, vt_mla = _ffn_proj_mla(
        xf, ffn1_norm[0], ffn1, 0, mix_norm[0], bf(w_mla), bf(w_in[:, mla_cols + MLA_ROPE:]),
        mla_q_a_norm[0].reshape(1, -1), mla_kv_a_norm[0].reshape(1, -1), bf(wq), bf(wk), bf(wv),
        _with_swapped_rope(mla_q_norm[0].reshape(1, -1)), _with_swapped_rope(mla_k_norm[0].reshape(1, -1)),
        cos_t * q_scale, sin_t * q_scale, cos_t, sin_t, seq)
    o_a = _mla_attn(q_mla.reshape(batch, seq, -1), k_mla.reshape(batch, seq, -1),
                    vt_mla.reshape(batch, seq // MLA_TQ, MLA_HEADS * MLA_V, MLA_TQ),
                    batch, seq).reshape(n, MLA_HEADS * MLA_V)

    z_dil3 = z_dil.reshape(batch, seq, -1)
    gq, gk = dil_q_norm[0], dil_k_norm[0]
    dil_outs, dil_lses = [], []
    for grp, (window, dilation) in enumerate(DIL_PATTERNS):
        bm = _band_bias(rel_bias[:, grp * DIL_HPG:(grp + 1) * DIL_HPG], window // dilation + 1, 1, dilation,
                        DIL_HPG, 1)
        o, lse = _dilated_group(z_dil3, batch, seq, grp, dilation, gq, gk, bm)
        dil_outs.append(o)
        dil_lses.append(lse)
    w_out = ab_w_out[0]
    xf = _out_ab_ffn(xf, o_a, dil_outs, dil_lses, bf(w_out[:MLA_HEADS * MLA_V]), bf(w_out[MLA_HEADS * MLA_V:]),
                     ffn2_norm[0], ffn2, 0)

    w_in = cd_w_in[0]
    swa_cols = (SWA_HEADS + 2 * SWA_KV_HEADS) * HEAD_DIM
    w_nsa_src = w_in[:, swa_cols:]
    w_nsa = jnp.concatenate([_pad_cols(w_nsa_src[:, np.asarray(cols)], NSA_GROUP_COLS)
                             for cols in _nsa_column_order()], axis=1)
    w_cmp = w_nsa_src[:, np.asarray(_nsa_cmp_columns())]
    xf, z_swa, z_nsa, z_cmp = _ffn_proj(xf, ffn1_norm[1], ffn1, 1, mix_norm[1],
                                        [bf(w_in[:, :swa_cols]), bf(w_nsa), bf(w_cmp)])

    swa_reps = SWA_HEADS // SWA_KV_HEADS
    swa_prev = -(-(SWA_WINDOW - 1) // QB)
    bm_swa = _band_bias(rel_bias[:, :SWA_HEADS], SWA_WINDOW, swa_prev, 1, SWA_KV_HEADS, swa_reps)
    sinks = jnp.broadcast_to((swa_sinks[0].astype(F32) * LOG2E).reshape(SWA_KV_HEADS, 1, swa_reps, 1),
                             (SWA_KV_HEADS, 1, swa_reps, QB)).reshape(SWA_KV_HEADS, 1, swa_reps * QB)
    o_c = _swa(z_swa.reshape(batch, seq, -1), batch, seq, swa_q_norm[0].reshape(HEAD_DIM, 1),
               swa_k_norm[0].reshape(1, HEAD_DIM), bm_swa, sinks)

    o_d = _nsa_mixer(z_nsa.reshape(batch, seq, NSA_KV_HEADS * NSA_GROUP_COLS), z_cmp, rel_bias, nsa_q_norm[0],
                     nsa_k_norm[0], nsa_cmp_pos[0], nsa_cmp_w1[0], nsa_cmp_w2[0], batch, seq)

    w_out = cd_w_out[0]
    xf = _out_cd_ffn(xf, o_c.reshape(n, -1), o_d.reshape(n, -1), bf(w_out[:SWA_HEADS * HEAD_DIM]),
                     bf(w_out[SWA_HEADS * HEAD_DIM:]), ffn2_norm[1], ffn2, 1)
    return xf.reshape(batch, seq, D_MODEL)
```

```python
import functools
import math

import numpy as np
import jax
import jax.numpy as jnp
from jax import lax
from jax.experimental import pallas as pl
from jax.experimental.pallas import tpu as pltpu

F32 = jnp.float32
BF16 = jnp.bfloat16

EPS = 1e-6
NEG = -1e30
TINY = 1e-30
LOG2E = math.log2(math.e)
LN2 = math.log(2.0)
D_MODEL = 1024
D_FF = 2816
NUM_BUCKETS = 32
MAX_DISTANCE = 2048
HEAD_DIM = 64
QB = 128

MLA_HEADS = 8
MLA_Q_RANK = 256
MLA_KV_RANK = 128
MLA_NOPE = 64
MLA_ROPE = 32
MLA_V = 64
MLA_QK_DIM = MLA_NOPE + MLA_ROPE
MLA_PAD = 128
ROPE_THETA = 10000.0

DIL_PATTERNS = ((128, 1), (512, 4), (2048, 16))
DIL_HPG = 4
SWA_HEADS = 8
SWA_KV_HEADS = 2
SWA_WINDOW = 128
NSA_HEADS = 8
NSA_KV_HEADS = 2
NSA_CMP_LEN = 32
NSA_CMP_STRIDE = 16
NSA_CMP_HIDDEN = 128
NSA_SLC_BLOCK = 64
NSA_TOP_N = 16
NSA_WINDOW = 512
NSA_FORCE = 1e6
NSA_BIAS_COL0 = 8
NSA_GROUP_COLS = 640

VMEM_LIMIT = 56 * 1024 * 1024
TM = 512

_NT = (((1,), (1,)), ((), ()))
_HI = lax.Precision.HIGHEST


def _cparams(n_axes):
    return pltpu.CompilerParams(dimension_semantics=("arbitrary",) * n_axes,
                                vmem_limit_bytes=VMEM_LIMIT)


def _resident(shape):
    nd = len(shape)
    return pl.BlockSpec(shape, lambda *_: (0,) * nd, pipeline_mode=pl.Buffered(1))


def _layer_block(w, layer):
    nd = w.ndim - 1
    return pl.BlockSpec((None,) + w.shape[1:], lambda *_: (layer,) + (0,) * nd, pipeline_mode=pl.Buffered(1))


def _rms(x):
    return x * lax.rsqrt(jnp.mean(x * x, axis=-1, keepdims=True) + EPS)


def _dot(a, b):
    return jnp.dot(a, b, preferred_element_type=F32)


def _dot_nt(a, b, precision=None):
    return lax.dot_general(a, b, _NT, preferred_element_type=F32, precision=precision)


FF_CHUNK = 256


N_CHUNK = 512


def _swiglu_half_step(x, g_ref, wg_ref, wu_ref, wd_ref):
    hb = (_rms(x) * g_ref[...]).astype(BF16)
    acc = jnp.zeros(x.shape, F32)
    for c in range(D_FF // FF_CHUNK):
        sl = slice(c * FF_CHUNK, (c + 1) * FF_CHUNK)
        gate = _dot(hb, wg_ref[:, sl])
        up = _dot(hb, wu_ref[:, sl])
        act = (gate * jax.nn.sigmoid(gate) * up).astype(BF16)
        acc = acc + _dot(act, wd_ref[sl, :])
    return x + 0.5 * acc


def _ffn_proj_kernel(n_out, x_ref, g_ref, wg_ref, wu_ref, wd_ref, gm_ref, *refs):
    w_refs, x_out_ref, z_refs = refs[:n_out], refs[n_out], refs[n_out + 1:]
    x = _swiglu_half_step(x_ref[...], g_ref, wg_ref, wu_ref, wd_ref)
    x_out_ref[...] = x
    hb = (_rms(x) * gm_ref[...]).astype(BF16)
    for w_ref, z_ref in zip(w_refs, z_refs):
        width = w_ref.shape[1]
        for c0 in range(0, width, N_CHUNK):
            sl = slice(c0, min(c0 + N_CHUNK, width))
            z_ref[:, sl] = _dot(hb, w_ref[:, sl]).astype(z_ref.dtype)


def _tile(width):
    return pl.BlockSpec((TM, width), lambda i: (i, 0))


def _ffn_proj(x, g, ffn, layer, g_mix, ws, z_dtypes=None):
    n = x.shape[0]
    z_dtypes = z_dtypes or [F32] * len(ws)
    ffn_w = [_layer_block(w, layer) for w in ffn]
    return pl.pallas_call(
        functools.partial(_ffn_proj_kernel, len(ws)),
        grid=(n // TM,),
        in_specs=[_tile(D_MODEL), _resident((1, D_MODEL))] + ffn_w + [_resident((1, D_MODEL))]
        + [_resident(w.shape) for w in ws],
        out_specs=[_tile(D_MODEL)] + [_tile(w.shape[1]) for w in ws],
        out_shape=[jax.ShapeDtypeStruct((n, D_MODEL), F32)]
        + [jax.ShapeDtypeStruct((n, w.shape[1]), dt) for w, dt in zip(ws, z_dtypes)],
        compiler_params=_cparams(1),
        name="ffn_proj",
    )(x, g.reshape(1, D_MODEL), *ffn, g_mix.reshape(1, D_MODEL), *ws)


def _out_ab_ffn_kernel(x_ref, oa_ref, o0_ref, o1_ref, o2_ref, l0_ref, l1_ref, l2_ref, wa_ref, wb_ref,
                       g_ref, wg_ref, wu_ref, wd_ref, out_ref):
    l0, l1, l2 = l0_ref[...], l1_ref[...], l2_ref[...]
    m = jnp.maximum(jnp.maximum(l0, l1), l2)
    e0, e1, e2 = jnp.exp(l0 - m), jnp.exp(l1 - m), jnp.exp(l2 - m)
    ob = (e0 * o0_ref[...] + e1 * o1_ref[...] + e2 * o2_ref[...]) / (e0 + e1 + e2)
    x = x_ref[...] + _dot(oa_ref[...], wa_ref[...]) + _dot(ob.astype(BF16), wb_ref[...])
    out_ref[...] = _swiglu_half_step(x, g_ref, wg_ref, wu_ref, wd_ref)


def _out_ab_ffn(x, oa, dil_outs, dil_lses, wa, wb, g, ffn, layer):
    n = x.shape[0]
    dw = DIL_HPG * HEAD_DIM
    ffn_w = [_layer_block(w, layer) for w in ffn]
    return pl.pallas_call(
        _out_ab_ffn_kernel,
        grid=(n // TM,),
        in_specs=[_tile(D_MODEL), _tile(oa.shape[1])] + [_tile(dw)] * 6 + [_resident(wa.shape), _resident(wb.shape),
                                                                        _resident((1, D_MODEL))] + ffn_w,
        out_specs=_tile(D_MODEL),
        out_shape=jax.ShapeDtypeStruct((n, D_MODEL), F32),
        compiler_params=_cparams(1),
        name="out_ab_ffn",
    )(x, oa, *dil_outs, *dil_lses, wa, wb, g.reshape(1, D_MODEL), *ffn)


def _out_cd_ffn_kernel(x_ref, oc_ref, od_ref, wc_ref, wdd_ref, g_ref, wg_ref, wu_ref, wd_ref, out_ref):
    x = x_ref[...] + _dot(oc_ref[...], wc_ref[...]) + _dot(od_ref[...], wdd_ref[...])
    out_ref[...] = _swiglu_half_step(x, g_ref, wg_ref, wu_ref, wd_ref)


def _out_cd_ffn(x, oc, od, wc, wdd, g, ffn, layer):
    n = x.shape[0]
    ffn_w = [_layer_block(w, layer) for w in ffn]
    return pl.pallas_call(
        _out_cd_ffn_kernel,
        grid=(n // TM,),
        in_specs=[_tile(D_MODEL), _tile(oc.shape[1]), _tile(od.shape[1]), _resident(wc.shape), _resident(wdd.shape),
                  _resident((1, D_MODEL))] + ffn_w,
        out_specs=_tile(D_MODEL),
        out_shape=jax.ShapeDtypeStruct((n, D_MODEL), F32),
        compiler_params=_cparams(1),
        name="out_cd_ffn",
    )(x, oc, od, wc, wdd, g.reshape(1, D_MODEL), *ffn)


MLA_TQ = 256
MLA_STEP = 2


def _ffn_proj_mla_kernel(x_ref, g_ref, wg_ref, wu_ref, wd_ref, gm_ref, wmla_ref, wdil_ref,
                         qan_ref, kvan_ref, wq_ref, wk_ref, wv_ref, gq_ref, gk_ref,
                         qc_ref, qs_ref, kc_ref, ks_ref,
                         x_out_ref, zdil_ref, q_ref, k_ref, vt_ref, zprev_ref):
    @pl.when(pl.program_id(0) == 0)
    def _():
        zprev_ref[...] = jnp.zeros(zprev_ref.shape, F32)

    cq = (_rms(zprev_ref[:, :MLA_Q_RANK]) * qan_ref[...]).astype(BF16)
    ckv = (_rms(zprev_ref[:, MLA_Q_RANK:MLA_Q_RANK + MLA_KV_RANK]) * kvan_ref[...]).astype(BF16)
    kr = zprev_ref[:, MLA_Q_RANK + MLA_KV_RANK:]
    real = jnp.where(lax.broadcasted_iota(jnp.int32, (1, MLA_PAD), 1) < MLA_QK_DIM, 1.0, 0.0)

    def head_norm_rope(x, gain, cos, sin):
        ssq = jnp.sum(x * x * real, axis=-1, keepdims=True)
        x = x * lax.rsqrt(ssq * (1.0 / MLA_QK_DIM) + EPS) * gain
        return x * cos + pltpu.roll(x, MLA_PAD - MLA_ROPE, 1) * sin

    def head(h):
        sl = slice(h * MLA_PAD, (h + 1) * MLA_PAD)
        qh = head_norm_rope(_dot(cq, wq_ref[:, sl]), gq_ref[...], qc_ref[...], qs_ref[...])
        q_ref[:, sl] = qh.astype(BF16)
        kh = head_norm_rope(_dot(ckv, wk_ref[:, sl]) + kr, gk_ref[...], kc_ref[...], ks_ref[...])
        k_ref[:, sl] = kh.astype(BF16)

    def values():
        v = _dot(ckv, wv_ref[...])
        for c in range(TM // MLA_TQ):
            vt_ref[c] = v[c * MLA_TQ:(c + 1) * MLA_TQ].T.astype(BF16)

    latent_work = [functools.partial(head, h) for h in range(MLA_HEADS)] + [values]

    x = x_ref[...]
    hb = (_rms(x) * g_ref[...]).astype(BF16)
    acc = jnp.zeros(x.shape, F32)
    for c in range(D_FF // FF_CHUNK):
        sl = slice(c * FF_CHUNK, (c + 1) * FF_CHUNK)
        gate = _dot(hb, wg_ref[:, sl])
        up = _dot(hb, wu_ref[:, sl])
        act = (gate * jax.nn.sigmoid(gate) * up).astype(BF16)
        acc = acc + _dot(act, wd_ref[sl, :])
        if c < len(latent_work):
            latent_work[c]()
    x = x + 0.5 * acc
    x_out_ref[...] = x
    hm = (_rms(x) * gm_ref[...]).astype(BF16)
    zprev_ref[...] = _dot(hm, wmla_ref[...])
    width = wdil_ref.shape[1]
    for c0 in range(0, width, N_CHUNK):
        sl = slice(c0, min(c0 + N_CHUNK, width))
        zdil_ref[:, sl] = _dot(hm, wdil_ref[:, sl])


def _ffn_proj_mla(x, g, ffn, layer, g_mix, w_mla, w_dil, qan, kvan, wq, wk, wv, gq, gk,
                  q_cos, q_sin, k_cos, k_sin, seq):
    n = x.shape[0]
    n_tiles = n // TM
    assert D_FF // FF_CHUNK >= MLA_HEADS + 1
    per_seq = seq // TM
    hw = MLA_HEADS * MLA_PAD
    vw = MLA_HEADS * MLA_V

    def cur(width):
        return pl.BlockSpec((TM, width), lambda i: (jnp.minimum(i, n_tiles - 1), 0))

    def prev(width):
        return pl.BlockSpec((TM, width), lambda i: (jnp.maximum(i - 1, 0), 0))

    rope_spec = pl.BlockSpec((TM, MLA_PAD), lambda i: (jnp.maximum(i - 1, 0) % per_seq, 0))
    small = [qan, kvan, wq, wk, wv, gq, gk]
    return pl.pallas_call(
        _ffn_proj_mla_kernel,
        grid=(n_tiles + 1,),
        in_specs=[cur(D_MODEL), _resident((1, D_MODEL))] + [_layer_block(w, layer) for w in ffn]
        + [_resident((1, D_MODEL)), _resident(w_mla.shape), _resident(w_dil.shape)]
        + [_resident(a.shape) for a in small] + [rope_spec] * 4,
        out_specs=[cur(D_MODEL), cur(w_dil.shape[1]), prev(hw), prev(hw),
                   pl.BlockSpec((TM // MLA_TQ, vw, MLA_TQ), lambda i: (jnp.maximum(i - 1, 0), 0, 0))],
        out_shape=[jax.ShapeDtypeStruct((n, D_MODEL), F32), jax.ShapeDtypeStruct((n, w_dil.shape[1]), F32),
                   jax.ShapeDtypeStruct((n, hw), BF16), jax.ShapeDtypeStruct((n, hw), BF16),
                   jax.ShapeDtypeStruct((n // MLA_TQ, vw, MLA_TQ), BF16)],
        scratch_shapes=[pltpu.VMEM((TM, w_mla.shape[1]), F32)],
        compiler_params=_cparams(1),
        name="ffn_proj_mla",
    )(x, g.reshape(1, D_MODEL), *ffn, g_mix.reshape(1, D_MODEL), w_mla, w_dil, *small,
      q_cos, q_sin, k_cos, k_sin)


def _flash_step_t(s, m, l, acc, pv, vt):
    m_new = jnp.maximum(m, jnp.max(s, axis=0, keepdims=True))
    p = jnp.exp2(s - m_new)
    alpha = jnp.exp2(m - m_new)
    l = alpha * l + jnp.sum(p, axis=0, keepdims=True)
    acc = alpha * (acc + pv)
    return m_new, l, acc, _dot(vt, p.astype(BF16))


def _flash_init(dv, n_queries):
    return (jnp.full((1, n_queries), NEG, F32), jnp.zeros((1, n_queries), F32), jnp.zeros((dv, n_queries), F32),
            jnp.zeros((dv, n_queries), F32))


def _flash_finish(m, l, acc, pv):
    return (acc + pv) * (1.0 / l)


def _mla_attn_kernel(seq, q_ref, k_ref, vt_ref, o_ref):
    t = MLA_TQ
    key = lax.broadcasted_iota(jnp.int32, (t, t), 0)
    qry = lax.broadcasted_iota(jnp.int32, (t, t), 1)
    diag_ok = key <= qry
    heads = [(slice(hh * MLA_PAD, (hh + 1) * MLA_PAD), slice(hh * MLA_V, (hh + 1) * MLA_V)) for hh in range(2)]

    init = _flash_init(MLA_V, t)
    n_blocks = seq // t
    for first in range(0, n_blocks // 2, n_blocks // 2):
        blocks = [i for p in range(first, first + n_blocks // 2) for i in (p, n_blocks - 1 - p)]
        chains = [(i, h) for i in blocks for h in range(2)]
        qs = {(i, h): q_ref[i * t:(i + 1) * t, heads[h][0]] for i, h in chains}
        steps = {i: [(j, min(j + MLA_STEP, i + 1)) for j in range(0, i + 1, MLA_STEP)] for i in blocks}

        def scores(chain, step):
            i, h = chain
            lo, hi = steps[i][step]
            s = _dot_nt(k_ref[lo * t:hi * t, heads[h][0]], qs[chain])
            if hi == i + 1:
                last = jnp.where(diag_ok, s[(i - lo) * t:], NEG)
                s = last if i == lo else jnp.concatenate([s[:(i - lo) * t], last], axis=0)
            return s

        def values(chain, step):
            i, h = chain
            lo, hi = steps[i][step]
            return jnp.concatenate([vt_ref[j, heads[h][1], :] for j in range(lo, hi)], axis=1)

        cur = {c: scores(c, 0) for c in chains}
        states = {c: init for c in chains}
        for step in range(max(len(s) for s in steps.values())):
            for c in chains:
                n_steps = len(steps[c[0]])
                if step >= n_steps:
                    continue
                nxt = scores(c, step + 1) if step + 1 < n_steps else None
                states[c] = _flash_step_t(cur[c], *states[c], values(c, step))
                cur[c] = nxt
        for i in blocks:
            outs = [_flash_finish(*states[(i, h)]) for h in range(2)]
            o_ref[i * t:(i + 1) * t, :] = jnp.concatenate(outs, axis=0).T.astype(o_ref.dtype)


def _mla_attn(q, k, vt, batch, seq):
    return pl.pallas_call(
        functools.partial(_mla_attn_kernel, seq),
        grid=(batch, MLA_HEADS // 2),
        in_specs=[pl.BlockSpec((None, seq, 2 * MLA_PAD), lambda b, h: (b, 0, h)),
                  pl.BlockSpec((None, seq, 2 * MLA_PAD), lambda b, h: (b, 0, h)),
                  pl.BlockSpec((None, seq // MLA_TQ, 2 * MLA_V, MLA_TQ), lambda b, h: (b, 0, h, 0))],
        out_specs=pl.BlockSpec((None, seq, 2 * MLA_V), lambda b, h: (b, 0, h)),
        out_shape=jax.ShapeDtypeStruct((batch, seq, MLA_HEADS * MLA_V), BF16),
        compiler_params=_cparams(2),
        name="mla_attn",
    )(q, k, vt)


def _head_norm(x, gain):
    return _rms(x) * gain


def _heads_t(qt, n_heads, gain):
    tiles = []
    for c0 in range(0, n_heads * HEAD_DIM, 2 * HEAD_DIM):
        pair = qt[:, c0:c0 + 2 * HEAD_DIM].T
        for h in range(2):
            x = pair[h * HEAD_DIM:(h + 1) * HEAD_DIM]
            tiles.append(x * lax.rsqrt(jnp.mean(x * x, axis=0, keepdims=True) + EPS) * gain)
    return tiles


def _q_gain(gq_ref):
    return jnp.broadcast_to(gq_ref[...] * (HEAD_DIM ** -0.5 * LOG2E), (HEAD_DIM, QB))


def _pair_ones():
    shape = (2 * HEAD_DIM, 2 * HEAD_DIM)
    same = lax.broadcasted_iota(jnp.int32, shape, 0) // HEAD_DIM == lax.broadcasted_iota(jnp.int32, shape, 1) // HEAD_DIM
    return jnp.where(same, 1.0, 0.0).astype(BF16)


def _pair_norm(x, seg, gain):
    sq = x * x
    hi = sq.astype(BF16)
    lo = (sq - hi.astype(F32)).astype(BF16)
    ssq = _dot(hi, seg) + _dot(lo, seg)
    return x * lax.rsqrt(ssq * (1.0 / HEAD_DIM) + EPS) * gain


BAND_KV = 2


def _banded_kernel(seq, dilation, reps, n_prev, with_sink, with_lse, group_size, paired, *refs):
    q_ref, k_ref, v_ref, gq_ref, gk_ref, bm_ref = refs[:6]
    refs = refs[6:]
    sink_ref = None
    if with_sink:
        sink_ref, refs = refs[0], refs[1:]
    o_ref, refs = refs[0], refs[1:]
    lse_ref = None
    if with_lse:
        lse_ref, refs = refs[0], refs[1:]
    kn_ref, vt_ref = refs
    length = seq // dilation
    n_blocks = length // QB
    pad = n_prev * QB
    kw = pad + QB
    win_row = lax.broadcasted_iota(jnp.int32, (kw, 1), 0)
    assert not paired or reps == 1
    if paired:
        gq, gk = gq_ref[...] * (HEAD_DIM ** -0.5 * LOG2E), gk_ref[...]
        seg = _pair_ones()
        lane = lax.broadcasted_iota(jnp.int32, (1, BAND_KV * HEAD_DIM), 1)
        head_lanes = [jnp.where(lane // HEAD_DIM == g, 1.0, 0.0) for g in range(BAND_KV)]
    else:
        gq, gk = _q_gain(gq_ref), gk_ref[...]

    def rows(start, size):
        return pl.ds(start, size) if dilation == 1 else pl.ds(start, size, stride=dilation)

    for res in range(dilation):
        kall, vall = k_ref[rows(res, length), :], v_ref[rows(res, length), :]
        if paired:
            kn_ref[res, :pad, :] = jnp.zeros((pad, BAND_KV * HEAD_DIM), BF16)
            kn_ref[res, pad:, :] = _pair_norm(kall, seg, gk).astype(BF16)
        else:
            for g in range(BAND_KV):
                kn_ref[res, g, :pad, :] = jnp.zeros((pad, HEAD_DIM), BF16)
                kn_ref[res, g, pad:, :] = _head_norm(kall[:, g * HEAD_DIM:(g + 1) * HEAD_DIM], gk).astype(BF16)
        for j in range(n_prev):
            vt_ref[res, j] = jnp.zeros((BAND_KV * HEAD_DIM, QB), BF16)
        for j in range(n_blocks):
            vt_ref[res, n_prev + j] = vall[j * QB:(j + 1) * QB].T.astype(BF16)

    def attend(res, i, g, s):
        bias = bm_ref[g]
        if i < n_prev:
            bias = jnp.where(win_row >= (n_prev - i) * QB, bias, NEG)
        s = s + bias
        m = jnp.max(s, axis=0, keepdims=True)
        if with_sink:
            m = jnp.maximum(m, sink_ref[g])
        p = jnp.exp2(s - m)
        l = jnp.sum(p, axis=0, keepdims=True)
        if with_sink:
            l = l + jnp.exp2(sink_ref[g] - m)
        p = p.astype(BF16)
        vl = slice(g * HEAD_DIM, (g + 1) * HEAD_DIM)
        o = _dot(vt_ref[res, i, vl, :], p[0:QB])
        for w in range(1, n_prev + 1):
            o = o + _dot(vt_ref[res, i + w, vl, :], p[w * QB:(w + 1) * QB])
        return o * (1.0 / l), m * LN2 + jnp.log(l)

    items = [(res, i) for res in range(dilation) for i in range(n_blocks)]
    for at in range(0, len(items), group_size):
        group = items[at:at + group_size]
        toks = [rows(res + i * QB * dilation, QB) for res, i in group]
        if paired:
            qns = [_pair_norm(q_ref[tok, :], seg, gq) for tok in toks]
            qss = [[(qn * head_lanes[g]).astype(BF16) for g in range(BAND_KV)] for qn in qns]
            scores = [[_dot_nt(kn_ref[res, i * QB:i * QB + kw, :], qs[g]) for g in range(BAND_KV)]
                      for (res, i), qs in zip(group, qss)]
        else:
            heads = [_heads_t(q_ref[tok, :], BAND_KV * reps, gq) for tok in toks]
            qss = [[jnp.concatenate(h[g * reps:(g + 1) * reps], axis=1).astype(BF16) for g in range(BAND_KV)]
                   for h in heads]
            scores = [[_dot(kn_ref[res, g, i * QB:i * QB + kw, :], qs[g]) for g in range(BAND_KV)]
                      for (res, i), qs in zip(group, qss)]
        for (res, i), tok, sc in zip(group, toks, scores):
            outs, lses = [], []
            for g in range(BAND_KV):
                o, lse = attend(res, i, g, sc[g])
                for r in range(reps):
                    outs.append(o[:, r * QB:(r + 1) * QB])
                    if with_lse:
                        lses.append(jnp.broadcast_to(lse[:, r * QB:(r + 1) * QB], (HEAD_DIM, QB)))
            o_ref[tok, :] = jnp.concatenate(outs, axis=0).T.astype(o_ref.dtype)
            if with_lse:
                lse_ref[tok, :] = jnp.concatenate(lses, axis=0).T


def _banded_scratch(seq, dilation, n_prev, paired):
    length = seq // dilation
    rows = n_prev * QB + length
    keys = (dilation, rows, BAND_KV * HEAD_DIM) if paired else (dilation, BAND_KV, rows, HEAD_DIM)
    return [pltpu.VMEM(keys, BF16),
            pltpu.VMEM((dilation, n_prev + length // QB, BAND_KV * HEAD_DIM, QB), BF16)]


PAIRED_MIN_DILATION = 16


def _dilated_group(z_dil, batch, seq, group, dilation, q_norm, k_norm, bm):
    gw = DIL_HPG * HEAD_DIM
    n_groups = len(DIL_PATTERNS)
    hps = BAND_KV
    halves = DIL_HPG // hps
    bm_t = jnp.transpose(bm, (0, 2, 1))
    paired = dilation >= PAIRED_MIN_DILATION
    if paired:
        gq = jnp.tile(q_norm.reshape(1, HEAD_DIM), (1, BAND_KV))
        gk = jnp.tile(k_norm.reshape(1, HEAD_DIM), (1, BAND_KV))
    else:
        gq, gk = q_norm.reshape(HEAD_DIM, 1), k_norm.reshape(1, HEAD_DIM)

    def spec(part):
        return pl.BlockSpec((None, seq, hps * HEAD_DIM), lambda b, h: (b, 0, (part * n_groups + group) * halves + h))

    out_spec = pl.BlockSpec((None, seq, hps * HEAD_DIM), lambda b, h: (b, 0, h))
    shape = jax.ShapeDtypeStruct((batch, seq, gw), F32)
    out, lse = pl.pallas_call(
        functools.partial(_banded_kernel, seq, dilation, 1, 1, False, True, 16, paired),
        grid=(batch, halves),
        in_specs=[spec(0), spec(1), spec(2), _resident(gq.shape), _resident(gk.shape),
                  pl.BlockSpec((hps,) + bm_t.shape[1:], lambda b, h: (h, 0, 0))],
        out_specs=[out_spec, out_spec],
        out_shape=[shape, shape],
        scratch_shapes=_banded_scratch(seq, dilation, 1, paired),
        compiler_params=_cparams(2),
        name="dilated_g%d" % group,
    )(z_dil, z_dil, z_dil, gq, gk, bm_t)
    return out.reshape(batch * seq, gw), lse.reshape(batch * seq, gw)


def _swa(z_swa, batch, seq, gq, gk, bm, sinks):
    qw = SWA_HEADS * HEAD_DIM
    kvw = SWA_KV_HEADS * HEAD_DIM
    reps = SWA_HEADS // SWA_KV_HEADS
    n_prev = -(-(SWA_WINDOW - 1) // QB)
    assert SWA_KV_HEADS == BAND_KV
    bm_t = jnp.transpose(bm, (0, 2, 1))
    return pl.pallas_call(
        functools.partial(_banded_kernel, seq, 1, reps, n_prev, True, False, 4, False),
        grid=(batch,),
        in_specs=[pl.BlockSpec((None, seq, qw), lambda b: (b, 0, 0)),
                  pl.BlockSpec((None, seq, kvw), lambda b: (b, 0, qw // kvw)),
                  pl.BlockSpec((None, seq, kvw), lambda b: (b, 0, qw // kvw + 1)),
                  _resident(gq.shape), _resident(gk.shape), _resident(bm_t.shape), _resident(sinks.shape)],
        out_specs=pl.BlockSpec((None, seq, qw), lambda b: (b, 0, 0)),
        out_shape=jax.ShapeDtypeStruct((batch, seq, qw), BF16),
        scratch_shapes=_banded_scratch(seq, 1, n_prev, False),
        compiler_params=_cparams(1),
        name="swa",
    )(z_swa, z_swa, z_swa, gq, gk, bm_t, sinks)


NSA_REPS = NSA_HEADS // NSA_KV_HEADS
NSA_N_SLC = 32
NSA_N_CMP_PAD = 128
NSA_WIN_PREV = -(-(NSA_WINDOW - 1) // QB)
_NQ = NSA_REPS * HEAD_DIM
_OFF_KS, _OFF_VS, _OFF_KW, _OFF_VW, _OFF_GATE = (_NQ + i * HEAD_DIM for i in range(5))


NSA_STEP = 16
NSA_GROUP = 8


def _nsa_kernel(seq, z_ref, zc_ref, w1a_ref, w1b_ref, w2_ref, pos_ref, gq_ref, gk_ref, ovt_ref,
                bslc_ref, bwin_ref, o_ref, ks_ref, vst_ref, kw_ref, vwt_ref):
    n_chunks = seq // QB
    lanes = NSA_REPS * QB
    gq = _q_gain(gq_ref)
    win_pad = NSA_WIN_PREV * QB
    win_kw = win_pad + QB

    first = jnp.zeros((NSA_N_CMP_PAD, 2 * NSA_CMP_HIDDEN), F32)
    second = jnp.zeros((NSA_N_CMP_PAD, 2 * NSA_CMP_HIDDEN), F32)
    for l in range(NSA_CMP_STRIDE):
        tok = zc_ref[pl.ds(l, NSA_N_CMP_PAD, stride=NSA_CMP_STRIDE), :]
        first = first + _dot((tok + pos_ref[l:l + 1, :]).astype(BF16), w1a_ref[l])
        second = second + _dot((tok + pos_ref[NSA_CMP_STRIDE + l:NSA_CMP_STRIDE + l + 1, :]).astype(BF16),
                               w1b_ref[l])
    hidden = jax.nn.gelu(first + pltpu.roll(second, NSA_N_CMP_PAD - 1, 0)).astype(BF16)
    cmp_kv = _dot(hidden, w2_ref[...])
    kc = _head_norm(cmp_kv[:, :HEAD_DIM], gk_ref[0]).astype(BF16)
    vct = cmp_kv.T[HEAD_DIM:].astype(BF16)

    ks_ref[...] = _head_norm(z_ref[:, _OFF_KS:_OFF_KS + HEAD_DIM], gk_ref[1]).astype(BF16)
    kw_ref[:win_pad, :] = jnp.zeros((win_pad, HEAD_DIM), BF16)
    kw_ref[win_pad:, :] = _head_norm(z_ref[:, _OFF_KW:_OFF_KW + HEAD_DIM], gk_ref[2]).astype(BF16)
    for j in range(NSA_WIN_PREV):
        vwt_ref[j] = jnp.zeros((HEAD_DIM, QB), BF16)
    for j in range(n_chunks):
        rows = slice(j * QB, (j + 1) * QB)
        vst_ref[:, rows] = z_ref[rows, _OFF_KS:_OFF_KS + 2 * HEAD_DIM].T[HEAD_DIM:].astype(BF16)
        vwt_ref[NSA_WIN_PREV + j] = z_ref[rows, _OFF_KW:_OFF_KW + 2 * HEAD_DIM].T[HEAD_DIM:].astype(BF16)

    cmp_id = lax.broadcasted_iota(jnp.int32, (NSA_N_CMP_PAD, lanes), 0)
    cmp_end = cmp_id * NSA_CMP_STRIDE + (NSA_CMP_LEN - 1)
    cmp_real = cmp_id < NSA_N_CMP_PAD - 1
    q_in_blk = lax.broadcasted_iota(jnp.int32, (NSA_N_CMP_PAD, lanes), 1) & (QB - 1)
    blk_id = lax.broadcasted_iota(jnp.int32, (NSA_N_SLC, QB), 0)
    q_lane = lax.broadcasted_iota(jnp.int32, (NSA_N_SLC, QB), 1)
    win_row = lax.broadcasted_iota(jnp.int32, (win_kw, 1), 0)
    init = _flash_init(HEAD_DIM, lanes)

    def q_blocks(blk):
        both = range(len(blk))
        n_keys = [i + 1 for i in blk]
        mask_pad = [i < NSA_WIN_PREV for i in blk]
        step = [max(d for d in range(1, NSA_STEP + 1) if n % d == 0) for n in n_keys]
        step_keys = [s * QB for s in step]
        blocks_per_step = [s // NSA_SLC_BLOCK for s in step_keys]
        r0 = [i * QB for i in blk]
        qs = [jnp.concatenate(_heads_t(z_ref[pl.ds(r, QB), :_NQ], NSA_REPS, gq), axis=1).astype(BF16)
              for r in r0]

        sc = [_dot(kc, q) for q in qs]
        s_win = [_dot(kw_ref[pl.ds(r, win_kw), :], q) for r, q in zip(r0, qs)]

        def slc_scores(b, j):
            return _dot(ks_ref[j * step_keys[b]:(j + 1) * step_keys[b], :], qs[b])

        o_cmp, imp = [], []
        for b in both:
            ok = (cmp_end <= q_in_blk + blk[b] * QB) & cmp_real
            s = jnp.where(ok, sc[b], NEG)
            e = jnp.where(ok, jnp.exp2(s - jnp.max(s, axis=0, keepdims=True)), 0.0)
            p = e * (1.0 / jnp.maximum(jnp.sum(e, axis=0, keepdims=True), TINY))
            o_cmp.append(_dot(vct, p.astype(BF16)))
            p_sum = p[:, 0:QB] + p[:, QB:2 * QB] + p[:, 2 * QB:3 * QB] + p[:, 3 * QB:4 * QB]
            imp.append(jnp.dot(ovt_ref[...], p_sum, precision=_HI, preferred_element_type=F32))

        o_win = []
        for b in both:
            bias = bwin_ref[...]
            if mask_pad[b]:
                bias = jnp.where(win_row >= (NSA_WIN_PREV - blk[b]) * QB, bias, NEG)
            s = s_win[b] + bias
            pw = jnp.exp2(s - jnp.max(s, axis=0, keepdims=True))
            lw = jnp.sum(pw, axis=0, keepdims=True)
            pw = pw.astype(BF16)
            o = _dot(vwt_ref[blk[b]], pw[0:QB])
            for c in range(1, NSA_WIN_PREV + 1):
                o = o + _dot(vwt_ref[blk[b] + c], pw[c * QB:(c + 1) * QB])
            o_win.append(o * (1.0 / lw))

        drop = []
        for b in both:
            tb = lax.shift_right_logical(q_lane + blk[b] * QB, 6)
            forced = (blk_id == 0) | (blk_id == tb) | (blk_id == tb - 1)
            score = jnp.where(blk_id <= tb, imp[b] + jnp.where(forced, NSA_FORCE, 0.0), -NSA_FORCE)
            rank = jnp.zeros((NSA_N_SLC, QB), F32)
            for other in range(NSA_N_SLC):
                s_o = score[other:other + 1, :]
                beats = (s_o > score) | ((s_o == score) & (blk_id > other))
                rank = rank + jnp.where(beats, 1.0, 0.0)
            drop.append(jnp.where(rank < NSA_TOP_N, 0.0, NEG))

        states = [init for _ in both]
        n_steps = [n // s for n, s in zip(n_keys, step)]
        work = [(b, j) for j in range(max(n_steps)) for b in both if j < n_steps[b]]
        nxt = slc_scores(*work[0])
        for at, (b, j) in enumerate(work):
            cur = nxt
            nxt = slc_scores(*work[at + 1]) if at + 1 < len(work) else None
            bias = jnp.concatenate(
                [bslc_ref[blk[b] - (j * step[b] + c) + 1] for c in range(step[b])], axis=0)
            km = jnp.concatenate([jnp.broadcast_to(drop[b][k:k + 1], (NSA_SLC_BLOCK, QB))
                                  for k in range(j * blocks_per_step[b], (j + 1) * blocks_per_step[b])], axis=0)
            s = cur + bias + jnp.concatenate([km] * NSA_REPS, axis=1)
            states[b] = _flash_step_t(s, *states[b], vst_ref[:, j * step_keys[b]:(j + 1) * step_keys[b]])

        for b in both:
            o_slc = _flash_finish(*states[b])
            gate = jax.nn.sigmoid(z_ref[pl.ds(r0[b], QB), _OFF_GATE:_OFF_GATE + QB]).T
            outs = []
            for r in range(NSA_REPS):
                qsl = slice(r * QB, (r + 1) * QB)
                outs.append(gate[3 * r:3 * r + 1] * o_cmp[b][:, qsl] + gate[3 * r + 1:3 * r + 2] * o_slc[:, qsl]
                            + gate[3 * r + 2:3 * r + 3] * o_win[b][:, qsl])
            o_ref[pl.ds(r0[b], QB), :] = jnp.concatenate(outs, axis=0).T.astype(o_ref.dtype)
    for first in range(0, n_chunks // 2, NSA_GROUP // 2):
        q_blocks([i for p in range(first, first + NSA_GROUP // 2) for i in (p, n_chunks - 1 - p)])


def _nsa(z_nsa, zc, w1a, w1b, w2, pos, gq, gk, ovt, bslc, bwin, batch, seq):
    gw = NSA_GROUP_COLS
    n_chunks = seq // QB
    return pl.pallas_call(
        functools.partial(_nsa_kernel, seq),
        grid=(batch, NSA_KV_HEADS),
        in_specs=[pl.BlockSpec((None, seq, gw), lambda b, g: (b, 0, g)),
                  pl.BlockSpec((None, seq, 2 * HEAD_DIM), lambda b, g: (b, 0, g)),
                  _resident(w1a.shape), _resident(w1b.shape), _resident(w2.shape), _resident(pos.shape),
                  _resident(gq.shape), _resident(gk.shape), _resident(ovt.shape),
                  pl.BlockSpec((None,) + bslc.shape[1:], lambda b, g: (g, 0, 0, 0)),
                  pl.BlockSpec((None,) + bwin.shape[1:], lambda b, g: (g, 0, 0))],
        out_specs=pl.BlockSpec((None, seq, _NQ), lambda b, g: (b, 0, g)),
        out_shape=jax.ShapeDtypeStruct((batch, seq, NSA_HEADS * HEAD_DIM), BF16),
        scratch_shapes=[pltpu.VMEM((seq, HEAD_DIM), BF16),
                        pltpu.VMEM((HEAD_DIM, seq), BF16),
                        pltpu.VMEM((NSA_WIN_PREV * QB + seq, HEAD_DIM), BF16),
                        pltpu.VMEM((NSA_WIN_PREV + n_chunks, HEAD_DIM, QB), BF16)],
        compiler_params=_cparams(2),
        name="nsa",
    )(z_nsa, zc, w1a, w1b, w2, pos, gq, gk, ovt, bslc, bwin)


def _t5_bucket(dist):
    n = jnp.maximum(dist, 0)
    max_exact = NUM_BUCKETS // 2
    nf = jnp.maximum(n, 1).astype(F32)
    large = max_exact + (jnp.log(nf / max_exact) / math.log(MAX_DISTANCE / max_exact)
                         * (NUM_BUCKETS - max_exact)).astype(jnp.int32)
    large = jnp.minimum(large, NUM_BUCKETS - 1)
    return jnp.where(n < max_exact, n, large)


def _toeplitz(u, rows, cols):
    lead = u.shape[:-1]
    lu = rows + cols - 1
    assert u.shape[-1] == lu
    padded = jnp.pad(u, [(0, 0)] * len(lead) + [(0, 1)])
    flat = jnp.broadcast_to(padded[..., None, :], lead + (rows, lu + 1)).reshape(lead + (rows * (lu + 1),))
    return flat[..., :rows * lu].reshape(lead + (rows, lu))[..., rows - 1:]


def _bias_by_distance(bias_cols, delta, valid, dist_scale=1):
    bucket = _t5_bucket(jnp.asarray(np.maximum(delta, 0) * dist_scale, dtype=jnp.int32))
    return jnp.where(jnp.asarray(valid)[None, :], bias_cols.astype(F32)[bucket].T * LOG2E, NEG)


def _band_bias(bias_cols, window, n_prev, dist_scale, n_kv, reps):
    kw = (n_prev + 1) * QB
    delta = n_prev * QB + QB - 1 - np.arange(kw + QB - 1)
    u = _bias_by_distance(bias_cols, delta, (delta >= 0) & (delta < window), dist_scale)
    return _toeplitz(u, QB, kw).reshape(n_kv, reps * QB, kw)


def _slc_bias_t(bias_cols, seq):
    n_chunks = seq // QB
    delta = seq - 1 - np.arange(seq + QB - 1)
    strip = _toeplitz(_bias_by_distance(bias_cols, delta, delta >= 0), QB, seq)
    tile = jnp.flip(strip.reshape(NSA_KV_HEADS, NSA_REPS, QB, n_chunks, QB), axis=3)
    tile = jnp.transpose(tile, (0, 3, 4, 1, 2)).reshape(NSA_KV_HEADS, n_chunks, QB, NSA_REPS * QB)
    return jnp.concatenate([jnp.full_like(tile[:, :1], NEG), tile], axis=1)


def _rope_tables(seq):
    half = MLA_ROPE // 2
    inv = jnp.power(ROPE_THETA, -jnp.arange(half, dtype=F32) / half)
    ang = jnp.arange(seq, dtype=F32)[:, None] * inv[None, :]
    cos, sin = jnp.cos(ang), jnp.sin(ang)
    zeros = lambda w: jnp.zeros((seq, w), F32)
    tail = MLA_PAD - MLA_QK_DIM
    cos_t = jnp.concatenate([jnp.ones((seq, MLA_NOPE), F32), cos, cos, zeros(tail)], axis=1)
    sin_t = jnp.concatenate([zeros(MLA_NOPE), -sin, sin, zeros(tail)], axis=1)
    return cos_t, sin_t


def _pad_cols(w, width):
    return jnp.pad(w, ((0, 0), (0, width - w.shape[1])))


def _with_swapped_rope(a):
    half = MLA_ROPE // 2
    return jnp.concatenate([a, a[..., MLA_NOPE + half:], a[..., MLA_NOPE:MLA_NOPE + half]], axis=-1)


def _nsa_column_order():
    g_cols = NSA_KV_HEADS * HEAD_DIM
    q_cols = NSA_HEADS * HEAD_DIM
    order = []
    for g in range(NSA_KV_HEADS):
        cols = list(range(g * _NQ, (g + 1) * _NQ))
        for part in range(2, 6):
            start = q_cols + part * g_cols + g * HEAD_DIM
            cols += list(range(start, start + HEAD_DIM))
        gate0 = q_cols + 6 * g_cols + g * NSA_REPS * 3
        cols += list(range(gate0, gate0 + NSA_REPS * 3))
        order.append(cols)
    return order


def _nsa_cmp_columns():
    q_cols = NSA_HEADS * HEAD_DIM
    g_cols = NSA_KV_HEADS * HEAD_DIM
    cols = []
    for g in range(NSA_KV_HEADS):
        for part in range(2):
            start = q_cols + part * g_cols + g * HEAD_DIM
            cols += list(range(start, start + HEAD_DIM))
    return cols


def _nsa_mixer(z_nsa3, z_cmp, rel_bias, q_norm, k_norm, cmp_pos, cmp_w1, cmp_w2, batch, seq):
    w1 = cmp_w1.reshape(2, 2, NSA_CMP_STRIDE, HEAD_DIM, NSA_CMP_HIDDEN)
    zero1 = jnp.zeros_like(w1[0])
    w1_bd = jnp.concatenate([jnp.concatenate([w1[0], zero1], axis=-1),
                             jnp.concatenate([zero1, w1[1]], axis=-1)], axis=-2)
    zero2 = jnp.zeros_like(cmp_w2[0])
    w2_bd = jnp.concatenate([jnp.concatenate([cmp_w2[0], zero2], axis=-1),
                             jnp.concatenate([zero2, cmp_w2[1]], axis=-1)], axis=-2)
    pos_kv = jnp.concatenate([cmp_pos[0], cmp_pos[1]], axis=-1)
    n_cmp = (seq - NSA_CMP_LEN) // NSA_CMP_STRIDE + 1
    ci = np.arange(NSA_N_CMP_PAD)[:, None] * NSA_CMP_STRIDE
    sj = np.arange(NSA_N_SLC)[None, :] * NSA_SLC_BLOCK
    overlap = ((ci < sj + NSA_SLC_BLOCK) & (ci + NSA_CMP_LEN > sj) & (np.arange(NSA_N_CMP_PAD)[:, None] < n_cmp))
    ovt = jnp.asarray(overlap.T.astype(np.float32))
    nsa_cols = rel_bias[:, NSA_BIAS_COL0:NSA_BIAS_COL0 + NSA_HEADS]
    bslc = _slc_bias_t(nsa_cols, seq)
    bwin = jnp.transpose(_band_bias(nsa_cols, NSA_WINDOW, NSA_WIN_PREV, 1, NSA_KV_HEADS, NSA_REPS), (0, 2, 1))
    return _nsa(z_nsa3, z_cmp.reshape(batch, seq, -1), w1_bd[0].astype(BF16), w1_bd[1].astype(BF16),
                w2_bd.astype(BF16), pos_kv, q_norm.reshape(HEAD_DIM, 1), k_norm.reshape(3, 1, HEAD_DIM),
                ovt, bslc, bwin, batch, seq)


def kernel(x, rel_bias, ffn1_norm, ffn1_w_gate, ffn1_w_up, ffn1_w_down, mix_norm, ffn2_norm, ffn2_w_gate,
           ffn2_w_up, ffn2_w_down, ab_w_in, mla_q_a_norm, mla_w_q_b, mla_kv_a_norm, mla_w_kv_b, mla_q_norm,
           mla_k_norm, dil_q_norm, dil_k_norm, ab_w_out, cd_w_in, swa_q_norm, swa_k_norm, swa_sinks,
           nsa_q_norm, nsa_k_norm, nsa_cmp_pos, nsa_cmp_w1, nsa_cmp_w2, cd_w_out):
    batch, seq, _ = x.shape
    n = batch * seq
    assert seq % (16 * QB) == 0 and n % TM == 0 and seq % TM == 0
    bf = lambda a: a.astype(BF16)
    xf = x.reshape(n, D_MODEL)
    ffn1 = (bf(ffn1_w_gate), bf(ffn1_w_up), bf(ffn1_w_down))
    ffn2 = (bf(ffn2_w_gate), bf(ffn2_w_up), bf(ffn2_w_down))

    w_in = ab_w_in[0]
    mla_cols = MLA_Q_RANK + MLA_KV_RANK
    w_krope = _with_swapped_rope(jnp.pad(w_in[:, mla_cols:mla_cols + MLA_ROPE], ((0, 0), (MLA_NOPE, 0))))
    w_mla = jnp.concatenate([w_in[:, :mla_cols], w_krope], axis=1)
    wq = _with_swapped_rope(mla_w_q_b[0].reshape(MLA_Q_RANK, MLA_HEADS, MLA_QK_DIM))
    wq = wq.reshape(MLA_Q_RANK, MLA_HEADS * MLA_PAD)
    wkv = mla_w_kv_b[0].reshape(MLA_KV_RANK, MLA_HEADS, MLA_NOPE + MLA_V)
    wk = _pad_cols(wkv[:, :, :MLA_NOPE].reshape(MLA_KV_RANK * MLA_HEADS, MLA_NOPE), MLA_PAD)
    wk = wk.reshape(MLA_KV_RANK, MLA_HEADS * MLA_PAD)
    wv = wkv[:, :, MLA_NOPE:].reshape(MLA_KV_RANK, MLA_HEADS * MLA_V)
    cos_t, sin_t = _rope_tables(seq)
    q_scale = MLA_QK_DIM ** -0.5 * LOG2E
    xf, z_dil, q_mla, k_mla, vt_mla = _ffn_proj_mla(
        xf, ffn1_norm[0], ffn1, 0, mix_norm[0], bf(w_mla), bf(w_in[:, mla_cols + MLA_ROPE:]),
        mla_q_a_norm[0].reshape(1, -1), mla_kv_a_norm[0].reshape(1, -1), bf(wq), bf(wk), bf(wv),
        _with_swapped_rope(mla_q_norm[0].reshape(1, -1)), _with_swapped_rope(mla_k_norm[0].reshape(1, -1)),
        cos_t * q_scale, sin_t * q_scale, cos_t, sin_t, seq)
    o_a = _mla_attn(q_mla.reshape(batch, seq, -1), k_mla.reshape(batch, seq, -1),
                    vt_mla.reshape(batch, seq // MLA_TQ, MLA_HEADS * MLA_V, MLA_TQ),
                    batch, seq).reshape(n, MLA_HEADS * MLA_V)

    z_dil3 = z_dil.reshape(batch, seq, -1)
    gq, gk = dil_q_norm[0], dil_k_norm[0]
    dil_outs, dil_lses = [], []
    for grp, (window, dilation) in enumerate(DIL_PATTERNS):
        bm = _band_bias(rel_bias[:, grp * DIL_HPG:(grp + 1) * DIL_HPG], window // dilation + 1, 1, dilation,
                        DIL_HPG, 1)
        o, lse = _dilated_group(z_dil3, batch, seq, grp, dilation, gq, gk, bm)
        dil_outs.append(o)
        dil_lses.append(lse)
    w_out = ab_w_out[0]
    xf = _out_ab_ffn(xf, o_a, dil_outs, dil_lses, bf(w_out[:MLA_HEADS * MLA_V]), bf(w_out[MLA_HEADS * MLA_V:]),
                     ffn2_norm[0], ffn2, 0)

    w_in = cd_w_in[0]
    swa_cols = (SWA_HEADS + 2 * SWA_KV_HEADS) * HEAD_DIM
    w_nsa_src = w_in[:, swa_cols:]
    w_nsa = jnp.concatenate([_pad_cols(w_nsa_src[:, np.asarray(cols)], NSA_GROUP_COLS)
                             for cols in _nsa_column_order()], axis=1)
    w_cmp = w_nsa_src[:, np.asarray(_nsa_cmp_columns())]
    xf, z_swa, z_nsa, z_cmp = _ffn_proj(xf, ffn1_norm[1], ffn1, 1, mix_norm[1],
                                        [bf(w_in[:, :swa_cols]), bf(w_nsa), bf(w_cmp)])

    swa_reps = SWA_HEADS // SWA_KV_HEADS
    swa_prev = -(-(SWA_WINDOW - 1) // QB)
    bm_swa = _band_bias(rel_bias[:, :SWA_HEADS], SWA_WINDOW, swa_prev, 1, SWA_KV_HEADS, swa_reps)
    sinks = jnp.broadcast_to((swa_sinks[0].astype(F32) * LOG2E).reshape(SWA_KV_HEADS, 1, swa_reps, 1),
                             (SWA_KV_HEADS, 1, swa_reps, QB)).reshape(SWA_KV_HEADS, 1, swa_reps * QB)
    o_c = _swa(z_swa.reshape(batch, seq, -1), batch, seq, swa_q_norm[0].reshape(HEAD_DIM, 1),
               swa_k_norm[0].reshape(1, HEAD_DIM), bm_swa, sinks)

    o_d = _nsa_mixer(z_nsa.reshape(batch, seq, NSA_KV_HEADS * NSA_GROUP_COLS), z_cmp, rel_bias, nsa_q_norm[0],
                     nsa_k_norm[0], nsa_cmp_pos[0], nsa_cmp_w1[0], nsa_cmp_w2[0], batch, seq)

    w_out = cd_w_out[0]
    xf = _out_cd_ffn(xf, o_c.reshape(n, -1), o_d.reshape(n, -1), bf(w_out[:SWA_HEADS * HEAD_DIM]),
                     bf(w_out[SWA_HEADS * HEAD_DIM:]), ffn2_norm[1], ffn2, 1)
    return xf.reshape(batch, seq, D_MODEL)
```
